```python
import jax, jax.numpy as jnp
from jax import lax
import numpy as np

D_MODEL = 1024
BATCH = 32
SEQ = 2048
DEPTH = 4

HEAD_DIM = 64
N_MIXERS = 4
GROUP_WIDTH = D_MODEL // N_MIXERS
GROUP_HEADS = GROUP_WIDTH // HEAD_DIM
MIX_WIDTH = N_MIXERS * GROUP_WIDTH
BLOCK = 128
GMLP_CHUNK = 128
MLA_Q_RANK = 192
MLA_KV_RANK = 128
MLA_NOPE_DIM = 64
MLA_ROPE_DIM = 32
MLA_V_DIM = 64
DILATED_BRANCHES = ((128, 1), (512, 4), (2048, 16))
MLSTM_CHUNK = 128
CONV_WIDTH = 4
D_FF = ((8 * D_MODEL + 3 * 256 - 1) // (3 * 256)) * 256
ROPE_THETA = 10000.0
LN_EPS = 1e-5
RMS_EPS = 1e-6
DEEPNORM_ALPHA = (2 * DEPTH) ** 0.25
DEEPNORM_BETA = (8 * DEPTH) ** -0.25
A_COLS = 2 * GROUP_WIDTH
B_COLS = MLA_Q_RANK + MLA_KV_RANK + MLA_ROPE_DIM
C_COLS = 3 * GROUP_WIDTH
D_COLS = 4 * GROUP_WIDTH + 2 * GROUP_HEADS
IN_COLS = A_COLS + B_COLS + C_COLS + D_COLS

kernel_name = 'hymba_style_gmlp_mla_dilated_mlstm_deepnorm'


def _layer_norm(x, g, b):
    xf = x.astype(jnp.float32)
    mu = xf.mean(-1, keepdims=True)
    var = jnp.square(xf - mu).mean(-1, keepdims=True)
    return ((xf - mu) * lax.rsqrt(var + LN_EPS) * g + b).astype(x.dtype)


def _rms_norm(x, g):
    xf = x.astype(jnp.float32)
    return (xf * lax.rsqrt(jnp.square(xf).mean(-1, keepdims=True) + RMS_EPS) * g).astype(x.dtype)


def _rope(x, pos):
    half = x.shape[-1] // 2
    inv = jnp.power(ROPE_THETA, -jnp.arange(half, dtype=jnp.float32) / half)
    ang = pos.astype(jnp.float32)[..., None] * inv
    cos = jnp.cos(ang)[:, :, None, :]
    sin = jnp.sin(ang)[:, :, None, :]
    x1 = x[..., :half].astype(jnp.float32)
    x2 = x[..., half:].astype(jnp.float32)
    return jnp.concatenate([x1 * cos - x2 * sin, x1 * sin + x2 * cos], -1).astype(x.dtype)


def _causal_block_attention(q, k, v, scale):
    bsz, seq, heads, dq = q.shape
    nb = seq // BLOCK
    qb = q.reshape(bsz, nb, BLOCK, heads, dq).swapaxes(0, 1)
    key_pos = jnp.arange(seq)

    def one_block(args):
        q_blk, start = args
        s = jnp.einsum('bqhd,bkhd->bhqk', q_blk, k).astype(jnp.float32) * scale
        q_pos = start + jnp.arange(BLOCK)
        s = jnp.where(key_pos[None, :] <= q_pos[:, None], s, -jnp.inf)
        p = jax.nn.softmax(s, axis=-1).astype(v.dtype)
        return jnp.einsum('bhqk,bkhd->bqhd', p, v)

    out = lax.map(one_block, (qb, jnp.arange(nb, dtype=jnp.int32) * BLOCK))
    return out.swapaxes(0, 1).reshape(bsz, seq, heads, v.shape[-1])


def _gmlp_mixer(z, ln_g, ln_b, w_s, b_s):
    bsz, seq, _ = z.shape
    u, v = jnp.split(jax.nn.gelu(z, approximate=False), 2, axis=-1)
    v = _layer_norm(v, ln_g, ln_b)
    n_chunks = seq // GMLP_CHUNK
    v = v.reshape(bsz, n_chunks, GMLP_CHUNK, GROUP_HEADS, HEAD_DIM)
    causal = jnp.tril(jnp.ones((GMLP_CHUNK, GMLP_CHUNK), dtype=bool))
    w_causal = jnp.where(causal[None], w_s, 0.0)
    mixed = jnp.einsum('hts,bcshd->bcthd', w_causal, v) + b_s.T[None, None, :, :, None]
    return u * mixed.reshape(bsz, seq, GROUP_WIDTH)


def _mla_mixer(z, pos, q_norm_g, kv_norm_g, w_uq, w_ukv):
    bsz, seq, _ = z.shape
    c_q, c_kv, k_r = jnp.split(z, [MLA_Q_RANK, MLA_Q_RANK + MLA_KV_RANK], axis=-1)
    q = (_rms_norm(c_q, q_norm_g) @ w_uq).reshape(bsz, seq, GROUP_HEADS, MLA_NOPE_DIM + MLA_ROPE_DIM)
    q = jnp.concatenate([q[..., :MLA_NOPE_DIM], _rope(q[..., MLA_NOPE_DIM:], pos)], -1)
    kv = (_rms_norm(c_kv, kv_norm_g) @ w_ukv).reshape(bsz, seq, GROUP_HEADS, MLA_NOPE_DIM + MLA_V_DIM)
    k_nope, v = kv[..., :MLA_NOPE_DIM], kv[..., MLA_NOPE_DIM:]
    k_rope = jnp.broadcast_to(_rope(k_r[:, :, None, :], pos), (bsz, seq, GROUP_HEADS, MLA_ROPE_DIM))
    k = jnp.concatenate([k_nope, k_rope], -1)
    o = _causal_block_attention(q, k, v, (MLA_NOPE_DIM + MLA_ROPE_DIM) ** -0.5)
    return o.reshape(bsz, seq, GROUP_HEADS * MLA_V_DIM)


def _dilated_branch(q, k, v, window, dilation):
    bsz, seq, heads, dh = q.shape
    n_back = window // dilation
    span = dilation * BLOCK
    padded = -(-seq // span) * span
    sub_len = padded // dilation
    nb = sub_len // BLOCK

    def to_residue_blocks(x):
        x = jnp.pad(x, ((0, 0), (0, padded - seq), (0, 0), (0, 0)))
        x = x.reshape(bsz, sub_len, dilation, heads, dh).transpose(0, 2, 1, 3, 4)
        return x.reshape(bsz * dilation, nb, BLOCK, heads, dh)

    qb, kb, vb = to_residue_blocks(q), to_residue_blocks(k), to_residue_blocks(v)
    k_prev = jnp.pad(kb, ((0, 0), (1, 0), (0, 0), (0, 0), (0, 0)))[:, :-1]
    v_prev = jnp.pad(vb, ((0, 0), (1, 0), (0, 0), (0, 0), (0, 0)))[:, :-1]
    kk = jnp.concatenate([k_prev, kb], axis=2)
    vv = jnp.concatenate([v_prev, vb], axis=2)
    s = jnp.einsum('gnqhd,gnkhd->gnhqk', qb, kk).astype(jnp.float32) * (dh ** -0.5)
    rel = (BLOCK + jnp.arange(BLOCK))[:, None] - jnp.arange(2 * BLOCK)[None, :]
    band = (rel >= 0) & (rel <= n_back)
    not_pad = (jnp.arange(nb)[:, None, None] > 0) | (jnp.arange(2 * BLOCK)[None, None, :] >= BLOCK)
    mask = band[None] & not_pad
    s = jnp.where(mask[None, :, None], s, -jnp.inf)
    lse = jax.nn.logsumexp(s, axis=-1)
    p = jnp.exp(s - lse[..., None]).astype(v.dtype)
    o = jnp.einsum('gnhqk,gnkhd->gnqhd', p, vv)
    o = o.reshape(bsz, dilation, sub_len, heads, dh).transpose(0, 2, 1, 3, 4)
    o = o.reshape(bsz, padded, heads, dh)[:, :seq]
    lse = lse.transpose(0, 1, 3, 2).reshape(bsz, dilation, sub_len, heads).transpose(0, 2, 1, 3)
    lse = lse.reshape(bsz, padded, heads)[:, :seq]
    return o, lse


def _dilated_mixer(z, pos):
    bsz, seq, _ = z.shape
    q, k, v = [t.reshape(bsz, seq, GROUP_HEADS, HEAD_DIM) for t in jnp.split(z, 3, axis=-1)]
    q, k = _rope(q, pos), _rope(k, pos)
    outs, lses = [], []
    for window, dilation in DILATED_BRANCHES:
        o, lse = _dilated_branch(q, k, v, window, dilation)
        outs.append(o)
        lses.append(lse)
    w = jax.nn.softmax(jnp.stack(lses, 0), axis=0).astype(v.dtype)
    o = jnp.einsum('rbsh,rbshd->bshd', w, jnp.stack(outs, 0))
    return o.reshape(bsz, seq, GROUP_WIDTH)


def _causal_conv(x, w, b):
    ch = x.shape[-1]
    y = lax.conv_general_dilated(x, w[:, None, :], window_strides=(1,), padding=[(CONV_WIDTH - 1, 0)],
                                 dimension_numbers=('NWC', 'WIO', 'NWC'), feature_group_count=ch)
    return y + b


def _mlstm_chunkwise(q, k, v, i_pre, f_pre):
    dtype = v.dtype
    q, k, v = q.astype(jnp.float32), k.astype(jnp.float32), v.astype(jnp.float32)
    i_log = i_pre.astype(jnp.float32)
    f_log = jax.nn.log_sigmoid(f_pre.astype(jnp.float32))
    bsz, seq, heads, dh = q.shape
    n_chunks = seq // MLSTM_CHUNK
    causal = jnp.tril(jnp.ones((MLSTM_CHUNK, MLSTM_CHUNK), dtype=bool))

    def chunks(x):
        return x.reshape((bsz, n_chunks, MLSTM_CHUNK) + x.shape[2:]).swapaxes(0, 1)

    def step(carry, xs):
        c_mat, n_vec, m_run = carry
        qc, kc, vc, ic, fc = xs
        b_cum = jnp.cumsum(fc, axis=1).transpose(0, 2, 1)
        ic = ic.transpose(0, 2, 1)
        d_log = b_cum[:, :, :, None] - b_cum[:, :, None, :] + ic[:, :, None, :]
        d_log = jnp.where(causal, d_log, -jnp.inf)
        m_inter = b_cum + m_run[:, :, None]
        m_t = jnp.maximum(m_inter, d_log.max(-1))
        w_intra = jnp.exp(d_log - m_t[..., None])
        sqk = jnp.einsum('bthd,bshd->bhts', qc, kc) * w_intra
        scale = jnp.exp(m_inter - m_t)
        num = jnp.einsum('bhts,bshd->bthd', sqk, vc) + \
            scale.transpose(0, 2, 1)[..., None] * jnp.einsum('bthk,bhkv->bthv', qc, c_mat)
        den = sqk.sum(-1) + scale * jnp.einsum('bthk,bhk->bht', qc, n_vec)
        h = num / jnp.maximum(jnp.abs(den), jnp.exp(-m_t)).transpose(0, 2, 1)[..., None]
        b_end = b_cum[:, :, -1]
        w_end = b_end[:, :, None] - b_cum + ic
        m_new = jnp.maximum(b_end + m_run, w_end.max(-1))
        decay = jnp.exp(b_end + m_run - m_new)
        w_k = jnp.exp(w_end - m_new[..., None])
        c_mat = decay[..., None, None] * c_mat + jnp.einsum('bhs,bshk,bshv->bhkv', w_k, kc, vc)
        n_vec = decay[..., None] * n_vec + jnp.einsum('bhs,bshk->bhk', w_k, kc)
        return (c_mat, n_vec, m_new), h

    init = (jnp.zeros((bsz, heads, dh, dh), jnp.float32),
            jnp.zeros((bsz, heads, dh), jnp.float32),
            jnp.zeros((bsz, heads), jnp.float32))
    _, h = lax.scan(step, init, (chunks(q), chunks(k), chunks(v), chunks(i_log), chunks(f_log)))
    return h.swapaxes(0, 1).reshape(bsz, seq, heads, dh).astype(dtype)


def _mlstm_mixer(z, conv_w, conv_b, igate_b, fgate_b):
    bsz, seq, _ = z.shape
    qk, v, o, gates = jnp.split(z, [2 * GROUP_WIDTH, 3 * GROUP_WIDTH, 4 * GROUP_WIDTH], axis=-1)
    qk = jax.nn.silu(_causal_conv(qk, conv_w, conv_b))
    q, k = jnp.split(qk, 2, axis=-1)
    q = q.reshape(bsz, seq, GROUP_HEADS, HEAD_DIM)
    k = k.reshape(bsz, seq, GROUP_HEADS, HEAD_DIM) * (HEAD_DIM ** -0.5)
    v = v.reshape(bsz, seq, GROUP_HEADS, HEAD_DIM)
    i_pre = gates[..., :GROUP_HEADS] + igate_b
    f_pre = gates[..., GROUP_HEADS:] + fgate_b
    h = _mlstm_chunkwise(q, k, v, i_pre, f_pre)
    return jax.nn.sigmoid(o) * h.reshape(bsz, seq, GROUP_WIDTH)


def setup_inputs(seed: int = 0) -> dict:
    key = jax.random.key(seed)
    ks = jax.random.split(key, 23)

    def nrm(k, shape, scale):
        return jax.random.normal(k, shape, jnp.float32) * scale

    L = DEPTH
    x = nrm(ks[0], (BATCH, SEQ, D_MODEL), 1.0)
    offsets = jax.random.randint(ks[1], (BATCH, 1), 0, 4096, dtype=jnp.int32)
    positions = offsets + jnp.arange(SEQ, dtype=jnp.int32)[None, :]
    return {
        'x': x,
        'positions': positions,
        'w_in': nrm(ks[2], (L, D_MODEL, IN_COLS), D_MODEL ** -0.5),
        'a_ln_g': 1.0 + nrm(ks[3], (L, GROUP_WIDTH), 0.02),
        'a_ln_b': nrm(ks[4], (L, GROUP_WIDTH), 0.02),
        'a_ws': nrm(ks[5], (L, GROUP_HEADS, GMLP_CHUNK, GMLP_CHUNK), GMLP_CHUNK ** -0.5),
        'a_bs': 1.0 + nrm(ks[6], (L, GROUP_HEADS, GMLP_CHUNK), 0.02),
        'b_q_norm': 1.0 + nrm(ks[7], (L, MLA_Q_RANK), 0.02),
        'b_kv_norm': 1.0 + nrm(ks[8], (L, MLA_KV_RANK), 0.02),
        'b_w_uq': nrm(ks[9], (L, MLA_Q_RANK, GROUP_HEADS * (MLA_NOPE_DIM + MLA_ROPE_DIM)), MLA_Q_RANK ** -0.5),
        'b_w_ukv': nrm(ks[10], (L, MLA_KV_RANK, GROUP_HEADS * (MLA_NOPE_DIM + MLA_V_DIM)), MLA_KV_RANK ** -0.5),
        'd_conv_w': nrm(ks[11], (L, CONV_WIDTH, 2 * GROUP_WIDTH), CONV_WIDTH ** -0.5),
        'd_conv_b': nrm(ks[12], (L, 2 * GROUP_WIDTH), 0.02),
        'd_igate_b': nrm(ks[13], (L, GROUP_HEADS), 0.1),
        'd_fgate_b': jnp.linspace(3.0, 6.0, GROUP_HEADS, dtype=jnp.float32)[None, :] + nrm(ks[14], (L, GROUP_HEADS), 0.02),
        'w_out': nrm(ks[15], (L, MIX_WIDTH, D_MODEL), DEEPNORM_BETA * MIX_WIDTH ** -0.5),
        'ln1_g': 1.0 + nrm(ks[16], (L, D_MODEL), 0.02),
        'ln1_b': nrm(ks[17], (L, D_MODEL), 0.02),
        'w_gate': nrm(ks[18], (L, D_MODEL, D_FF), D_MODEL ** -0.5),
        'w_up': nrm(ks[19], (L, D_MODEL, D_FF), D_MODEL ** -0.5),
        'w_down': nrm(ks[20], (L, D_FF, D_MODEL), DEEPNORM_BETA * D_FF ** -0.5),
        'ln2_g': 1.0 + nrm(ks[21], (L, D_MODEL), 0.02),
        'ln2_b': nrm(ks[22], (L, D_MODEL), 0.02),
    }


def reference(x, positions, w_in, a_ln_g, a_ln_b, a_ws, a_bs, b_q_norm, b_kv_norm, b_w_uq, b_w_ukv,
              d_conv_w, d_conv_b, d_igate_b, d_fgate_b, w_out, ln1_g, ln1_b, w_gate, w_up, w_down,
              ln2_g, ln2_b):
    for l in range(DEPTH):
        z = x @ w_in[l]
        z_a, z_b, z_c, z_d = jnp.split(z, [A_COLS, A_COLS + B_COLS, A_COLS + B_COLS + C_COLS], axis=-1)
        y_a = _gmlp_mixer(z_a, a_ln_g[l], a_ln_b[l], a_ws[l], a_bs[l])
        y_b = _mla_mixer(z_b, positions, b_q_norm[l], b_kv_norm[l], b_w_uq[l], b_w_ukv[l])
        y_c = _dilated_mixer(z_c, positions)
        y_d = _mlstm_mixer(z_d, d_conv_w[l], d_conv_b[l], d_igate_b[l], d_fgate_b[l])
        y = jnp.concatenate([y_a, y_b, y_c, y_d], axis=-1) @ w_out[l]
        x = _layer_norm(DEEPNORM_ALPHA * x + y, ln1_g[l], ln1_b[l])
        h = (jax.nn.silu(x @ w_gate[l]) * (x @ w_up[l])) @ w_down[l]
        x = _layer_norm(DEEPNORM_ALPHA * x + h, ln2_g[l], ln2_b[l])
    return x
```

```python
import functools

import jax
import jax.numpy as jnp
from jax import lax
from jax.experimental import pallas as pl
from jax.experimental.pallas import tpu as pltpu

F32 = jnp.float32
BF16 = jnp.bfloat16

D_MODEL = 1024
DEPTH = 4
HEAD_DIM = 64
HEADS = 4
GW = HEADS * HEAD_DIM
BLOCK = 128
Q_RANK = 192
KV_RANK = 128
NOPE = 64
ROPE = 32
D_FF = 2816
ROPE_THETA = 10000.0
LN_EPS = 1e-5
RMS_EPS = 1e-6
ALPHA = (2 * DEPTH) ** 0.25
LANES = 128
NEG_INF = float("-inf")

ZD_W, ZA_W, ZC_W, ZB_W = 4 * GW, 2 * GW, 3 * GW, 384
ZMAIN_W = ZD_W + ZA_W + ZC_W + ZB_W
KR_LANE = 64

VMEM_LIMIT = 56 * 1024 * 1024


def _dot(a, b):
    return jnp.dot(a, b, preferred_element_type=F32)


def _dot_nt(a, b):
    return lax.dot_general(a, b, (((1,), (1,)), ((), ())), preferred_element_type=F32)


def _params(n_axes=1):
    return pltpu.CompilerParams(
        dimension_semantics=("arbitrary",) * n_axes, vmem_limit_bytes=VMEM_LIMIT)


def _layer_norm(r, g, b):
    mu = jnp.mean(r, axis=-1, keepdims=True)
    d = r - mu
    var = jnp.mean(d * d, axis=-1, keepdims=True)
    return d * lax.rsqrt(var + LN_EPS) * g + b


def _tables_kernel(pos_ref, invd_ref, invm_ref, cosd_ref, sind_ref, cosm_ref, sinm_ref):
    pos = pos_ref[0].astype(F32)
    lane = lax.broadcasted_iota(jnp.int32, (1, LANES), 1)
    angd = pos * invd_ref[...]
    cosd_ref[0] = jnp.cos(angd)
    sind_ref[0] = jnp.sin(angd) * jnp.where((lane % HEAD_DIM) < HEAD_DIM // 2, -1.0, 1.0)
    angm = pos * invm_ref[...]
    in_rope = (lane >= KR_LANE) & (lane < KR_LANE + ROPE)
    cosm_ref[0] = jnp.where(in_rope, jnp.cos(angm), jnp.where(lane < KR_LANE, 1.0, 0.0))
    sinm_ref[0] = jnp.where(in_rope, jnp.sin(angm), 0.0)


def _rope_tables(positions):
    bsz, seq = positions.shape
    lane = jnp.arange(LANES)
    half_d = HEAD_DIM // 2
    inv_d = jnp.power(ROPE_THETA, -jnp.arange(half_d, dtype=F32) / half_d)
    half_m = ROPE // 2
    inv_m = jnp.power(ROPE_THETA, -jnp.arange(half_m, dtype=F32) / half_m)
    invd = inv_d[lane % half_d][None, :]
    invm = inv_m[lane % half_m][None, :]
    tab = jax.ShapeDtypeStruct((bsz, seq, LANES), F32)
    spec = pl.BlockSpec((1, seq, LANES), lambda b: (b, 0, 0))
    cspec = pl.BlockSpec((1, LANES), lambda b: (0, 0))
    return pl.pallas_call(
        _tables_kernel,
        grid=(bsz,),
        in_specs=[pl.BlockSpec((1, seq, 1), lambda b: (b, 0, 0)), cspec, cspec],
        out_specs=[spec] * 4,
        out_shape=[tab] * 4,
        compiler_params=_params(),
        name="rope_tables",
    )(positions.reshape(bsz, seq, 1), invd, invm)


def _inproj_kernel(x_ref, w_ref, wm_ref, zd_ref, za_ref, zc_ref, zb_ref, zm_ref, xb_ref):
    xb_ref[...] = x_ref[...].astype(BF16)
    off = 0
    for ref in (zd_ref, za_ref, zc_ref, zb_ref):
        width = ref.shape[1]
        for c in range(0, width, 256):
            cw = min(256, width - c)
            ref[:, c:c + cw] = _dot(xb_ref[...], w_ref[:, off + c:off + c + cw]).astype(BF16)
        off += width
    zm_ref[...] = _dot(xb_ref[...], wm_ref[...])


def _inproj(x2, w_main, w_misc, tm):
    n = x2.shape[0]
    row = lambda w: pl.BlockSpec((tm, w), lambda i: (i, 0))
    full = lambda a: pl.BlockSpec(a.shape, lambda i: (0, 0))
    return pl.pallas_call(
        _inproj_kernel,
        grid=(n // tm,),
        in_specs=[row(D_MODEL), full(w_main), full(w_misc)],
        out_specs=[row(ZD_W), row(ZA_W), row(ZC_W), row(ZB_W), row(LANES)],
        out_shape=[jax.ShapeDtypeStruct((n, ZD_W), BF16), jax.ShapeDtypeStruct((n, ZA_W), BF16),
                   jax.ShapeDtypeStruct((n, ZC_W), BF16), jax.ShapeDtypeStruct((n, ZB_W), BF16),
                   jax.ShapeDtypeStruct((n, LANES), F32)],
        scratch_shapes=[pltpu.VMEM((tm, D_MODEL), BF16)],
        compiler_params=_params(),
        name="inproj",
    )(x2, w_main, w_misc)


def _gmlp_kernel(za_ref, lng_ref, lnb_ref, ws_ref, bias_ref, y_ref):
    seq = za_ref.shape[0]
    r_i = lax.broadcasted_iota(jnp.int32, (BLOCK, BLOCK), 0)
    c_i = lax.broadcasted_iota(jnp.int32, (BLOCK, BLOCK), 1)
    w_causal = [jnp.where(c_i <= r_i, ws_ref[h], 0.0).astype(BF16) for h in range(HEADS)]
    lane_head = lax.broadcasted_iota(jnp.int32, (BLOCK, GW), 1) // HEAD_DIM

    def chunk(c, carry):
        r0 = pl.multiple_of(c * BLOCK, BLOCK)
        z = za_ref[pl.ds(r0, BLOCK), :].astype(F32)
        g = 0.5 * z * (1.0 + lax.erf(z * (0.5 ** 0.5)))
        u, v = g[:, :GW], g[:, GW:]
        vb = _layer_norm(v, lng_ref[...], lnb_ref[...]).astype(BF16)
        mixed = jnp.zeros((BLOCK, GW), F32)
        for h in range(HEADS):
            mixed = jnp.where(lane_head == h, _dot(w_causal[h], vb), mixed)
        y_ref[pl.ds(r0, BLOCK), :] = (u * (mixed + bias_ref[...])).astype(BF16)
        return carry

    lax.fori_loop(0, seq // BLOCK, chunk, 0)


def _gmlp(za, ln_g, ln_b, ws, bias, bsz, seq):
    full = lambda a: pl.BlockSpec(a.shape, lambda b: (0,) * a.ndim)
    return pl.pallas_call(
        _gmlp_kernel,
        grid=(bsz,),
        in_specs=[pl.BlockSpec((seq, ZA_W), lambda b: (b, 0)), full(ln_g), full(ln_b), full(ws),
                  full(bias)],
        out_specs=pl.BlockSpec((seq, GW), lambda b: (b, 0)),
        out_shape=jax.ShapeDtypeStruct((bsz * seq, GW), BF16),
        compiler_params=_params(),
        name="gmlp",
    )(za, ln_g, ln_b, ws, bias)


MLA_TQ = 256


def _mla_kernel(zb_ref, zm_ref, cos_ref, sin_ref, qg_ref, kvg_ref, wqm_ref, wqs_ref, wk_ref, wv_ref,
                y_ref, q_s, k_s, v_s):
    seq = zb_ref.shape[0]
    scale = (NOPE + ROPE) ** -0.5
    lane = lax.broadcasted_iota(jnp.int32, (1, LANES), 1)
    in_rope = (lane >= KR_LANE) & (lane < KR_LANE + ROPE)
    rows = 512
    for r in range(0, seq, rows):
        cos = cos_ref[0, r:r + rows, :]
        sin = sin_ref[0, r:r + rows, :]
        cq = zb_ref[r:r + rows, 0:256].astype(F32)
        ms = jnp.sum(cq * cq, axis=-1, keepdims=True) * (1.0 / Q_RANK)
        cqn = (cq * lax.rsqrt(ms + RMS_EPS) * qg_ref[...]).astype(BF16)
        qm = _dot(cqn, wqm_ref[...])
        qs = _dot(cqn, wqs_ref[...])
        ckv = zb_ref[r:r + rows, 256:384].astype(F32)
        ms = jnp.mean(ckv * ckv, axis=-1, keepdims=True)
        ckvn = (ckv * lax.rsqrt(ms + RMS_EPS) * kvg_ref[...]).astype(BF16)
        kn = _dot(ckvn, wk_ref[...])
        v_s[r:r + rows, :] = _dot(ckvn, wv_ref[...]).astype(BF16)
        zm = zm_ref[r:r + rows, :]
        kr = jnp.where(in_rope, zm * cos + pltpu.roll(zm, LANES - ROPE, 1) * sin, 0.0)
        for h in range(HEADS):
            sl = slice(h * LANES, (h + 1) * LANES)
            q_s[r:r + rows, sl] = ((qm[:, sl] * cos + qs[:, sl] * sin) * scale).astype(BF16)
            k_s[r:r + rows, sl] = (kn[:, sl] + kr).astype(BF16)

    tq = MLA_TQ
    lane_head = lax.broadcasted_iota(jnp.int32, (tq, GW), 1) // HEAD_DIM
    r_i = lax.broadcasted_iota(jnp.int32, (tq, tq), 0)
    c_i = lax.broadcasted_iota(jnp.int32, (tq, tq), 1)

    def qblock(i, carry):
        q0 = pl.multiple_of(i * tq, tq)
        out = jnp.zeros((tq, GW), F32)
        for h in range(HEADS):
            sl = slice(h * LANES, (h + 1) * LANES)
            q = q_s[pl.ds(q0, tq), sl]

            def update(state, k0, masked):
                m, l, acc = state
                s = _dot_nt(q, k_s[pl.ds(k0, tq), sl])
                if masked:
                    s = jnp.where(c_i <= r_i, s, NEG_INF)
                m_new = jnp.maximum(m, jnp.max(s, axis=-1, keepdims=True))
                a = jnp.exp(m - m_new)
                p = jnp.exp(s - m_new)
                l = a * l + jnp.sum(p, axis=-1, keepdims=True)
                acc = a * acc + _dot(p.astype(BF16), v_s[pl.ds(k0, tq), :])
                return m_new, l, acc

            init = (jnp.full((tq, 1), NEG_INF, F32), jnp.zeros((tq, 1), F32),
                    jnp.zeros((tq, GW), F32))
            state = lax.fori_loop(
                0, i, lambda j, st: update(st, pl.multiple_of(j * tq, tq), False), init)
            m, l, acc = update(state, q0, True)
            out = jnp.where(lane_head == h, acc / l, out)
        y_ref[pl.ds(q0, tq), :] = out.astype(BF16)
        return carry

    lax.fori_loop(0, seq // tq, qblock, 0)


def _mla(zb, zm, cosm, sinm, qg, kvg, wqm, wqs, wk, wv, bsz, seq):
    full = lambda a: pl.BlockSpec(a.shape, lambda b: (0,) * a.ndim)
    tab = pl.BlockSpec((1, seq, LANES), lambda b: (b, 0, 0))
    return pl.pallas_call(
        _mla_kernel,
        grid=(bsz,),
        in_specs=[pl.BlockSpec((seq, ZB_W), lambda b: (b, 0)),
                  pl.BlockSpec((seq, LANES), lambda b: (b, 0)), tab, tab,
                  full(qg), full(kvg), full(wqm), full(wqs), full(wk), full(wv)],
        out_specs=pl.BlockSpec((seq, GW), lambda b: (b, 0)),
        out_shape=jax.ShapeDtypeStruct((bsz * seq, GW), BF16),
        scratch_shapes=[pltpu.VMEM((seq, HEADS * LANES), BF16), pltpu.VMEM((seq, HEADS * LANES), BF16),
                        pltpu.VMEM((seq, GW), BF16)],
        compiler_params=_params(),
        name="mla",
    )(zb, zm, cosm, sinm, qg, kvg, wqm, wqs, wk, wv)


DILATIONS = (1, 4, 16)
HALVES = GW // LANES


def _band_block(qb, kw, vw, mask):
    lane_head = lax.broadcasted_iota(jnp.int32, (BLOCK, GW), 1) // HEAD_DIM
    o = jnp.zeros((BLOCK, GW), F32)
    lse = jnp.zeros((BLOCK, GW), F32)
    for h in range(HEADS):
        qh = jnp.where(lane_head == h, qb, jnp.zeros_like(qb))
        s = jnp.where(mask, _dot_nt(qh, kw), NEG_INF)
        m = jnp.max(s, axis=-1, keepdims=True)
        p = jnp.exp(s - m)
        l = jnp.sum(p, axis=-1, keepdims=True)
        pv = _dot(p.astype(BF16), vw)
        o = jnp.where(lane_head == h, pv / l, o)
        lse = jnp.where(lane_head == h, m + jnp.log(l), lse)
    return o, lse


def _dilated_kernel(zc_ref, cos_ref, sin_ref, y_ref, qf, kf, vf, qb, kb, vb, o_acc, l_acc, o_br, l_br):
    seq = zc_ref.shape[0]
    lane = lax.broadcasted_iota(jnp.int32, (1, LANES), 1)
    first = (lane % HEAD_DIM) < HEAD_DIM // 2
    rows = 512
    for r in range(0, seq, rows):
        cos = cos_ref[0, r:r + rows, :]
        sin = sin_ref[0, r:r + rows, :]

        def rope(x):
            rot = jnp.where(first, pltpu.roll(x, LANES - HEAD_DIM // 2, 1),
                            pltpu.roll(x, HEAD_DIM // 2, 1))
            return x * cos + rot * sin

        for half in range(HALVES):
            col = lambda base: slice(base + half * LANES, base + (half + 1) * LANES)
            qf[half, r:r + rows, :] = rope(zc_ref[r:r + rows, col(0)].astype(F32)) * (HEAD_DIM ** -0.5)
            kf[half, r:r + rows, :] = rope(zc_ref[r:r + rows, col(GW)].astype(F32))
            vf[half, r:r + rows, :] = zc_ref[r:r + rows, col(2 * GW)].astype(F32)

    r_i = lax.broadcasted_iota(jnp.int32, (BLOCK, 2 * BLOCK), 0)
    c_i = lax.broadcasted_iota(jnp.int32, (BLOCK, 2 * BLOCK), 1)
    band = (c_i >= r_i) & (c_i <= r_i + BLOCK)
    causal = (lax.broadcasted_iota(jnp.int32, (BLOCK, BLOCK), 1)
              <= lax.broadcasted_iota(jnp.int32, (BLOCK, BLOCK), 0))
    kb[0:BLOCK, :] = jnp.zeros((BLOCK, GW), BF16)
    vb[0:BLOCK, :] = jnp.zeros((BLOCK, GW), BF16)

    for d in DILATIONS:
        sub = seq // d
        for res in range(d):
            src = pl.ds(res, sub, stride=d) if d > 1 else slice(0, seq)
            for half in range(HALVES):
                sl = slice(half * LANES, (half + 1) * LANES)
                qb[res * sub:(res + 1) * sub, sl] = qf[half, src, :].astype(BF16)
                kb[BLOCK + res * sub:BLOCK + (res + 1) * sub, sl] = kf[half, src, :].astype(BF16)
                vb[BLOCK + res * sub:BLOCK + (res + 1) * sub, sl] = vf[half, src, :].astype(BF16)
        blocks_per_class = sub // BLOCK

        def block(g, carry, d=d, blocks_per_class=blocks_per_class):
            r0 = pl.multiple_of(g * BLOCK, BLOCK)
            q = qb[pl.ds(r0, BLOCK), :]
            if blocks_per_class == 1:
                o, lse = _band_block(q, kb[pl.ds(r0 + BLOCK, BLOCK), :],
                                     vb[pl.ds(r0 + BLOCK, BLOCK), :], causal)
            else:
                first_key = jnp.where(g % blocks_per_class > 0, 0, BLOCK)
                o, lse = _band_block(q, kb[pl.ds(r0, 2 * BLOCK), :], vb[pl.ds(r0, 2 * BLOCK), :],
                                     band & (c_i >= first_key))
            o_dst, l_dst = (o_acc, l_acc) if d == 1 else (o_br, l_br)
            for half in range(HALVES):
                sl = slice(half * LANES, (half + 1) * LANES)
                o_dst[half, pl.ds(r0, BLOCK), :] = o[:, sl]
                l_dst[half, pl.ds(r0, BLOCK), :] = lse[:, sl]
            return carry

        lax.fori_loop(0, seq // BLOCK, block, 0)

        if d > 1:
            for res in range(d):
                nat = pl.ds(res, sub, stride=d)
                cls = slice(res * sub, (res + 1) * sub)
                for half in range(HALVES):
                    la, lb = l_acc[half, nat, :], l_br[half, cls, :]
                    ln = jnp.maximum(la, lb)
                    ln = ln + jnp.log(jnp.exp(la - ln) + jnp.exp(lb - ln))
                    o_acc[half, nat, :] = (o_acc[half, nat, :] * jnp.exp(la - ln)
                                           + o_br[half, cls, :] * jnp.exp(lb - ln))
                    l_acc[half, nat, :] = ln

    for half in range(HALVES):
        y_ref[:, half * LANES:(half + 1) * LANES] = o_acc[half].astype(BF16)


def _dilated(zc, cosd, sind, bsz, seq):
    tab = pl.BlockSpec((1, seq, LANES), lambda b: (b, 0, 0))
    f32buf = pltpu.VMEM((HALVES, seq, LANES), F32)
    return pl.pallas_call(
        _dilated_kernel,
        grid=(bsz,),
        in_specs=[pl.BlockSpec((seq, ZC_W), lambda b: (b, 0)), tab, tab],
        out_specs=pl.BlockSpec((seq, GW), lambda b: (b, 0)),
        out_shape=jax.ShapeDtypeStruct((bsz * seq, GW), BF16),
        scratch_shapes=[f32buf, f32buf, f32buf,
                        pltpu.VMEM((seq, GW), BF16), pltpu.VMEM((seq + BLOCK, GW), BF16),
                        pltpu.VMEM((seq + BLOCK, GW), BF16),
                        f32buf, f32buf, f32buf, f32buf],
        compiler_params=_params(),
        name="dilated",
    )(zc, cosd, sind)


CONV_W = 4
CONV_PAD = 8


def _log_sigmoid(x):
    return jnp.minimum(x, 0.0) - jnp.log1p(jnp.exp(-jnp.abs(x)))


def _mlstm_kernel(zd_ref, zm_ref, gb_ref, cw_ref, cb_ref, y_ref, xpad, a_r, b_r, cm_c, b_c, c_s, n_s):
    seq = zd_ref.shape[0]
    xpad[0:CONV_PAD, :] = jnp.zeros((CONV_PAD, 2 * GW), F32)
    for r in range(0, seq, 512):
        xpad[CONV_PAD + r:CONV_PAD + r + 512, :] = zd_ref[r:r + 512, 0:2 * GW].astype(F32)

    gt = (zm_ref[...] + gb_ref[...]).T[0:8, :]
    f_log = _log_sigmoid(pltpu.roll(gt, HEADS, 0))
    pos_in_chunk = lax.broadcasted_iota(jnp.int32, (8, seq), 1) % BLOCK
    steps = [1 << s for s in range(BLOCK.bit_length() - 1)]
    b = f_log
    for k in steps:
        b = b + jnp.where(pos_in_chunk >= k, pltpu.roll(b, k, 1), 0.0)
    a = gt - b
    cm = a
    for k in steps:
        cm = jnp.maximum(cm, jnp.where(pos_in_chunk >= k, pltpu.roll(cm, k, 1), NEG_INF))
    a_r[...] = a
    b_r[...] = b
    pad = jnp.zeros((LANES - 8, seq), F32)
    cm_c[...] = jnp.concatenate([cm, pad], axis=0).T
    b_c[...] = jnp.concatenate([b, pad], axis=0).T

    c_s[...] = jnp.zeros((GW, GW), F32)
    n_s[...] = jnp.zeros((GW, GW), F32)
    lane_head = lax.broadcasted_iota(jnp.int32, (1, GW), 1) // HEAD_DIM
    same_head = (lax.broadcasted_iota(jnp.int32, (GW, GW), 0) // HEAD_DIM
                 == lax.broadcasted_iota(jnp.int32, (GW, GW), 1) // HEAD_DIM)
    causal = (lax.broadcasted_iota(jnp.int32, (BLOCK, BLOCK), 1)
              <= lax.broadcasted_iota(jnp.int32, (BLOCK, BLOCK), 0))

    def chunk(c, m_run):
        r0 = pl.multiple_of(c * BLOCK, BLOCK)
        win = xpad[pl.ds(r0, BLOCK + CONV_PAD), :]
        acc = cb_ref[...] + cw_ref[CONV_W - 1:CONV_W, :] * win[CONV_PAD:, :]
        for k in range(1, CONV_W):
            acc = acc + cw_ref[CONV_W - 1 - k:CONV_W - k, :] * pltpu.roll(win, k, 0)[CONV_PAD:, :]
        qk = acc * jax.nn.sigmoid(acc)
        q = qk[:, :GW]
        k = qk[:, GW:] * (HEAD_DIM ** -0.5)
        qb, kb = q.astype(BF16), k.astype(BF16)
        v = zd_ref[pl.ds(r0, BLOCK), 2 * GW:3 * GW]
        og = zd_ref[pl.ds(r0, BLOCK), 3 * GW:4 * GW].astype(F32)
        a_rows = a_r[:, pl.ds(r0, BLOCK)]
        b_rows = b_r[:, pl.ds(r0, BLOCK)]
        cm_cols = cm_c[pl.ds(r0, BLOCK), :]
        b_cols = b_c[pl.ds(r0, BLOCK), :]

        num = jnp.zeros((BLOCK, GW), F32)
        den = jnp.zeros((BLOCK, GW), F32)
        inter_scale = jnp.zeros((BLOCK, GW), F32)
        floor = jnp.zeros((BLOCK, GW), F32)
        wk_rows, decays, m_next = [], [], []
        for h in range(HEADS):
            hm = lane_head == h
            a_row = a_rows[h:h + 1, :]
            g_col = jnp.maximum(cm_cols[:, h:h + 1], m_run[h])
            w_intra = jnp.where(causal, jnp.exp(a_row - g_col), 0.0)
            sqk = _dot_nt(jnp.where(hm, qb, jnp.zeros_like(qb)), kb) * w_intra
            num = jnp.where(hm, _dot(sqk.astype(BF16), v), num)
            den = jnp.where(hm, jnp.sum(sqk, axis=-1, keepdims=True), den)
            inter_scale = jnp.where(hm, jnp.exp(m_run[h] - g_col), inter_scale)
            floor = jnp.where(hm, jnp.exp(-(b_cols[:, h:h + 1] + g_col)), floor)
            g_end = jnp.maximum(m_run[h], jnp.max(a_row, axis=-1, keepdims=True))
            m_next.append(b_rows[h:h + 1, BLOCK - 1:BLOCK] + g_end)
            decays.append(jnp.exp(m_run[h] - g_end))
            wk_rows.append(jnp.exp(a_row - g_end))

        num = num + inter_scale * _dot(qb, c_s[...].astype(BF16))
        den = den + inter_scale * _dot(qb, n_s[...].astype(BF16))
        h_out = num / jnp.maximum(jnp.abs(den), floor)
        y_ref[pl.ds(r0, BLOCK), :] = (jax.nn.sigmoid(og) * h_out).astype(BF16)

        wk = jnp.concatenate([jnp.broadcast_to(w, (HEAD_DIM, BLOCK)) for w in wk_rows], axis=0)
        decay = jnp.concatenate([jnp.broadcast_to(dd, (HEAD_DIM, 1)) for dd in decays], axis=0)
        ktw = k.T * wk
        c_s[...] = decay * c_s[...] + jnp.where(same_head, _dot(ktw.astype(BF16), v), 0.0)
        n_s[...] = decay * n_s[...] + jnp.where(same_head, jnp.sum(ktw, axis=-1, keepdims=True), 0.0)
        return tuple(m_next)

    lax.fori_loop(0, seq // BLOCK, chunk, tuple(jnp.zeros((1, 1), F32) for _ in range(HEADS)))


def _mlstm(zd, zm, gb, cw, cb, bsz, seq):
    full = lambda a: pl.BlockSpec(a.shape, lambda b: (0,) * a.ndim)
    return pl.pallas_call(
        _mlstm_kernel,
        grid=(bsz,),
        in_specs=[pl.BlockSpec((seq, ZD_W), lambda b: (b, 0)),
                  pl.BlockSpec((seq, LANES), lambda b: (b, 0)), full(gb), full(cw), full(cb)],
        out_specs=pl.BlockSpec((seq, GW), lambda b: (b, 0)),
        out_shape=jax.ShapeDtypeStruct((bsz * seq, GW), BF16),
        scratch_shapes=[pltpu.VMEM((seq + CONV_PAD, 2 * GW), F32),
                        pltpu.VMEM((8, seq), F32), pltpu.VMEM((8, seq), F32),
                        pltpu.VMEM((seq, LANES), F32), pltpu.VMEM((seq, LANES), F32),
                        pltpu.VMEM((GW, GW), F32), pltpu.VMEM((GW, GW), F32)],
        compiler_params=_params(),
        name="mlstm",
    )(zd, zm, gb, cw, cb)


def _outproj_kernel(ya_ref, yb_ref, yc_ref, yd_ref, x_ref, w_ref, g_ref, b_ref, o_ref):
    acc = ALPHA * x_ref[...]
    for i, y_ref in enumerate((ya_ref, yb_ref, yc_ref, yd_ref)):
        acc = acc + _dot(y_ref[...], w_ref[i * GW:(i + 1) * GW, :])
    o_ref[...] = _layer_norm(acc, g_ref[...], b_ref[...])


def _outproj(ys, x2, w, g, b, tm):
    n = x2.shape[0]
    row = lambda w_: pl.BlockSpec((tm, w_), lambda i: (i, 0))
    full = lambda a: pl.BlockSpec(a.shape, lambda i: (0, 0))
    return pl.pallas_call(
        _outproj_kernel,
        grid=(n // tm,),
        in_specs=[row(GW)] * 4 + [row(D_MODEL), full(w), full(g), full(b)],
        out_specs=row(D_MODEL),
        out_shape=jax.ShapeDtypeStruct((n, D_MODEL), F32),
        compiler_params=_params(),
        name="outproj_ln",
    )(*ys, x2, w, g, b)


FF_CHUNK = 256


def _ffn_kernel(x_ref, wg_ref, wu_ref, wd_ref, g_ref, b_ref, o_ref, xb_ref, acc_ref):
    xb_ref[...] = x_ref[...].astype(BF16)
    acc_ref[...] = ALPHA * x_ref[...]
    for c in range(0, D_FF, FF_CHUNK):
        gate = _dot(xb_ref[...], wg_ref[:, c:c + FF_CHUNK])
        up = _dot(xb_ref[...], wu_ref[:, c:c + FF_CHUNK])
        act = (gate * jax.nn.sigmoid(gate) * up).astype(BF16)
        acc_ref[...] += _dot(act, wd_ref[c:c + FF_CHUNK, :])
    o_ref[...] = _layer_norm(acc_ref[...], g_ref[...], b_ref[...])


def _ffn(x2, wg, wu, wd, g, b, tm):
    n = x2.shape[0]
    row = pl.BlockSpec((tm, D_MODEL), lambda i: (i, 0))
    once = lambda a: pl.BlockSpec(a.shape, lambda i: (0, 0), pipeline_mode=pl.Buffered(1))
    return pl.pallas_call(
        _ffn_kernel,
        grid=(n // tm,),
        in_specs=[row, once(wg), once(wu), once(wd), once(g), once(b)],
        out_specs=row,
        out_shape=jax.ShapeDtypeStruct((n, D_MODEL), F32),
        scratch_shapes=[pltpu.VMEM((tm, D_MODEL), BF16), pltpu.VMEM((tm, D_MODEL), F32)],
        compiler_params=_params(),
        name="swiglu_ln",
    )(x2, wg, wu, wd, g, b)


def _pack_weights(w_in, b_w_uq, b_w_ukv, b_q_norm, a_bs, d_igate_b, d_fgate_b):
    nl = w_in.shape[0]
    o_b = ZA_W
    o_c = o_b + Q_RANK + KV_RANK + ROPE
    o_d = o_c + ZC_W
    o_g = o_d + ZD_W
    w_a = w_in[..., :o_b]
    w_cq = w_in[..., o_b:o_b + Q_RANK]
    w_ckv = w_in[..., o_b + Q_RANK:o_b + Q_RANK + KV_RANK]
    w_kr = w_in[..., o_b + Q_RANK + KV_RANK:o_c]
    w_c = w_in[..., o_c:o_d]
    w_d = w_in[..., o_d:o_g]
    w_gates = w_in[..., o_g:]
    zeros = lambda *s: jnp.zeros((nl,) + s, F32)
    w_b = jnp.concatenate([w_cq, zeros(D_MODEL, 256 - Q_RANK), w_ckv], -1)
    w_main = jnp.concatenate([w_d, w_a, w_c, w_b], -1).astype(BF16)
    half = ROPE // 2
    w_kr_rot = jnp.concatenate([-w_kr[..., half:], w_kr[..., :half]], -1)
    w_misc = jnp.concatenate(
        [w_gates, zeros(D_MODEL, KR_LANE - 2 * HEADS), w_kr, w_kr_rot], -1).astype(BF16)

    wq = b_w_uq.reshape(nl, Q_RANK, HEADS, NOPE + ROPE)
    nope, x1, x2 = wq[..., :NOPE], wq[..., NOPE:NOPE + half], wq[..., NOPE + half:]
    tail = zeros(Q_RANK, HEADS, LANES - NOPE - ROPE)
    rowpad = ((0, 0), (0, 256 - Q_RANK), (0, 0))
    wqm = jnp.pad(jnp.concatenate([nope, x1, x2, tail], -1).reshape(nl, Q_RANK, HEADS * LANES), rowpad)
    wqs = jnp.pad(jnp.concatenate([zeros(Q_RANK, HEADS, NOPE), -x2, x1, tail], -1)
                  .reshape(nl, Q_RANK, HEADS * LANES), rowpad)
    wkv = b_w_ukv.reshape(nl, KV_RANK, HEADS, NOPE + HEAD_DIM)
    wk = jnp.concatenate([wkv[..., :NOPE], zeros(KV_RANK, HEADS, LANES - NOPE)], -1)
    wk = wk.reshape(nl, KV_RANK, HEADS * LANES)
    wv = wkv[..., NOPE:].reshape(nl, KV_RANK, GW)
    qg = jnp.pad(b_q_norm, ((0, 0), (0, 256 - Q_RANK)))[:, None, :]
    bias = jnp.repeat(jnp.swapaxes(a_bs, 1, 2), HEAD_DIM, axis=-1)
    gb = jnp.concatenate([d_igate_b, d_fgate_b, zeros(LANES - 2 * HEADS)], -1)[:, None, :]
    return dict(w_main=w_main, w_misc=w_misc, wqm=wqm.astype(BF16), wqs=wqs.astype(BF16),
                wk=wk.astype(BF16), wv=wv.astype(BF16), qg=qg, bias=bias, gb=gb)


@jax.jit
def _forward(x, positions, w_in, a_ln_g, a_ln_b, a_ws, a_bs, b_q_norm, b_kv_norm, b_w_uq, b_w_ukv,
             d_conv_w, d_conv_b, d_igate_b, d_fgate_b, w_out, ln1_g, ln1_b, w_gate, w_up, w_down,
             ln2_g, ln2_b):
    bsz, seq, _ = x.shape
    pk = _pack_weights(w_in, b_w_uq, b_w_ukv, b_q_norm, a_bs, d_igate_b, d_fgate_b)
    w_out_b, w_gate_b, w_up_b, w_down_b = (w.astype(BF16) for w in (w_out, w_gate, w_up, w_down))
    cosd, sind, cosm, sinm = _rope_tables(positions)
    tm = 512
    x2 = x.reshape(bsz * seq, D_MODEL)
    row = lambda a, l: a[l][None, :]
    for l in range(DEPTH):
        zd, za, zc, zb, zm = _inproj(x2, pk["w_main"][l], pk["w_misc"][l], tm)
        ya = _gmlp(za, row(a_ln_g, l), row(a_ln_b, l), a_ws[l], pk["bias"][l], bsz, seq)
        yb = _mla(zb, zm, cosm, sinm, pk["qg"][l], row(b_kv_norm, l), pk["wqm"][l], pk["wqs"][l],
                  pk["wk"][l], pk["wv"][l], bsz, seq)
        yc = _dilated(zc, cosd, sind, bsz, seq)
        yd = _mlstm(zd, zm, pk["gb"][l], d_conv_w[l], row(d_conv_b, l), bsz, seq)
        x2 = _outproj((ya, yb, yc, yd), x2, w_out_b[l], row(ln1_g, l), row(ln1_b, l), tm)
        x2 = _ffn(x2, w_gate_b[l], w_up_b[l], w_down_b[l], row(ln2_g, l), row(ln2_b, l), tm)
    return x2.reshape(bsz, seq, D_MODEL)


def kernel(x, positions, w_in, a_ln_g, a_ln_b, a_ws, a_bs, b_q_norm, b_kv_norm, b_w_uq, b_w_ukv,
           d_conv_w, d_conv_b, d_igate_b, d_fgate_b, w_out, ln1_g, ln1_b, w_gate, w_up, w_down,
           ln2_g, ln2_b):
    return _forward(x, positions, w_in, a_ln_g, a_ln_b, a_ws, a_bs, b_q_norm, b_kv_norm, b_w_uq,
                    b_w_ukv, d_conv_w, d_conv_b, d_igate_b, d_fgate_b, w_out, ln1_g, ln1_b, w_gate,
                    w_up, w_down, ln2_g, ln2_b)
```

```python
import functools

import jax
import jax.numpy as jnp
from jax import lax
from jax.experimental import pallas as pl
from jax.experimental.pallas import tpu as pltpu

F32 = jnp.float32
BF16 = jnp.bfloat16

D_MODEL = 1024
DEPTH = 4
HEAD_DIM = 64
HEADS = 4
GW = HEADS * HEAD_DIM
BLOCK = 128
Q_RANK = 192
KV_RANK = 128
NOPE = 64
ROPE = 32
D_FF = 2816
ROPE_THETA = 10000.0
LN_EPS = 1e-5
RMS_EPS = 1e-6
ALPHA = (2 * DEPTH) ** 0.25
LANES = 128
NEG_INF = float("-inf")

ZD_W, ZA_W, ZC_W, ZB_W = 4 * GW, 2 * GW, 3 * GW, 384
ZMAIN_W = ZD_W + ZA_W + ZC_W + ZB_W
KR_LANE = 64

VMEM_LIMIT = 56 * 1024 * 1024


def _dot(a, b):
    return jnp.dot(a, b, preferred_element_type=F32)


def _dot_nt(a, b):
    return lax.dot_general(a, b, (((1,), (1,)), ((), ())), preferred_element_type=F32)


def _params(n_axes=1):
    return pltpu.CompilerParams(
        dimension_semantics=("arbitrary",) * n_axes, vmem_limit_bytes=VMEM_LIMIT)


def _layer_norm(r, g, b):
    mu = jnp.mean(r, axis=-1, keepdims=True)
    d = r - mu
    var = jnp.mean(d * d, axis=-1, keepdims=True)
    return d * lax.rsqrt(var + LN_EPS) * g + b


def _tables_kernel(pos_ref, invd_ref, invm_ref, cosd_ref, sind_ref, cosm_ref, sinm_ref):
    pos = pos_ref[0].astype(F32)
    lane = lax.broadcasted_iota(jnp.int32, (1, LANES), 1)
    angd = pos * invd_ref[...]
    cosd_ref[0] = jnp.cos(angd)
    sind_ref[0] = jnp.sin(angd) * jnp.where((lane % HEAD_DIM) < HEAD_DIM // 2, -1.0, 1.0)
    angm = pos * invm_ref[...]
    in_rope = (lane >= KR_LANE) & (lane < KR_LANE + ROPE)
    cosm_ref[0] = jnp.where(in_rope, jnp.cos(angm), jnp.where(lane < KR_LANE, 1.0, 0.0))
    sinm_ref[0] = jnp.where(in_rope, jnp.sin(angm), 0.0)


def _rope_tables(positions):
    bsz, seq = positions.shape
    lane = jnp.arange(LANES)
    half_d = HEAD_DIM // 2
    inv_d = jnp.power(ROPE_THETA, -jnp.arange(half_d, dtype=F32) / half_d)
    half_m = ROPE // 2
    inv_m = jnp.power(ROPE_THETA, -jnp.arange(half_m, dtype=F32) / half_m)
    invd = inv_d[lane % half_d][None, :]
    invm = inv_m[lane % half_m][None, :]
    tab = jax.ShapeDtypeStruct((bsz, seq, LANES), F32)
    spec = pl.BlockSpec((1, seq, LANES), lambda b: (b, 0, 0))
    cspec = pl.BlockSpec((1, LANES), lambda b: (0, 0))
    return pl.pallas_call(
        _tables_kernel,
        grid=(bsz,),
        in_specs=[pl.BlockSpec((1, seq, 1), lambda b: (b, 0, 0)), cspec, cspec],
        out_specs=[spec] * 4,
        out_shape=[tab] * 4,
        compiler_params=_params(),
        name="rope_tables",
    )(positions.reshape(bsz, seq, 1), invd, invm)


def _inproj_kernel(x_ref, w_ref, wm_ref, zd_ref, za_ref, zc_ref, zb_ref, zm_ref, xb_ref):
    xb_ref[...] = x_ref[...].astype(BF16)
    off = 0
    for ref in (zd_ref, za_ref, zc_ref, zb_ref):
        width = ref.shape[1]
        for c in range(0, width, 256):
            cw = min(256, width - c)
            ref[:, c:c + cw] = _dot(xb_ref[...], w_ref[:, off + c:off + c + cw]).astype(BF16)
        off += width
    zm_ref[...] = _dot(xb_ref[...], wm_ref[...])


def _inproj(x2, w_main, w_misc, tm):
    n = x2.shape[0]
    row = lambda w: pl.BlockSpec((tm, w), lambda i: (i, 0))
    full = lambda a: pl.BlockSpec(a.shape, lambda i: (0, 0))
    return pl.pallas_call(
        _inproj_kernel,
        grid=(n // tm,),
        in_specs=[row(D_MODEL), full(w_main), full(w_misc)],
        out_specs=[row(ZD_W), row(ZA_W), row(ZC_W), row(ZB_W), row(LANES)],
        out_shape=[jax.ShapeDtypeStruct((n, ZD_W), BF16), jax.ShapeDtypeStruct((n, ZA_W), BF16),
                   jax.ShapeDtypeStruct((n, ZC_W), BF16), jax.ShapeDtypeStruct((n, ZB_W), BF16),
                   jax.ShapeDtypeStruct((n, LANES), F32)],
        scratch_shapes=[pltpu.VMEM((tm, D_MODEL), BF16)],
        compiler_params=_params(),
        name="inproj",
    )(x2, w_main, w_misc)


def _gmlp_kernel(za_ref, lng_ref, lnb_ref, ws_ref, bias_ref, y_ref):
    seq = za_ref.shape[0]
    r_i = lax.broadcasted_iota(jnp.int32, (BLOCK, BLOCK), 0)
    c_i = lax.broadcasted_iota(jnp.int32, (BLOCK, BLOCK), 1)
    w_causal = [jnp.where(c_i <= r_i, ws_ref[h], 0.0).astype(BF16) for h in range(HEADS)]
    lane_head = lax.broadcasted_iota(jnp.int32, (BLOCK, GW), 1) // HEAD_DIM

    def chunk(c, carry):
        r0 = pl.multiple_of(c * BLOCK, BLOCK)
        z = za_ref[pl.ds(r0, BLOCK), :].astype(F32)
        g = 0.5 * z * (1.0 + lax.erf(z * (0.5 ** 0.5)))
        u, v = g[:, :GW], g[:, GW:]
        vb = _layer_norm(v, lng_ref[...], lnb_ref[...]).astype(BF16)
        mixed = jnp.zeros((BLOCK, GW), F32)
        for h in range(HEADS):
            mixed = jnp.where(lane_head == h, _dot(w_causal[h], vb), mixed)
        y_ref[pl.ds(r0, BLOCK), :] = (u * (mixed + bias_ref[...])).astype(BF16)
        return carry

    lax.fori_loop(0, seq // BLOCK, chunk, 0)


def _gmlp(za, ln_g, ln_b, ws, bias, bsz, seq):
    full = lambda a: pl.BlockSpec(a.shape, lambda b: (0,) * a.ndim)
    return pl.pallas_call(
        _gmlp_kernel,
        grid=(bsz,),
        in_specs=[pl.BlockSpec((seq, ZA_W), lambda b: (b, 0)), full(ln_g), full(ln_b), full(ws),
                  full(bias)],
        out_specs=pl.BlockSpec((seq, GW), lambda b: (b, 0)),
        out_shape=jax.ShapeDtypeStruct((bsz * seq, GW), BF16),
        compiler_params=_params(),
        name="gmlp",
    )(za, ln_g, ln_b, ws, bias)


MLA_TQ = 256


def _mla_kernel(zb_ref, zm_ref, cos_ref, sin_ref, qg_ref, kvg_ref, wqm_ref, wqs_ref, wk_ref, wv_ref,
                y_ref, q_s, k_s, vt_s, acc_s):
    seq = zb_ref.shape[0]
    tq = MLA_TQ
    scale = (NOPE + ROPE) ** -0.5
    lane = lax.broadcasted_iota(jnp.int32, (1, LANES), 1)
    in_rope = (lane >= KR_LANE) & (lane < KR_LANE + ROPE)
    rows = 512
    for r in range(0, seq, rows):
        cos = cos_ref[0, r:r + rows, :]
        sin = sin_ref[0, r:r + rows, :]
        cq = zb_ref[r:r + rows, 0:256].astype(F32)
        ms = jnp.sum(cq * cq, axis=-1, keepdims=True) * (1.0 / Q_RANK)
        cqn = (cq * lax.rsqrt(ms + RMS_EPS) * qg_ref[...]).astype(BF16)
        qm = _dot(cqn, wqm_ref[...])
        qs = _dot(cqn, wqs_ref[...])
        ckv = zb_ref[r:r + rows, 256:384].astype(F32)
        ms = jnp.mean(ckv * ckv, axis=-1, keepdims=True)
        ckvn = (ckv * lax.rsqrt(ms + RMS_EPS) * kvg_ref[...]).astype(BF16)
        kn = _dot(ckvn, wk_ref[...])
        v = _dot(ckvn, wv_ref[...])
        for sub in range(rows // tq):
            vt_s[r // tq + sub] = v[sub * tq:(sub + 1) * tq, :].T.astype(BF16)
        zm = zm_ref[r:r + rows, :]
        kr = jnp.where(in_rope, zm * cos + pltpu.roll(zm, LANES - ROPE, 1) * sin, 0.0)
        for h in range(HEADS):
            sl = slice(h * LANES, (h + 1) * LANES)
            q_s[r:r + rows, sl] = ((qm[:, sl] * cos + qs[:, sl] * sin) * scale).astype(BF16)
            k_s[r:r + rows, sl] = (kn[:, sl] + kr).astype(BF16)

    key_i = lax.broadcasted_iota(jnp.int32, (tq, tq), 0)
    qry_i = lax.broadcasted_iota(jnp.int32, (tq, tq), 1)
    heads = range(HEADS)
    head_lanes = [slice(h * LANES, (h + 1) * LANES) for h in heads]

    def qblock(i, carry):
        q0 = pl.multiple_of(i * tq, tq)
        acc_s[...] = jnp.zeros(acc_s.shape, F32)

        def step(j, state, masked):
            m_old, l_old = state
            k0 = pl.multiple_of(j * tq, tq)
            scores = [_dot_nt(k_s[pl.ds(k0, tq), head_lanes[h]], q_s[pl.ds(q0, tq), head_lanes[h]])
                      for h in heads]
            m_new, l_new, alpha, probs = [], [], [], []
            for h in heads:
                s = jnp.where(key_i <= qry_i, scores[h], NEG_INF) if masked else scores[h]
                m = jnp.maximum(m_old[h], jnp.max(s, axis=0, keepdims=True))
                a = jnp.exp(m_old[h] - m)
                p = jnp.exp(s - m)
                m_new.append(m)
                alpha.append(a)
                l_new.append(a * l_old[h] + jnp.sum(p, axis=0, keepdims=True))
                probs.append(p.astype(BF16))
            for h in heads:
                vt = vt_s[j, h * HEAD_DIM:(h + 1) * HEAD_DIM, :]
                acc_s[h] = alpha[h] * acc_s[h] + _dot(vt, probs[h])
            return tuple(m_new), tuple(l_new)

        init = (tuple(jnp.full((1, tq), NEG_INF, F32) for _ in heads),
                tuple(jnp.zeros((1, tq), F32) for _ in heads))
        state = lax.fori_loop(0, i, lambda j, st: step(j, st, False), init)
        _, l_fin = step(i, state, True)
        out_t = jnp.concatenate([acc_s[h] / l_fin[h] for h in heads], axis=0)
        y_ref[pl.ds(q0, tq), :] = out_t.T.astype(BF16)
        return carry

    lax.fori_loop(0, seq // tq, qblock, 0)


def _mla(zb, zm, cosm, sinm, qg, kvg, wqm, wqs, wk, wv, bsz, seq):
    full = lambda a: pl.BlockSpec(a.shape, lambda b: (0,) * a.ndim)
    tab = pl.BlockSpec((1, seq, LANES), lambda b: (b, 0, 0))
    return pl.pallas_call(
        _mla_kernel,
        grid=(bsz,),
        in_specs=[pl.BlockSpec((seq, ZB_W), lambda b: (b, 0)),
                  pl.BlockSpec((seq, LANES), lambda b: (b, 0)), tab, tab,
                  full(qg), full(kvg), full(wqm), full(wqs), full(wk), full(wv)],
        out_specs=pl.BlockSpec((seq, GW), lambda b: (b, 0)),
        out_shape=jax.ShapeDtypeStruct((bsz * seq, GW), BF16),
        scratch_shapes=[pltpu.VMEM((seq, HEADS * LANES), BF16), pltpu.VMEM((seq, HEADS * LANES), BF16),
                        pltpu.VMEM((seq // MLA_TQ, GW, MLA_TQ), BF16),
                        pltpu.VMEM((HEADS, HEAD_DIM, MLA_TQ), F32)],
        compiler_params=_params(),
        name="mla",
    )(zb, zm, cosm, sinm, qg, kvg, wqm, wqs, wk, wv)


DILATIONS = (1, 4, 16)
HALVES = GW // LANES


def _band_block(qb, kw, vw, mask):
    lane_head = lax.broadcasted_iota(jnp.int32, (BLOCK, GW), 1) // HEAD_DIM
    nk = kw.shape[0]
    q_heads = jnp.concatenate(
        [jnp.where(lane_head == h, qb, jnp.zeros_like(qb)) for h in range(HEADS)], axis=0)
    s = _dot_nt(q_heads, kw).reshape(HEADS, BLOCK, nk)
    s = jnp.where(mask[None], s, NEG_INF)
    m = jnp.max(s, axis=-1, keepdims=True)
    p = jnp.exp(s - m)
    l = jnp.sum(p, axis=-1, keepdims=True)
    pv = _dot(p.astype(BF16).reshape(HEADS * BLOCK, nk), vw).reshape(HEADS, BLOCK, GW) / l
    lse_h = m + jnp.log(l)
    o = jnp.zeros((BLOCK, GW), F32)
    lse = jnp.zeros((BLOCK, GW), F32)
    for h in range(HEADS):
        o = jnp.where(lane_head == h, pv[h], o)
        lse = jnp.where(lane_head == h, lse_h[h], lse)
    return o, lse


def _dilated_kernel(zc_ref, cos_ref, sin_ref, y_ref, qf, kf, vf, qb, kb, vb, o_acc, l_acc, o_br, l_br):
    seq = zc_ref.shape[0]
    lane = lax.broadcasted_iota(jnp.int32, (1, LANES), 1)
    first = (lane % HEAD_DIM) < HEAD_DIM // 2
    rows = 512
    for r in range(0, seq, rows):
        cos = cos_ref[0, r:r + rows, :]
        sin = sin_ref[0, r:r + rows, :]

        def rope(x):
            rot = jnp.where(first, pltpu.roll(x, LANES - HEAD_DIM // 2, 1),
                            pltpu.roll(x, HEAD_DIM // 2, 1))
            return x * cos + rot * sin

        for half in range(HALVES):
            col = lambda base: slice(base + half * LANES, base + (half + 1) * LANES)
            qf[half, r:r + rows, :] = rope(zc_ref[r:r + rows, col(0)].astype(F32)) * (HEAD_DIM ** -0.5)
            kf[half, r:r + rows, :] = rope(zc_ref[r:r + rows, col(GW)].astype(F32))
            vf[half, r:r + rows, :] = zc_ref[r:r + rows, col(2 * GW)].astype(F32)

    r_i = lax.broadcasted_iota(jnp.int32, (BLOCK, 2 * BLOCK), 0)
    c_i = lax.broadcasted_iota(jnp.int32, (BLOCK, 2 * BLOCK), 1)
    band = (c_i >= r_i) & (c_i <= r_i + BLOCK)
    causal = (lax.broadcasted_iota(jnp.int32, (BLOCK, BLOCK), 1)
              <= lax.broadcasted_iota(jnp.int32, (BLOCK, BLOCK), 0))
    kb[0:BLOCK, :] = jnp.zeros((BLOCK, GW), BF16)
    vb[0:BLOCK, :] = jnp.zeros((BLOCK, GW), BF16)

    for d in DILATIONS:
        sub = seq // d
        for res in range(d):
            src = pl.ds(res, sub, stride=d) if d > 1 else slice(0, seq)
            for half in range(HALVES):
                sl = slice(half * LANES, (half + 1) * LANES)
                qb[res * sub:(res + 1) * sub, sl] = qf[half, src, :].astype(BF16)
                kb[BLOCK + res * sub:BLOCK + (res + 1) * sub, sl] = kf[half, src, :].astype(BF16)
                vb[BLOCK + res * sub:BLOCK + (res + 1) * sub, sl] = vf[half, src, :].astype(BF16)
        blocks_per_class = sub // BLOCK

        def block(g, carry, d=d, blocks_per_class=blocks_per_class):
            r0 = pl.multiple_of(g * BLOCK, BLOCK)
            q = qb[pl.ds(r0, BLOCK), :]
            if blocks_per_class == 1:
                o, lse = _band_block(q, kb[pl.ds(r0 + BLOCK, BLOCK), :],
                                     vb[pl.ds(r0 + BLOCK, BLOCK), :], causal)
            else:
                first_key = jnp.where(g % blocks_per_class > 0, 0, BLOCK)
                o, lse = _band_block(q, kb[pl.ds(r0, 2 * BLOCK), :], vb[pl.ds(r0, 2 * BLOCK), :],
                                     band & (c_i >= first_key))
            o_dst, l_dst = (o_acc, l_acc) if d == 1 else (o_br, l_br)
            for half in range(HALVES):
                sl = slice(half * LANES, (half + 1) * LANES)
                o_dst[half, pl.ds(r0, BLOCK), :] = o[:, sl]
                l_dst[half, pl.ds(r0, BLOCK), :] = lse[:, sl]
            return carry

        lax.fori_loop(0, seq // BLOCK, block, 0, unroll=2)

        if d > 1:
            for res in range(d):
                nat = pl.ds(res, sub, stride=d)
                cls = slice(res * sub, (res + 1) * sub)
                for half in range(HALVES):
                    la, lb = l_acc[half, nat, :], l_br[half, cls, :]
                    ln = jnp.maximum(la, lb)
                    ln = ln + jnp.log(jnp.exp(la - ln) + jnp.exp(lb - ln))
                    o_acc[half, nat, :] = (o_acc[half, nat, :] * jnp.exp(la - ln)
                                           + o_br[half, cls, :] * jnp.exp(lb - ln))
                    l_acc[half, nat, :] = ln

    for half in range(HALVES):
        y_ref[:, half * LANES:(half + 1) * LANES] = o_acc[half].astype(BF16)


def _dilated(zc, cosd, sind, bsz, seq):
    tab = pl.BlockSpec((1, seq, LANES), lambda b: (b, 0, 0))
    f32buf = pltpu.VMEM((HALVES, seq, LANES), F32)
    return pl.pallas_call(
        _dilated_kernel,
        grid=(bsz,),
        in_specs=[pl.BlockSpec((seq, ZC_W), lambda b: (b, 0)), tab, tab],
        out_specs=pl.BlockSpec((seq, GW), lambda b: (b, 0)),
        out_shape=jax.ShapeDtypeStruct((bsz * seq, GW), BF16),
        scratch_shapes=[f32buf, f32buf, f32buf,
                        pltpu.VMEM((seq, GW), BF16), pltpu.VMEM((seq + BLOCK, GW), BF16),
                        pltpu.VMEM((seq + BLOCK, GW), BF16),
                        f32buf, f32buf, f32buf, f32buf],
        compiler_params=_params(),
        name="dilated",
    )(zc, cosd, sind)


CONV_W = 4
CONV_PAD = 8


def _log_sigmoid(x):
    return jnp.minimum(x, 0.0) - jnp.log1p(jnp.exp(-jnp.abs(x)))


def _mlstm_kernel(zd_ref, zm_ref, gb_ref, cw_ref, cb_ref, y_ref, xpad, q_s, k_s, kt_s, a_r, b_r, cm_c,
                  b_c, c_s, n_s):
    seq = zd_ref.shape[0]
    rows = 512
    xpad[0:CONV_PAD, :] = jnp.zeros((CONV_PAD, 2 * GW), F32)
    for r in range(0, seq, rows):
        xpad[CONV_PAD + r:CONV_PAD + r + rows, :] = zd_ref[r:r + rows, 0:2 * GW].astype(F32)
    for r in range(0, seq, rows):
        win = xpad[r:r + rows + CONV_PAD, :]
        acc = cb_ref[...] + cw_ref[CONV_W - 1:CONV_W, :] * win[CONV_PAD:, :]
        for k in range(1, CONV_W):
            acc = acc + cw_ref[CONV_W - 1 - k:CONV_W - k, :] * pltpu.roll(win, k, 0)[CONV_PAD:, :]
        qk = acc * jax.nn.sigmoid(acc)
        k = qk[:, GW:] * (HEAD_DIM ** -0.5)
        q_s[r:r + rows, :] = qk[:, :GW].astype(BF16)
        k_s[r:r + rows, :] = k.astype(BF16)
        for sub in range(rows // BLOCK):
            kt_s[r // BLOCK + sub] = k[sub * BLOCK:(sub + 1) * BLOCK, :].T

    gt = (zm_ref[...] + gb_ref[...]).T[0:8, :]
    f_log = _log_sigmoid(pltpu.roll(gt, HEADS, 0))
    pos_in_chunk = lax.broadcasted_iota(jnp.int32, (8, seq), 1) % BLOCK
    steps = [1 << s for s in range(BLOCK.bit_length() - 1)]
    b = f_log
    for k in steps:
        b = b + jnp.where(pos_in_chunk >= k, pltpu.roll(b, k, 1), 0.0)
    a = gt - b
    cm = a
    for k in steps:
        cm = jnp.maximum(cm, jnp.where(pos_in_chunk >= k, pltpu.roll(cm, k, 1), NEG_INF))
    a_r[...] = a
    b_r[...] = b
    pad = jnp.zeros((LANES - 8, seq), F32)
    cm_c[...] = jnp.concatenate([cm, pad], axis=0).T
    b_c[...] = jnp.concatenate([b, pad], axis=0).T

    c_s[...] = jnp.zeros((GW, GW), F32)
    n_s[...] = jnp.zeros((GW, GW), F32)
    lane_head = lax.broadcasted_iota(jnp.int32, (1, GW), 1) // HEAD_DIM
    same_head = (lax.broadcasted_iota(jnp.int32, (GW, GW), 0) // HEAD_DIM
                 == lax.broadcasted_iota(jnp.int32, (GW, GW), 1) // HEAD_DIM)
    causal = (lax.broadcasted_iota(jnp.int32, (BLOCK, BLOCK), 1)
              <= lax.broadcasted_iota(jnp.int32, (BLOCK, BLOCK), 0))

    heads = range(HEADS)
    low_half = lax.broadcasted_iota(jnp.int32, (1, LANES), 1) < HEAD_DIM

    def head_columns(cols):
        return jnp.concatenate([jnp.where(low_half, cols[0], cols[1]),
                                jnp.where(low_half, cols[2], cols[3])], axis=1)

    def chunk(c, m_run):
        r0 = pl.multiple_of(c * BLOCK, BLOCK)
        qb = q_s[pl.ds(r0, BLOCK), :]
        kb = k_s[pl.ds(r0, BLOCK), :]
        v = zd_ref[pl.ds(r0, BLOCK), 2 * GW:3 * GW]
        og = zd_ref[pl.ds(r0, BLOCK), 3 * GW:4 * GW].astype(F32)
        a_rows = a_r[:, pl.ds(r0, BLOCK)]
        b_rows = b_r[:, pl.ds(r0, BLOCK)]
        cm_cols = cm_c[pl.ds(r0, BLOCK), :]
        b_cols = b_c[pl.ds(r0, BLOCK), :]

        k_heads = jnp.concatenate(
            [jnp.where(lane_head == h, kb, jnp.zeros_like(kb)) for h in heads], axis=0)
        v_heads = jnp.concatenate(
            [jnp.where(lane_head == h, v, jnp.zeros_like(v)) for h in heads], axis=0)
        scores = _dot_nt(qb, k_heads)
        carried = _dot(qb, jnp.concatenate([c_s[...], n_s[...]], axis=1).astype(BF16))

        w_intra, den, inter_scale, floor, wk_rows, decays, m_next = [], [], [], [], [], [], []
        for h in heads:
            a_row = a_rows[h:h + 1, :]
            g_col = jnp.maximum(cm_cols[:, h:h + 1], m_run[h])
            w_intra.append(jnp.where(causal, jnp.exp(a_row - g_col), 0.0))
            inter_scale.append(jnp.exp(m_run[h] - g_col))
            floor.append(jnp.exp(-(b_cols[:, h:h + 1] + g_col)))
            g_end = jnp.maximum(m_run[h], jnp.max(a_row, axis=-1, keepdims=True))
            m_next.append(b_rows[h:h + 1, BLOCK - 1:BLOCK] + g_end)
            decays.append(jnp.exp(m_run[h] - g_end))
            wk_rows.append(jnp.exp(a_row - g_end))

        wk = jnp.concatenate([jnp.broadcast_to(w, (HEAD_DIM, BLOCK)) for w in wk_rows], axis=0)
        decay = jnp.concatenate([jnp.broadcast_to(dd, (HEAD_DIM, 1)) for dd in decays], axis=0)
        ktw = kt_s[c] * wk
        update = _dot(ktw.astype(BF16), v)

        sqk = scores * jnp.concatenate(w_intra, axis=1)
        for h in heads:
            den.append(jnp.sum(sqk[:, h * BLOCK:(h + 1) * BLOCK], axis=-1, keepdims=True))
        inter_scale = head_columns(inter_scale)
        num = _dot(sqk.astype(BF16), v_heads) + inter_scale * carried[:, :GW]
        den = head_columns(den) + inter_scale * carried[:, GW:]
        h_out = num / jnp.maximum(jnp.abs(den), head_columns(floor))
        y_ref[pl.ds(r0, BLOCK), :] = (jax.nn.sigmoid(og) * h_out).astype(BF16)

        c_s[...] = decay * c_s[...] + jnp.where(same_head, update, 0.0)
        n_s[...] = decay * n_s[...] + jnp.where(same_head, jnp.sum(ktw, axis=-1, keepdims=True), 0.0)
        return tuple(m_next)

    lax.fori_loop(0, seq // BLOCK, chunk, tuple(jnp.zeros((1, 1), F32) for _ in range(HEADS)))


def _mlstm(zd, zm, gb, cw, cb, bsz, seq):
    full = lambda a: pl.BlockSpec(a.shape, lambda b: (0,) * a.ndim)
    return pl.pallas_call(
        _mlstm_kernel,
        grid=(bsz,),
        in_specs=[pl.BlockSpec((seq, ZD_W), lambda b: (b, 0)),
                  pl.BlockSpec((seq, LANES), lambda b: (b, 0)), full(gb), full(cw), full(cb)],
        out_specs=pl.BlockSpec((seq, GW), lambda b: (b, 0)),
        out_shape=jax.ShapeDtypeStruct((bsz * seq, GW), BF16),
        scratch_shapes=[pltpu.VMEM((seq + CONV_PAD, 2 * GW), F32),
                        pltpu.VMEM((seq, GW), BF16), pltpu.VMEM((seq, GW), BF16),
                        pltpu.VMEM((seq // BLOCK, GW, BLOCK), F32),
                        pltpu.VMEM((8, seq), F32), pltpu.VMEM((8, seq), F32),
                        pltpu.VMEM((seq, LANES), F32), pltpu.VMEM((seq, LANES), F32),
                        pltpu.VMEM((GW, GW), F32), pltpu.VMEM((GW, GW), F32)],
        compiler_params=_params(),
        name="mlstm",
    )(zd, zm, gb, cw, cb)


def _outproj_kernel(ya_ref, yb_ref, yc_ref, yd_ref, x_ref, w_ref, g_ref, b_ref, o_ref):
    acc = ALPHA * x_ref[...]
    for i, y_ref in enumerate((ya_ref, yb_ref, yc_ref, yd_ref)):
        acc = acc + _dot(y_ref[...], w_ref[i * GW:(i + 1) * GW, :])
    o_ref[...] = _layer_norm(acc, g_ref[...], b_ref[...])


def _outproj(ys, x2, w, g, b, tm):
    n = x2.shape[0]
    row = lambda w_: pl.BlockSpec((tm, w_), lambda i: (i, 0))
    full = lambda a: pl.BlockSpec(a.shape, lambda i: (0, 0))
    return pl.pallas_call(
        _outproj_kernel,
        grid=(n // tm,),
        in_specs=[row(GW)] * 4 + [row(D_MODEL), full(w), full(g), full(b)],
        out_specs=row(D_MODEL),
        out_shape=jax.ShapeDtypeStruct((n, D_MODEL), F32),
        compiler_params=_params(),
        name="outproj_ln",
    )(*ys, x2, w, g, b)


FF_CHUNK = 256


def _ffn_kernel(x_ref, wg_ref, wu_ref, wd_ref, g_ref, b_ref, o_ref, xb_ref, acc_ref):
    xb_ref[...] = x_ref[...].astype(BF16)
    acc_ref[...] = ALPHA * x_ref[...]
    for c in range(0, D_FF, FF_CHUNK):
        gate = _dot(xb_ref[...], wg_ref[:, c:c + FF_CHUNK])
        up = _dot(xb_ref[...], wu_ref[:, c:c + FF_CHUNK])
        act = (gate * jax.nn.sigmoid(gate) * up).astype(BF16)
        acc_ref[...] += _dot(act, wd_ref[c:c + FF_CHUNK, :])
    o_ref[...] = _layer_norm(acc_ref[...], g_ref[...], b_ref[...])


def _ffn(x2, wg, wu, wd, g, b, tm):
    n = x2.shape[0]
    row = pl.BlockSpec((tm, D_MODEL), lambda i: (i, 0))
    once = lambda a: pl.BlockSpec(a.shape, lambda i: (0, 0), pipeline_mode=pl.Buffered(1))
    return pl.pallas_call(
        _ffn_kernel,
        grid=(n // tm,),
        in_specs=[row, once(wg), once(wu), once(wd), once(g), once(b)],
        out_specs=row,
        out_shape=jax.ShapeDtypeStruct((n, D_MODEL), F32),
        scratch_shapes=[pltpu.VMEM((tm, D_MODEL), BF16), pltpu.VMEM((tm, D_MODEL), F32)],
        compiler_params=_params(),
        name="swiglu_ln",
    )(x2, wg, wu, wd, g, b)


def _pack_weights(w_in, b_w_uq, b_w_ukv, b_q_norm, a_bs, d_igate_b, d_fgate_b):
    nl = w_in.shape[0]
    o_b = ZA_W
    o_c = o_b + Q_RANK + KV_RANK + ROPE
    o_d = o_c + ZC_W
    o_g = o_d + ZD_W
    w_a = w_in[..., :o_b]
    w_cq = w_in[..., o_b:o_b + Q_RANK]
    w_ckv = w_in[..., o_b + Q_RANK:o_b + Q_RANK + KV_RANK]
    w_kr = w_in[..., o_b + Q_RANK + KV_RANK:o_c]
    w_c = w_in[..., o_c:o_d]
    w_d = w_in[..., o_d:o_g]
    w_gates = w_in[..., o_g:]
    zeros = lambda *s: jnp.zeros((nl,) + s, F32)
    w_b = jnp.concatenate([w_cq, zeros(D_MODEL, 256 - Q_RANK), w_ckv], -1)
    w_main = jnp.concatenate([w_d, w_a, w_c, w_b], -1).astype(BF16)
    half = ROPE // 2
    w_kr_rot = jnp.concatenate([-w_kr[..., half:], w_kr[..., :half]], -1)
    w_misc = jnp.concatenate(
        [w_gates, zeros(D_MODEL, KR_LANE - 2 * HEADS), w_kr, w_kr_rot], -1).astype(BF16)

    wq = b_w_uq.reshape(nl, Q_RANK, HEADS, NOPE + ROPE)
    nope, x1, x2 = wq[..., :NOPE], wq[..., NOPE:NOPE + half], wq[..., NOPE + half:]
    tail = zeros(Q_RANK, HEADS, LANES - NOPE - ROPE)
    rowpad = ((0, 0), (0, 256 - Q_RANK), (0, 0))
    wqm = jnp.pad(jnp.concatenate([nope, x1, x2, tail], -1).reshape(nl, Q_RANK, HEADS * LANES), rowpad)
    wqs = jnp.pad(jnp.concatenate([zeros(Q_RANK, HEADS, NOPE), -x2, x1, tail], -1)
                  .reshape(nl, Q_RANK, HEADS * LANES), rowpad)
    wkv = b_w_ukv.reshape(nl, KV_RANK, HEADS, NOPE + HEAD_DIM)
    wk = jnp.concatenate([wkv[..., :NOPE], zeros(KV_RANK, HEADS, LANES - NOPE)], -1)
    wk = wk.reshape(nl, KV_RANK, HEADS * LANES)
    wv = wkv[..., NOPE:].reshape(nl, KV_RANK, GW)
    qg = jnp.pad(b_q_norm, ((0, 0), (0, 256 - Q_RANK)))[:, None, :]
    bias = jnp.repeat(jnp.swapaxes(a_bs, 1, 2), HEAD_DIM, axis=-1)
    gb = jnp.concatenate([d_igate_b, d_fgate_b, zeros(LANES - 2 * HEADS)], -1)[:, None, :]
    return dict(w_main=w_main, w_misc=w_misc, wqm=wqm.astype(BF16), wqs=wqs.astype(BF16),
                wk=wk.astype(BF16), wv=wv.astype(BF16), qg=qg, bias=bias, gb=gb)


@jax.jit
def _forward(x, positions, w_in, a_ln_g, a_ln_b, a_ws, a_bs, b_q_norm, b_kv_norm, b_w_uq, b_w_ukv,
             d_conv_w, d_conv_b, d_igate_b, d_fgate_b, w_out, ln1_g, ln1_b, w_gate, w_up, w_down,
             ln2_g, ln2_b):
    bsz, seq, _ = x.shape
    pk = _pack_weights(w_in, b_w_uq, b_w_ukv, b_q_norm, a_bs, d_igate_b, d_fgate_b)
    w_out_b, w_gate_b, w_up_b, w_down_b = (w.astype(BF16) for w in (w_out, w_gate, w_up, w_down))
    cosd, sind, cosm, sinm = _rope_tables(positions)
    tm = 512
    x2 = x.reshape(bsz * seq, D_MODEL)
    row = lambda a, l: a[l][None, :]
    for l in range(DEPTH):
        zd, za, zc, zb, zm = _inproj(x2, pk["w_main"][l], pk["w_misc"][l], tm)
        ya = _gmlp(za, row(a_ln_g, l), row(a_ln_b, l), a_ws[l], pk["bias"][l], bsz, seq)
        yb = _mla(zb, zm, cosm, sinm, pk["qg"][l], row(b_kv_norm, l), pk["wqm"][l], pk["wqs"][l],
                  pk["wk"][l], pk["wv"][l], bsz, seq)
        yc = _dilated(zc, cosd, sind, bsz, seq)
        yd = _mlstm(zd, zm, pk["gb"][l], d_conv_w[l], row(d_conv_b, l), bsz, seq)
        x2 = _outproj((ya, yb, yc, yd), x2, w_out_b[l], row(ln1_g, l), row(ln1_b, l), tm)
        x2 = _ffn(x2, w_gate_b[l], w_up_b[l], w_down_b[l], row(ln2_g, l), row(ln2_b, l), tm)
    return x2.reshape(bsz, seq, D_MODEL)


def kernel(x, positions, w_in, a_ln_g, a_ln_b, a_ws, a_bs, b_q_norm, b_kv_norm, b_w_uq, b_w_ukv,
           d_conv_w, d_conv_b, d_igate_b, d_fgate_b, w_out, ln1_g, ln1_b, w_gate, w_up, w_down,
           ln2_g, ln2_b):
    return _forward(x, positions, w_in, a_ln_g, a_ln_b, a_ws, a_bs, b_q_norm, b_kv_norm, b_w_uq,
                    b_w_ukv, d_conv_w, d_conv_b, d_igate_b, d_fgate_b, w_out, ln1_g, ln1_b, w_gate,
                    w_up, w_down, ln2_g, ln2_b)
```

```python
import functools

import jax
import jax.numpy as jnp
from jax import lax
from jax.experimental import pallas as pl
from jax.experimental.pallas import tpu as pltpu

F32 = jnp.float32
BF16 = jnp.bfloat16

D_MODEL = 1024
DEPTH = 4
HEAD_DIM = 64
HEADS = 4
GW = HEADS * HEAD_DIM
BLOCK = 128
Q_RANK = 192
KV_RANK = 128
NOPE = 64
ROPE = 32
D_FF = 2816
ROPE_THETA = 10000.0
LN_EPS = 1e-5
RMS_EPS = 1e-6
ALPHA = (2 * DEPTH) ** 0.25
LANES = 128
NEG_INF = float("-inf")
LOG2_E = 1.4426950408889634
LN_2 = 0.6931471805599453

ZD_W, ZA_W, ZC_W, ZB_W = 4 * GW, 2 * GW, 3 * GW, 384
ZMAIN_W = ZD_W + ZA_W + ZC_W + ZB_W
KR_LANE = 64

VMEM_LIMIT = 56 * 1024 * 1024


def _dot(a, b):
    return jnp.dot(a, b, preferred_element_type=F32)


def _dot_nt(a, b):
    return lax.dot_general(a, b, (((1,), (1,)), ((), ())), preferred_element_type=F32)


def _params(n_axes=1):
    return pltpu.CompilerParams(
        dimension_semantics=("arbitrary",) * n_axes, vmem_limit_bytes=VMEM_LIMIT)


def _layer_norm(r, g, b):
    mu = jnp.mean(r, axis=-1, keepdims=True)
    d = r - mu
    var = jnp.mean(d * d, axis=-1, keepdims=True)
    return d * lax.rsqrt(var + LN_EPS) * g + b


def _tables_kernel(pos_ref, invd_ref, invm_ref, cosd_ref, sind_ref, cosm_ref, sinm_ref):
    pos = pos_ref[0].astype(F32)
    lane = lax.broadcasted_iota(jnp.int32, (1, LANES), 1)
    angd = pos * invd_ref[...]
    cosd_ref[0] = jnp.cos(angd)
    sind_ref[0] = jnp.sin(angd) * jnp.where((lane % HEAD_DIM) < HEAD_DIM // 2, -1.0, 1.0)
    angm = pos * invm_ref[...]
    in_rope = (lane >= KR_LANE) & (lane < KR_LANE + ROPE)
    cosm_ref[0] = jnp.where(in_rope, jnp.cos(angm), jnp.where(lane < KR_LANE, 1.0, 0.0))
    sinm_ref[0] = jnp.where(in_rope, jnp.sin(angm), 0.0)


def _rope_tables(positions):
    bsz, seq = positions.shape
    lane = jnp.arange(LANES)
    half_d = HEAD_DIM // 2
    inv_d = jnp.power(ROPE_THETA, -jnp.arange(half_d, dtype=F32) / half_d)
    half_m = ROPE // 2
    inv_m = jnp.power(ROPE_THETA, -jnp.arange(half_m, dtype=F32) / half_m)
    invd = inv_d[lane % half_d][None, :]
    invm = inv_m[lane % half_m][None, :]
    tab = jax.ShapeDtypeStruct((bsz, seq, LANES), F32)
    spec = pl.BlockSpec((1, seq, LANES), lambda b: (b, 0, 0))
    cspec = pl.BlockSpec((1, LANES), lambda b: (0, 0))
    return pl.pallas_call(
        _tables_kernel,
        grid=(bsz,),
        in_specs=[pl.BlockSpec((1, seq, 1), lambda b: (b, 0, 0)), cspec, cspec],
        out_specs=[spec] * 4,
        out_shape=[tab] * 4,
        compiler_params=_params(),
        name="rope_tables",
    )(positions.reshape(bsz, seq, 1), invd, invm)


def _project_in(xb_ref, w_ref, wm_ref, z_refs, zm_ref):
    off = 0
    for ref in z_refs:
        width = ref.shape[1]
        for c in range(0, width, 256):
            cw = min(256, width - c)
            ref[:, c:c + cw] = _dot(xb_ref[...], w_ref[:, off + c:off + c + cw]).astype(BF16)
        off += width
    zm_ref[...] = _dot(xb_ref[...], wm_ref[...])


def _inproj_kernel(x_ref, w_ref, wm_ref, zd_ref, za_ref, zc_ref, zb_ref, zm_ref, xb_ref):
    xb_ref[...] = x_ref[...].astype(BF16)
    _project_in(xb_ref, w_ref, wm_ref, (zd_ref, za_ref, zc_ref, zb_ref), zm_ref)


def _inproj(x2, w_main, w_misc, tm):
    n = x2.shape[0]
    row = lambda w: pl.BlockSpec((tm, w), lambda i: (i, 0))
    full = lambda a: pl.BlockSpec(a.shape, lambda i: (0, 0))
    return pl.pallas_call(
        _inproj_kernel,
        grid=(n // tm,),
        in_specs=[row(D_MODEL), full(w_main), full(w_misc)],
        out_specs=[row(ZD_W), row(ZA_W), row(ZC_W), row(ZB_W), row(LANES)],
        out_shape=[jax.ShapeDtypeStruct((n, ZD_W), BF16), jax.ShapeDtypeStruct((n, ZA_W), BF16),
                   jax.ShapeDtypeStruct((n, ZC_W), BF16), jax.ShapeDtypeStruct((n, ZB_W), BF16),
                   jax.ShapeDtypeStruct((n, LANES), F32)],
        scratch_shapes=[pltpu.VMEM((tm, D_MODEL), BF16)],
        compiler_params=_params(),
        name="inproj",
    )(x2, w_main, w_misc)


def _gmlp_kernel(za_ref, lng_ref, lnb_ref, ws_ref, bias_ref, y_ref):
    seq = za_ref.shape[0]
    r_i = lax.broadcasted_iota(jnp.int32, (BLOCK, BLOCK), 0)
    c_i = lax.broadcasted_iota(jnp.int32, (BLOCK, BLOCK), 1)
    w_causal = [jnp.where(c_i <= r_i, ws_ref[h], 0.0).astype(BF16) for h in range(HEADS)]
    lane_head = lax.broadcasted_iota(jnp.int32, (BLOCK, GW), 1) // HEAD_DIM

    def chunk(c, carry):
        r0 = pl.multiple_of(c * BLOCK, BLOCK)
        z = za_ref[pl.ds(r0, BLOCK), :].astype(F32)
        g = 0.5 * z * (1.0 + lax.erf(z * (0.5 ** 0.5)))
        u, v = g[:, :GW], g[:, GW:]
        vb = _layer_norm(v, lng_ref[...], lnb_ref[...]).astype(BF16)
        mixed = jnp.zeros((BLOCK, GW), F32)
        for h in range(HEADS):
            mixed = jnp.where(lane_head == h, _dot(w_causal[h], vb), mixed)
        y_ref[pl.ds(r0, BLOCK), :] = (u * (mixed + bias_ref[...])).astype(BF16)
        return carry

    lax.fori_loop(0, seq // BLOCK, chunk, 0, unroll=4)


def _gmlp(za, ln_g, ln_b, ws, bias, bsz, seq):
    full = lambda a: pl.BlockSpec(a.shape, lambda b: (0,) * a.ndim)
    return pl.pallas_call(
        _gmlp_kernel,
        grid=(bsz,),
        in_specs=[pl.BlockSpec((seq, ZA_W), lambda b: (b, 0)), full(ln_g), full(ln_b), full(ws),
                  full(bias)],
        out_specs=pl.BlockSpec((seq, GW), lambda b: (b, 0)),
        out_shape=jax.ShapeDtypeStruct((bsz * seq, GW), BF16),
        compiler_params=_params(),
        name="gmlp",
    )(za, ln_g, ln_b, ws, bias)


MLA_TQ = 256


def _mla_kernel(zb_ref, zm_ref, cos_ref, sin_ref, qg_ref, kvg_ref, wqm_ref, wqs_ref, wk_ref, wv_ref,
                y_ref, q_s, k_s, vt_s, acc_s, sc_a, sc_b):
    seq = zb_ref.shape[0]
    tq = MLA_TQ
    scale = (NOPE + ROPE) ** -0.5 * LOG2_E
    lane = lax.broadcasted_iota(jnp.int32, (1, LANES), 1)
    in_rope = (lane >= KR_LANE) & (lane < KR_LANE + ROPE)
    rows = 512
    for r in range(0, seq, rows):
        cos = cos_ref[0, r:r + rows, :]
        sin = sin_ref[0, r:r + rows, :]
        cq = zb_ref[r:r + rows, 0:256].astype(F32)
        ms = jnp.sum(cq * cq, axis=-1, keepdims=True) * (1.0 / Q_RANK)
        cqn = (cq * lax.rsqrt(ms + RMS_EPS) * qg_ref[...]).astype(BF16)
        qm = _dot(cqn, wqm_ref[...])
        qs = _dot(cqn, wqs_ref[...])
        ckv = zb_ref[r:r + rows, 256:384].astype(F32)
        ms = jnp.mean(ckv * ckv, axis=-1, keepdims=True)
        ckvn = (ckv * lax.rsqrt(ms + RMS_EPS) * kvg_ref[...]).astype(BF16)
        kn = _dot(ckvn, wk_ref[...])
        v = _dot(ckvn, wv_ref[...])
        for sub in range(rows // tq):
            vt_s[r // tq + sub] = v[sub * tq:(sub + 1) * tq, :].T.astype(BF16)
        zm = zm_ref[r:r + rows, :]
        kr = jnp.where(in_rope, zm * cos + pltpu.roll(zm, LANES - ROPE, 1) * sin, 0.0)
        for h in range(HEADS):
            sl = slice(h * LANES, (h + 1) * LANES)
            q_s[r:r + rows, sl] = ((qm[:, sl] * cos + qs[:, sl] * sin) * scale).astype(BF16)
            k_s[r:r + rows, sl] = (kn[:, sl] + kr).astype(BF16)

    key_i = lax.broadcasted_iota(jnp.int32, (tq, tq), 0)
    qry_i = lax.broadcasted_iota(jnp.int32, (tq, tq), 1)
    heads = range(HEADS)
    head_lanes = [slice(h * LANES, (h + 1) * LANES) for h in heads]

    def qblock(i, carry):
        q0 = pl.multiple_of(i * tq, tq)
        acc_s[...] = jnp.zeros(acc_s.shape, F32)

        def put_scores(j, dst):
            k0 = pl.multiple_of(j * tq, tq)
            for h in heads:
                dst[h] = _dot_nt(k_s[pl.ds(k0, tq), head_lanes[h]], q_s[pl.ds(q0, tq), head_lanes[h]])

        def absorb(j, src, m_old, l_old, masked):
            m_new, l_new, alpha, probs = [], [], [], []
            for h in heads:
                s = src[h]
                if masked:
                    s = jnp.where(key_i <= qry_i, s, NEG_INF)
                m = jnp.maximum(m_old[h], jnp.max(s, axis=0, keepdims=True))
                a = jnp.exp2(m_old[h] - m)
                p = jnp.exp2(s - m)
                m_new.append(m)
                alpha.append(a)
                l_new.append(a * l_old[h] + jnp.sum(p, axis=0, keepdims=True))
                probs.append(p.astype(BF16))
            for h in heads:
                vt = vt_s[j, h * HEAD_DIM:(h + 1) * HEAD_DIM, :]
                acc_s[h] = alpha[h] * acc_s[h] + _dot(vt, probs[h])
            return tuple(m_new), tuple(l_new)

        def on_parity(j, fn, state):
            return lax.cond(j % 2 == 0, lambda st: fn(sc_a, sc_b, st), lambda st: fn(sc_b, sc_a, st), state)

        def step(j, state):
            def run(src, dst, st):
                put_scores(j + 1, dst)
                return absorb(j, src, *st, False)
            return on_parity(j, run, state)

        put_scores(0, sc_a)
        init = (tuple(jnp.full((1, tq), NEG_INF, F32) for _ in heads),
                tuple(jnp.zeros((1, tq), F32) for _ in heads))
        state = lax.fori_loop(0, i, step, init)
        _, l_fin = on_parity(i, lambda src, dst, st: absorb(i, src, *st, True), state)
        out_t = jnp.concatenate([acc_s[h] / l_fin[h] for h in heads], axis=0)
        y_ref[pl.ds(q0, tq), :] = out_t.T.astype(BF16)
        return carry

    lax.fori_loop(0, seq // tq, qblock, 0)


def _mla(zb, zm, cosm, sinm, qg, kvg, wqm, wqs, wk, wv, bsz, seq):
    full = lambda a: pl.BlockSpec(a.shape, lambda b: (0,) * a.ndim)
    tab = pl.BlockSpec((1, seq, LANES), lambda b: (b, 0, 0))
    return pl.pallas_call(
        _mla_kernel,
        grid=(bsz,),
        in_specs=[pl.BlockSpec((seq, ZB_W), lambda b: (b, 0)),
                  pl.BlockSpec((seq, LANES), lambda b: (b, 0)), tab, tab,
                  full(qg), full(kvg), full(wqm), full(wqs), full(wk), full(wv)],
        out_specs=pl.BlockSpec((seq, GW), lambda b: (b, 0)),
        out_shape=jax.ShapeDtypeStruct((bsz * seq, GW), BF16),
        scratch_shapes=[pltpu.VMEM((seq, HEADS * LANES), BF16), pltpu.VMEM((seq, HEADS * LANES), BF16),
                        pltpu.VMEM((seq // MLA_TQ, GW, MLA_TQ), BF16),
                        pltpu.VMEM((HEADS, HEAD_DIM, MLA_TQ), F32),
                        pltpu.VMEM((HEADS, MLA_TQ, MLA_TQ), F32),
                        pltpu.VMEM((HEADS, MLA_TQ, MLA_TQ), F32)],
        compiler_params=_params(),
        name="mla",
    )(zb, zm, cosm, sinm, qg, kvg, wqm, wqs, wk, wv)


DILATIONS = (1, 4, 16)
HALVES = GW // LANES


def _head_columns(cols):
    low_half = lax.broadcasted_iota(jnp.int32, (1, LANES), 1) < HEAD_DIM
    return jnp.concatenate([jnp.where(low_half, cols[0], cols[1]),
                            jnp.where(low_half, cols[2], cols[3])], axis=1)


def _band_block(q, kh, vh, w0, nk, mask):
    pairs = range(HEADS // 2)
    scores = []
    for pr in pairs:
        k_win = jnp.concatenate([kh[2 * pr + e, pl.ds(w0, nk), :] for e in range(2)], axis=0)
        scores.append(_dot_nt(q[:, pr * LANES:(pr + 1) * LANES], k_win))
    m_cols, l_cols, probs = [], [], []
    for h in range(HEADS):
        s = scores[h // 2][:, (h % 2) * nk:(h % 2 + 1) * nk]
        s = jnp.where(mask, s, NEG_INF)
        m = jnp.max(s, axis=-1, keepdims=True)
        p = jnp.exp2(s - m)
        m_cols.append(m)
        l_cols.append(jnp.sum(p, axis=-1, keepdims=True))
        probs.append(p.astype(BF16))
    outs = []
    for pr in pairs:
        v_win = jnp.concatenate([vh[2 * pr + e, pl.ds(w0, nk), :] for e in range(2)], axis=0)
        outs.append(_dot(jnp.concatenate(probs[2 * pr:2 * pr + 2], axis=1), v_win))
    l_full = _head_columns(l_cols)
    o = jnp.concatenate(outs, axis=1) / l_full
    return o, _head_columns(m_cols) + jnp.log2(l_full)


def _dilated_kernel(zc_ref, cos_ref, sin_ref, y_ref, qf, kf, vf, qb, kh, vh, o_1, l_1, o_4, l_4, o_16, l_16):
    seq = zc_ref.shape[0]
    lane = lax.broadcasted_iota(jnp.int32, (1, LANES), 1)
    first = (lane % HEAD_DIM) < HEAD_DIM // 2
    rows = 512
    for r in range(0, seq, rows):
        cos = cos_ref[0, r:r + rows, :]
        sin = sin_ref[0, r:r + rows, :]

        def rope(x):
            rot = jnp.where(first, pltpu.roll(x, LANES - HEAD_DIM // 2, 1),
                            pltpu.roll(x, HEAD_DIM // 2, 1))
            return x * cos + rot * sin

        for half in range(HALVES):
            col = lambda base: slice(base + half * LANES, base + (half + 1) * LANES)
            qf[half, r:r + rows, :] = (rope(zc_ref[r:r + rows, col(0)].astype(F32))
                                       * (HEAD_DIM ** -0.5 * LOG2_E))
            kf[half, r:r + rows, :] = rope(zc_ref[r:r + rows, col(GW)].astype(F32))
            vf[half, r:r + rows, :] = zc_ref[r:r + rows, col(2 * GW)].astype(F32)

    r_i = lax.broadcasted_iota(jnp.int32, (BLOCK, 2 * BLOCK), 0)
    c_i = lax.broadcasted_iota(jnp.int32, (BLOCK, 2 * BLOCK), 1)
    band = (c_i >= r_i) & (c_i <= r_i + BLOCK)
    causal = (lax.broadcasted_iota(jnp.int32, (BLOCK, BLOCK), 1)
              <= lax.broadcasted_iota(jnp.int32, (BLOCK, BLOCK), 0))
    low_half = lane < HEAD_DIM
    for h in range(HEADS):
        kh[h, 0:BLOCK, :] = jnp.zeros((BLOCK, LANES), BF16)
        vh[h, 0:BLOCK, :] = jnp.zeros((BLOCK, LANES), BF16)

    results = {1: (o_1, l_1), 4: (o_4, l_4), 16: (o_16, l_16)}
    for d in DILATIONS:
        sub = seq // d
        for res in range(d):
            src = pl.ds(res, sub, stride=d) if d > 1 else slice(0, seq)
            dst = slice(BLOCK + res * sub, BLOCK + (res + 1) * sub)
            for half in range(HALVES):
                qb[res * sub:(res + 1) * sub, half * LANES:(half + 1) * LANES] = qf[half, src, :].astype(BF16)
                k_half = kf[half, src, :].astype(BF16)
                v_half = vf[half, src, :].astype(BF16)
                zero = jnp.zeros_like(k_half)
                kh[2 * half, dst, :] = jnp.where(low_half, k_half, zero)
                kh[2 * half + 1, dst, :] = jnp.where(low_half, zero, k_half)
                vh[2 * half, dst, :] = jnp.where(low_half, v_half, zero)
                vh[2 * half + 1, dst, :] = jnp.where(low_half, zero, v_half)
        blocks_per_class = sub // BLOCK
        o_dst, l_dst = results[d]

        def block(g, carry, blocks_per_class=blocks_per_class, o_dst=o_dst, l_dst=l_dst):
            r0 = pl.multiple_of(g * BLOCK, BLOCK)
            q = qb[pl.ds(r0, BLOCK), :]
            if blocks_per_class == 1:
                o, lse = _band_block(q, kh, vh, r0 + BLOCK, BLOCK, causal)
            else:
                first_key = jnp.where(g % blocks_per_class > 0, 0, BLOCK)
                o, lse = _band_block(q, kh, vh, r0, 2 * BLOCK, band & (c_i >= first_key))
            for half in range(HALVES):
                sl = slice(half * LANES, (half + 1) * LANES)
                o_dst[half, pl.ds(r0, BLOCK), :] = o[:, sl]
                l_dst[half, pl.ds(r0, BLOCK), :] = lse[:, sl]
            return carry

        lax.fori_loop(0, seq // BLOCK, block, 0, unroll=2)

    d4, d16 = DILATIONS[1], DILATIONS[2]
    for res in range(d16):
        nat = pl.ds(res, seq // d16, stride=d16)
        cls4 = pl.ds((res % d4) * (seq // d4) + res // d4, seq // d16, stride=d16 // d4)
        cls16 = slice(res * (seq // d16), (res + 1) * (seq // d16))
        for half in range(HALVES):
            la, lb, lc = l_1[half, nat, :], l_4[half, cls4, :], l_16[half, cls16, :]
            top = jnp.maximum(jnp.maximum(la, lb), lc)
            wa, wb, wc = jnp.exp2(la - top), jnp.exp2(lb - top), jnp.exp2(lc - top)
            mix = wa * o_1[half, nat, :] + wb * o_4[half, cls4, :] + wc * o_16[half, cls16, :]
            o_1[half, nat, :] = mix / (wa + wb + wc)

    for half in range(HALVES):
        y_ref[:, half * LANES:(half + 1) * LANES] = o_1[half].astype(BF16)


def _dilated(zc, cosd, sind, bsz, seq):
    tab = pl.BlockSpec((1, seq, LANES), lambda b: (b, 0, 0))
    f32buf = pltpu.VMEM((HALVES, seq, LANES), F32)
    headbuf = pltpu.VMEM((HEADS, seq + BLOCK, LANES), BF16)
    return pl.pallas_call(
        _dilated_kernel,
        grid=(bsz,),
        in_specs=[pl.BlockSpec((seq, ZC_W), lambda b: (b, 0)), tab, tab],
        out_specs=pl.BlockSpec((seq, GW), lambda b: (b, 0)),
        out_shape=jax.ShapeDtypeStruct((bsz * seq, GW), BF16),
        scratch_shapes=[f32buf, f32buf, f32buf, pltpu.VMEM((seq, GW), BF16), headbuf, headbuf,
                        f32buf, f32buf, f32buf, f32buf, f32buf, f32buf],
        compiler_params=_params(),
        name="dilated",
    )(zc, cosd, sind)


CONV_W = 4
CONV_PAD = 8


def _log_sigmoid(x):
    return jnp.minimum(x, 0.0) - jnp.log1p(jnp.exp(-jnp.abs(x)))


def _mlstm_kernel(zd_ref, zm_ref, gb_ref, cw_ref, cb_ref, y_ref, xpad, q_s, k_s, kt_s, a_r, b_r, cm_c,
                  b_c, b_f, c_s, n_s):
    seq = zd_ref.shape[0]
    rows = 512
    xpad[0:CONV_PAD, :] = jnp.zeros((CONV_PAD, 2 * GW), F32)
    for r in range(0, seq, rows):
        xpad[CONV_PAD + r:CONV_PAD + r + rows, :] = zd_ref[r:r + rows, 0:2 * GW].astype(F32)
    for r in range(0, seq, rows):
        win = xpad[r:r + rows + CONV_PAD, :]
        acc = cb_ref[...] + cw_ref[CONV_W - 1:CONV_W, :] * win[CONV_PAD:, :]
        for k in range(1, CONV_W):
            acc = acc + cw_ref[CONV_W - 1 - k:CONV_W - k, :] * pltpu.roll(win, k, 0)[CONV_PAD:, :]
        qk = acc * jax.nn.sigmoid(acc)
        k = qk[:, GW:] * (HEAD_DIM ** -0.5)
        q_s[r:r + rows, :] = qk[:, :GW].astype(BF16)
        k_s[r:r + rows, :] = k.astype(BF16)
        for sub in range(rows // BLOCK):
            kt_s[r // BLOCK + sub] = k[sub * BLOCK:(sub + 1) * BLOCK, :].T

    gt = (zm_ref[...] + gb_ref[...]).T[0:8, :]
    f_log = _log_sigmoid(pltpu.roll(gt, HEADS, 0))
    pos_in_chunk = lax.broadcasted_iota(jnp.int32, (8, seq), 1) % BLOCK
    steps = [1 << s for s in range(BLOCK.bit_length() - 1)]
    b = f_log
    for k in steps:
        b = b + jnp.where(pos_in_chunk >= k, pltpu.roll(b, k, 1), 0.0)
    a = gt - b
    cm = a
    for k in steps:
        cm = jnp.maximum(cm, jnp.where(pos_in_chunk >= k, pltpu.roll(cm, k, 1), NEG_INF))
    a_r[...] = a
    b_r[...] = b
    pad = jnp.zeros((LANES - 8, seq), F32)
    cm_c[...] = jnp.concatenate([cm, pad], axis=0).T
    b_c[...] = jnp.concatenate([b, pad], axis=0).T

    c_s[...] = jnp.zeros((GW, GW), F32)
    n_s[...] = jnp.zeros((GW, GW), F32)
    lane_head = lax.broadcasted_iota(jnp.int32, (1, GW), 1) // HEAD_DIM
    same_head = (lax.broadcasted_iota(jnp.int32, (GW, GW), 0) // HEAD_DIM
                 == lax.broadcasted_iota(jnp.int32, (GW, GW), 1) // HEAD_DIM)
    causal = (lax.broadcasted_iota(jnp.int32, (BLOCK, BLOCK), 1)
              <= lax.broadcasted_iota(jnp.int32, (BLOCK, BLOCK), 0))

    heads = range(HEADS)
    low_half = lax.broadcasted_iota(jnp.int32, (1, LANES), 1) < HEAD_DIM
    ones_blk = jnp.ones((BLOCK, LANES), BF16)
    head_sum = (lax.broadcasted_iota(jnp.int32, (HEADS * BLOCK, GW), 0) // BLOCK
                == lax.broadcasted_iota(jnp.int32, (HEADS * BLOCK, GW), 1) // HEAD_DIM).astype(BF16)

    def head_lanes(per_head):
        return jnp.concatenate([jnp.where(low_half, per_head[0], per_head[1]),
                                jnp.where(low_half, per_head[2], per_head[3])], axis=1)

    for r in range(0, seq, rows):
        b_cols = b_c[r:r + rows, :]
        b_f[r:r + rows, :] = head_lanes([b_cols[:, h:h + 1] for h in heads])

    def chunk(c, m_run):
        r0 = pl.multiple_of(c * BLOCK, BLOCK)
        qb = q_s[pl.ds(r0, BLOCK), :]
        kb = k_s[pl.ds(r0, BLOCK), :]
        v = zd_ref[pl.ds(r0, BLOCK), 2 * GW:3 * GW]
        og = zd_ref[pl.ds(r0, BLOCK), 3 * GW:4 * GW].astype(F32)
        a_rows = a_r[:, pl.ds(r0, BLOCK)]
        b_rows = b_r[:, pl.ds(r0, BLOCK)]
        cm_cols = cm_c[pl.ds(r0, BLOCK), :]

        k_heads = jnp.concatenate(
            [jnp.where(lane_head == h, kb, jnp.zeros_like(kb)) for h in heads], axis=0)
        v_heads = jnp.concatenate(
            [jnp.where(lane_head == h, v, jnp.zeros_like(v)) for h in heads], axis=0)
        scores = _dot_nt(qb, k_heads)
        carried = _dot(qb, jnp.concatenate([c_s[...], n_s[...]], axis=1).astype(BF16))

        w_intra, g_rep, wk_rows, decays, m_next = [], [], [], [], []
        for h in heads:
            a_row = a_rows[h:h + 1, :]
            g = jnp.maximum(jnp.broadcast_to(cm_cols[:, h:h + 1], (BLOCK, BLOCK)), m_run[h])
            g_rep.append(g)
            w_intra.append(jnp.where(causal, jnp.exp(a_row - g), 0.0))
            g_end = jnp.maximum(m_run[h], jnp.max(a_row, axis=-1, keepdims=True))
            m_next.append(b_rows[h:h + 1, BLOCK - 1:BLOCK] + g_end)
            decays.append(jnp.exp(m_run[h] - g_end))
            wk_rows.append(jnp.exp(a_row - g_end))
        g_full = head_lanes(g_rep)
        inter_scale = jnp.exp(head_lanes(m_run) - g_full)
        floor = jnp.exp(-(b_f[pl.ds(r0, BLOCK), :] + g_full))

        wk = jnp.concatenate([jnp.broadcast_to(w, (HEAD_DIM, BLOCK)) for w in wk_rows], axis=0)
        decay = jnp.concatenate([jnp.broadcast_to(dd, (HEAD_DIM, 1)) for dd in decays], axis=0)
        ktw = kt_s[c] * wk
        update = _dot(ktw.astype(BF16), jnp.concatenate([v, ones_blk], axis=1))

        sqk = (scores * jnp.concatenate(w_intra, axis=1)).astype(BF16)
        intra = _dot(sqk, jnp.concatenate([v_heads, head_sum], axis=1))
        num = intra[:, :GW] + inter_scale * carried[:, :GW]
        den = intra[:, GW:] + inter_scale * carried[:, GW:]
        h_out = num / jnp.maximum(jnp.abs(den), floor)
        y_ref[pl.ds(r0, BLOCK), :] = (jax.nn.sigmoid(og) * h_out).astype(BF16)

        k_sum = update[:, GW:]
        c_s[...] = decay * c_s[...] + jnp.where(same_head, update[:, :GW], 0.0)
        n_s[...] = decay * n_s[...] + jnp.where(same_head, jnp.concatenate([k_sum, k_sum], axis=1), 0.0)
        return tuple(m_next)

    lax.fori_loop(0, seq // BLOCK, chunk, tuple(jnp.zeros((1, 1), F32) for _ in range(HEADS)))


def _mlstm(zd, zm, gb, cw, cb, bsz, seq):
    full = lambda a: pl.BlockSpec(a.shape, lambda b: (0,) * a.ndim)
    return pl.pallas_call(
        _mlstm_kernel,
        grid=(bsz,),
        in_specs=[pl.BlockSpec((seq, ZD_W), lambda b: (b, 0)),
                  pl.BlockSpec((seq, LANES), lambda b: (b, 0)), full(gb), full(cw), full(cb)],
        out_specs=pl.BlockSpec((seq, GW), lambda b: (b, 0)),
        out_shape=jax.ShapeDtypeStruct((bsz * seq, GW), BF16),
        scratch_shapes=[pltpu.VMEM((seq + CONV_PAD, 2 * GW), F32),
                        pltpu.VMEM((seq, GW), BF16), pltpu.VMEM((seq, GW), BF16),
                        pltpu.VMEM((seq // BLOCK, GW, BLOCK), F32),
                        pltpu.VMEM((8, seq), F32), pltpu.VMEM((8, seq), F32),
                        pltpu.VMEM((seq, LANES), F32), pltpu.VMEM((seq, LANES), F32),
                        pltpu.VMEM((seq, GW), F32),
                        pltpu.VMEM((GW, GW), F32), pltpu.VMEM((GW, GW), F32)],
        compiler_params=_params(),
        name="mlstm",
    )(zd, zm, gb, cw, cb)


FF_CHUNK = 256


def _dense_kernel(with_next, ya_ref, yb_ref, yc_ref, yd_ref, x_ref, wo_ref, g1_ref, b1_ref, wg_ref, wu_ref,
                  wd_ref, g2_ref, b2_ref, *rest):
    if with_next:
        w_ref, wm_ref, o_ref, zd_ref, za_ref, zc_ref, zb_ref, zm_ref, xb_ref, acc_ref = rest
    else:
        o_ref, xb_ref, acc_ref = rest
    acc = ALPHA * x_ref[...]
    for i, y_ref in enumerate((ya_ref, yb_ref, yc_ref, yd_ref)):
        acc = acc + _dot(y_ref[...], wo_ref[i * GW:(i + 1) * GW, :])
    x1 = _layer_norm(acc, g1_ref[...], b1_ref[...])
    xb_ref[...] = x1.astype(BF16)
    acc_ref[...] = ALPHA * x1
    for c in range(0, D_FF, FF_CHUNK):
        gate = _dot(xb_ref[...], wg_ref[:, c:c + FF_CHUNK])
        up = _dot(xb_ref[...], wu_ref[:, c:c + FF_CHUNK])
        act = (gate * jax.nn.sigmoid(gate) * up).astype(BF16)
        acc_ref[...] += _dot(act, wd_ref[c:c + FF_CHUNK, :])
    x2 = _layer_norm(acc_ref[...], g2_ref[...], b2_ref[...])
    o_ref[...] = x2
    if with_next:
        xb_ref[...] = x2.astype(BF16)
        _project_in(xb_ref, w_ref, wm_ref, (zd_ref, za_ref, zc_ref, zb_ref), zm_ref)


def _dense(ys, x2, wo, g1, b1, wg, wu, wd, g2, b2, next_w, tm):
    n = x2.shape[0]
    row = lambda w_: pl.BlockSpec((tm, w_), lambda i: (i, 0))
    once = lambda a: pl.BlockSpec(a.shape, lambda i: (0, 0), pipeline_mode=pl.Buffered(1))
    weights = [wo, g1, b1, wg, wu, wd, g2, b2] + list(next_w)
    out_specs = [row(D_MODEL)]
    out_shape = [jax.ShapeDtypeStruct((n, D_MODEL), F32)]
    if next_w:
        out_specs += [row(ZD_W), row(ZA_W), row(ZC_W), row(ZB_W), row(LANES)]
        out_shape += [jax.ShapeDtypeStruct((n, w_), BF16) for w_ in (ZD_W, ZA_W, ZC_W, ZB_W)]
        out_shape += [jax.ShapeDtypeStruct((n, LANES), F32)]
    return pl.pallas_call(
        functools.partial(_dense_kernel, bool(next_w)),
        grid=(n // tm,),
        in_specs=[row(GW)] * 4 + [row(D_MODEL)] + [once(w) for w in weights],
        out_specs=out_specs,
        out_shape=out_shape,
        scratch_shapes=[pltpu.VMEM((tm, D_MODEL), BF16), pltpu.VMEM((tm, D_MODEL), F32)],
        compiler_params=_params(),
        name="dense_block",
    )(*ys, x2, *weights)


def _pack_weights(w_in, b_w_uq, b_w_ukv, b_q_norm, a_bs, d_igate_b, d_fgate_b):
    nl = w_in.shape[0]
    o_b = ZA_W
    o_c = o_b + Q_RANK + KV_RANK + ROPE
    o_d = o_c + ZC_W
    o_g = o_d + ZD_W
    w_a = w_in[..., :o_b]
    w_cq = w_in[..., o_b:o_b + Q_RANK]
    w_ckv = w_in[..., o_b + Q_RANK:o_b + Q_RANK + KV_RANK]
    w_kr = w_in[..., o_b + Q_RANK + KV_RANK:o_c]
    w_c = w_in[..., o_c:o_d]
    w_d = w_in[..., o_d:o_g]
    w_gates = w_in[..., o_g:]
    zeros = lambda *s: jnp.zeros((nl,) + s, F32)
    w_b = jnp.concatenate([w_cq, zeros(D_MODEL, 256 - Q_RANK), w_ckv], -1)
    w_main = jnp.concatenate([w_d, w_a, w_c, w_b], -1).astype(BF16)
    half = ROPE // 2
    w_kr_rot = jnp.concatenate([-w_kr[..., half:], w_kr[..., :half]], -1)
    w_misc = jnp.concatenate(
        [w_gates, zeros(D_MODEL, KR_LANE - 2 * HEADS), w_kr, w_kr_rot], -1).astype(BF16)

    wq = b_w_uq.reshape(nl, Q_RANK, HEADS, NOPE + ROPE)
    nope, x1, x2 = wq[..., :NOPE], wq[..., NOPE:NOPE + half], wq[..., NOPE + half:]
    tail = zeros(Q_RANK, HEADS, LANES - NOPE - ROPE)
    rowpad = ((0, 0), (0, 256 - Q_RANK), (0, 0))
    wqm = jnp.pad(jnp.concatenate([nope, x1, x2, tail], -1).reshape(nl, Q_RANK, HEADS * LANES), rowpad)
    wqs = jnp.pad(jnp.concatenate([zeros(Q_RANK, HEADS, NOPE), -x2, x1, tail], -1)
                  .reshape(nl, Q_RANK, HEADS * LANES), rowpad)
    wkv = b_w_ukv.reshape(nl, KV_RANK, HEADS, NOPE + HEAD_DIM)
    wk = jnp.concatenate([wkv[..., :NOPE], zeros(KV_RANK, HEADS, LANES - NOPE)], -1)
    wk = wk.reshape(nl, KV_RANK, HEADS * LANES)
    wv = wkv[..., NOPE:].reshape(nl, KV_RANK, GW)
    qg = jnp.pad(b_q_norm, ((0, 0), (0, 256 - Q_RANK)))[:, None, :]
    bias = jnp.repeat(jnp.swapaxes(a_bs, 1, 2), HEAD_DIM, axis=-1)
    gb = jnp.concatenate([d_igate_b, d_fgate_b, zeros(LANES - 2 * HEADS)], -1)[:, None, :]
    return dict(w_main=w_main, w_misc=w_misc, wqm=wqm.astype(BF16), wqs=wqs.astype(BF16),
                wk=wk.astype(BF16), wv=wv.astype(BF16), qg=qg, bias=bias, gb=gb)


@jax.jit
def _forward(x, positions, w_in, a_ln_g, a_ln_b, a_ws, a_bs, b_q_norm, b_kv_norm, b_w_uq, b_w_ukv,
             d_conv_w, d_conv_b, d_igate_b, d_fgate_b, w_out, ln1_g, ln1_b, w_gate, w_up, w_down,
             ln2_g, ln2_b):
    bsz, seq, _ = x.shape
    pk = _pack_weights(w_in, b_w_uq, b_w_ukv, b_q_norm, a_bs, d_igate_b, d_fgate_b)
    w_out_b, w_gate_b, w_up_b, w_down_b = (w.astype(BF16) for w in (w_out, w_gate, w_up, w_down))
    cosd, sind, cosm, sinm = _rope_tables(positions)
    tm = 512
    x2 = x.reshape(bsz * seq, D_MODEL)
    row = lambda a, l: a[l][None, :]
    zd, za, zc, zb, zm = _inproj(x2, pk["w_main"][0], pk["w_misc"][0], tm)
    for l in range(DEPTH):
        ya = _gmlp(za, row(a_ln_g, l), row(a_ln_b, l), a_ws[l], pk["bias"][l], bsz, seq)
        yb = _mla(zb, zm, cosm, sinm, pk["qg"][l], row(b_kv_norm, l), pk["wqm"][l], pk["wqs"][l],
                  pk["wk"][l], pk["wv"][l], bsz, seq)
        yc = _dilated(zc, cosd, sind, bsz, seq)
        yd = _mlstm(zd, zm, pk["gb"][l], d_conv_w[l], row(d_conv_b, l), bsz, seq)
        next_w = (pk["w_main"][l + 1], pk["w_misc"][l + 1]) if l + 1 < DEPTH else ()
        outs = _dense((ya, yb, yc, yd), x2, w_out_b[l], row(ln1_g, l), row(ln1_b, l), w_gate_b[l],
                      w_up_b[l], w_down_b[l], row(ln2_g, l), row(ln2_b, l), next_w, tm)
        if next_w:
            x2, zd, za, zc, zb, zm = outs
        else:
            x2 = outs[0]
    return x2.reshape(bsz, seq, D_MODEL)


def kernel(x, positions, w_in, a_ln_g, a_ln_b, a_ws, a_bs, b_q_norm, b_kv_norm, b_w_uq, b_w_ukv,
           d_conv_w, d_conv_b, d_igate_b, d_fgate_b, w_out, ln1_g, ln1_b, w_gate, w_up, w_down,
           ln2_g, ln2_b):
    return _forward(x, positions, w_in, a_ln_g, a_ln_b, a_ws, a_bs, b_q_norm, b_kv_norm, b_w_uq,
                    b_w_ukv, d_conv_w, d_conv_b, d_igate_b, d_fgate_b, w_out, ln1_g, ln1_b, w_gate,
                    w_up, w_down, ln2_g, ln2_b)
```

```python
import functools

import jax
import jax.numpy as jnp
from jax import lax
from jax.experimental import pallas as pl
from jax.experimental.pallas import tpu as pltpu

F32 = jnp.float32
BF16 = jnp.bfloat16

D_MODEL = 1024
DEPTH = 4
HEAD_DIM = 64
HEADS = 4
GW = HEADS * HEAD_DIM
BLOCK = 128
Q_RANK = 192
KV_RANK = 128
NOPE = 64
ROPE = 32
D_FF = 2816
ROPE_THETA = 10000.0
LN_EPS = 1e-5
RMS_EPS = 1e-6
ALPHA = (2 * DEPTH) ** 0.25
LANES = 128
NEG_INF = float("-inf")
LOG2_E = 1.4426950408889634
LN_2 = 0.6931471805599453

ZD_W, ZA_W, ZC_W, ZB_W = 4 * GW, 2 * GW, 3 * GW, 384
ZMAIN_W = ZD_W + ZA_W + ZC_W + ZB_W
KR_LANE = 64

VMEM_LIMIT = 56 * 1024 * 1024


def _dot(a, b):
    return jnp.dot(a, b, preferred_element_type=F32)


def _dot_nt(a, b):
    return lax.dot_general(a, b, (((1,), (1,)), ((), ())), preferred_element_type=F32)


def _params(n_axes=1):
    return pltpu.CompilerParams(
        dimension_semantics=("arbitrary",) * n_axes, vmem_limit_bytes=VMEM_LIMIT)


def _layer_norm(r, g, b):
    mu = jnp.mean(r, axis=-1, keepdims=True)
    d = r - mu
    var = jnp.mean(d * d, axis=-1, keepdims=True)
    return d * lax.rsqrt(var + LN_EPS) * g + b


def _tables_kernel(pos_ref, invd_ref, invm_ref, cosd_ref, sind_ref, cosm_ref, sinm_ref):
    pos = pos_ref[0].astype(F32)
    lane = lax.broadcasted_iota(jnp.int32, (1, LANES), 1)
    angd = pos * invd_ref[...]
    cosd_ref[0] = jnp.cos(angd)
    sind_ref[0] = jnp.sin(angd) * jnp.where((lane % HEAD_DIM) < HEAD_DIM // 2, -1.0, 1.0)
    angm = pos * invm_ref[...]
    in_rope = (lane >= KR_LANE) & (lane < KR_LANE + ROPE)
    cosm_ref[0] = jnp.where(in_rope, jnp.cos(angm), jnp.where(lane < KR_LANE, 1.0, 0.0))
    sinm_ref[0] = jnp.where(in_rope, jnp.sin(angm), 0.0)


def _rope_tables(positions):
    bsz, seq = positions.shape
    lane = jnp.arange(LANES)
    half_d = HEAD_DIM // 2
    inv_d = jnp.power(ROPE_THETA, -jnp.arange(half_d, dtype=F32) / half_d)
    half_m = ROPE // 2
    inv_m = jnp.power(ROPE_THETA, -jnp.arange(half_m, dtype=F32) / half_m)
    invd = inv_d[lane % half_d][None, :]
    invm = inv_m[lane % half_m][None, :]
    tab = jax.ShapeDtypeStruct((bsz, seq, LANES), F32)
    spec = pl.BlockSpec((1, seq, LANES), lambda b: (b, 0, 0))
    cspec = pl.BlockSpec((1, LANES), lambda b: (0, 0))
    return pl.pallas_call(
        _tables_kernel,
        grid=(bsz,),
        in_specs=[pl.BlockSpec((1, seq, 1), lambda b: (b, 0, 0)), cspec, cspec],
        out_specs=[spec] * 4,
        out_shape=[tab] * 4,
        compiler_params=_params(),
        name="rope_tables",
    )(positions.reshape(bsz, seq, 1), invd, invm)


def _rope_heads(x, cos, sin):
    first = (lax.broadcasted_iota(jnp.int32, (1, LANES), 1) % HEAD_DIM) < HEAD_DIM // 2
    halves = []
    for half in range(GW // LANES):
        xh = x[:, half * LANES:(half + 1) * LANES]
        rot = jnp.where(first, pltpu.roll(xh, LANES - HEAD_DIM // 2, 1), pltpu.roll(xh, HEAD_DIM // 2, 1))
        halves.append(xh * cos + rot * sin)
    return jnp.concatenate(halves, axis=1)


def _project_in(xb_ref, w_ref, wm_ref, cos_ref, sin_ref, z_refs, zm_ref):
    zc_ref = z_refs[2]
    off = 0
    for ref in z_refs:
        width = ref.shape[1]
        for c in range(0, width, 256):
            cw = min(256, width - c)
            z = _dot(xb_ref[...], w_ref[:, off + c:off + c + cw])
            if ref is zc_ref and c < 2 * GW:
                z = _rope_heads(z, cos_ref[...], sin_ref[...])
                if c == 0:
                    z = z * (HEAD_DIM ** -0.5 * LOG2_E)
            ref[:, c:c + cw] = z.astype(BF16)
        off += width
    zm_ref[...] = _dot(xb_ref[...], wm_ref[...])


def _inproj_kernel(x_ref, w_ref, wm_ref, cos_ref, sin_ref, zd_ref, za_ref, zc_ref, zb_ref, zm_ref, xb_ref):
    xb_ref[...] = x_ref[...].astype(BF16)
    _project_in(xb_ref, w_ref, wm_ref, cos_ref, sin_ref, (zd_ref, za_ref, zc_ref, zb_ref), zm_ref)


def _inproj(x2, w_main, w_misc, cosd, sind, tm):
    n = x2.shape[0]
    row = lambda w: pl.BlockSpec((tm, w), lambda i: (i, 0))
    full = lambda a: pl.BlockSpec(a.shape, lambda i: (0, 0))
    return pl.pallas_call(
        _inproj_kernel,
        grid=(n // tm,),
        in_specs=[row(D_MODEL), full(w_main), full(w_misc), row(LANES), row(LANES)],
        out_specs=[row(ZD_W), row(ZA_W), row(ZC_W), row(ZB_W), row(LANES)],
        out_shape=[jax.ShapeDtypeStruct((n, ZD_W), BF16), jax.ShapeDtypeStruct((n, ZA_W), BF16),
                   jax.ShapeDtypeStruct((n, ZC_W), BF16), jax.ShapeDtypeStruct((n, ZB_W), BF16),
                   jax.ShapeDtypeStruct((n, LANES), F32)],
        scratch_shapes=[pltpu.VMEM((tm, D_MODEL), BF16)],
        compiler_params=_params(),
        name="inproj",
    )(x2, w_main, w_misc, cosd, sind)


def _gmlp_kernel(za_ref, lng_ref, lnb_ref, ws_ref, bias_ref, y_ref):
    seq = za_ref.shape[0]
    r_i = lax.broadcasted_iota(jnp.int32, (BLOCK, BLOCK), 0)
    c_i = lax.broadcasted_iota(jnp.int32, (BLOCK, BLOCK), 1)
    w_causal = [jnp.where(c_i <= r_i, ws_ref[h], 0.0).astype(BF16) for h in range(HEADS)]
    lane_head = lax.broadcasted_iota(jnp.int32, (BLOCK, GW), 1) // HEAD_DIM

    def chunk(c, carry):
        r0 = pl.multiple_of(c * BLOCK, BLOCK)
        z = za_ref[pl.ds(r0, BLOCK), :].astype(F32)
        g = 0.5 * z * (1.0 + lax.erf(z * (0.5 ** 0.5)))
        u, v = g[:, :GW], g[:, GW:]
        vb = _layer_norm(v, lng_ref[...], lnb_ref[...]).astype(BF16)
        mixed = jnp.zeros((BLOCK, GW), F32)
        for h in range(HEADS):
            mixed = jnp.where(lane_head == h, _dot(w_causal[h], vb), mixed)
        y_ref[pl.ds(r0, BLOCK), :] = (u * (mixed + bias_ref[...])).astype(BF16)
        return carry

    lax.fori_loop(0, seq // BLOCK, chunk, 0, unroll=4)


def _gmlp(za, ln_g, ln_b, ws, bias, bsz, seq):
    full = lambda a: pl.BlockSpec(a.shape, lambda b: (0,) * a.ndim)
    return pl.pallas_call(
        _gmlp_kernel,
        grid=(bsz,),
        in_specs=[pl.BlockSpec((seq, ZA_W), lambda b: (b, 0)), full(ln_g), full(ln_b), full(ws),
                  full(bias)],
        out_specs=pl.BlockSpec((seq, GW), lambda b: (b, 0)),
        out_shape=jax.ShapeDtypeStruct((bsz * seq, GW), BF16),
        compiler_params=_params(),
        name="gmlp",
    )(za, ln_g, ln_b, ws, bias)


MLA_TQ = 256


def _mla_kernel(zb_ref, zm_ref, cos_ref, sin_ref, qg_ref, kvg_ref, wqm_ref, wqs_ref, wk_ref, wv_ref,
                y_ref, q_s, k_s, vt_s, acc_s, sc_a, sc_b):
    seq = zb_ref.shape[0]
    tq = MLA_TQ
    scale = (NOPE + ROPE) ** -0.5 * LOG2_E
    lane = lax.broadcasted_iota(jnp.int32, (1, LANES), 1)
    in_rope = (lane >= KR_LANE) & (lane < KR_LANE + ROPE)
    rows = 512
    for r in range(0, seq, rows):
        cos = cos_ref[0, r:r + rows, :]
        sin = sin_ref[0, r:r + rows, :]
        cq = zb_ref[r:r + rows, 0:256].astype(F32)
        ms = jnp.sum(cq * cq, axis=-1, keepdims=True) * (1.0 / Q_RANK)
        cqn = (cq * lax.rsqrt(ms + RMS_EPS) * qg_ref[...]).astype(BF16)
        qm = _dot(cqn, wqm_ref[...])
        qs = _dot(cqn, wqs_ref[...])
        ckv = zb_ref[r:r + rows, 256:384].astype(F32)
        ms = jnp.mean(ckv * ckv, axis=-1, keepdims=True)
        ckvn = (ckv * lax.rsqrt(ms + RMS_EPS) * kvg_ref[...]).astype(BF16)
        kn = _dot(ckvn, wk_ref[...])
        v = _dot(ckvn, wv_ref[...])
        for sub in range(rows // tq):
            vt_s[r // tq + sub] = v[sub * tq:(sub + 1) * tq, :].T.astype(BF16)
        zm = zm_ref[r:r + rows, :]
        kr = jnp.where(in_rope, zm * cos + pltpu.roll(zm, LANES - ROPE, 1) * sin, 0.0)
        for h in range(HEADS):
            sl = slice(h * LANES, (h + 1) * LANES)
            q_s[r:r + rows, sl] = ((qm[:, sl] * cos + qs[:, sl] * sin) * scale).astype(BF16)
            k_s[r:r + rows, sl] = (kn[:, sl] + kr).astype(BF16)

    key_i = lax.broadcasted_iota(jnp.int32, (tq, tq), 0)
    qry_i = lax.broadcasted_iota(jnp.int32, (tq, tq), 1)
    heads = range(HEADS)
    head_lanes = [slice(h * LANES, (h + 1) * LANES) for h in heads]

    def qblock(i, carry):
        q0 = pl.multiple_of(i * tq, tq)
        acc_s[...] = jnp.zeros(acc_s.shape, F32)

        def put_scores(j, dst):
            k0 = pl.multiple_of(j * tq, tq)
            for h in heads:
                dst[h] = _dot_nt(k_s[pl.ds(k0, tq), head_lanes[h]], q_s[pl.ds(q0, tq), head_lanes[h]])

        def absorb(j, src, m_old, l_old, masked):
            m_new, l_new, alpha, probs = [], [], [], []
            for h in heads:
                s = src[h]
                if masked:
                    s = jnp.where(key_i <= qry_i, s, NEG_INF)
                m = jnp.maximum(m_old[h], jnp.max(s, axis=0, keepdims=True))
                a = jnp.exp2(m_old[h] - m)
                p = jnp.exp2(s - m)
                m_new.append(m)
                alpha.append(a)
                l_new.append(a * l_old[h] + jnp.sum(p, axis=0, keepdims=True))
                probs.append(p.astype(BF16))
            for h in heads:
                vt = vt_s[j, h * HEAD_DIM:(h + 1) * HEAD_DIM, :]
                acc_s[h] = alpha[h] * acc_s[h] + _dot(vt, probs[h])
            return tuple(m_new), tuple(l_new)

        def on_parity(j, fn, state):
            return lax.cond(j % 2 == 0, lambda st: fn(sc_a, sc_b, st), lambda st: fn(sc_b, sc_a, st), state)

        def step(j, state):
            def run(src, dst, st):
                put_scores(j + 1, dst)
                return absorb(j, src, *st, False)
            return on_parity(j, run, state)

        put_scores(0, sc_a)
        init = (tuple(jnp.full((1, tq), NEG_INF, F32) for _ in heads),
                tuple(jnp.zeros((1, tq), F32) for _ in heads))
        state = lax.fori_loop(0, i, step, init)
        _, l_fin = on_parity(i, lambda src, dst, st: absorb(i, src, *st, True), state)
        out_t = jnp.concatenate([acc_s[h] / l_fin[h] for h in heads], axis=0)
        y_ref[pl.ds(q0, tq), :] = out_t.T.astype(BF16)
        return carry

    lax.fori_loop(0, seq // tq, qblock, 0)


def _mla(zb, zm, cosm, sinm, qg, kvg, wqm, wqs, wk, wv, bsz, seq):
    full = lambda a: pl.BlockSpec(a.shape, lambda b: (0,) * a.ndim)
    tab = pl.BlockSpec((1, seq, LANES), lambda b: (b, 0, 0))
    return pl.pallas_call(
        _mla_kernel,
        grid=(bsz,),
        in_specs=[pl.BlockSpec((seq, ZB_W), lambda b: (b, 0)),
                  pl.BlockSpec((seq, LANES), lambda b: (b, 0)), tab, tab,
                  full(qg), full(kvg), full(wqm), full(wqs), full(wk), full(wv)],
        out_specs=pl.BlockSpec((seq, GW), lambda b: (b, 0)),
        out_shape=jax.ShapeDtypeStruct((bsz * seq, GW), BF16),
        scratch_shapes=[pltpu.VMEM((seq, HEADS * LANES), BF16), pltpu.VMEM((seq, HEADS * LANES), BF16),
                        pltpu.VMEM((seq // MLA_TQ, GW, MLA_TQ), BF16),
                        pltpu.VMEM((HEADS, HEAD_DIM, MLA_TQ), F32),
                        pltpu.VMEM((HEADS, MLA_TQ, MLA_TQ), F32),
                        pltpu.VMEM((HEADS, MLA_TQ, MLA_TQ), F32)],
        compiler_params=_params(),
        name="mla",
    )(zb, zm, cosm, sinm, qg, kvg, wqm, wqs, wk, wv)


DILATIONS = (1, 4, 16)
HALVES = GW // LANES


def _head_columns(cols):
    low_half = lax.broadcasted_iota(jnp.int32, (1, LANES), 1) < HEAD_DIM
    return jnp.concatenate([jnp.where(low_half, cols[0], cols[1]),
                            jnp.where(low_half, cols[2], cols[3])], axis=1)


def _band_block(q, kh, vh, w0, nk, mask):
    pairs = range(HEADS // 2)
    scores = []
    for pr in pairs:
        k_win = jnp.concatenate([kh[2 * pr + e, pl.ds(w0, nk), :] for e in range(2)], axis=0)
        scores.append(_dot_nt(q[:, pr * LANES:(pr + 1) * LANES], k_win))
    m_cols, l_cols, probs = [], [], []
    for h in range(HEADS):
        s = scores[h // 2][:, (h % 2) * nk:(h % 2 + 1) * nk]
        s = jnp.where(mask, s, NEG_INF)
        m = jnp.max(s, axis=-1, keepdims=True)
        p = jnp.exp2(s - m)
        m_cols.append(m)
        l_cols.append(jnp.sum(p, axis=-1, keepdims=True))
        probs.append(p.astype(BF16))
    outs = []
    for pr in pairs:
        v_win = jnp.concatenate([vh[2 * pr + e, pl.ds(w0, nk), :] for e in range(2)], axis=0)
        outs.append(_dot(jnp.concatenate(probs[2 * pr:2 * pr + 2], axis=1), v_win))
    l_full = _head_columns(l_cols)
    o = jnp.concatenate(outs, axis=1) / l_full
    return o, _head_columns(m_cols) + jnp.log2(l_full)


def _dilated_kernel(z1_ref, z4_ref, z16_ref, o1_ref, o4_ref, o16_ref, l1_ref, l4_ref, l16_ref, qb, kh, vh):
    seq = z1_ref.shape[0]
    lane = lax.broadcasted_iota(jnp.int32, (1, LANES), 1)
    views = {1: (z1_ref, o1_ref, l1_ref), 4: (z4_ref, o4_ref, l4_ref), 16: (z16_ref, o16_ref, l16_ref)}

    r_i = lax.broadcasted_iota(jnp.int32, (BLOCK, 2 * BLOCK), 0)
    c_i = lax.broadcasted_iota(jnp.int32, (BLOCK, 2 * BLOCK), 1)
    band = (c_i >= r_i) & (c_i <= r_i + BLOCK)
    causal = (lax.broadcasted_iota(jnp.int32, (BLOCK, BLOCK), 1)
              <= lax.broadcasted_iota(jnp.int32, (BLOCK, BLOCK), 0))
    low_half = lane < HEAD_DIM
    for h in range(HEADS):
        kh[h, 0:BLOCK, :] = jnp.zeros((BLOCK, LANES), BF16)
        vh[h, 0:BLOCK, :] = jnp.zeros((BLOCK, LANES), BF16)

    for d in DILATIONS:
        sub = seq // d
        z_ref, o_ref, l_ref = views[d]
        for res in range(d):
            dst = slice(BLOCK + res * sub, BLOCK + (res + 1) * sub)
            qb[res * sub:(res + 1) * sub, :] = z_ref[:, res * ZC_W:res * ZC_W + GW]
            for half in range(HALVES):
                col = lambda base: slice(res * ZC_W + base + half * LANES,
                                         res * ZC_W + base + (half + 1) * LANES)
                k_half = z_ref[:, col(GW)]
                v_half = z_ref[:, col(2 * GW)]
                zero = jnp.zeros_like(k_half)
                kh[2 * half, dst, :] = jnp.where(low_half, k_half, zero)
                kh[2 * half + 1, dst, :] = jnp.where(low_half, zero, k_half)
                vh[2 * half, dst, :] = jnp.where(low_half, v_half, zero)
                vh[2 * half + 1, dst, :] = jnp.where(low_half, zero, v_half)
        blocks_per_class = sub // BLOCK

        for res in range(d):
            cols = slice(res * GW, (res + 1) * GW)

            def block(n, carry, res=res, cols=cols, o_ref=o_ref, l_ref=l_ref):
                r0 = pl.multiple_of(res * sub + n * BLOCK, BLOCK)
                q = qb[pl.ds(r0, BLOCK), :]
                if blocks_per_class == 1:
                    o, lse = _band_block(q, kh, vh, r0 + BLOCK, BLOCK, causal)
                else:
                    first_key = jnp.where(n > 0, 0, BLOCK)
                    o, lse = _band_block(q, kh, vh, r0, 2 * BLOCK, band & (c_i >= first_key))
                rows = pl.ds(pl.multiple_of(n * BLOCK, BLOCK), BLOCK)
                o_ref[rows, cols] = o.astype(BF16)
                l_ref[rows, cols] = lse
                return carry

            if blocks_per_class == 1:
                block(0, 0)
            else:
                lax.fori_loop(0, blocks_per_class, block, 0, unroll=2)


def _dilated(zc, bsz, seq):
    headbuf = pltpu.VMEM((HEADS, seq + BLOCK, LANES), BF16)
    n = bsz * seq
    view = lambda width, d: pl.BlockSpec((seq // d, d * width), lambda b: (b, 0))
    outs = pl.pallas_call(
        _dilated_kernel,
        grid=(bsz,),
        in_specs=[view(ZC_W, d) for d in DILATIONS],
        out_specs=[view(GW, d) for d in DILATIONS] * 2,
        out_shape=([jax.ShapeDtypeStruct((n // d, d * GW), BF16) for d in DILATIONS]
                   + [jax.ShapeDtypeStruct((n // d, d * GW), F32) for d in DILATIONS]),
        scratch_shapes=[pltpu.VMEM((seq, GW), BF16), headbuf, headbuf],
        compiler_params=_params(),
        name="dilated",
    )(*[zc.reshape(n // d, d * ZC_W) for d in DILATIONS])
    return [o.reshape(n, GW) for o in outs]


CONV_W = 4
CONV_PAD = 8


def _log_sigmoid(x):
    return jnp.minimum(x, 0.0) - jnp.log1p(jnp.exp(-jnp.abs(x)))


def _mlstm_kernel(zd_ref, zm_ref, gb_ref, cw_ref, cb_ref, y_ref, xpad, q_s, k_s, kt_s, a_r, b_r, cm_c,
                  b_c, b_f, c_s, n_s):
    seq = zd_ref.shape[0]
    rows = 512
    xpad[0:CONV_PAD, :] = jnp.zeros((CONV_PAD, 2 * GW), F32)
    for r in range(0, seq, rows):
        xpad[CONV_PAD + r:CONV_PAD + r + rows, :] = zd_ref[r:r + rows, 0:2 * GW].astype(F32)
    for r in range(0, seq, rows):
        win = xpad[r:r + rows + CONV_PAD, :]
        acc = cb_ref[...] + cw_ref[CONV_W - 1:CONV_W, :] * win[CONV_PAD:, :]
        for k in range(1, CONV_W):
            acc = acc + cw_ref[CONV_W - 1 - k:CONV_W - k, :] * pltpu.roll(win, k, 0)[CONV_PAD:, :]
        qk = acc * jax.nn.sigmoid(acc)
        k = qk[:, GW:] * (HEAD_DIM ** -0.5)
        q_s[r:r + rows, :] = qk[:, :GW].astype(BF16)
        k_s[r:r + rows, :] = k.astype(BF16)
        for sub in range(rows // BLOCK):
            kt_s[r // BLOCK + sub] = k[sub * BLOCK:(sub + 1) * BLOCK, :].T

    gt = (zm_ref[...] + gb_ref[...]).T[0:8, :]
    f_log = _log_sigmoid(pltpu.roll(gt, HEADS, 0))
    pos_in_chunk = lax.broadcasted_iota(jnp.int32, (8, seq), 1) % BLOCK
    steps = [1 << s for s in range(BLOCK.bit_length() - 1)]
    b = f_log
    for k in steps:
        b = b + jnp.where(pos_in_chunk >= k, pltpu.roll(b, k, 1), 0.0)
    a = gt - b
    cm = a
    for k in steps:
        cm = jnp.maximum(cm, jnp.where(pos_in_chunk >= k, pltpu.roll(cm, k, 1), NEG_INF))
    a_r[...] = a
    b_r[...] = b
    pad = jnp.zeros((LANES - 8, seq), F32)
    cm_c[...] = jnp.concatenate([cm, pad], axis=0).T
    b_c[...] = jnp.concatenate([b, pad], axis=0).T

    c_s[...] = jnp.zeros((GW, GW), F32)
    n_s[...] = jnp.zeros((GW, GW), F32)
    lane_head = lax.broadcasted_iota(jnp.int32, (1, GW), 1) // HEAD_DIM
    same_head = (lax.broadcasted_iota(jnp.int32, (GW, GW), 0) // HEAD_DIM
                 == lax.broadcasted_iota(jnp.int32, (GW, GW), 1) // HEAD_DIM)
    causal = (lax.broadcasted_iota(jnp.int32, (BLOCK, BLOCK), 1)
              <= lax.broadcasted_iota(jnp.int32, (BLOCK, BLOCK), 0))

    heads = range(HEADS)
    low_half = lax.broadcasted_iota(jnp.int32, (1, LANES), 1) < HEAD_DIM
    ones_blk = jnp.ones((BLOCK, LANES), BF16)
    head_sum = (lax.broadcasted_iota(jnp.int32, (HEADS * BLOCK, GW), 0) // BLOCK
                == lax.broadcasted_iota(jnp.int32, (HEADS * BLOCK, GW), 1) // HEAD_DIM).astype(BF16)

    def head_lanes(per_head):
        return jnp.concatenate([jnp.where(low_half, per_head[0], per_head[1]),
                                jnp.where(low_half, per_head[2], per_head[3])], axis=1)

    for r in range(0, seq, rows):
        b_cols = b_c[r:r + rows, :]
        b_f[r:r + rows, :] = head_lanes([b_cols[:, h:h + 1] for h in heads])

    def chunk(c, m_run):
        r0 = pl.multiple_of(c * BLOCK, BLOCK)
        qb = q_s[pl.ds(r0, BLOCK), :]
        kb = k_s[pl.ds(r0, BLOCK), :]
        v = zd_ref[pl.ds(r0, BLOCK), 2 * GW:3 * GW]
        og = zd_ref[pl.ds(r0, BLOCK), 3 * GW:4 * GW].astype(F32)
        a_rows = a_r[:, pl.ds(r0, BLOCK)]
        b_rows = b_r[:, pl.ds(r0, BLOCK)]
        cm_cols = cm_c[pl.ds(r0, BLOCK), :]

        k_heads = jnp.concatenate(
            [jnp.where(lane_head == h, kb, jnp.zeros_like(kb)) for h in heads], axis=0)
        v_heads = jnp.concatenate(
            [jnp.where(lane_head == h, v, jnp.zeros_like(v)) for h in heads], axis=0)
        scores = _dot_nt(qb, k_heads)
        carried = _dot(qb, jnp.concatenate([c_s[...], n_s[...]], axis=1).astype(BF16))

        w_intra, g_rep, wk_rows, decays, m_next = [], [], [], [], []
        for h in heads:
            a_row = a_rows[h:h + 1, :]
            g = jnp.maximum(jnp.broadcast_to(cm_cols[:, h:h + 1], (BLOCK, BLOCK)), m_run[h])
            g_rep.append(g)
            w_intra.append(jnp.where(causal, jnp.exp(a_row - g), 0.0))
            g_end = jnp.maximum(m_run[h], jnp.max(a_row, axis=-1, keepdims=True))
            m_next.append(b_rows[h:h + 1, BLOCK - 1:BLOCK] + g_end)
            decays.append(jnp.exp(m_run[h] - g_end))
            wk_rows.append(jnp.exp(a_row - g_end))
        g_full = head_lanes(g_rep)
        inter_scale = jnp.exp(head_lanes(m_run) - g_full)
        floor = jnp.exp(-(b_f[pl.ds(r0, BLOCK), :] + g_full))

        wk = jnp.concatenate([jnp.broadcast_to(w, (HEAD_DIM, BLOCK)) for w in wk_rows], axis=0)
        decay = jnp.concatenate([jnp.broadcast_to(dd, (HEAD_DIM, 1)) for dd in decays], axis=0)
        ktw = kt_s[c] * wk
        update = _dot(ktw.astype(BF16), jnp.concatenate([v, ones_blk], axis=1))

        sqk = (scores * jnp.concatenate(w_intra, axis=1)).astype(BF16)
        intra = _dot(sqk, jnp.concatenate([v_heads, head_sum], axis=1))
        num = intra[:, :GW] + inter_scale * carried[:, :GW]
        den = intra[:, GW:] + inter_scale * carried[:, GW:]
        h_out = num / jnp.maximum(jnp.abs(den), floor)
        y_ref[pl.ds(r0, BLOCK), :] = (jax.nn.sigmoid(og) * h_out).astype(BF16)

        k_sum = update[:, GW:]
        c_s[...] = decay * c_s[...] + jnp.where(same_head, update[:, :GW], 0.0)
        n_s[...] = decay * n_s[...] + jnp.where(same_head, jnp.concatenate([k_sum, k_sum], axis=1), 0.0)
        return tuple(m_next)

    lax.fori_loop(0, seq // BLOCK, chunk, tuple(jnp.zeros((1, 1), F32) for _ in range(HEADS)))


def _mlstm(zd, zm, gb, cw, cb, bsz, seq):
    full = lambda a: pl.BlockSpec(a.shape, lambda b: (0,) * a.ndim)
    return pl.pallas_call(
        _mlstm_kernel,
        grid=(bsz,),
        in_specs=[pl.BlockSpec((seq, ZD_W), lambda b: (b, 0)),
                  pl.BlockSpec((seq, LANES), lambda b: (b, 0)), full(gb), full(cw), full(cb)],
        out_specs=pl.BlockSpec((seq, GW), lambda b: (b, 0)),
        out_shape=jax.ShapeDtypeStruct((bsz * seq, GW), BF16),
        scratch_shapes=[pltpu.VMEM((seq + CONV_PAD, 2 * GW), F32),
                        pltpu.VMEM((seq, GW), BF16), pltpu.VMEM((seq, GW), BF16),
                        pltpu.VMEM((seq // BLOCK, GW, BLOCK), F32),
                        pltpu.VMEM((8, seq), F32), pltpu.VMEM((8, seq), F32),
                        pltpu.VMEM((seq, LANES), F32), pltpu.VMEM((seq, LANES), F32),
                        pltpu.VMEM((seq, GW), F32),
                        pltpu.VMEM((GW, GW), F32), pltpu.VMEM((GW, GW), F32)],
        compiler_params=_params(),
        name="mlstm",
    )(zd, zm, gb, cw, cb)


FF_CHUNK = 256


def _dense_kernel(with_next, ya_ref, yb_ref, yd_ref, oc1_ref, oc4_ref, oc16_ref, lc1_ref, lc4_ref, lc16_ref,
                  x_ref, wo_ref, g1_ref, b1_ref, wg_ref, wu_ref, wd_ref, g2_ref, b2_ref, *rest):
    if with_next:
        w_ref, wm_ref, cos_ref, sin_ref, o_ref, zd_ref, za_ref, zc_ref, zb_ref, zm_ref, xb_ref, acc_ref = rest
    else:
        o_ref, xb_ref, acc_ref = rest
    lse = [lc1_ref[...], lc4_ref[...], lc16_ref[...]]
    top = jnp.maximum(jnp.maximum(lse[0], lse[1]), lse[2])
    wts = [jnp.exp2(l - top) for l in lse]
    mix = sum(w * o_ref_[...].astype(F32) for w, o_ref_ in zip(wts, (oc1_ref, oc4_ref, oc16_ref)))
    yc = (mix / (wts[0] + wts[1] + wts[2])).astype(BF16)
    acc = ALPHA * x_ref[...]
    for i, y in enumerate((ya_ref[...], yb_ref[...], yc, yd_ref[...])):
        acc = acc + _dot(y, wo_ref[i * GW:(i + 1) * GW, :])
    x1 = _layer_norm(acc, g1_ref[...], b1_ref[...])
    xb_ref[...] = x1.astype(BF16)
    acc_ref[...] = ALPHA * x1
    for c in range(0, D_FF, FF_CHUNK):
        gate = _dot(xb_ref[...], wg_ref[:, c:c + FF_CHUNK])
        up = _dot(xb_ref[...], wu_ref[:, c:c + FF_CHUNK])
        act = (gate * jax.nn.sigmoid(gate) * up).astype(BF16)
        acc_ref[...] += _dot(act, wd_ref[c:c + FF_CHUNK, :])
    x2 = _layer_norm(acc_ref[...], g2_ref[...], b2_ref[...])
    o_ref[...] = x2
    if with_next:
        xb_ref[...] = x2.astype(BF16)
        _project_in(xb_ref, w_ref, wm_ref, cos_ref, sin_ref, (zd_ref, za_ref, zc_ref, zb_ref), zm_ref)


def _dense(ys, dil, x2, wo, g1, b1, wg, wu, wd, g2, b2, next_w, tables, tm):
    n = x2.shape[0]
    row = lambda w_: pl.BlockSpec((tm, w_), lambda i: (i, 0))
    once = lambda a: pl.BlockSpec(a.shape, lambda i: (0, 0), pipeline_mode=pl.Buffered(1))
    weights = [wo, g1, b1, wg, wu, wd, g2, b2] + list(next_w)
    tables = list(tables) if next_w else []
    out_specs = [row(D_MODEL)]
    out_shape = [jax.ShapeDtypeStruct((n, D_MODEL), F32)]
    if next_w:
        out_specs += [row(ZD_W), row(ZA_W), row(ZC_W), row(ZB_W), row(LANES)]
        out_shape += [jax.ShapeDtypeStruct((n, w_), BF16) for w_ in (ZD_W, ZA_W, ZC_W, ZB_W)]
        out_shape += [jax.ShapeDtypeStruct((n, LANES), F32)]
    return pl.pallas_call(
        functools.partial(_dense_kernel, bool(next_w)),
        grid=(n // tm,),
        in_specs=([row(GW)] * (len(ys) + len(dil)) + [row(D_MODEL)] + [once(w) for w in weights]
                  + [row(LANES)] * len(tables)),
        out_specs=out_specs,
        out_shape=out_shape,
        scratch_shapes=[pltpu.VMEM((tm, D_MODEL), BF16), pltpu.VMEM((tm, D_MODEL), F32)],
        compiler_params=_params(),
        name="dense_block",
    )(*ys, *dil, x2, *weights, *tables)


def _pack_weights(w_in, b_w_uq, b_w_ukv, b_q_norm, a_bs, d_igate_b, d_fgate_b):
    nl = w_in.shape[0]
    o_b = ZA_W
    o_c = o_b + Q_RANK + KV_RANK + ROPE
    o_d = o_c + ZC_W
    o_g = o_d + ZD_W
    w_a = w_in[..., :o_b]
    w_cq = w_in[..., o_b:o_b + Q_RANK]
    w_ckv = w_in[..., o_b + Q_RANK:o_b + Q_RANK + KV_RANK]
    w_kr = w_in[..., o_b + Q_RANK + KV_RANK:o_c]
    w_c = w_in[..., o_c:o_d]
    w_d = w_in[..., o_d:o_g]
    w_gates = w_in[..., o_g:]
    zeros = lambda *s: jnp.zeros((nl,) + s, F32)
    w_b = jnp.concatenate([w_cq, zeros(D_MODEL, 256 - Q_RANK), w_ckv], -1)
    w_main = jnp.concatenate([w_d, w_a, w_c, w_b], -1).astype(BF16)
    half = ROPE // 2
    w_kr_rot = jnp.concatenate([-w_kr[..., half:], w_kr[..., :half]], -1)
    w_misc = jnp.concatenate(
        [w_gates, zeros(D_MODEL, KR_LANE - 2 * HEADS), w_kr, w_kr_rot], -1).astype(BF16)

    wq = b_w_uq.reshape(nl, Q_RANK, HEADS, NOPE + ROPE)
    nope, x1, x2 = wq[..., :NOPE], wq[..., NOPE:NOPE + half], wq[..., NOPE + half:]
    tail = zeros(Q_RANK, HEADS, LANES - NOPE - ROPE)
    rowpad = ((0, 0), (0, 256 - Q_RANK), (0, 0))
    wqm = jnp.pad(jnp.concatenate([nope, x1, x2, tail], -1).reshape(nl, Q_RANK, HEADS * LANES), rowpad)
    wqs = jnp.pad(jnp.concatenate([zeros(Q_RANK, HEADS, NOPE), -x2, x1, tail], -1)
                  .reshape(nl, Q_RANK, HEADS * LANES), rowpad)
    wkv = b_w_ukv.reshape(nl, KV_RANK, HEADS, NOPE + HEAD_DIM)
    wk = jnp.concatenate([wkv[..., :NOPE], zeros(KV_RANK, HEADS, LANES - NOPE)], -1)
    wk = wk.reshape(nl, KV_RANK, HEADS * LANES)
    wv = wkv[..., NOPE:].reshape(nl, KV_RANK, GW)
    qg = jnp.pad(b_q_norm, ((0, 0), (0, 256 - Q_RANK)))[:, None, :]
    bias = jnp.repeat(jnp.swapaxes(a_bs, 1, 2), HEAD_DIM, axis=-1)
    gb = jnp.concatenate([d_igate_b, d_fgate_b, zeros(LANES - 2 * HEADS)], -1)[:, None, :]
    return dict(w_main=w_main, w_misc=w_misc, wqm=wqm.astype(BF16), wqs=wqs.astype(BF16),
                wk=wk.astype(BF16), wv=wv.astype(BF16), qg=qg, bias=bias, gb=gb)


@jax.jit
def _forward(x, positions, w_in, a_ln_g, a_ln_b, a_ws, a_bs, b_q_norm, b_kv_norm, b_w_uq, b_w_ukv,
             d_conv_w, d_conv_b, d_igate_b, d_fgate_b, w_out, ln1_g, ln1_b, w_gate, w_up, w_down,
             ln2_g, ln2_b):
    bsz, seq, _ = x.shape
    pk = _pack_weights(w_in, b_w_uq, b_w_ukv, b_q_norm, a_bs, d_igate_b, d_fgate_b)
    w_out_b, w_gate_b, w_up_b, w_down_b = (w.astype(BF16) for w in (w_out, w_gate, w_up, w_down))
    cosd, sind, cosm, sinm = _rope_tables(positions)
    tm = 512
    x2 = x.reshape(bsz * seq, D_MODEL)
    row = lambda a, l: a[l][None, :]
    rope_d = (cosd.reshape(bsz * seq, LANES), sind.reshape(bsz * seq, LANES))
    zd, za, zc, zb, zm = _inproj(x2, pk["w_main"][0], pk["w_misc"][0], *rope_d, tm)
    for l in range(DEPTH):
        ya = _gmlp(za, row(a_ln_g, l), row(a_ln_b, l), a_ws[l], pk["bias"][l], bsz, seq)
        yb = _mla(zb, zm, cosm, sinm, pk["qg"][l], row(b_kv_norm, l), pk["wqm"][l], pk["wqs"][l],
                  pk["wk"][l], pk["wv"][l], bsz, seq)
        dil = _dilated(zc, bsz, seq)
        yd = _mlstm(zd, zm, pk["gb"][l], d_conv_w[l], row(d_conv_b, l), bsz, seq)
        next_w = (pk["w_main"][l + 1], pk["w_misc"][l + 1]) if l + 1 < DEPTH else ()
        outs = _dense((ya, yb, yd), dil, x2, w_out_b[l], row(ln1_g, l), row(ln1_b, l), w_gate_b[l],
                      w_up_b[l], w_down_b[l], row(ln2_g, l), row(ln2_b, l), next_w, rope_d, tm)
        if next_w:
            x2, zd, za, zc, zb, zm = outs
        else:
            x2 = outs[0]
    return x2.reshape(bsz, seq, D_MODEL)


def kernel(x, positions, w_in, a_ln_g, a_ln_b, a_ws, a_bs, b_q_norm, b_kv_norm, b_w_uq, b_w_ukv,
           d_conv_w, d_conv_b, d_igate_b, d_fgate_b, w_out, ln1_g, ln1_b, w_gate, w_up, w_down,
           ln2_g, ln2_b):
    return _forward(x, positions, w_in, a_ln_g, a_ln_b, a_ws, a_bs, b_q_norm, b_kv_norm, b_w_uq,
                    b_w_ukv, d_conv_w, d_conv_b, d_igate_b, d_fgate_b, w_out, ln1_g, ln1_b, w_gate,
                    w_up, w_down, ln2_g, ln2_b)
```

```python
import functools

import jax
import jax.numpy as jnp
from jax import lax
from jax.experimental import pallas as pl
from jax.experimental.pallas import tpu as pltpu

F32 = jnp.float32
BF16 = jnp.bfloat16

D_MODEL = 1024
DEPTH = 4
HEAD_DIM = 64
HEADS = 4
GW = HEADS * HEAD_DIM
BLOCK = 128
Q_RANK = 192
KV_RANK = 128
NOPE = 64
ROPE = 32
D_FF = 2816
ROPE_THETA = 10000.0
LN_EPS = 1e-5
RMS_EPS = 1e-6
ALPHA = (2 * DEPTH) ** 0.25
LANES = 128
NEG_INF = float("-inf")
LOG2_E = 1.4426950408889634
LN_2 = 0.6931471805599453

ZD_W, ZA_W, ZC_W, ZB_W = 4 * GW, 2 * GW, 3 * GW, 384
ZMAIN_W = ZD_W + ZA_W + ZC_W + ZB_W
KR_LANE = 64

VMEM_LIMIT = 60000 * 1024


def _dot(a, b):
    return jnp.dot(a, b, preferred_element_type=F32)


def _dot_nt(a, b):
    return lax.dot_general(a, b, (((1,), (1,)), ((), ())), preferred_element_type=F32)


def _params(n_axes=1):
    return pltpu.CompilerParams(
        dimension_semantics=("arbitrary",) * n_axes, vmem_limit_bytes=VMEM_LIMIT)


def _layer_norm(r, g, b):
    mu = jnp.mean(r, axis=-1, keepdims=True)
    d = r - mu
    var = jnp.mean(d * d, axis=-1, keepdims=True)
    return d * lax.rsqrt(var + LN_EPS) * g + b


def _tables_kernel(pos_ref, invd_ref, invm_ref, cosd_ref, sind_ref, cosm_ref, sinm_ref):
    pos = pos_ref[0].astype(F32)
    lane = lax.broadcasted_iota(jnp.int32, (1, LANES), 1)
    angd = pos * invd_ref[...]
    cosd_ref[0] = jnp.cos(angd)
    sind_ref[0] = jnp.sin(angd) * jnp.where((lane % HEAD_DIM) < HEAD_DIM // 2, -1.0, 1.0)
    angm = pos * invm_ref[...]
    in_rope = (lane >= KR_LANE) & (lane < KR_LANE + ROPE)
    cosm_ref[0] = jnp.where(in_rope, jnp.cos(angm), jnp.where(lane < KR_LANE, 1.0, 0.0))
    sinm_ref[0] = jnp.where(in_rope, jnp.sin(angm), 0.0)


def _rope_tables(positions):
    bsz, seq = positions.shape
    lane = jnp.arange(LANES)
    half_d = HEAD_DIM // 2
    inv_d = jnp.power(ROPE_THETA, -jnp.arange(half_d, dtype=F32) / half_d)
    half_m = ROPE // 2
    inv_m = jnp.power(ROPE_THETA, -jnp.arange(half_m, dtype=F32) / half_m)
    invd = inv_d[lane % half_d][None, :]
    invm = inv_m[lane % half_m][None, :]
    tab = jax.ShapeDtypeStruct((bsz, seq, LANES), F32)
    spec = pl.BlockSpec((1, seq, LANES), lambda b: (b, 0, 0))
    cspec = pl.BlockSpec((1, LANES), lambda b: (0, 0))
    return pl.pallas_call(
        _tables_kernel,
        grid=(bsz,),
        in_specs=[pl.BlockSpec((1, seq, 1), lambda b: (b, 0, 0)), cspec, cspec],
        out_specs=[spec] * 4,
        out_shape=[tab] * 4,
        compiler_params=_params(),
        name="rope_tables",
    )(positions.reshape(bsz, seq, 1), invd, invm)


def _rope_heads(x, cos, sin):
    first = (lax.broadcasted_iota(jnp.int32, (1, LANES), 1) % HEAD_DIM) < HEAD_DIM // 2
    halves = []
    for half in range(GW // LANES):
        xh = x[:, half * LANES:(half + 1) * LANES]
        rot = jnp.where(first, pltpu.roll(xh, LANES - HEAD_DIM // 2, 1), pltpu.roll(xh, HEAD_DIM // 2, 1))
        halves.append(xh * cos + rot * sin)
    return jnp.concatenate(halves, axis=1)


def _project_in(xb_ref, w_ref, wm_ref, cos_ref, sin_ref, zd_ref, za_ref, zc_refs, zb_ref, zm_ref, stage_ref):
    tm = xb_ref.shape[0]
    groups = {id(zd_ref): 0, id(za_ref): ZD_W, id(zc_refs): ZD_W + ZA_W, id(zb_ref): ZD_W + ZA_W + ZC_W}
    for ref in (zc_refs, zd_ref, za_ref, zb_ref):
        off = groups[id(ref)]
        width = ZC_W if ref is zc_refs else ref.shape[1]
        for c in range(0, width, 256):
            cw = min(256, width - c)
            z = _dot(xb_ref[...], w_ref[:, off + c:off + c + cw])
            if ref is not zc_refs:
                ref[:, c:c + cw] = z.astype(BF16)
                continue
            if c < 2 * GW:
                z = _rope_heads(z, cos_ref[...], sin_ref[...])
                if c == 0:
                    z = z * (HEAD_DIM ** -0.5 * LOG2_E)
            zc_refs[0][:, c:c + cw] = z.astype(BF16)
            for half in range(GW // LANES):
                stage_ref[c // 256, half] = z[:, half * LANES:(half + 1) * LANES]
            for d, view in zip(DILATIONS[1:], zc_refs[1:]):
                for r in range(d):
                    for half in range(GW // LANES):
                        col = r * ZC_W + c + half * LANES
                        rows = stage_ref[c // 256, half, pl.ds(r, tm // d, stride=d), :]
                        view[:, col:col + LANES] = rows.astype(BF16)
    zm_ref[...] = _dot(xb_ref[...], wm_ref[...])


def _inproj_kernel(x_ref, w_ref, wm_ref, cos_ref, sin_ref, zd_ref, za_ref, zc1_ref, zc4_ref, zc16_ref, zb_ref,
                   zm_ref, xb_ref, stage_ref):
    xb_ref[...] = x_ref[...].astype(BF16)
    _project_in(xb_ref, w_ref, wm_ref, cos_ref, sin_ref, zd_ref, za_ref, (zc1_ref, zc4_ref, zc16_ref), zb_ref,
                zm_ref, stage_ref)


def _z_outputs(n, tm):
    row = lambda w: pl.BlockSpec((tm, w), lambda i: (i, 0))
    view = lambda d: pl.BlockSpec((tm // d, d * ZC_W), lambda i: (i, 0))
    specs = [row(ZD_W), row(ZA_W)] + [view(d) for d in DILATIONS] + [row(ZB_W), row(LANES)]
    shapes = ([jax.ShapeDtypeStruct((n, ZD_W), BF16), jax.ShapeDtypeStruct((n, ZA_W), BF16)]
              + [jax.ShapeDtypeStruct((n // d, d * ZC_W), BF16) for d in DILATIONS]
              + [jax.ShapeDtypeStruct((n, ZB_W), BF16), jax.ShapeDtypeStruct((n, LANES), F32)])
    stage = pltpu.VMEM((ZC_W // 256, GW // LANES, tm, LANES), F32)
    return specs, shapes, stage


def _inproj(x2, w_main, w_misc, cosd, sind, tm):
    n = x2.shape[0]
    row = lambda w: pl.BlockSpec((tm, w), lambda i: (i, 0))
    full = lambda a: pl.BlockSpec(a.shape, lambda i: (0, 0))
    z_specs, z_shapes, stage = _z_outputs(n, tm)
    return pl.pallas_call(
        _inproj_kernel,
        grid=(n // tm,),
        in_specs=[row(D_MODEL), full(w_main), full(w_misc), row(LANES), row(LANES)],
        out_specs=z_specs,
        out_shape=z_shapes,
        scratch_shapes=[pltpu.VMEM((tm, D_MODEL), BF16), stage],
        compiler_params=_params(),
        name="inproj",
    )(x2, w_main, w_misc, cosd, sind)


def _gmlp_kernel(za_ref, lng_ref, lnb_ref, ws_ref, bias_ref, y_ref):
    seq = za_ref.shape[0]
    r_i = lax.broadcasted_iota(jnp.int32, (BLOCK, BLOCK), 0)
    c_i = lax.broadcasted_iota(jnp.int32, (BLOCK, BLOCK), 1)
    w_causal = [jnp.where(c_i <= r_i, ws_ref[h], 0.0).astype(BF16) for h in range(HEADS)]
    lane_head = lax.broadcasted_iota(jnp.int32, (BLOCK, GW), 1) // HEAD_DIM

    def chunk(c, carry):
        r0 = pl.multiple_of(c * BLOCK, BLOCK)
        z = za_ref[pl.ds(r0, BLOCK), :].astype(F32)
        g = 0.5 * z * (1.0 + lax.erf(z * (0.5 ** 0.5)))
        u, v = g[:, :GW], g[:, GW:]
        vb = _layer_norm(v, lng_ref[...], lnb_ref[...]).astype(BF16)
        mixed = jnp.zeros((BLOCK, GW), F32)
        for h in range(HEADS):
            mixed = jnp.where(lane_head == h, _dot(w_causal[h], vb), mixed)
        y_ref[pl.ds(r0, BLOCK), :] = (u * (mixed + bias_ref[...])).astype(BF16)
        return carry

    lax.fori_loop(0, seq // BLOCK, chunk, 0, unroll=4)


def _gmlp(za, ln_g, ln_b, ws, bias, bsz, seq):
    full = lambda a: pl.BlockSpec(a.shape, lambda b: (0,) * a.ndim)
    return pl.pallas_call(
        _gmlp_kernel,
        grid=(bsz,),
        in_specs=[pl.BlockSpec((seq, ZA_W), lambda b: (b, 0)), full(ln_g), full(ln_b), full(ws),
                  full(bias)],
        out_specs=pl.BlockSpec((seq, GW), lambda b: (b, 0)),
        out_shape=jax.ShapeDtypeStruct((bsz * seq, GW), BF16),
        compiler_params=_params(),
        name="gmlp",
    )(za, ln_g, ln_b, ws, bias)


MLA_TQ = 256


def _mla_kernel(zb_ref, zm_ref, cos_ref, sin_ref, qg_ref, kvg_ref, wqm_ref, wqs_ref, wk_ref, wv_ref,
                y_ref, q_s, k_s, vt_s, acc_s, sc_a, sc_b):
    seq = zb_ref.shape[0]
    tq = MLA_TQ
    scale = (NOPE + ROPE) ** -0.5 * LOG2_E
    lane = lax.broadcasted_iota(jnp.int32, (1, LANES), 1)
    in_rope = (lane >= KR_LANE) & (lane < KR_LANE + ROPE)
    rows = 512
    for r in range(0, seq, rows):
        cos = cos_ref[0, r:r + rows, :]
        sin = sin_ref[0, r:r + rows, :]
        cq = zb_ref[r:r + rows, 0:256].astype(F32)
        ms = jnp.sum(cq * cq, axis=-1, keepdims=True) * (1.0 / Q_RANK)
        cqn = (cq * lax.rsqrt(ms + RMS_EPS) * qg_ref[...]).astype(BF16)
        qm = _dot(cqn, wqm_ref[...])
        qs = _dot(cqn, wqs_ref[...])
        ckv = zb_ref[r:r + rows, 256:384].astype(F32)
        ms = jnp.mean(ckv * ckv, axis=-1, keepdims=True)
        ckvn = (ckv * lax.rsqrt(ms + RMS_EPS) * kvg_ref[...]).astype(BF16)
        kn = _dot(ckvn, wk_ref[...])
        v = _dot(ckvn, wv_ref[...])
        for sub in range(rows // tq):
            vt_s[r // tq + sub] = v[sub * tq:(sub + 1) * tq, :].T.astype(BF16)
        zm = zm_ref[r:r + rows, :]
        kr = jnp.where(in_rope, zm * cos + pltpu.roll(zm, LANES - ROPE, 1) * sin, 0.0)
        for h in range(HEADS):
            sl = slice(h * LANES, (h + 1) * LANES)
            q_s[r:r + rows, sl] = ((qm[:, sl] * cos + qs[:, sl] * sin) * scale).astype(BF16)
            k_s[r:r + rows, sl] = (kn[:, sl] + kr).astype(BF16)

    key_i = lax.broadcasted_iota(jnp.int32, (tq, tq), 0)
    qry_i = lax.broadcasted_iota(jnp.int32, (tq, tq), 1)
    heads = range(HEADS)
    head_lanes = [slice(h * LANES, (h + 1) * LANES) for h in heads]

    def qblock(i, carry):
        q0 = pl.multiple_of(i * tq, tq)
        acc_s[...] = jnp.zeros(acc_s.shape, F32)

        def put_scores(j, dst):
            k0 = pl.multiple_of(j * tq, tq)
            for h in heads:
                dst[h] = _dot_nt(k_s[pl.ds(k0, tq), head_lanes[h]], q_s[pl.ds(q0, tq), head_lanes[h]])

        def absorb(j, src, m_old, l_old, masked):
            m_new, l_new, alpha, probs = [], [], [], []
            for h in heads:
                s = src[h]
                if masked:
                    s = jnp.where(key_i <= qry_i, s, NEG_INF)
                m = jnp.maximum(m_old[h], jnp.max(s, axis=0, keepdims=True))
                a = jnp.exp2(m_old[h] - m)
                p = jnp.exp2(s - m)
                m_new.append(m)
                alpha.append(a)
                l_new.append(a * l_old[h] + jnp.sum(p, axis=0, keepdims=True))
                probs.append(p.astype(BF16))
            for h in heads:
                vt = vt_s[j, h * HEAD_DIM:(h + 1) * HEAD_DIM, :]
                acc_s[h] = alpha[h] * acc_s[h] + _dot(vt, probs[h])
            return tuple(m_new), tuple(l_new)

        def on_parity(j, fn, state):
            return lax.cond(j % 2 == 0, lambda st: fn(sc_a, sc_b, st), lambda st: fn(sc_b, sc_a, st), state)

        def step(j, state):
            def run(src, dst, st):
                put_scores(j + 1, dst)
                return absorb(j, src, *st, False)
            return on_parity(j, run, state)

        put_scores(0, sc_a)
        init = (tuple(jnp.full((1, tq), NEG_INF, F32) for _ in heads),
                tuple(jnp.zeros((1, tq), F32) for _ in heads))
        state = lax.fori_loop(0, i, step, init)
        _, l_fin = on_parity(i, lambda src, dst, st: absorb(i, src, *st, True), state)
        out_t = jnp.concatenate([acc_s[h] / l_fin[h] for h in heads], axis=0)
        y_ref[pl.ds(q0, tq), :] = out_t.T.astype(BF16)
        return carry

    lax.fori_loop(0, seq // tq, qblock, 0)


def _mla(zb, zm, cosm, sinm, qg, kvg, wqm, wqs, wk, wv, bsz, seq):
    full = lambda a: pl.BlockSpec(a.shape, lambda b: (0,) * a.ndim)
    tab = pl.BlockSpec((1, seq, LANES), lambda b: (b, 0, 0))
    return pl.pallas_call(
        _mla_kernel,
        grid=(bsz,),
        in_specs=[pl.BlockSpec((seq, ZB_W), lambda b: (b, 0)),
                  pl.BlockSpec((seq, LANES), lambda b: (b, 0)), tab, tab,
                  full(qg), full(kvg), full(wqm), full(wqs), full(wk), full(wv)],
        out_specs=pl.BlockSpec((seq, GW), lambda b: (b, 0)),
        out_shape=jax.ShapeDtypeStruct((bsz * seq, GW), BF16),
        scratch_shapes=[pltpu.VMEM((seq, HEADS * LANES), BF16), pltpu.VMEM((seq, HEADS * LANES), BF16),
                        pltpu.VMEM((seq // MLA_TQ, GW, MLA_TQ), BF16),
                        pltpu.VMEM((HEADS, HEAD_DIM, MLA_TQ), F32),
                        pltpu.VMEM((HEADS, MLA_TQ, MLA_TQ), F32),
                        pltpu.VMEM((HEADS, MLA_TQ, MLA_TQ), F32)],
        compiler_params=_params(),
        name="mla",
    )(zb, zm, cosm, sinm, qg, kvg, wqm, wqs, wk, wv)


DILATIONS = (1, 4, 16)
HALVES = GW // LANES


def _head_columns(cols):
    low_half = lax.broadcasted_iota(jnp.int32, (1, LANES), 1) < HEAD_DIM
    return jnp.concatenate([jnp.where(low_half, cols[0], cols[1]),
                            jnp.where(low_half, cols[2], cols[3])], axis=1)


def _band_block(q, kh, vh, w0, nk, mask):
    pairs = range(HEADS // 2)
    scores = []
    for pr in pairs:
        k_win = jnp.concatenate([kh[2 * pr + e, pl.ds(w0, nk), :] for e in range(2)], axis=0)
        scores.append(_dot_nt(q[:, pr * LANES:(pr + 1) * LANES], k_win))
    m_cols, l_cols, probs = [], [], []
    for h in range(HEADS):
        s = scores[h // 2][:, (h % 2) * nk:(h % 2 + 1) * nk]
        s = jnp.where(mask, s, NEG_INF)
        m = jnp.max(s, axis=-1, keepdims=True)
        p = jnp.exp2(s - m)
        m_cols.append(m)
        l_cols.append(jnp.sum(p, axis=-1, keepdims=True))
        probs.append(p.astype(BF16))
    outs = []
    for pr in pairs:
        v_win = jnp.concatenate([vh[2 * pr + e, pl.ds(w0, nk), :] for e in range(2)], axis=0)
        outs.append(_dot(jnp.concatenate(probs[2 * pr:2 * pr + 2], axis=1), v_win))
    l_full = _head_columns(l_cols)
    o = jnp.concatenate(outs, axis=1) / l_full
    return o, _head_columns(m_cols) + jnp.log2(l_full)


def _dilated_kernel(z1_ref, z4_ref, z16_ref, o1_ref, o4_ref, o16_ref, l1_ref, l4_ref, l16_ref, qb, kh, vh):
    seq = z1_ref.shape[0]
    lane = lax.broadcasted_iota(jnp.int32, (1, LANES), 1)
    views = {1: (z1_ref, o1_ref, l1_ref), 4: (z4_ref, o4_ref, l4_ref), 16: (z16_ref, o16_ref, l16_ref)}

    r_i = lax.broadcasted_iota(jnp.int32, (BLOCK, 2 * BLOCK), 0)
    c_i = lax.broadcasted_iota(jnp.int32, (BLOCK, 2 * BLOCK), 1)
    band = (c_i >= r_i) & (c_i <= r_i + BLOCK)
    causal = (lax.broadcasted_iota(jnp.int32, (BLOCK, BLOCK), 1)
              <= lax.broadcasted_iota(jnp.int32, (BLOCK, BLOCK), 0))
    low_half = lane < HEAD_DIM
    for h in range(HEADS):
        kh[h, 0:BLOCK, :] = jnp.zeros((BLOCK, LANES), BF16)
        vh[h, 0:BLOCK, :] = jnp.zeros((BLOCK, LANES), BF16)

    for d in DILATIONS:
        sub = seq // d
        z_ref, o_ref, l_ref = views[d]
        for res in range(d):
            dst = slice(BLOCK + res * sub, BLOCK + (res + 1) * sub)
            qb[res * sub:(res + 1) * sub, :] = z_ref[:, res * ZC_W:res * ZC_W + GW]
            for half in range(HALVES):
                col = lambda base: slice(res * ZC_W + base + half * LANES,
                                         res * ZC_W + base + (half + 1) * LANES)
                k_half = z_ref[:, col(GW)]
                v_half = z_ref[:, col(2 * GW)]
                zero = jnp.zeros_like(k_half)
                kh[2 * half, dst, :] = jnp.where(low_half, k_half, zero)
                kh[2 * half + 1, dst, :] = jnp.where(low_half, zero, k_half)
                vh[2 * half, dst, :] = jnp.where(low_half, v_half, zero)
                vh[2 * half + 1, dst, :] = jnp.where(low_half, zero, v_half)
        blocks_per_class = sub // BLOCK

        for res in range(d):
            cols = slice(res * GW, (res + 1) * GW)

            def block(n, carry, res=res, cols=cols, o_ref=o_ref, l_ref=l_ref):
                r0 = pl.multiple_of(res * sub + n * BLOCK, BLOCK)
                q = qb[pl.ds(r0, BLOCK), :]
                if blocks_per_class == 1:
                    o, lse = _band_block(q, kh, vh, r0 + BLOCK, BLOCK, causal)
                else:
                    first_key = jnp.where(n > 0, 0, BLOCK)
                    o, lse = _band_block(q, kh, vh, r0, 2 * BLOCK, band & (c_i >= first_key))
                rows = pl.ds(pl.multiple_of(n * BLOCK, BLOCK), BLOCK)
                o_ref[rows, cols] = o.astype(BF16)
                l_ref[rows, cols] = lse
                return carry

            if blocks_per_class == 1:
                block(0, 0)
            else:
                lax.fori_loop(0, blocks_per_class, block, 0, unroll=2)


def _dilated(zc_views, bsz, seq):
    headbuf = pltpu.VMEM((HEADS, seq + BLOCK, LANES), BF16)
    n = bsz * seq
    view = lambda width, d: pl.BlockSpec((seq // d, d * width), lambda b: (b, 0))
    return pl.pallas_call(
        _dilated_kernel,
        grid=(bsz,),
        in_specs=[view(ZC_W, d) for d in DILATIONS],
        out_specs=[view(GW, d) for d in DILATIONS] * 2,
        out_shape=([jax.ShapeDtypeStruct((n // d, d * GW), BF16) for d in DILATIONS]
                   + [jax.ShapeDtypeStruct((n // d, d * GW), F32) for d in DILATIONS]),
        scratch_shapes=[pltpu.VMEM((seq, GW), BF16), headbuf, headbuf],
        compiler_params=_params(),
        name="dilated",
    )(*zc_views)


CONV_W = 4
CONV_PAD = 8


def _log_sigmoid(x):
    return jnp.minimum(x, 0.0) - jnp.log1p(jnp.exp(-jnp.abs(x)))


def _mlstm_kernel(zd_ref, zm_ref, gb_ref, cw_ref, cb_ref, y_ref, xpad, q_s, k_s, kt_s, a_r, b_r, cm_c,
                  b_c, b_f, c_s, n_s):
    seq = zd_ref.shape[0]
    rows = 512
    xpad[0:CONV_PAD, :] = jnp.zeros((CONV_PAD, 2 * GW), F32)
    for r in range(0, seq, rows):
        xpad[CONV_PAD + r:CONV_PAD + r + rows, :] = zd_ref[r:r + rows, 0:2 * GW].astype(F32)
    for r in range(0, seq, rows):
        win = xpad[r:r + rows + CONV_PAD, :]
        acc = cb_ref[...] + cw_ref[CONV_W - 1:CONV_W, :] * win[CONV_PAD:, :]
        for k in range(1, CONV_W):
            acc = acc + cw_ref[CONV_W - 1 - k:CONV_W - k, :] * pltpu.roll(win, k, 0)[CONV_PAD:, :]
        qk = acc * jax.nn.sigmoid(acc)
        k = qk[:, GW:] * (HEAD_DIM ** -0.5)
        q_s[r:r + rows, :] = qk[:, :GW].astype(BF16)
        k_s[r:r + rows, :] = k.astype(BF16)
        for sub in range(rows // BLOCK):
            kt_s[r // BLOCK + sub] = k[sub * BLOCK:(sub + 1) * BLOCK, :].T

    gt = (zm_ref[...] + gb_ref[...]).T[0:8, :]
    f_log = _log_sigmoid(pltpu.roll(gt, HEADS, 0))
    pos_in_chunk = lax.broadcasted_iota(jnp.int32, (8, seq), 1) % BLOCK
    steps = [1 << s for s in range(BLOCK.bit_length() - 1)]
    b = f_log
    for k in steps:
        b = b + jnp.where(pos_in_chunk >= k, pltpu.roll(b, k, 1), 0.0)
    a = gt - b
    cm = a
    for k in steps:
        cm = jnp.maximum(cm, jnp.where(pos_in_chunk >= k, pltpu.roll(cm, k, 1), NEG_INF))
    a_r[...] = a
    b_r[...] = b
    pad = jnp.zeros((LANES - 8, seq), F32)
    cm_c[...] = jnp.concatenate([cm, pad], axis=0).T
    b_c[...] = jnp.concatenate([b, pad], axis=0).T

    c_s[...] = jnp.zeros((GW, GW), F32)
    n_s[...] = jnp.zeros((GW, GW), F32)
    lane_head = lax.broadcasted_iota(jnp.int32, (1, GW), 1) // HEAD_DIM
    same_head = (lax.broadcasted_iota(jnp.int32, (GW, GW), 0) // HEAD_DIM
                 == lax.broadcasted_iota(jnp.int32, (GW, GW), 1) // HEAD_DIM)
    causal = (lax.broadcasted_iota(jnp.int32, (BLOCK, BLOCK), 1)
              <= lax.broadcasted_iota(jnp.int32, (BLOCK, BLOCK), 0))

    heads = range(HEADS)
    low_half = lax.broadcasted_iota(jnp.int32, (1, LANES), 1) < HEAD_DIM
    ones_blk = jnp.ones((BLOCK, LANES), BF16)
    head_sum = (lax.broadcasted_iota(jnp.int32, (HEADS * BLOCK, GW), 0) // BLOCK
                == lax.broadcasted_iota(jnp.int32, (HEADS * BLOCK, GW), 1) // HEAD_DIM).astype(BF16)

    def head_lanes(per_head):
        return jnp.concatenate([jnp.where(low_half, per_head[0], per_head[1]),
                                jnp.where(low_half, per_head[2], per_head[3])], axis=1)

    for r in range(0, seq, rows):
        b_cols = b_c[r:r + rows, :]
        b_f[r:r + rows, :] = head_lanes([b_cols[:, h:h + 1] for h in heads])

    def chunk(c, m_run):
        r0 = pl.multiple_of(c * BLOCK, BLOCK)
        qb = q_s[pl.ds(r0, BLOCK), :]
        kb = k_s[pl.ds(r0, BLOCK), :]
        v = zd_ref[pl.ds(r0, BLOCK), 2 * GW:3 * GW]
        og = zd_ref[pl.ds(r0, BLOCK), 3 * GW:4 * GW].astype(F32)
        a_rows = a_r[:, pl.ds(r0, BLOCK)]
        b_rows = b_r[:, pl.ds(r0, BLOCK)]
        cm_cols = cm_c[pl.ds(r0, BLOCK), :]

        k_heads = jnp.concatenate(
            [jnp.where(lane_head == h, kb, jnp.zeros_like(kb)) for h in heads], axis=0)
        v_heads = jnp.concatenate(
            [jnp.where(lane_head == h, v, jnp.zeros_like(v)) for h in heads], axis=0)
        scores = _dot_nt(qb, k_heads)
        carried = _dot(qb, jnp.concatenate([c_s[...], n_s[...]], axis=1).astype(BF16))

        w_intra, g_rep, wk_rows, decays, m_next = [], [], [], [], []
        for h in heads:
            a_row = a_rows[h:h + 1, :]
            g = jnp.maximum(jnp.broadcast_to(cm_cols[:, h:h + 1], (BLOCK, BLOCK)), m_run[h])
            g_rep.append(g)
            w_intra.append(jnp.where(causal, jnp.exp(a_row - g), 0.0))
            g_end = jnp.maximum(m_run[h], jnp.max(a_row, axis=-1, keepdims=True))
            m_next.append(b_rows[h:h + 1, BLOCK - 1:BLOCK] + g_end)
            decays.append(jnp.exp(m_run[h] - g_end))
            wk_rows.append(jnp.exp(a_row - g_end))
        g_full = head_lanes(g_rep)
        inter_scale = jnp.exp(head_lanes(m_run) - g_full)
        floor = jnp.exp(-(b_f[pl.ds(r0, BLOCK), :] + g_full))

        wk = jnp.concatenate([jnp.broadcast_to(w, (HEAD_DIM, BLOCK)) for w in wk_rows], axis=0)
        decay = jnp.concatenate([jnp.broadcast_to(dd, (HEAD_DIM, 1)) for dd in decays], axis=0)
        ktw = kt_s[c] * wk
        update = _dot(ktw.astype(BF16), jnp.concatenate([v, ones_blk], axis=1))

        sqk = (scores * jnp.concatenate(w_intra, axis=1)).astype(BF16)
        intra = _dot(sqk, jnp.concatenate([v_heads, head_sum], axis=1))
        num = intra[:, :GW] + inter_scale * carried[:, :GW]
        den = intra[:, GW:] + inter_scale * carried[:, GW:]
        h_out = num / jnp.maximum(jnp.abs(den), floor)
        y_ref[pl.ds(r0, BLOCK), :] = (jax.nn.sigmoid(og) * h_out).astype(BF16)

        k_sum = update[:, GW:]
        c_s[...] = decay * c_s[...] + jnp.where(same_head, update[:, :GW], 0.0)
        n_s[...] = decay * n_s[...] + jnp.where(same_head, jnp.concatenate([k_sum, k_sum], axis=1), 0.0)
        return tuple(m_next)

    lax.fori_loop(0, seq // BLOCK, chunk, tuple(jnp.zeros((1, 1), F32) for _ in range(HEADS)))


def _mlstm(zd, zm, gb, cw, cb, bsz, seq):
    full = lambda a: pl.BlockSpec(a.shape, lambda b: (0,) * a.ndim)
    return pl.pallas_call(
        _mlstm_kernel,
        grid=(bsz,),
        in_specs=[pl.BlockSpec((seq, ZD_W), lambda b: (b, 0)),
                  pl.BlockSpec((seq, LANES), lambda b: (b, 0)), full(gb), full(cw), full(cb)],
        out_specs=pl.BlockSpec((seq, GW), lambda b: (b, 0)),
        out_shape=jax.ShapeDtypeStruct((bsz * seq, GW), BF16),
        scratch_shapes=[pltpu.VMEM((seq + CONV_PAD, 2 * GW), F32),
                        pltpu.VMEM((seq, GW), BF16), pltpu.VMEM((seq, GW), BF16),
                        pltpu.VMEM((seq // BLOCK, GW, BLOCK), F32),
                        pltpu.VMEM((8, seq), F32), pltpu.VMEM((8, seq), F32),
                        pltpu.VMEM((seq, LANES), F32), pltpu.VMEM((seq, LANES), F32),
                        pltpu.VMEM((seq, GW), F32),
                        pltpu.VMEM((GW, GW), F32), pltpu.VMEM((GW, GW), F32)],
        compiler_params=_params(),
        name="mlstm",
    )(zd, zm, gb, cw, cb)


FF_CHUNK = 256


def _dense_kernel(with_next, ya_ref, yb_ref, yd_ref, oc1_ref, oc4_ref, oc16_ref, lc1_ref, lc4_ref, lc16_ref,
                  x_ref, wo_ref, g1_ref, b1_ref, wg_ref, wu_ref, wd_ref, g2_ref, b2_ref, *rest):
    if with_next:
        (w_ref, wm_ref, cos_ref, sin_ref, o_ref, zd_ref, za_ref, zc1_ref, zc4_ref, zc16_ref, zb_ref, zm_ref,
         xb_ref, acc_ref, nat_ref, stage_ref) = rest
    else:
        o_ref, xb_ref, acc_ref, nat_ref = rest
    tm = x_ref.shape[0]

    def natural(view, d, slot):
        if d == 1:
            return view[...].astype(F32)
        for r in range(d):
            for half in range(GW // LANES):
                col = r * GW + half * LANES
                nat_ref[slot, half, pl.ds(r, tm // d, stride=d), :] = view[:, col:col + LANES].astype(F32)
        return jnp.concatenate([nat_ref[slot, half] for half in range(GW // LANES)], axis=1)

    views = (oc1_ref, oc4_ref, oc16_ref, lc1_ref, lc4_ref, lc16_ref)
    slots = iter(range(nat_ref.shape[0]))
    nat = [natural(v, d, next(slots) if d > 1 else None) for v, d in zip(views, DILATIONS * 2)]
    outs, lse = nat[:3], nat[3:]
    top = jnp.maximum(jnp.maximum(lse[0], lse[1]), lse[2])
    wts = [jnp.exp2(l - top) for l in lse]
    mix = wts[0] * outs[0] + wts[1] * outs[1] + wts[2] * outs[2]
    yc = (mix / (wts[0] + wts[1] + wts[2])).astype(BF16)
    acc = ALPHA * x_ref[...]
    for i, y in ((0, ya_ref[...]), (1, yb_ref[...]), (3, yd_ref[...]), (2, yc)):
        acc = acc + _dot(y, wo_ref[i * GW:(i + 1) * GW, :])
    x1 = _layer_norm(acc, g1_ref[...], b1_ref[...])
    xb_ref[...] = x1.astype(BF16)
    acc_ref[...] = ALPHA * x1
    for c in range(0, D_FF, FF_CHUNK):
        gate = _dot(xb_ref[...], wg_ref[:, c:c + FF_CHUNK])
        up = _dot(xb_ref[...], wu_ref[:, c:c + FF_CHUNK])
        act = (gate * jax.nn.sigmoid(gate) * up).astype(BF16)
        acc_ref[...] += _dot(act, wd_ref[c:c + FF_CHUNK, :])
    x2 = _layer_norm(acc_ref[...], g2_ref[...], b2_ref[...])
    o_ref[...] = x2
    if with_next:
        xb_ref[...] = x2.astype(BF16)
        _project_in(xb_ref, w_ref, wm_ref, cos_ref, sin_ref, zd_ref, za_ref, (zc1_ref, zc4_ref, zc16_ref),
                    zb_ref, zm_ref, stage_ref)


def _dense(ys, dil, x2, wo, g1, b1, wg, wu, wd, g2, b2, next_w, tables, tm):
    n = x2.shape[0]
    row = lambda w_: pl.BlockSpec((tm, w_), lambda i: (i, 0))
    view = lambda d: pl.BlockSpec((tm // d, d * GW), lambda i: (i, 0))
    once = lambda a: pl.BlockSpec(a.shape, lambda i: (0, 0), pipeline_mode=pl.Buffered(1))
    weights = [wo, g1, b1, wg, wu, wd, g2, b2] + list(next_w)
    tables = list(tables) if next_w else []
    out_specs = [row(D_MODEL)]
    out_shape = [jax.ShapeDtypeStruct((n, D_MODEL), F32)]
    scratch = [pltpu.VMEM((tm, D_MODEL), BF16), pltpu.VMEM((tm, D_MODEL), F32),
               pltpu.VMEM((2 * (len(DILATIONS) - 1), GW // LANES, tm, LANES), F32)]
    if next_w:
        z_specs, z_shapes, stage = _z_outputs(n, tm)
        out_specs += z_specs
        out_shape += z_shapes
        scratch.append(stage)
    return pl.pallas_call(
        functools.partial(_dense_kernel, bool(next_w)),
        grid=(n // tm,),
        in_specs=([row(GW)] * len(ys) + [view(d) for d in DILATIONS] * 2 + [row(D_MODEL)]
                  + [once(w) for w in weights] + [row(LANES)] * len(tables)),
        out_specs=out_specs,
        out_shape=out_shape,
        scratch_shapes=scratch,
        compiler_params=_params(),
        name="dense_block",
    )(*ys, *dil, x2, *weights, *tables)


def _pack_weights(w_in, b_w_uq, b_w_ukv, b_q_norm, a_bs, d_igate_b, d_fgate_b):
    nl = w_in.shape[0]
    o_b = ZA_W
    o_c = o_b + Q_RANK + KV_RANK + ROPE
    o_d = o_c + ZC_W
    o_g = o_d + ZD_W
    w_a = w_in[..., :o_b]
    w_cq = w_in[..., o_b:o_b + Q_RANK]
    w_ckv = w_in[..., o_b + Q_RANK:o_b + Q_RANK + KV_RANK]
    w_kr = w_in[..., o_b + Q_RANK + KV_RANK:o_c]
    w_c = w_in[..., o_c:o_d]
    w_d = w_in[..., o_d:o_g]
    w_gates = w_in[..., o_g:]
    zeros = lambda *s: jnp.zeros((nl,) + s, F32)
    w_b = jnp.concatenate([w_cq, zeros(D_MODEL, 256 - Q_RANK), w_ckv], -1)
    w_main = jnp.concatenate([w_d, w_a, w_c, w_b], -1).astype(BF16)
    half = ROPE // 2
    w_kr_rot = jnp.concatenate([-w_kr[..., half:], w_kr[..., :half]], -1)
    w_misc = jnp.concatenate(
        [w_gates, zeros(D_MODEL, KR_LANE - 2 * HEADS), w_kr, w_kr_rot], -1).astype(BF16)

    wq = b_w_uq.reshape(nl, Q_RANK, HEADS, NOPE + ROPE)
    nope, x1, x2 = wq[..., :NOPE], wq[..., NOPE:NOPE + half], wq[..., NOPE + half:]
    tail = zeros(Q_RANK, HEADS, LANES - NOPE - ROPE)
    rowpad = ((0, 0), (0, 256 - Q_RANK), (0, 0))
    wqm = jnp.pad(jnp.concatenate([nope, x1, x2, tail], -1).reshape(nl, Q_RANK, HEADS * LANES), rowpad)
    wqs = jnp.pad(jnp.concatenate([zeros(Q_RANK, HEADS, NOPE), -x2, x1, tail], -1)
                  .reshape(nl, Q_RANK, HEADS * LANES), rowpad)
    wkv = b_w_ukv.reshape(nl, KV_RANK, HEADS, NOPE + HEAD_DIM)
    wk = jnp.concatenate([wkv[..., :NOPE], zeros(KV_RANK, HEADS, LANES - NOPE)], -1)
    wk = wk.reshape(nl, KV_RANK, HEADS * LANES)
    wv = wkv[..., NOPE:].reshape(nl, KV_RANK, GW)
    qg = jnp.pad(b_q_norm, ((0, 0), (0, 256 - Q_RANK)))[:, None, :]
    bias = jnp.repeat(jnp.swapaxes(a_bs, 1, 2), HEAD_DIM, axis=-1)
    gb = jnp.concatenate([d_igate_b, d_fgate_b, zeros(LANES - 2 * HEADS)], -1)[:, None, :]
    return dict(w_main=w_main, w_misc=w_misc, wqm=wqm.astype(BF16), wqs=wqs.astype(BF16),
                wk=wk.astype(BF16), wv=wv.astype(BF16), qg=qg, bias=bias, gb=gb)


@jax.jit
def _forward(x, positions, w_in, a_ln_g, a_ln_b, a_ws, a_bs, b_q_norm, b_kv_norm, b_w_uq, b_w_ukv,
             d_conv_w, d_conv_b, d_igate_b, d_fgate_b, w_out, ln1_g, ln1_b, w_gate, w_up, w_down,
             ln2_g, ln2_b):
    bsz, seq, _ = x.shape
    pk = _pack_weights(w_in, b_w_uq, b_w_ukv, b_q_norm, a_bs, d_igate_b, d_fgate_b)
    w_out_b, w_gate_b, w_up_b, w_down_b = (w.astype(BF16) for w in (w_out, w_gate, w_up, w_down))
    cosd, sind, cosm, sinm = _rope_tables(positions)
    tm = 512
    x2 = x.reshape(bsz * seq, D_MODEL)
    row = lambda a, l: a[l][None, :]
    rope_d = (cosd.reshape(bsz * seq, LANES), sind.reshape(bsz * seq, LANES))
    zd, za, *zc, zb, zm = _inproj(x2, pk["w_main"][0], pk["w_misc"][0], *rope_d, tm)
    for l in range(DEPTH):
        ya = _gmlp(za, row(a_ln_g, l), row(a_ln_b, l), a_ws[l], pk["bias"][l], bsz, seq)
        yb = _mla(zb, zm, cosm, sinm, pk["qg"][l], row(b_kv_norm, l), pk["wqm"][l], pk["wqs"][l],
                  pk["wk"][l], pk["wv"][l], bsz, seq)
        dil = _dilated(zc, bsz, seq)
        yd = _mlstm(zd, zm, pk["gb"][l], d_conv_w[l], row(d_conv_b, l), bsz, seq)
        next_w = (pk["w_main"][l + 1], pk["w_misc"][l + 1]) if l + 1 < DEPTH else ()
        outs = _dense((ya, yb, yd), dil, x2, w_out_b[l], row(ln1_g, l), row(ln1_b, l), w_gate_b[l],
                      w_up_b[l], w_down_b[l], row(ln2_g, l), row(ln2_b, l), next_w, rope_d, tm)
        if next_w:
            x2, zd, za, *zc, zb, zm = outs
        else:
            x2 = outs[0]
    return x2.reshape(bsz, seq, D_MODEL)


def kernel(x, positions, w_in, a_ln_g, a_ln_b, a_ws, a_bs, b_q_norm, b_kv_norm, b_w_uq, b_w_ukv,
           d_conv_w, d_conv_b, d_igate_b, d_fgate_b, w_out, ln1_g, ln1_b, w_gate, w_up, w_down,
           ln2_g, ln2_b):
    return _forward(x, positions, w_in, a_ln_g, a_ln_b, a_ws, a_bs, b_q_norm, b_kv_norm, b_w_uq,
                    b_w_ukv, d_conv_w, d_conv_b, d_igate_b, d_fgate_b, w_out, ln1_g, ln1_b, w_gate,
                    w_up, w_down, ln2_g, ln2_b)
```

```python
import functools

import jax
import jax.numpy as jnp
from jax import lax
from jax.experimental import pallas as pl
from jax.experimental.pallas import tpu as pltpu

F32 = jnp.float32
BF16 = jnp.bfloat16

D_MODEL = 1024
DEPTH = 4
HEAD_DIM = 64
HEADS = 4
GW = HEADS * HEAD_DIM
BLOCK = 128
Q_RANK = 192
KV_RANK = 128
NOPE = 64
ROPE = 32
D_FF = 2816
ROPE_THETA = 10000.0
LN_EPS = 1e-5
RMS_EPS = 1e-6
ALPHA = (2 * DEPTH) ** 0.25
LANES = 128
NEG_INF = float("-inf")
LOG2_E = 1.4426950408889634
LN_2 = 0.6931471805599453

ZD_W, ZA_W, ZC_W, ZB_W = 4 * GW, 2 * GW, 3 * GW, 384
ZMAIN_W = ZD_W + ZA_W + ZC_W + ZB_W
KR_LANE = 64

VMEM_LIMIT = 60000 * 1024


def _dot(a, b):
    return jnp.dot(a, b, preferred_element_type=F32)


def _dot_nt(a, b):
    return lax.dot_general(a, b, (((1,), (1,)), ((), ())), preferred_element_type=F32)


def _params(n_axes=1):
    return pltpu.CompilerParams(
        dimension_semantics=("arbitrary",) * n_axes, vmem_limit_bytes=VMEM_LIMIT)


def _layer_norm(r, g, b):
    mu = jnp.mean(r, axis=-1, keepdims=True)
    d = r - mu
    var = jnp.mean(d * d, axis=-1, keepdims=True)
    return d * lax.rsqrt(var + LN_EPS) * g + b


def _tables_kernel(pos_ref, invd_ref, invm_ref, cosd_ref, sind_ref, cosm_ref, sinm_ref):
    pos = pos_ref[0].astype(F32)
    lane = lax.broadcasted_iota(jnp.int32, (1, LANES), 1)
    angd = pos * invd_ref[...]
    cosd_ref[0] = jnp.cos(angd)
    sind_ref[0] = jnp.sin(angd) * jnp.where((lane % HEAD_DIM) < HEAD_DIM // 2, -1.0, 1.0)
    angm = pos * invm_ref[...]
    in_rope = (lane >= KR_LANE) & (lane < KR_LANE + ROPE)
    cosm_ref[0] = jnp.where(in_rope, jnp.cos(angm), jnp.where(lane < KR_LANE, 1.0, 0.0))
    sinm_ref[0] = jnp.where(in_rope, jnp.sin(angm), 0.0)


def _rope_tables(positions):
    bsz, seq = positions.shape
    lane = jnp.arange(LANES)
    half_d = HEAD_DIM // 2
    inv_d = jnp.power(ROPE_THETA, -jnp.arange(half_d, dtype=F32) / half_d)
    half_m = ROPE // 2
    inv_m = jnp.power(ROPE_THETA, -jnp.arange(half_m, dtype=F32) / half_m)
    invd = inv_d[lane % half_d][None, :]
    invm = inv_m[lane % half_m][None, :]
    tab = jax.ShapeDtypeStruct((bsz, seq, LANES), F32)
    spec = pl.BlockSpec((1, seq, LANES), lambda b: (b, 0, 0))
    cspec = pl.BlockSpec((1, LANES), lambda b: (0, 0))
    return pl.pallas_call(
        _tables_kernel,
        grid=(bsz,),
        in_specs=[pl.BlockSpec((1, seq, 1), lambda b: (b, 0, 0)), cspec, cspec],
        out_specs=[spec] * 4,
        out_shape=[tab] * 4,
        compiler_params=_params(),
        name="rope_tables",
    )(positions.reshape(bsz, seq, 1), invd, invm)


def _rope_heads(x, cos, sin):
    first = (lax.broadcasted_iota(jnp.int32, (1, LANES), 1) % HEAD_DIM) < HEAD_DIM // 2
    halves = []
    for half in range(GW // LANES):
        xh = x[:, half * LANES:(half + 1) * LANES]
        rot = jnp.where(first, pltpu.roll(xh, LANES - HEAD_DIM // 2, 1), pltpu.roll(xh, HEAD_DIM // 2, 1))
        halves.append(xh * cos + rot * sin)
    return jnp.concatenate(halves, axis=1)


def _project_in(xb_ref, w_ref, wm_ref, cos_ref, sin_ref, zd_ref, za_ref, zc_refs, zb_ref, zm_ref, stage_ref,
                r0, nrows):
    rs = slice(r0, r0 + nrows)
    groups = {id(zd_ref): 0, id(za_ref): ZD_W, id(zc_refs): ZD_W + ZA_W, id(zb_ref): ZD_W + ZA_W + ZC_W}
    for ref in (zc_refs, zd_ref, za_ref, zb_ref):
        off = groups[id(ref)]
        width = ZC_W if ref is zc_refs else ref.shape[1]
        for c in range(0, width, 256):
            cw = min(256, width - c)
            z = _dot(xb_ref[rs, :], w_ref[:, off + c:off + c + cw])
            if ref is not zc_refs:
                ref[rs, c:c + cw] = z.astype(BF16)
                continue
            if c < 2 * GW:
                z = _rope_heads(z, cos_ref[rs, :], sin_ref[rs, :])
                if c == 0:
                    z = z * (HEAD_DIM ** -0.5 * LOG2_E)
            zc_refs[0][rs, c:c + cw] = z.astype(BF16)
            for half in range(GW // LANES):
                stage_ref[c // 256, half, rs, :] = z[:, half * LANES:(half + 1) * LANES]
            for d, view in zip(DILATIONS[1:], zc_refs[1:]):
                for r in range(d):
                    for half in range(GW // LANES):
                        col = r * ZC_W + c + half * LANES
                        rows = stage_ref[c // 256, half, pl.ds(r0 + r, nrows // d, stride=d), :]
                        view[r0 // d:(r0 + nrows) // d, col:col + LANES] = rows.astype(BF16)
    zm_ref[rs, :] = _dot(xb_ref[rs, :], wm_ref[...])


def _inproj_kernel(x_ref, w_ref, wm_ref, cos_ref, sin_ref, zd_ref, za_ref, zc1_ref, zc4_ref, zc16_ref, zb_ref,
                   zm_ref, xb_ref, stage_ref):
    xb_ref[...] = x_ref[...].astype(BF16)
    _project_in(xb_ref, w_ref, wm_ref, cos_ref, sin_ref, zd_ref, za_ref, (zc1_ref, zc4_ref, zc16_ref),
                zb_ref, zm_ref, stage_ref, 0, x_ref.shape[0])


def _z_outputs(n, tm):
    row = lambda w: pl.BlockSpec((tm, w), lambda i: (i, 0))
    view = lambda d: pl.BlockSpec((tm // d, d * ZC_W), lambda i: (i, 0))
    specs = [row(ZD_W), row(ZA_W)] + [view(d) for d in DILATIONS] + [row(ZB_W), row(LANES)]
    shapes = ([jax.ShapeDtypeStruct((n, ZD_W), BF16), jax.ShapeDtypeStruct((n, ZA_W), BF16)]
              + [jax.ShapeDtypeStruct((n // d, d * ZC_W), BF16) for d in DILATIONS]
              + [jax.ShapeDtypeStruct((n, ZB_W), BF16), jax.ShapeDtypeStruct((n, LANES), F32)])
    stage = pltpu.VMEM((ZC_W // 256, GW // LANES, tm, LANES), F32)
    return specs, shapes, stage


def _inproj(x2, w_main, w_misc, cosd, sind, tm):
    n = x2.shape[0]
    row = lambda w: pl.BlockSpec((tm, w), lambda i: (i, 0))
    full = lambda a: pl.BlockSpec(a.shape, lambda i: (0, 0))
    z_specs, z_shapes, stage = _z_outputs(n, tm)
    return pl.pallas_call(
        _inproj_kernel,
        grid=(n // tm,),
        in_specs=[row(D_MODEL), full(w_main), full(w_misc), row(LANES), row(LANES)],
        out_specs=z_specs,
        out_shape=z_shapes,
        scratch_shapes=[pltpu.VMEM((tm, D_MODEL), BF16), stage],
        compiler_params=_params(),
        name="inproj",
    )(x2, w_main, w_misc, cosd, sind)


def _gmlp_kernel(za_ref, lng_ref, lnb_ref, ws_ref, bias_ref, y_ref):
    seq = za_ref.shape[0]
    r_i = lax.broadcasted_iota(jnp.int32, (BLOCK, BLOCK), 0)
    c_i = lax.broadcasted_iota(jnp.int32, (BLOCK, BLOCK), 1)
    w_causal = [jnp.where(c_i <= r_i, ws_ref[h], 0.0).astype(BF16) for h in range(HEADS)]
    lane_head = lax.broadcasted_iota(jnp.int32, (BLOCK, GW), 1) // HEAD_DIM

    def chunk(c, carry):
        r0 = pl.multiple_of(c * BLOCK, BLOCK)
        z = za_ref[pl.ds(r0, BLOCK), :].astype(F32)
        g = 0.5 * z * (1.0 + lax.erf(z * (0.5 ** 0.5)))
        u, v = g[:, :GW], g[:, GW:]
        vb = _layer_norm(v, lng_ref[...], lnb_ref[...]).astype(BF16)
        mixed = jnp.zeros((BLOCK, GW), F32)
        for h in range(HEADS):
            mixed = jnp.where(lane_head == h, _dot(w_causal[h], vb), mixed)
        y_ref[pl.ds(r0, BLOCK), :] = (u * (mixed + bias_ref[...])).astype(BF16)
        return carry

    lax.fori_loop(0, seq // BLOCK, chunk, 0, unroll=4)


def _gmlp(za, ln_g, ln_b, ws, bias, bsz, seq):
    full = lambda a: pl.BlockSpec(a.shape, lambda b: (0,) * a.ndim)
    return pl.pallas_call(
        _gmlp_kernel,
        grid=(bsz,),
        in_specs=[pl.BlockSpec((seq, ZA_W), lambda b: (b, 0)), full(ln_g), full(ln_b), full(ws),
                  full(bias)],
        out_specs=pl.BlockSpec((seq, GW), lambda b: (b, 0)),
        out_shape=jax.ShapeDtypeStruct((bsz * seq, GW), BF16),
        compiler_params=_params(),
        name="gmlp",
    )(za, ln_g, ln_b, ws, bias)


MLA_TQ = 256


def _mla_kernel(zb_ref, zm_ref, cos_ref, sin_ref, qg_ref, kvg_ref, wqm_ref, wqs_ref, wk_ref, wv_ref,
                y_ref, q_s, k_s, vt_s, acc_s, sc_a, sc_b, p_a, p_b):
    seq = zb_ref.shape[0]
    tq = MLA_TQ
    scale = (NOPE + ROPE) ** -0.5 * LOG2_E
    lane = lax.broadcasted_iota(jnp.int32, (1, LANES), 1)
    in_rope = (lane >= KR_LANE) & (lane < KR_LANE + ROPE)
    rows = 512
    for r in range(0, seq, rows):
        cos = cos_ref[0, r:r + rows, :]
        sin = sin_ref[0, r:r + rows, :]
        cq = zb_ref[r:r + rows, 0:256].astype(F32)
        ms = jnp.sum(cq * cq, axis=-1, keepdims=True) * (1.0 / Q_RANK)
        cqn = (cq * lax.rsqrt(ms + RMS_EPS) * qg_ref[...]).astype(BF16)
        qm = _dot(cqn, wqm_ref[...])
        qs = _dot(cqn, wqs_ref[...])
        ckv = zb_ref[r:r + rows, 256:384].astype(F32)
        ms = jnp.mean(ckv * ckv, axis=-1, keepdims=True)
        ckvn = (ckv * lax.rsqrt(ms + RMS_EPS) * kvg_ref[...]).astype(BF16)
        kn = _dot(ckvn, wk_ref[...])
        v = _dot(ckvn, wv_ref[...])
        for sub in range(rows // tq):
            vt_s[r // tq + sub] = v[sub * tq:(sub + 1) * tq, :].T.astype(BF16)
        zm = zm_ref[r:r + rows, :]
        kr = jnp.where(in_rope, zm * cos + pltpu.roll(zm, LANES - ROPE, 1) * sin, 0.0)
        for h in range(HEADS):
            sl = slice(h * LANES, (h + 1) * LANES)
            q_s[r:r + rows, sl] = ((qm[:, sl] * cos + qs[:, sl] * sin) * scale).astype(BF16)
            k_s[r:r + rows, sl] = (kn[:, sl] + kr).astype(BF16)

    key_i = lax.broadcasted_iota(jnp.int32, (tq, tq), 0)
    qry_i = lax.broadcasted_iota(jnp.int32, (tq, tq), 1)
    heads = range(HEADS)
    head_lanes = [slice(h * LANES, (h + 1) * LANES) for h in heads]

    def qblock(i, carry):
        q0 = pl.multiple_of(i * tq, tq)
        acc_s[...] = jnp.zeros(acc_s.shape, F32)

        def put_scores(j, dst):
            k0 = pl.multiple_of(j * tq, tq)
            for h in heads:
                dst[h] = _dot_nt(k_s[pl.ds(k0, tq), head_lanes[h]], q_s[pl.ds(q0, tq), head_lanes[h]])

        def softmax(src, p_dst, m_old, l_old, masked):
            m_new, l_new, alpha = [], [], []
            for h in heads:
                s = src[h]
                if masked:
                    s = jnp.where(key_i <= qry_i, s, NEG_INF)
                m = jnp.maximum(m_old[h], jnp.max(s, axis=0, keepdims=True))
                a = jnp.exp2(m_old[h] - m)
                p = jnp.exp2(s - m)
                m_new.append(m)
                alpha.append(a)
                l_new.append(a * l_old[h] + jnp.sum(p, axis=0, keepdims=True))
                p_dst[h] = p.astype(BF16)
            return tuple(m_new), tuple(l_new), tuple(alpha)

        def add_values(j, p_src, alpha):
            for h in heads:
                vt = vt_s[j, h * HEAD_DIM:(h + 1) * HEAD_DIM, :]
                acc_s[h] = alpha[h] * acc_s[h] + _dot(vt, p_src[h])

        even, odd = (sc_a, p_a), (sc_b, p_b)

        def on_parity(j, fn, state):
            return lax.cond(j % 2 == 0, lambda st: fn(even, odd, st), lambda st: fn(odd, even, st), state)

        def step(j, state):
            def run(cur, other, st):
                m_old, l_old, alpha_prev = st
                put_scores(j + 1, other[0])
                m_new, l_new, alpha = softmax(cur[0], cur[1], m_old, l_old, False)
                add_values(jnp.maximum(j - 1, 0), other[1], alpha_prev)
                return m_new, l_new, alpha
            return on_parity(j, run, state)

        def finish(cur, other, st):
            m_old, l_old, alpha_prev = st
            _, l_new, alpha = softmax(cur[0], cur[1], m_old, l_old, True)
            add_values(jnp.maximum(i - 1, 0), other[1], alpha_prev)
            add_values(i, cur[1], alpha)
            return l_new

        put_scores(0, sc_a)
        p_b[...] = jnp.zeros(p_b.shape, BF16)
        init = (tuple(jnp.full((1, tq), NEG_INF, F32) for _ in heads),
                tuple(jnp.zeros((1, tq), F32) for _ in heads),
                tuple(jnp.ones((1, tq), F32) for _ in heads))
        state = lax.fori_loop(0, i, step, init)
        l_fin = on_parity(i, finish, state)
        out_t = jnp.concatenate([acc_s[h] / l_fin[h] for h in heads], axis=0)
        y_ref[pl.ds(q0, tq), :] = out_t.T.astype(BF16)
        return carry

    lax.fori_loop(0, seq // tq, qblock, 0)


def _mla(zb, zm, cosm, sinm, qg, kvg, wqm, wqs, wk, wv, bsz, seq):
    full = lambda a: pl.BlockSpec(a.shape, lambda b: (0,) * a.ndim)
    tab = pl.BlockSpec((1, seq, LANES), lambda b: (b, 0, 0))
    return pl.pallas_call(
        _mla_kernel,
        grid=(bsz,),
        in_specs=[pl.BlockSpec((seq, ZB_W), lambda b: (b, 0)),
                  pl.BlockSpec((seq, LANES), lambda b: (b, 0)), tab, tab,
                  full(qg), full(kvg), full(wqm), full(wqs), full(wk), full(wv)],
        out_specs=pl.BlockSpec((seq, GW), lambda b: (b, 0)),
        out_shape=jax.ShapeDtypeStruct((bsz * seq, GW), BF16),
        scratch_shapes=[pltpu.VMEM((seq, HEADS * LANES), BF16), pltpu.VMEM((seq, HEADS * LANES), BF16),
                        pltpu.VMEM((seq // MLA_TQ, GW, MLA_TQ), BF16),
                        pltpu.VMEM((HEADS, HEAD_DIM, MLA_TQ), F32),
                        pltpu.VMEM((HEADS, MLA_TQ, MLA_TQ), F32),
                        pltpu.VMEM((HEADS, MLA_TQ, MLA_TQ), F32),
                        pltpu.VMEM((HEADS, MLA_TQ, MLA_TQ), BF16),
                        pltpu.VMEM((HEADS, MLA_TQ, MLA_TQ), BF16)],
        compiler_params=_params(),
        name="mla",
    )(zb, zm, cosm, sinm, qg, kvg, wqm, wqs, wk, wv)


DILATIONS = (1, 4, 16)
HALVES = GW // LANES


def _head_columns(cols):
    low_half = lax.broadcasted_iota(jnp.int32, (1, LANES), 1) < HEAD_DIM
    return jnp.concatenate([jnp.where(low_half, cols[0], cols[1]),
                            jnp.where(low_half, cols[2], cols[3])], axis=1)


def _band_block(q, kh, vh, w0, nk, mask):
    pairs = range(HEADS // 2)
    scores = []
    for pr in pairs:
        k_win = jnp.concatenate([kh[2 * pr + e, pl.ds(w0, nk), :] for e in range(2)], axis=0)
        scores.append(_dot_nt(q[:, pr * LANES:(pr + 1) * LANES], k_win))
    m_cols, l_cols, probs = [], [], []
    for h in range(HEADS):
        s = scores[h // 2][:, (h % 2) * nk:(h % 2 + 1) * nk]
        s = jnp.where(mask, s, NEG_INF)
        m = jnp.max(s, axis=-1, keepdims=True)
        p = jnp.exp2(s - m)
        m_cols.append(m)
        l_cols.append(jnp.sum(p, axis=-1, keepdims=True))
        probs.append(p.astype(BF16))
    outs = []
    for pr in pairs:
        v_win = jnp.concatenate([vh[2 * pr + e, pl.ds(w0, nk), :] for e in range(2)], axis=0)
        outs.append(_dot(jnp.concatenate(probs[2 * pr:2 * pr + 2], axis=1), v_win))
    l_full = _head_columns(l_cols)
    o = jnp.concatenate(outs, axis=1) / l_full
    return o, _head_columns(m_cols) + jnp.log2(l_full)


def _dilated_kernel(z1_ref, z4_ref, z16_ref, o1_ref, o4_ref, o16_ref, l1_ref, l4_ref, l16_ref, qb, kh, vh):
    seq = z1_ref.shape[0]
    lane = lax.broadcasted_iota(jnp.int32, (1, LANES), 1)
    views = {1: (z1_ref, o1_ref, l1_ref), 4: (z4_ref, o4_ref, l4_ref), 16: (z16_ref, o16_ref, l16_ref)}

    r_i = lax.broadcasted_iota(jnp.int32, (BLOCK, 2 * BLOCK), 0)
    c_i = lax.broadcasted_iota(jnp.int32, (BLOCK, 2 * BLOCK), 1)
    band = (c_i >= r_i) & (c_i <= r_i + BLOCK)
    causal = (lax.broadcasted_iota(jnp.int32, (BLOCK, BLOCK), 1)
              <= lax.broadcasted_iota(jnp.int32, (BLOCK, BLOCK), 0))
    low_half = lane < HEAD_DIM
    for h in range(HEADS):
        kh[h, 0:BLOCK, :] = jnp.zeros((BLOCK, LANES), BF16)
        vh[h, 0:BLOCK, :] = jnp.zeros((BLOCK, LANES), BF16)

    for d in DILATIONS:
        sub = seq // d
        z_ref, o_ref, l_ref = views[d]
        for res in range(d):
            dst = slice(BLOCK + res * sub, BLOCK + (res + 1) * sub)
            qb[res * sub:(res + 1) * sub, :] = z_ref[:, res * ZC_W:res * ZC_W + GW]
            for half in range(HALVES):
                col = lambda base: slice(res * ZC_W + base + half * LANES,
                                         res * ZC_W + base + (half + 1) * LANES)
                k_half = z_ref[:, col(GW)]
                v_half = z_ref[:, col(2 * GW)]
                zero = jnp.zeros_like(k_half)
                kh[2 * half, dst, :] = jnp.where(low_half, k_half, zero)
                kh[2 * half + 1, dst, :] = jnp.where(low_half, zero, k_half)
                vh[2 * half, dst, :] = jnp.where(low_half, v_half, zero)
                vh[2 * half + 1, dst, :] = jnp.where(low_half, zero, v_half)
        blocks_per_class = sub // BLOCK

        for res in range(d):
            cols = slice(res * GW, (res + 1) * GW)

            def block(n, carry, res=res, cols=cols, o_ref=o_ref, l_ref=l_ref):
                r0 = pl.multiple_of(res * sub + n * BLOCK, BLOCK)
                q = qb[pl.ds(r0, BLOCK), :]
                if blocks_per_class == 1:
                    o, lse = _band_block(q, kh, vh, r0 + BLOCK, BLOCK, causal)
                else:
                    first_key = jnp.where(n > 0, 0, BLOCK)
                    o, lse = _band_block(q, kh, vh, r0, 2 * BLOCK, band & (c_i >= first_key))
                rows = pl.ds(pl.multiple_of(n * BLOCK, BLOCK), BLOCK)
                o_ref[rows, cols] = o.astype(BF16)
                l_ref[rows, cols] = lse
                return carry

            if blocks_per_class == 1:
                block(0, 0)
            else:
                lax.fori_loop(0, blocks_per_class, block, 0, unroll=8)


def _dilated(zc_views, bsz, seq):
    headbuf = pltpu.VMEM((HEADS, seq + BLOCK, LANES), BF16)
    n = bsz * seq
    view = lambda width, d: pl.BlockSpec((seq // d, d * width), lambda b: (b, 0))
    return pl.pallas_call(
        _dilated_kernel,
        grid=(bsz,),
        in_specs=[view(ZC_W, d) for d in DILATIONS],
        out_specs=[view(GW, d) for d in DILATIONS] * 2,
        out_shape=([jax.ShapeDtypeStruct((n // d, d * GW), BF16) for d in DILATIONS]
                   + [jax.ShapeDtypeStruct((n // d, d * GW), F32) for d in DILATIONS]),
        scratch_shapes=[pltpu.VMEM((seq, GW), BF16), headbuf, headbuf],
        compiler_params=_params(),
        name="dilated",
    )(*zc_views)


CONV_W = 4
CONV_PAD = 8


def _log_sigmoid(x):
    return jnp.minimum(x, 0.0) - jnp.log1p(jnp.exp(-jnp.abs(x)))


def _mlstm_kernel(zd_ref, zm_ref, gb_ref, cw_ref, cb_ref, y_ref, xpad, q_s, k_s, kt_s, a_r, b_r, cm_c,
                  b_c, b_f, c_s, n_s):
    seq = zd_ref.shape[0]
    rows = 512
    xpad[0:CONV_PAD, :] = jnp.zeros((CONV_PAD, 2 * GW), F32)
    for r in range(0, seq, rows):
        xpad[CONV_PAD + r:CONV_PAD + r + rows, :] = zd_ref[r:r + rows, 0:2 * GW].astype(F32)
    for r in range(0, seq, rows):
        win = xpad[r:r + rows + CONV_PAD, :]
        acc = cb_ref[...] + cw_ref[CONV_W - 1:CONV_W, :] * win[CONV_PAD:, :]
        for k in range(1, CONV_W):
            acc = acc + cw_ref[CONV_W - 1 - k:CONV_W - k, :] * pltpu.roll(win, k, 0)[CONV_PAD:, :]
        qk = acc * jax.nn.sigmoid(acc)
        k = qk[:, GW:] * (HEAD_DIM ** -0.5)
        q_s[r:r + rows, :] = qk[:, :GW].astype(BF16)
        k_s[r:r + rows, :] = k.astype(BF16)
        for sub in range(rows // BLOCK):
            kt_s[r // BLOCK + sub] = k[sub * BLOCK:(sub + 1) * BLOCK, :].T

    gt = (zm_ref[...] + gb_ref[...]).T[0:8, :]
    f_log = _log_sigmoid(pltpu.roll(gt, HEADS, 0))
    pos_in_chunk = lax.broadcasted_iota(jnp.int32, (8, seq), 1) % BLOCK
    steps = [1 << s for s in range(BLOCK.bit_length() - 1)]
    b = f_log
    for k in steps:
        b = b + jnp.where(pos_in_chunk >= k, pltpu.roll(b, k, 1), 0.0)
    a = gt - b
    cm = a
    for k in steps:
        cm = jnp.maximum(cm, jnp.where(pos_in_chunk >= k, pltpu.roll(cm, k, 1), NEG_INF))
    a_r[...] = a
    b_r[...] = b
    pad = jnp.zeros((LANES - 8, seq), F32)
    cm_c[...] = jnp.concatenate([cm, pad], axis=0).T
    b_c[...] = jnp.concatenate([b, pad], axis=0).T

    c_s[...] = jnp.zeros((GW, GW), F32)
    n_s[...] = jnp.zeros((GW, GW), F32)
    lane_head = lax.broadcasted_iota(jnp.int32, (1, GW), 1) // HEAD_DIM
    same_head = (lax.broadcasted_iota(jnp.int32, (GW, GW), 0) // HEAD_DIM
                 == lax.broadcasted_iota(jnp.int32, (GW, GW), 1) // HEAD_DIM)
    causal = (lax.broadcasted_iota(jnp.int32, (BLOCK, BLOCK), 1)
              <= lax.broadcasted_iota(jnp.int32, (BLOCK, BLOCK), 0))

    heads = range(HEADS)
    low_half = lax.broadcasted_iota(jnp.int32, (1, LANES), 1) < HEAD_DIM
    ones_blk = jnp.ones((BLOCK, LANES), BF16)
    head_sum = (lax.broadcasted_iota(jnp.int32, (HEADS * BLOCK, GW), 0) // BLOCK
                == lax.broadcasted_iota(jnp.int32, (HEADS * BLOCK, GW), 1) // HEAD_DIM).astype(BF16)

    def head_lanes(per_head):
        return jnp.concatenate([jnp.where(low_half, per_head[0], per_head[1]),
                                jnp.where(low_half, per_head[2], per_head[3])], axis=1)

    for r in range(0, seq, rows):
        b_cols = b_c[r:r + rows, :]
        b_f[r:r + rows, :] = head_lanes([b_cols[:, h:h + 1] for h in heads])

    def chunk(c, m_run):
        r0 = pl.multiple_of(c * BLOCK, BLOCK)
        qb = q_s[pl.ds(r0, BLOCK), :]
        kb = k_s[pl.ds(r0, BLOCK), :]
        v = zd_ref[pl.ds(r0, BLOCK), 2 * GW:3 * GW]
        og = zd_ref[pl.ds(r0, BLOCK), 3 * GW:4 * GW].astype(F32)
        a_rows = a_r[:, pl.ds(r0, BLOCK)]
        b_rows = b_r[:, pl.ds(r0, BLOCK)]
        cm_cols = cm_c[pl.ds(r0, BLOCK), :]

        k_heads = jnp.concatenate(
            [jnp.where(lane_head == h, kb, jnp.zeros_like(kb)) for h in heads], axis=0)
        v_heads = jnp.concatenate(
            [jnp.where(lane_head == h, v, jnp.zeros_like(v)) for h in heads], axis=0)
        scores = _dot_nt(qb, k_heads)
        carried = _dot(qb, jnp.concatenate([c_s[...], n_s[...]], axis=1).astype(BF16))

        w_intra, g_rep, wk_rows, decays, m_next = [], [], [], [], []
        for h in heads:
            a_row = a_rows[h:h + 1, :]
            g = jnp.maximum(jnp.broadcast_to(cm_cols[:, h:h + 1], (BLOCK, BLOCK)), m_run[h])
            g_rep.append(g)
            w_intra.append(jnp.where(causal, jnp.exp(a_row - g), 0.0))
            g_end = jnp.maximum(m_run[h], jnp.max(a_row, axis=-1, keepdims=True))
            m_next.append(b_rows[h:h + 1, BLOCK - 1:BLOCK] + g_end)
            decays.append(jnp.exp(m_run[h] - g_end))
            wk_rows.append(jnp.exp(a_row - g_end))
        g_full = head_lanes(g_rep)
        inter_scale = jnp.exp(head_lanes(m_run) - g_full)
        floor = jnp.exp(-(b_f[pl.ds(r0, BLOCK), :] + g_full))

        wk = jnp.concatenate([jnp.broadcast_to(w, (HEAD_DIM, BLOCK)) for w in wk_rows], axis=0)
        decay = jnp.concatenate([jnp.broadcast_to(dd, (HEAD_DIM, 1)) for dd in decays], axis=0)
        ktw = kt_s[c] * wk
        update = _dot(ktw.astype(BF16), jnp.concatenate([v, ones_blk], axis=1))

        sqk = (scores * jnp.concatenate(w_intra, axis=1)).astype(BF16)
        intra = _dot(sqk, jnp.concatenate([v_heads, head_sum], axis=1))
        num = intra[:, :GW] + inter_scale * carried[:, :GW]
        den = intra[:, GW:] + inter_scale * carried[:, GW:]
        h_out = num / jnp.maximum(jnp.abs(den), floor)
        y_ref[pl.ds(r0, BLOCK), :] = (jax.nn.sigmoid(og) * h_out).astype(BF16)

        k_sum = update[:, GW:]
        c_s[...] = decay * c_s[...] + jnp.where(same_head, update[:, :GW], 0.0)
        n_s[...] = decay * n_s[...] + jnp.where(same_head, jnp.concatenate([k_sum, k_sum], axis=1), 0.0)
        return tuple(m_next)

    lax.fori_loop(0, seq // BLOCK, chunk, tuple(jnp.zeros((1, 1), F32) for _ in range(HEADS)))


def _mlstm(zd, zm, gb, cw, cb, bsz, seq):
    full = lambda a: pl.BlockSpec(a.shape, lambda b: (0,) * a.ndim)
    return pl.pallas_call(
        _mlstm_kernel,
        grid=(bsz,),
        in_specs=[pl.BlockSpec((seq, ZD_W), lambda b: (b, 0)),
                  pl.BlockSpec((seq, LANES), lambda b: (b, 0)), full(gb), full(cw), full(cb)],
        out_specs=pl.BlockSpec((seq, GW), lambda b: (b, 0)),
        out_shape=jax.ShapeDtypeStruct((bsz * seq, GW), BF16),
        scratch_shapes=[pltpu.VMEM((seq + CONV_PAD, 2 * GW), F32),
                        pltpu.VMEM((seq, GW), BF16), pltpu.VMEM((seq, GW), BF16),
                        pltpu.VMEM((seq // BLOCK, GW, BLOCK), F32),
                        pltpu.VMEM((8, seq), F32), pltpu.VMEM((8, seq), F32),
                        pltpu.VMEM((seq, LANES), F32), pltpu.VMEM((seq, LANES), F32),
                        pltpu.VMEM((seq, GW), F32),
                        pltpu.VMEM((GW, GW), F32), pltpu.VMEM((GW, GW), F32)],
        compiler_params=_params(),
        name="mlstm",
    )(zd, zm, gb, cw, cb)


FF_CHUNK = 256


def _dense_kernel(with_next, ya_ref, yb_ref, yd_ref, oc1_ref, oc4_ref, oc16_ref, lc1_ref, lc4_ref, lc16_ref,
                  x_ref, wo_ref, g1_ref, b1_ref, wg_ref, wu_ref, wd_ref, g2_ref, b2_ref, *rest):
    if with_next:
        (w_ref, wm_ref, cos_ref, sin_ref, o_ref, zd_ref, za_ref, zc1_ref, zc4_ref, zc16_ref, zb_ref, zm_ref,
         xb_ref, acc_ref, nat_ref, stage_ref) = rest
    else:
        o_ref, xb_ref, acc_ref, nat_ref = rest
    tm = x_ref.shape[0]

    def natural(view, d, slot):
        if d == 1:
            return view[...].astype(F32)
        for r in range(d):
            for half in range(GW // LANES):
                col = r * GW + half * LANES
                nat_ref[slot, half, pl.ds(r, tm // d, stride=d), :] = view[:, col:col + LANES].astype(F32)
        return jnp.concatenate([nat_ref[slot, half] for half in range(GW // LANES)], axis=1)

    views = (oc1_ref, oc4_ref, oc16_ref, lc1_ref, lc4_ref, lc16_ref)
    slots = iter(range(nat_ref.shape[0]))
    nat = [natural(v, d, next(slots) if d > 1 else None) for v, d in zip(views, DILATIONS * 2)]
    outs, lse = nat[:3], nat[3:]
    top = jnp.maximum(jnp.maximum(lse[0], lse[1]), lse[2])
    wts = [jnp.exp2(l - top) for l in lse]
    mix = wts[0] * outs[0] + wts[1] * outs[1] + wts[2] * outs[2]
    yc = (mix / (wts[0] + wts[1] + wts[2])).astype(BF16)
    acc = ALPHA * x_ref[...]
    for i, y in ((0, ya_ref[...]), (1, yb_ref[...]), (3, yd_ref[...]), (2, yc)):
        acc = acc + _dot(y, wo_ref[i * GW:(i + 1) * GW, :])
    x1 = _layer_norm(acc, g1_ref[...], b1_ref[...])
    xb_ref[...] = x1.astype(BF16)
    acc_ref[...] = ALPHA * x1
    for c in range(0, D_FF, FF_CHUNK):
        gate = _dot(xb_ref[...], wg_ref[:, c:c + FF_CHUNK])
        up = _dot(xb_ref[...], wu_ref[:, c:c + FF_CHUNK])
        act = (gate * jax.nn.sigmoid(gate) * up).astype(BF16)
        acc_ref[...] += _dot(act, wd_ref[c:c + FF_CHUNK, :])
    x2 = _layer_norm(acc_ref[...], g2_ref[...], b2_ref[...])
    o_ref[...] = x2
    if with_next:
        xb_ref[...] = x2.astype(BF16)
        _project_in(xb_ref, w_ref, wm_ref, cos_ref, sin_ref, zd_ref, za_ref, (zc1_ref, zc4_ref, zc16_ref),
                    zb_ref, zm_ref, stage_ref, 0, tm)


def _dense(ys, dil, x2, wo, g1, b1, wg, wu, wd, g2, b2, next_w, tables, tm):
    n = x2.shape[0]
    row = lambda w_: pl.BlockSpec((tm, w_), lambda i: (i, 0))
    view = lambda d: pl.BlockSpec((tm // d, d * GW), lambda i: (i, 0))
    once = lambda a: pl.BlockSpec(a.shape, lambda i: (0, 0), pipeline_mode=pl.Buffered(1))
    weights = [wo, g1, b1, wg, wu, wd, g2, b2] + list(next_w)
    tables = list(tables) if next_w else []
    out_specs = [row(D_MODEL)]
    out_shape = [jax.ShapeDtypeStruct((n, D_MODEL), F32)]
    scratch = [pltpu.VMEM((tm, D_MODEL), BF16), pltpu.VMEM((tm, D_MODEL), F32),
               pltpu.VMEM((2 * (len(DILATIONS) - 1), GW // LANES, tm, LANES), F32)]
    if next_w:
        z_specs, z_shapes, stage = _z_outputs(n, tm)
        out_specs += z_specs
        out_shape += z_shapes
        scratch.append(stage)
    return pl.pallas_call(
        functools.partial(_dense_kernel, bool(next_w)),
        grid=(n // tm,),
        in_specs=([row(GW)] * len(ys) + [view(d) for d in DILATIONS] * 2 + [row(D_MODEL)]
                  + [once(w) for w in weights] + [row(LANES)] * len(tables)),
        out_specs=out_specs,
        out_shape=out_shape,
        scratch_shapes=scratch,
        compiler_params=_params(),
        name="dense_block",
    )(*ys, *dil, x2, *weights, *tables)


def _pack_weights(w_in, b_w_uq, b_w_ukv, b_q_norm, a_bs, d_igate_b, d_fgate_b):
    nl = w_in.shape[0]
    o_b = ZA_W
    o_c = o_b + Q_RANK + KV_RANK + ROPE
    o_d = o_c + ZC_W
    o_g = o_d + ZD_W
    w_a = w_in[..., :o_b]
    w_cq = w_in[..., o_b:o_b + Q_RANK]
    w_ckv = w_in[..., o_b + Q_RANK:o_b + Q_RANK + KV_RANK]
    w_kr = w_in[..., o_b + Q_RANK + KV_RANK:o_c]
    w_c = w_in[..., o_c:o_d]
    w_d = w_in[..., o_d:o_g]
    w_gates = w_in[..., o_g:]
    zeros = lambda *s: jnp.zeros((nl,) + s, F32)
    w_b = jnp.concatenate([w_cq, zeros(D_MODEL, 256 - Q_RANK), w_ckv], -1)
    w_main = jnp.concatenate([w_d, w_a, w_c, w_b], -1).astype(BF16)
    half = ROPE // 2
    w_kr_rot = jnp.concatenate([-w_kr[..., half:], w_kr[..., :half]], -1)
    w_misc = jnp.concatenate(
        [w_gates, zeros(D_MODEL, KR_LANE - 2 * HEADS), w_kr, w_kr_rot], -1).astype(BF16)

    wq = b_w_uq.reshape(nl, Q_RANK, HEADS, NOPE + ROPE)
    nope, x1, x2 = wq[..., :NOPE], wq[..., NOPE:NOPE + half], wq[..., NOPE + half:]
    tail = zeros(Q_RANK, HEADS, LANES - NOPE - ROPE)
    rowpad = ((0, 0), (0, 256 - Q_RANK), (0, 0))
    wqm = jnp.pad(jnp.concatenate([nope, x1, x2, tail], -1).reshape(nl, Q_RANK, HEADS * LANES), rowpad)
    wqs = jnp.pad(jnp.concatenate([zeros(Q_RANK, HEADS, NOPE), -x2, x1, tail], -1)
                  .reshape(nl, Q_RANK, HEADS * LANES), rowpad)
    wkv = b_w_ukv.reshape(nl, KV_RANK, HEADS, NOPE + HEAD_DIM)
    wk = jnp.concatenate([wkv[..., :NOPE], zeros(KV_RANK, HEADS, LANES - NOPE)], -1)
    wk = wk.reshape(nl, KV_RANK, HEADS * LANES)
    wv = wkv[..., NOPE:].reshape(nl, KV_RANK, GW)
    qg = jnp.pad(b_q_norm, ((0, 0), (0, 256 - Q_RANK)))[:, None, :]
    bias = jnp.repeat(jnp.swapaxes(a_bs, 1, 2), HEAD_DIM, axis=-1)
    gb = jnp.concatenate([d_igate_b, d_fgate_b, zeros(LANES - 2 * HEADS)], -1)[:, None, :]
    return dict(w_main=w_main, w_misc=w_misc, wqm=wqm.astype(BF16), wqs=wqs.astype(BF16),
                wk=wk.astype(BF16), wv=wv.astype(BF16), qg=qg, bias=bias, gb=gb)


@jax.jit
def _forward(x, positions, w_in, a_ln_g, a_ln_b, a_ws, a_bs, b_q_norm, b_kv_norm, b_w_uq, b_w_ukv,
             d_conv_w, d_conv_b, d_igate_b, d_fgate_b, w_out, ln1_g, ln1_b, w_gate, w_up, w_down,
             ln2_g, ln2_b):
    bsz, seq, _ = x.shape
    pk = _pack_weights(w_in, b_w_uq, b_w_ukv, b_q_norm, a_bs, d_igate_b, d_fgate_b)
    w_out_b, w_gate_b, w_up_b, w_down_b = (w.astype(BF16) for w in (w_out, w_gate, w_up, w_down))
    cosd, sind, cosm, sinm = _rope_tables(positions)
    tm = 512
    x2 = x.reshape(bsz * seq, D_MODEL)
    row = lambda a, l: a[l][None, :]
    rope_d = (cosd.reshape(bsz * seq, LANES), sind.reshape(bsz * seq, LANES))
    zd, za, *zc, zb, zm = _inproj(x2, pk["w_main"][0], pk["w_misc"][0], *rope_d, tm)
    for l in range(DEPTH):
        ya = _gmlp(za, row(a_ln_g, l), row(a_ln_b, l), a_ws[l], pk["bias"][l], bsz, seq)
        yb = _mla(zb, zm, cosm, sinm, pk["qg"][l], row(b_kv_norm, l), pk["wqm"][l], pk["wqs"][l],
                  pk["wk"][l], pk["wv"][l], bsz, seq)
        dil = _dilated(zc, bsz, seq)
        yd = _mlstm(zd, zm, pk["gb"][l], d_conv_w[l], row(d_conv_b, l), bsz, seq)
        next_w = (pk["w_main"][l + 1], pk["w_misc"][l + 1]) if l + 1 < DEPTH else ()
        outs = _dense((ya, yb, yd), dil, x2, w_out_b[l], row(ln1_g, l), row(ln1_b, l), w_gate_b[l],
                      w_up_b[l], w_down_b[l], row(ln2_g, l), row(ln2_b, l), next_w, rope_d, tm)
        if next_w:
            x2, zd, za, *zc, zb, zm = outs
        else:
            x2 = outs[0]
    return x2.reshape(bsz, seq, D_MODEL)


def kernel(x, positions, w_in, a_ln_g, a_ln_b, a_ws, a_bs, b_q_norm, b_kv_norm, b_w_uq, b_w_ukv,
           d_conv_w, d_conv_b, d_igate_b, d_fgate_b, w_out, ln1_g, ln1_b, w_gate, w_up, w_down,
           ln2_g, ln2_b):
    return _forward(x, positions, w_in, a_ln_g, a_ln_b, a_ws, a_bs, b_q_norm, b_kv_norm, b_w_uq,
                    b_w_ukv, d_conv_w, d_conv_b, d_igate_b, d_fgate_b, w_out, ln1_g, ln1_b, w_gate,
                    w_up, w_down, ln2_g, ln2_b)
```

```python
import functools

import jax
import jax.numpy as jnp
from jax import lax
from jax.experimental import pallas as pl
from jax.experimental.pallas import tpu as pltpu

F32 = jnp.float32
BF16 = jnp.bfloat16

D_MODEL = 1024
DEPTH = 4
HEAD_DIM = 64
HEADS = 4
GW = HEADS * HEAD_DIM
BLOCK = 128
Q_RANK = 192
KV_RANK = 128
NOPE = 64
ROPE = 32
D_FF = 2816
ROPE_THETA = 10000.0
LN_EPS = 1e-5
RMS_EPS = 1e-6
ALPHA = (2 * DEPTH) ** 0.25
LANES = 128
NEG_INF = float("-inf")
LOG2_E = 1.4426950408889634
LN_2 = 0.6931471805599453

ZD_W, ZA_W, ZC_W, ZB_W = 4 * GW, 2 * GW, 3 * GW, 384
ZMAIN_W = ZD_W + ZA_W + ZC_W + ZB_W
KR_LANE = 64

VMEM_LIMIT = 60000 * 1024


def _dot(a, b):
    return jnp.dot(a, b, preferred_element_type=F32)


def _dot_nt(a, b):
    return lax.dot_general(a, b, (((1,), (1,)), ((), ())), preferred_element_type=F32)


def _params(n_axes=1):
    return pltpu.CompilerParams(
        dimension_semantics=("arbitrary",) * n_axes, vmem_limit_bytes=VMEM_LIMIT)


def _layer_norm(r, g, b):
    mu = jnp.mean(r, axis=-1, keepdims=True)
    d = r - mu
    var = jnp.mean(d * d, axis=-1, keepdims=True)
    return d * lax.rsqrt(var + LN_EPS) * g + b


def _tables_kernel(pos_ref, inv_ref, cosd_ref, sind_ref, cosm_ref, sinm_ref):
    pos = pos_ref[0].astype(F32)
    lane = lax.broadcasted_iota(jnp.int32, (1, LANES), 1)
    ang = pos * inv_ref[...]
    cos, sin = jnp.cos(ang), jnp.sin(ang)
    low = lane < HEAD_DIM
    cosd_ref[0] = jnp.where(low, cos, pltpu.roll(cos, HEAD_DIM, 1))
    sign = jnp.where((lane % HEAD_DIM) < HEAD_DIM // 2, -1.0, 1.0)
    sind_ref[0] = jnp.where(low, sin, pltpu.roll(sin, HEAD_DIM, 1)) * sign
    in_rope = (lane >= KR_LANE) & (lane < KR_LANE + ROPE)
    cosm_ref[0] = jnp.where(in_rope, cos, jnp.where(lane < KR_LANE, 1.0, 0.0))
    sinm_ref[0] = jnp.where(in_rope, sin, 0.0)


def _rope_tables(positions):
    bsz, seq = positions.shape
    lane = jnp.arange(LANES)
    half_d = HEAD_DIM // 2
    inv_d = jnp.power(ROPE_THETA, -jnp.arange(half_d, dtype=F32) / half_d)
    half_m = ROPE // 2
    inv_m = jnp.power(ROPE_THETA, -jnp.arange(half_m, dtype=F32) / half_m)
    inv = jnp.where(lane < HEAD_DIM, inv_d[lane % half_d],
                    jnp.where(lane < KR_LANE + ROPE, inv_m[lane % half_m], 0.0))[None, :]
    tab = jax.ShapeDtypeStruct((bsz, seq, LANES), F32)
    spec = pl.BlockSpec((1, seq, LANES), lambda b: (b, 0, 0))
    return pl.pallas_call(
        _tables_kernel,
        grid=(bsz,),
        in_specs=[pl.BlockSpec((1, seq, 1), lambda b: (b, 0, 0)), pl.BlockSpec((1, LANES), lambda b: (0, 0))],
        out_specs=[spec] * 4,
        out_shape=[tab] * 4,
        compiler_params=_params(),
        name="rope_tables",
    )(positions.reshape(bsz, seq, 1), inv)


def _rope_heads(x, cos, sin):
    first = (lax.broadcasted_iota(jnp.int32, (1, LANES), 1) % HEAD_DIM) < HEAD_DIM // 2
    halves = []
    for half in range(GW // LANES):
        xh = x[:, half * LANES:(half + 1) * LANES]
        rot = jnp.where(first, pltpu.roll(xh, LANES - HEAD_DIM // 2, 1), pltpu.roll(xh, HEAD_DIM // 2, 1))
        halves.append(xh * cos + rot * sin)
    return jnp.concatenate(halves, axis=1)


def _project_in(xb_ref, w_ref, wm_ref, cos_ref, sin_ref, zd_ref, za_ref, zc_refs, zb_ref, zm_ref, stage_ref,
                r0, nrows):
    rs = slice(r0, r0 + nrows)
    groups = {id(zd_ref): 0, id(za_ref): ZD_W, id(zc_refs): ZD_W + ZA_W, id(zb_ref): ZD_W + ZA_W + ZC_W}
    for ref in (zc_refs, zd_ref, za_ref, zb_ref):
        off = groups[id(ref)]
        width = ZC_W if ref is zc_refs else ref.shape[1]
        for c in range(0, width, 256):
            cw = min(256, width - c)
            z = _dot(xb_ref[rs, :], w_ref[:, off + c:off + c + cw])
            if ref is not zc_refs:
                ref[rs, c:c + cw] = z.astype(BF16)
                continue
            if c < 2 * GW:
                z = _rope_heads(z, cos_ref[rs, :], sin_ref[rs, :])
                if c == 0:
                    z = z * (HEAD_DIM ** -0.5 * LOG2_E)
            zc_refs[0][rs, c:c + cw] = z.astype(BF16)
            for half in range(GW // LANES):
                stage_ref[c // 256, half, rs, :] = z[:, half * LANES:(half + 1) * LANES]
            for d, view in zip(DILATIONS[1:], zc_refs[1:]):
                for r in range(d):
                    for half in range(GW // LANES):
                        col = r * ZC_W + c + half * LANES
                        rows = stage_ref[c // 256, half, pl.ds(r0 + r, nrows // d, stride=d), :]
                        view[r0 // d:(r0 + nrows) // d, col:col + LANES] = rows.astype(BF16)
    zm_ref[rs, :] = _dot(xb_ref[rs, :], wm_ref[...])


def _inproj_kernel(x_ref, w_ref, wm_ref, cos_ref, sin_ref, zd_ref, za_ref, zc1_ref, zc4_ref, zc16_ref, zb_ref,
                   zm_ref, xb_ref, stage_ref):
    xb_ref[...] = x_ref[...].astype(BF16)
    _project_in(xb_ref, w_ref, wm_ref, cos_ref, sin_ref, zd_ref, za_ref, (zc1_ref, zc4_ref, zc16_ref),
                zb_ref, zm_ref, stage_ref, 0, x_ref.shape[0])


def _z_outputs(n, tm):
    row = lambda w: pl.BlockSpec((tm, w), lambda i: (i, 0))
    view = lambda d: pl.BlockSpec((tm // d, d * ZC_W), lambda i: (i, 0))
    specs = [row(ZD_W), row(ZA_W)] + [view(d) for d in DILATIONS] + [row(ZB_W), row(LANES)]
    shapes = ([jax.ShapeDtypeStruct((n, ZD_W), BF16), jax.ShapeDtypeStruct((n, ZA_W), BF16)]
              + [jax.ShapeDtypeStruct((n // d, d * ZC_W), BF16) for d in DILATIONS]
              + [jax.ShapeDtypeStruct((n, ZB_W), BF16), jax.ShapeDtypeStruct((n, LANES), F32)])
    stage = pltpu.VMEM((ZC_W // 256, GW // LANES, tm, LANES), F32)
    return specs, shapes, stage


def _inproj(x2, w_main, w_misc, cosd, sind, tm):
    n = x2.shape[0]
    row = lambda w: pl.BlockSpec((tm, w), lambda i: (i, 0))
    full = lambda a: pl.BlockSpec(a.shape, lambda i: (0, 0))
    z_specs, z_shapes, stage = _z_outputs(n, tm)
    return pl.pallas_call(
        _inproj_kernel,
        grid=(n // tm,),
        in_specs=[row(D_MODEL), full(w_main), full(w_misc), row(LANES), row(LANES)],
        out_specs=z_specs,
        out_shape=z_shapes,
        scratch_shapes=[pltpu.VMEM((tm, D_MODEL), BF16), stage],
        compiler_params=_params(),
        name="inproj",
    )(x2, w_main, w_misc, cosd, sind)


def _gmlp_kernel(za_ref, lng_ref, lnb_ref, ws_ref, bias_ref, y_ref):
    seq = za_ref.shape[0]
    r_i = lax.broadcasted_iota(jnp.int32, (BLOCK, BLOCK), 0)
    c_i = lax.broadcasted_iota(jnp.int32, (BLOCK, BLOCK), 1)
    w_causal = [jnp.where(c_i <= r_i, ws_ref[h], 0.0).astype(BF16) for h in range(HEADS)]
    lane_head = lax.broadcasted_iota(jnp.int32, (BLOCK, GW), 1) // HEAD_DIM

    def chunk(c, carry):
        r0 = pl.multiple_of(c * BLOCK, BLOCK)
        z = za_ref[pl.ds(r0, BLOCK), :].astype(F32)
        g = 0.5 * z * (1.0 + lax.erf(z * (0.5 ** 0.5)))
        u, v = g[:, :GW], g[:, GW:]
        vb = _layer_norm(v, lng_ref[...], lnb_ref[...]).astype(BF16)
        mixed = jnp.zeros((BLOCK, GW), F32)
        for h in range(HEADS):
            mixed = jnp.where(lane_head == h, _dot(w_causal[h], vb), mixed)
        y_ref[pl.ds(r0, BLOCK), :] = (u * (mixed + bias_ref[...])).astype(BF16)
        return carry

    lax.fori_loop(0, seq // BLOCK, chunk, 0, unroll=4)


def _gmlp(za, ln_g, ln_b, ws, bias, bsz, seq):
    full = lambda a: pl.BlockSpec(a.shape, lambda b: (0,) * a.ndim)
    return pl.pallas_call(
        _gmlp_kernel,
        grid=(bsz,),
        in_specs=[pl.BlockSpec((seq, ZA_W), lambda b: (b, 0)), full(ln_g), full(ln_b), full(ws),
                  full(bias)],
        out_specs=pl.BlockSpec((seq, GW), lambda b: (b, 0)),
        out_shape=jax.ShapeDtypeStruct((bsz * seq, GW), BF16),
        compiler_params=_params(),
        name="gmlp",
    )(za, ln_g, ln_b, ws, bias)


MLA_TQ = 256


def _mla_kernel(zb_ref, zm_ref, cos_ref, sin_ref, qg_ref, kvg_ref, wqm_ref, wqs_ref, wk_ref, wv_ref,
                y_ref, q_s, k_s, vt_s, acc_s, sc_a, sc_b, p_a, p_b):
    seq = zb_ref.shape[0]
    tq = MLA_TQ
    scale = (NOPE + ROPE) ** -0.5 * LOG2_E
    lane = lax.broadcasted_iota(jnp.int32, (1, LANES), 1)
    in_rope = (lane >= KR_LANE) & (lane < KR_LANE + ROPE)
    rows = 512
    for r in range(0, seq, rows):
        cos = cos_ref[0, r:r + rows, :]
        sin = sin_ref[0, r:r + rows, :]
        cq = zb_ref[r:r + rows, 0:256].astype(F32)
        ms = jnp.sum(cq * cq, axis=-1, keepdims=True) * (1.0 / Q_RANK)
        cqn = (cq * lax.rsqrt(ms + RMS_EPS) * qg_ref[...]).astype(BF16)
        qm = _dot(cqn, wqm_ref[...])
        qs = _dot(cqn, wqs_ref[...])
        ckv = zb_ref[r:r + rows, 256:384].astype(F32)
        ms = jnp.mean(ckv * ckv, axis=-1, keepdims=True)
        ckvn = (ckv * lax.rsqrt(ms + RMS_EPS) * kvg_ref[...]).astype(BF16)
        kn = _dot(ckvn, wk_ref[...])
        v = _dot(ckvn, wv_ref[...])
        for sub in range(rows // tq):
            vt_s[r // tq + sub] = v[sub * tq:(sub + 1) * tq, :].T.astype(BF16)
        zm = zm_ref[r:r + rows, :]
        kr = jnp.where(in_rope, zm * cos + pltpu.roll(zm, LANES - ROPE, 1) * sin, 0.0)
        for h in range(HEADS):
            sl = slice(h * LANES, (h + 1) * LANES)
            q_s[r:r + rows, sl] = ((qm[:, sl] * cos + qs[:, sl] * sin) * scale).astype(BF16)
            k_s[r:r + rows, sl] = (kn[:, sl] + kr).astype(BF16)

    key_i = lax.broadcasted_iota(jnp.int32, (tq, tq), 0)
    qry_i = lax.broadcasted_iota(jnp.int32, (tq, tq), 1)
    heads = range(HEADS)
    head_lanes = [slice(h * LANES, (h + 1) * LANES) for h in heads]

    def qblock(i, carry):
        q0 = pl.multiple_of(i * tq, tq)
        acc_s[...] = jnp.zeros(acc_s.shape, F32)

        def put_scores(j, dst):
            k0 = pl.multiple_of(j * tq, tq)
            for h in heads:
                dst[h] = _dot_nt(k_s[pl.ds(k0, tq), head_lanes[h]], q_s[pl.ds(q0, tq), head_lanes[h]])

        def softmax(src, p_dst, m_old, l_old, masked):
            m_new, l_new, alpha = [], [], []
            for h in heads:
                s = src[h]
                if masked:
                    s = jnp.where(key_i <= qry_i, s, NEG_INF)
                m = jnp.maximum(m_old[h], jnp.max(s, axis=0, keepdims=True))
                a = jnp.exp2(m_old[h] - m)
                p = jnp.exp2(s - m)
                m_new.append(m)
                alpha.append(a)
                l_new.append(a * l_old[h] + jnp.sum(p, axis=0, keepdims=True))
                p_dst[h] = p.astype(BF16)
            return tuple(m_new), tuple(l_new), tuple(alpha)

        def add_values(j, p_src, alpha):
            for h in heads:
                vt = vt_s[j, h * HEAD_DIM:(h + 1) * HEAD_DIM, :]
                acc_s[h] = alpha[h] * acc_s[h] + _dot(vt, p_src[h])

        even, odd = (sc_a, p_a), (sc_b, p_b)

        def on_parity(j, fn, state):
            return lax.cond(j % 2 == 0, lambda st: fn(even, odd, st), lambda st: fn(odd, even, st), state)

        def step(j, state):
            def run(cur, other, st):
                m_old, l_old, alpha_prev = st
                put_scores(j + 1, other[0])
                m_new, l_new, alpha = softmax(cur[0], cur[1], m_old, l_old, False)
                add_values(jnp.maximum(j - 1, 0), other[1], alpha_prev)
                return m_new, l_new, alpha
            return on_parity(j, run, state)

        def finish(cur, other, st):
            m_old, l_old, alpha_prev = st
            _, l_new, alpha = softmax(cur[0], cur[1], m_old, l_old, True)
            add_values(jnp.maximum(i - 1, 0), other[1], alpha_prev)
            add_values(i, cur[1], alpha)
            return l_new

        put_scores(0, sc_a)
        p_b[...] = jnp.zeros(p_b.shape, BF16)
        init = (tuple(jnp.full((1, tq), NEG_INF, F32) for _ in heads),
                tuple(jnp.zeros((1, tq), F32) for _ in heads),
                tuple(jnp.ones((1, tq), F32) for _ in heads))
        state = lax.fori_loop(0, i, step, init)
        l_fin = on_parity(i, finish, state)
        out_t = jnp.concatenate([acc_s[h] / l_fin[h] for h in heads], axis=0)
        y_ref[pl.ds(q0, tq), :] = out_t.T.astype(BF16)
        return carry

    lax.fori_loop(0, seq // tq, qblock, 0)


def _mla(zb, zm, cosm, sinm, qg, kvg, wqm, wqs, wk, wv, bsz, seq):
    full = lambda a: pl.BlockSpec(a.shape, lambda b: (0,) * a.ndim)
    tab = pl.BlockSpec((1, seq, LANES), lambda b: (b, 0, 0))
    return pl.pallas_call(
        _mla_kernel,
        grid=(bsz,),
        in_specs=[pl.BlockSpec((seq, ZB_W), lambda b: (b, 0)),
                  pl.BlockSpec((seq, LANES), lambda b: (b, 0)), tab, tab,
                  full(qg), full(kvg), full(wqm), full(wqs), full(wk), full(wv)],
        out_specs=pl.BlockSpec((seq, GW), lambda b: (b, 0)),
        out_shape=jax.ShapeDtypeStruct((bsz * seq, GW), BF16),
        scratch_shapes=[pltpu.VMEM((seq, HEADS * LANES), BF16), pltpu.VMEM((seq, HEADS * LANES), BF16),
                        pltpu.VMEM((seq // MLA_TQ, GW, MLA_TQ), BF16),
                        pltpu.VMEM((HEADS, HEAD_DIM, MLA_TQ), F32),
                        pltpu.VMEM((HEADS, MLA_TQ, MLA_TQ), F32),
                        pltpu.VMEM((HEADS, MLA_TQ, MLA_TQ), F32),
                        pltpu.VMEM((HEADS, MLA_TQ, MLA_TQ), BF16),
                        pltpu.VMEM((HEADS, MLA_TQ, MLA_TQ), BF16)],
        compiler_params=_params(),
        name="mla",
    )(zb, zm, cosm, sinm, qg, kvg, wqm, wqs, wk, wv)


DILATIONS = (1, 4, 16)
HALVES = GW // LANES


def _head_columns(cols):
    low_half = lax.broadcasted_iota(jnp.int32, (1, LANES), 1) < HEAD_DIM
    return jnp.concatenate([jnp.where(low_half, cols[0], cols[1]),
                            jnp.where(low_half, cols[2], cols[3])], axis=1)


def _band_block(q, kh, vh, w0, nk, mask):
    pairs = range(HEADS // 2)
    scores = []
    for pr in pairs:
        k_win = jnp.concatenate([kh[2 * pr + e, pl.ds(w0, nk), :] for e in range(2)], axis=0)
        scores.append(_dot_nt(q[:, pr * LANES:(pr + 1) * LANES], k_win))
    m_cols, l_cols, probs = [], [], []
    for h in range(HEADS):
        s = scores[h // 2][:, (h % 2) * nk:(h % 2 + 1) * nk]
        s = jnp.where(mask, s, NEG_INF)
        m = jnp.max(s, axis=-1, keepdims=True)
        p = jnp.exp2(s - m)
        m_cols.append(m)
        l_cols.append(jnp.sum(p, axis=-1, keepdims=True))
        probs.append(p.astype(BF16))
    outs = []
    for pr in pairs:
        v_win = jnp.concatenate([vh[2 * pr + e, pl.ds(w0, nk), :] for e in range(2)], axis=0)
        outs.append(_dot(jnp.concatenate(probs[2 * pr:2 * pr + 2], axis=1), v_win))
    l_full = _head_columns(l_cols)
    o = jnp.concatenate(outs, axis=1) / l_full
    return o, _head_columns(m_cols) + jnp.log2(l_full)


def _dilated_kernel(z1_ref, z4_ref, z16_ref, o1_ref, o4_ref, o16_ref, l1_ref, l4_ref, l16_ref, qb, kh, vh):
    seq = z1_ref.shape[0]
    lane = lax.broadcasted_iota(jnp.int32, (1, LANES), 1)
    views = {1: (z1_ref, o1_ref, l1_ref), 4: (z4_ref, o4_ref, l4_ref), 16: (z16_ref, o16_ref, l16_ref)}

    r_i = lax.broadcasted_iota(jnp.int32, (BLOCK, 2 * BLOCK), 0)
    c_i = lax.broadcasted_iota(jnp.int32, (BLOCK, 2 * BLOCK), 1)
    band = (c_i >= r_i) & (c_i <= r_i + BLOCK)
    causal = (lax.broadcasted_iota(jnp.int32, (BLOCK, BLOCK), 1)
              <= lax.broadcasted_iota(jnp.int32, (BLOCK, BLOCK), 0))
    low_half = lane < HEAD_DIM
    for h in range(HEADS):
        kh[h, 0:BLOCK, :] = jnp.zeros((BLOCK, LANES), BF16)
        vh[h, 0:BLOCK, :] = jnp.zeros((BLOCK, LANES), BF16)

    for d in DILATIONS:
        sub = seq // d
        z_ref, o_ref, l_ref = views[d]
        for res in range(d):
            dst = slice(BLOCK + res * sub, BLOCK + (res + 1) * sub)
            qb[res * sub:(res + 1) * sub, :] = z_ref[:, res * ZC_W:res * ZC_W + GW]
            for half in range(HALVES):
                col = lambda base: slice(res * ZC_W + base + half * LANES,
                                         res * ZC_W + base + (half + 1) * LANES)
                k_half = z_ref[:, col(GW)]
                v_half = z_ref[:, col(2 * GW)]
                zero = jnp.zeros_like(k_half)
                kh[2 * half, dst, :] = jnp.where(low_half, k_half, zero)
                kh[2 * half + 1, dst, :] = jnp.where(low_half, zero, k_half)
                vh[2 * half, dst, :] = jnp.where(low_half, v_half, zero)
                vh[2 * half + 1, dst, :] = jnp.where(low_half, zero, v_half)
        blocks_per_class = sub // BLOCK

        for res in range(d):
            cols = slice(res * GW, (res + 1) * GW)

            def block(n, carry, res=res, cols=cols, o_ref=o_ref, l_ref=l_ref):
                r0 = pl.multiple_of(res * sub + n * BLOCK, BLOCK)
                q = qb[pl.ds(r0, BLOCK), :]
                if blocks_per_class == 1:
                    o, lse = _band_block(q, kh, vh, r0 + BLOCK, BLOCK, causal)
                else:
                    first_key = jnp.where(n > 0, 0, BLOCK)
                    o, lse = _band_block(q, kh, vh, r0, 2 * BLOCK, band & (c_i >= first_key))
                rows = pl.ds(pl.multiple_of(n * BLOCK, BLOCK), BLOCK)
                o_ref[rows, cols] = o.astype(BF16)
                l_ref[rows, cols] = lse
                return carry

            if blocks_per_class == 1:
                block(0, 0)
            else:
                lax.fori_loop(0, blocks_per_class, block, 0, unroll=8)


def _dilated(zc_views, bsz, seq):
    headbuf = pltpu.VMEM((HEADS, seq + BLOCK, LANES), BF16)
    n = bsz * seq
    view = lambda width, d: pl.BlockSpec((seq // d, d * width), lambda b: (b, 0))
    return pl.pallas_call(
        _dilated_kernel,
        grid=(bsz,),
        in_specs=[view(ZC_W, d) for d in DILATIONS],
        out_specs=[view(GW, d) for d in DILATIONS] * 2,
        out_shape=([jax.ShapeDtypeStruct((n // d, d * GW), BF16) for d in DILATIONS]
                   + [jax.ShapeDtypeStruct((n // d, d * GW), F32) for d in DILATIONS]),
        scratch_shapes=[pltpu.VMEM((seq, GW), BF16), headbuf, headbuf],
        compiler_params=_params(),
        name="dilated",
    )(*zc_views)


CONV_W = 4
CONV_PAD = 8


def _log_sigmoid(x):
    return jnp.minimum(x, 0.0) - jnp.log1p(jnp.exp(-jnp.abs(x)))


def _mlstm_kernel(zd_ref, zm_ref, gb_ref, cw_ref, cb_ref, y_ref, xpad, q_s, k_s, kt_s, a_r, b_r, cm_c,
                  b_c, b_f, c_s, n_s):
    seq = zd_ref.shape[0]
    rows = 512
    xpad[0:CONV_PAD, :] = jnp.zeros((CONV_PAD, 2 * GW), F32)
    for r in range(0, seq, rows):
        xpad[CONV_PAD + r:CONV_PAD + r + rows, :] = zd_ref[r:r + rows, 0:2 * GW].astype(F32)
    for r in range(0, seq, rows):
        win = xpad[r:r + rows + CONV_PAD, :]
        acc = cb_ref[...] + cw_ref[CONV_W - 1:CONV_W, :] * win[CONV_PAD:, :]
        for k in range(1, CONV_W):
            acc = acc + cw_ref[CONV_W - 1 - k:CONV_W - k, :] * pltpu.roll(win, k, 0)[CONV_PAD:, :]
        qk = acc * jax.nn.sigmoid(acc)
        k = qk[:, GW:] * (HEAD_DIM ** -0.5)
        q_s[r:r + rows, :] = qk[:, :GW].astype(BF16)
        k_s[r:r + rows, :] = k.astype(BF16)
        for sub in range(rows // BLOCK):
            kt_s[r // BLOCK + sub] = k[sub * BLOCK:(sub + 1) * BLOCK, :].T

    gt = (zm_ref[...] + gb_ref[...]).T[0:8, :]
    f_log = _log_sigmoid(pltpu.roll(gt, HEADS, 0))
    pos_in_chunk = lax.broadcasted_iota(jnp.int32, (8, seq), 1) % BLOCK
    steps = [1 << s for s in range(BLOCK.bit_length() - 1)]
    b = f_log
    for k in steps:
        b = b + jnp.where(pos_in_chunk >= k, pltpu.roll(b, k, 1), 0.0)
    a = gt - b
    cm = a
    for k in steps:
        cm = jnp.maximum(cm, jnp.where(pos_in_chunk >= k, pltpu.roll(cm, k, 1), NEG_INF))
    a_r[...] = a
    b_r[...] = b
    pad = jnp.zeros((LANES - 8, seq), F32)
    cm_c[...] = jnp.concatenate([cm, pad], axis=0).T
    b_c[...] = jnp.concatenate([b, pad], axis=0).T

    c_s[...] = jnp.zeros((GW, GW), F32)
    n_s[...] = jnp.zeros((GW, GW), F32)
    lane_head = lax.broadcasted_iota(jnp.int32, (1, GW), 1) // HEAD_DIM
    same_head = (lax.broadcasted_iota(jnp.int32, (GW, GW), 0) // HEAD_DIM
                 == lax.broadcasted_iota(jnp.int32, (GW, GW), 1) // HEAD_DIM)
    causal = (lax.broadcasted_iota(jnp.int32, (BLOCK, BLOCK), 1)
              <= lax.broadcasted_iota(jnp.int32, (BLOCK, BLOCK), 0))

    heads = range(HEADS)
    low_half = lax.broadcasted_iota(jnp.int32, (1, LANES), 1) < HEAD_DIM
    ones_blk = jnp.ones((BLOCK, LANES), BF16)
    head_sum = (lax.broadcasted_iota(jnp.int32, (HEADS * BLOCK, GW), 0) // BLOCK
                == lax.broadcasted_iota(jnp.int32, (HEADS * BLOCK, GW), 1) // HEAD_DIM).astype(BF16)

    def head_lanes(per_head):
        return jnp.concatenate([jnp.where(low_half, per_head[0], per_head[1]),
                                jnp.where(low_half, per_head[2], per_head[3])], axis=1)

    for r in range(0, seq, rows):
        b_cols = b_c[r:r + rows, :]
        b_f[r:r + rows, :] = head_lanes([b_cols[:, h:h + 1] for h in heads])

    def chunk(c, m_run):
        r0 = pl.multiple_of(c * BLOCK, BLOCK)
        qb = q_s[pl.ds(r0, BLOCK), :]
        kb = k_s[pl.ds(r0, BLOCK), :]
        v = zd_ref[pl.ds(r0, BLOCK), 2 * GW:3 * GW]
        og = zd_ref[pl.ds(r0, BLOCK), 3 * GW:4 * GW].astype(F32)
        a_rows = a_r[:, pl.ds(r0, BLOCK)]
        b_rows = b_r[:, pl.ds(r0, BLOCK)]
        cm_cols = cm_c[pl.ds(r0, BLOCK), :]

        k_heads = jnp.concatenate(
            [jnp.where(lane_head == h, kb, jnp.zeros_like(kb)) for h in heads], axis=0)
        v_heads = jnp.concatenate(
            [jnp.where(lane_head == h, v, jnp.zeros_like(v)) for h in heads], axis=0)
        scores = _dot_nt(qb, k_heads)
        carried = _dot(qb, jnp.concatenate([c_s[...], n_s[...]], axis=1).astype(BF16))

        w_intra, g_rep, wk_rows, decays, m_next = [], [], [], [], []
        for h in heads:
            a_row = a_rows[h:h + 1, :]
            g = jnp.maximum(jnp.broadcast_to(cm_cols[:, h:h + 1], (BLOCK, BLOCK)), m_run[h])
            g_rep.append(g)
            w_intra.append(jnp.where(causal, jnp.exp(a_row - g), 0.0))
            g_end = jnp.maximum(m_run[h], jnp.max(a_row, axis=-1, keepdims=True))
            m_next.append(b_rows[h:h + 1, BLOCK - 1:BLOCK] + g_end)
            decays.append(jnp.exp(m_run[h] - g_end))
            wk_rows.append(jnp.exp(a_row - g_end))
        g_full = head_lanes(g_rep)
        inter_scale = jnp.exp(head_lanes(m_run) - g_full)
        floor = jnp.exp(-(b_f[pl.ds(r0, BLOCK), :] + g_full))

        wk = jnp.concatenate([jnp.broadcast_to(w, (HEAD_DIM, BLOCK)) for w in wk_rows], axis=0)
        decay = jnp.concatenate([jnp.broadcast_to(dd, (HEAD_DIM, 1)) for dd in decays], axis=0)
        ktw = kt_s[c] * wk
        update = _dot(ktw.astype(BF16), jnp.concatenate([v, ones_blk], axis=1))

        sqk = (scores * jnp.concatenate(w_intra, axis=1)).astype(BF16)
        intra = _dot(sqk, jnp.concatenate([v_heads, head_sum], axis=1))
        num = intra[:, :GW] + inter_scale * carried[:, :GW]
        den = intra[:, GW:] + inter_scale * carried[:, GW:]
        h_out = num / jnp.maximum(jnp.abs(den), floor)
        y_ref[pl.ds(r0, BLOCK), :] = (jax.nn.sigmoid(og) * h_out).astype(BF16)

        k_sum = update[:, GW:]
        c_s[...] = decay * c_s[...] + jnp.where(same_head, update[:, :GW], 0.0)
        n_s[...] = decay * n_s[...] + jnp.where(same_head, jnp.concatenate([k_sum, k_sum], axis=1), 0.0)
        return tuple(m_next)

    lax.fori_loop(0, seq // BLOCK, chunk, tuple(jnp.zeros((1, 1), F32) for _ in range(HEADS)), unroll=4)


def _mlstm(zd, zm, gb, cw, cb, bsz, seq):
    full = lambda a: pl.BlockSpec(a.shape, lambda b: (0,) * a.ndim)
    return pl.pallas_call(
        _mlstm_kernel,
        grid=(bsz,),
        in_specs=[pl.BlockSpec((seq, ZD_W), lambda b: (b, 0)),
                  pl.BlockSpec((seq, LANES), lambda b: (b, 0)), full(gb), full(cw), full(cb)],
        out_specs=pl.BlockSpec((seq, GW), lambda b: (b, 0)),
        out_shape=jax.ShapeDtypeStruct((bsz * seq, GW), BF16),
        scratch_shapes=[pltpu.VMEM((seq + CONV_PAD, 2 * GW), F32),
                        pltpu.VMEM((seq, GW), BF16), pltpu.VMEM((seq, GW), BF16),
                        pltpu.VMEM((seq // BLOCK, GW, BLOCK), F32),
                        pltpu.VMEM((8, seq), F32), pltpu.VMEM((8, seq), F32),
                        pltpu.VMEM((seq, LANES), F32), pltpu.VMEM((seq, LANES), F32),
                        pltpu.VMEM((seq, GW), F32),
                        pltpu.VMEM((GW, GW), F32), pltpu.VMEM((GW, GW), F32)],
        compiler_params=_params(),
        name="mlstm",
    )(zd, zm, gb, cw, cb)


FF_CHUNK = 256


def _dense_kernel(with_next, ya_ref, yb_ref, yd_ref, oc1_ref, oc4_ref, oc16_ref, lc1_ref, lc4_ref, lc16_ref,
                  x_ref, wo_ref, g1_ref, b1_ref, wg_ref, wu_ref, wd_ref, g2_ref, b2_ref, *rest):
    if with_next:
        (w_ref, wm_ref, cos_ref, sin_ref, o_ref, zd_ref, za_ref, zc1_ref, zc4_ref, zc16_ref, zb_ref, zm_ref,
         xb_ref, acc_ref, nat_ref, stage_ref) = rest
    else:
        o_ref, xb_ref, acc_ref, nat_ref = rest
    tm = x_ref.shape[0]

    def natural(view, d, slot):
        if d == 1:
            return view[...].astype(F32)
        for r in range(d):
            for half in range(GW // LANES):
                col = r * GW + half * LANES
                nat_ref[slot, half, pl.ds(r, tm // d, stride=d), :] = view[:, col:col + LANES].astype(F32)
        return jnp.concatenate([nat_ref[slot, half] for half in range(GW // LANES)], axis=1)

    views = (oc1_ref, oc4_ref, oc16_ref, lc1_ref, lc4_ref, lc16_ref)
    slots = iter(range(nat_ref.shape[0]))
    nat = [natural(v, d, next(slots) if d > 1 else None) for v, d in zip(views, DILATIONS * 2)]
    outs, lse = nat[:3], nat[3:]
    top = jnp.maximum(jnp.maximum(lse[0], lse[1]), lse[2])
    wts = [jnp.exp2(l - top) for l in lse]
    mix = wts[0] * outs[0] + wts[1] * outs[1] + wts[2] * outs[2]
    yc = (mix / (wts[0] + wts[1] + wts[2])).astype(BF16)
    acc = ALPHA * x_ref[...]
    for i, y in ((0, ya_ref[...]), (1, yb_ref[...]), (3, yd_ref[...]), (2, yc)):
        acc = acc + _dot(y, wo_ref[i * GW:(i + 1) * GW, :])
    x1 = _layer_norm(acc, g1_ref[...], b1_ref[...])
    xb_ref[...] = x1.astype(BF16)
    acc_ref[...] = ALPHA * x1
    for c in range(0, D_FF, FF_CHUNK):
        gate = _dot(xb_ref[...], wg_ref[:, c:c + FF_CHUNK])
        up = _dot(xb_ref[...], wu_ref[:, c:c + FF_CHUNK])
        act = (gate * jax.nn.sigmoid(gate) * up).astype(BF16)
        acc_ref[...] += _dot(act, wd_ref[c:c + FF_CHUNK, :])
    x2 = _layer_norm(acc_ref[...], g2_ref[...], b2_ref[...])
    o_ref[...] = x2
    if with_next:
        xb_ref[...] = x2.astype(BF16)
        _project_in(xb_ref, w_ref, wm_ref, cos_ref, sin_ref, zd_ref, za_ref, (zc1_ref, zc4_ref, zc16_ref),
                    zb_ref, zm_ref, stage_ref, 0, tm)


def _dense(ys, dil, x2, wo, g1, b1, wg, wu, wd, g2, b2, next_w, tables, tm):
    n = x2.shape[0]
    row = lambda w_: pl.BlockSpec((tm, w_), lambda i: (i, 0))
    view = lambda d: pl.BlockSpec((tm // d, d * GW), lambda i: (i, 0))
    once = lambda a: pl.BlockSpec(a.shape, lambda i: (0, 0), pipeline_mode=pl.Buffered(1))
    weights = [wo, g1, b1, wg, wu, wd, g2, b2] + list(next_w)
    tables = list(tables) if next_w else []
    out_specs = [row(D_MODEL)]
    out_shape = [jax.ShapeDtypeStruct((n, D_MODEL), F32)]
    scratch = [pltpu.VMEM((tm, D_MODEL), BF16), pltpu.VMEM((tm, D_MODEL), F32),
               pltpu.VMEM((2 * (len(DILATIONS) - 1), GW // LANES, tm, LANES), F32)]
    if next_w:
        z_specs, z_shapes, stage = _z_outputs(n, tm)
        out_specs += z_specs
        out_shape += z_shapes
        scratch.append(stage)
    return pl.pallas_call(
        functools.partial(_dense_kernel, bool(next_w)),
        grid=(n // tm,),
        in_specs=([row(GW)] * len(ys) + [view(d) for d in DILATIONS] * 2 + [row(D_MODEL)]
                  + [once(w) for w in weights] + [row(LANES)] * len(tables)),
        out_specs=out_specs,
        out_shape=out_shape,
        scratch_shapes=scratch,
        compiler_params=_params(),
        name="dense_block",
    )(*ys, *dil, x2, *weights, *tables)


def _pack_weights(w_in, b_w_uq, b_w_ukv, b_q_norm, a_bs, d_igate_b, d_fgate_b):
    nl = w_in.shape[0]
    o_b = ZA_W
    o_c = o_b + Q_RANK + KV_RANK + ROPE
    o_d = o_c + ZC_W
    o_g = o_d + ZD_W
    w_a = w_in[..., :o_b]
    w_cq = w_in[..., o_b:o_b + Q_RANK]
    w_ckv = w_in[..., o_b + Q_RANK:o_b + Q_RANK + KV_RANK]
    w_kr = w_in[..., o_b + Q_RANK + KV_RANK:o_c]
    w_c = w_in[..., o_c:o_d]
    w_d = w_in[..., o_d:o_g]
    w_gates = w_in[..., o_g:]
    zeros = lambda *s: jnp.zeros((nl,) + s, F32)
    w_b = jnp.concatenate([w_cq, zeros(D_MODEL, 256 - Q_RANK), w_ckv], -1)
    w_main = jnp.concatenate([w_d, w_a, w_c, w_b], -1).astype(BF16)
    half = ROPE // 2
    w_kr_rot = jnp.concatenate([-w_kr[..., half:], w_kr[..., :half]], -1)
    w_misc = jnp.concatenate(
        [w_gates, zeros(D_MODEL, KR_LANE - 2 * HEADS), w_kr, w_kr_rot], -1).astype(BF16)

    wq = b_w_uq.reshape(nl, Q_RANK, HEADS, NOPE + ROPE)
    nope, x1, x2 = wq[..., :NOPE], wq[..., NOPE:NOPE + half], wq[..., NOPE + half:]
    tail = zeros(Q_RANK, HEADS, LANES - NOPE - ROPE)
    rowpad = ((0, 0), (0, 256 - Q_RANK), (0, 0))
    wqm = jnp.pad(jnp.concatenate([nope, x1, x2, tail], -1).reshape(nl, Q_RANK, HEADS * LANES), rowpad)
    wqs = jnp.pad(jnp.concatenate([zeros(Q_RANK, HEADS, NOPE), -x2, x1, tail], -1)
                  .reshape(nl, Q_RANK, HEADS * LANES), rowpad)
    wkv = b_w_ukv.reshape(nl, KV_RANK, HEADS, NOPE + HEAD_DIM)
    wk = jnp.concatenate([wkv[..., :NOPE], zeros(KV_RANK, HEADS, LANES - NOPE)], -1)
    wk = wk.reshape(nl, KV_RANK, HEADS * LANES)
    wv = wkv[..., NOPE:].reshape(nl, KV_RANK, GW)
    qg = jnp.pad(b_q_norm, ((0, 0), (0, 256 - Q_RANK)))[:, None, :]
    bias = jnp.repeat(jnp.swapaxes(a_bs, 1, 2), HEAD_DIM, axis=-1)
    gb = jnp.concatenate([d_igate_b, d_fgate_b, zeros(LANES - 2 * HEADS)], -1)[:, None, :]
    return dict(w_main=w_main, w_misc=w_misc, wqm=wqm.astype(BF16), wqs=wqs.astype(BF16),
                wk=wk.astype(BF16), wv=wv.astype(BF16), qg=qg, bias=bias, gb=gb)


@jax.jit
def _forward(x, positions, w_in, a_ln_g, a_ln_b, a_ws, a_bs, b_q_norm, b_kv_norm, b_w_uq, b_w_ukv,
             d_conv_w, d_conv_b, d_igate_b, d_fgate_b, w_out, ln1_g, ln1_b, w_gate, w_up, w_down,
             ln2_g, ln2_b):
    bsz, seq, _ = x.shape
    pk = _pack_weights(w_in, b_w_uq, b_w_ukv, b_q_norm, a_bs, d_igate_b, d_fgate_b)
    w_out_b, w_gate_b, w_up_b, w_down_b = (w.astype(BF16) for w in (w_out, w_gate, w_up, w_down))
    cosd, sind, cosm, sinm = _rope_tables(positions)
    tm = 512
    x2 = x.reshape(bsz * seq, D_MODEL)
    row = lambda a, l: a[l][None, :]
    rope_d = (cosd.reshape(bsz * seq, LANES), sind.reshape(bsz * seq, LANES))
    zd, za, *zc, zb, zm = _inproj(x2, pk["w_main"][0], pk["w_misc"][0], *rope_d, tm)
    for l in range(DEPTH):
        ya = _gmlp(za, row(a_ln_g, l), row(a_ln_b, l), a_ws[l], pk["bias"][l], bsz, seq)
        yb = _mla(zb, zm, cosm, sinm, pk["qg"][l], row(b_kv_norm, l), pk["wqm"][l], pk["wqs"][l],
                  pk["wk"][l], pk["wv"][l], bsz, seq)
        dil = _dilated(zc, bsz, seq)
        yd = _mlstm(zd, zm, pk["gb"][l], d_conv_w[l], row(d_conv_b, l), bsz, seq)
        next_w = (pk["w_main"][l + 1], pk["w_misc"][l + 1]) if l + 1 < DEPTH else ()
        outs = _dense((ya, yb, yd), dil, x2, w_out_b[l], row(ln1_g, l), row(ln1_b, l), w_gate_b[l],
                      w_up_b[l], w_down_b[l], row(ln2_g, l), row(ln2_b, l), next_w, rope_d, tm)
        if next_w:
            x2, zd, za, *zc, zb, zm = outs
        else:
            x2 = outs[0]
    return x2.reshape(bsz, seq, D_MODEL)


def kernel(x, positions, w_in, a_ln_g, a_ln_b, a_ws, a_bs, b_q_norm, b_kv_norm, b_w_uq, b_w_ukv,
           d_conv_w, d_conv_b, d_igate_b, d_fgate_b, w_out, ln1_g, ln1_b, w_gate, w_up, w_down,
           ln2_g, ln2_b):
    return _forward(x, positions, w_in, a_ln_g, a_ln_b, a_ws, a_bs, b_q_norm, b_kv_norm, b_w_uq,
                    b_w_ukv, d_conv_w, d_conv_b, d_igate_b, d_fgate_b, w_out, ln1_g, ln1_b, w_gate,
                    w_up, w_down, ln2_g, ln2_b)
```

```python
import functools

import jax
import jax.numpy as jnp
from jax import lax
from jax.experimental import pallas as pl
from jax.experimental.pallas import tpu as pltpu

F32 = jnp.float32
BF16 = jnp.bfloat16

D_MODEL = 1024
DEPTH = 4
HEAD_DIM = 64
HEADS = 4
GW = HEADS * HEAD_DIM
BLOCK = 128
Q_RANK = 192
KV_RANK = 128
NOPE = 64
ROPE = 32
D_FF = 2816
ROPE_THETA = 10000.0
LN_EPS = 1e-5
RMS_EPS = 1e-6
ALPHA = (2 * DEPTH) ** 0.25
LANES = 128
NEG_INF = float("-inf")
LOG2_E = 1.4426950408889634
LN_2 = 0.6931471805599453

ZD_W, ZA_W, ZC_W, ZB_W = 4 * GW, 2 * GW, 3 * GW, 384
ZMAIN_W = ZD_W + ZA_W + ZC_W + ZB_W
KR_LANE = 64

VMEM_LIMIT = 60000 * 1024


def _dot(a, b):
    return jnp.dot(a, b, preferred_element_type=F32)


def _dot_nt(a, b):
    return lax.dot_general(a, b, (((1,), (1,)), ((), ())), preferred_element_type=F32)


def _params(n_axes=1):
    return pltpu.CompilerParams(
        dimension_semantics=("arbitrary",) * n_axes, vmem_limit_bytes=VMEM_LIMIT)


def _layer_norm(r, g, b):
    mu = jnp.mean(r, axis=-1, keepdims=True)
    d = r - mu
    var = jnp.mean(d * d, axis=-1, keepdims=True)
    return d * lax.rsqrt(var + LN_EPS) * g + b


def _tables_kernel(pos_ref, inv_ref, cosd_ref, sind_ref, cosm_ref, sinm_ref):
    pos = pos_ref[0].astype(F32)
    lane = lax.broadcasted_iota(jnp.int32, (1, LANES), 1)
    ang = pos * inv_ref[...]
    cos, sin = jnp.cos(ang), jnp.sin(ang)
    low = lane < HEAD_DIM
    cosd_ref[0] = jnp.where(low, cos, pltpu.roll(cos, HEAD_DIM, 1))
    sign = jnp.where((lane % HEAD_DIM) < HEAD_DIM // 2, -1.0, 1.0)
    sind_ref[0] = jnp.where(low, sin, pltpu.roll(sin, HEAD_DIM, 1)) * sign
    in_rope = (lane >= KR_LANE) & (lane < KR_LANE + ROPE)
    cosm_ref[0] = jnp.where(in_rope, cos, jnp.where(lane < KR_LANE, 1.0, 0.0))
    sinm_ref[0] = jnp.where(in_rope, sin, 0.0)


def _rope_tables(positions):
    bsz, seq = positions.shape
    lane = jnp.arange(LANES)
    half_d = HEAD_DIM // 2
    inv_d = jnp.power(ROPE_THETA, -jnp.arange(half_d, dtype=F32) / half_d)
    half_m = ROPE // 2
    inv_m = jnp.power(ROPE_THETA, -jnp.arange(half_m, dtype=F32) / half_m)
    inv = jnp.where(lane < HEAD_DIM, inv_d[lane % half_d],
                    jnp.where(lane < KR_LANE + ROPE, inv_m[lane % half_m], 0.0))[None, :]
    tab = jax.ShapeDtypeStruct((bsz, seq, LANES), F32)
    spec = pl.BlockSpec((1, seq, LANES), lambda b: (b, 0, 0))
    return pl.pallas_call(
        _tables_kernel,
        grid=(bsz,),
        in_specs=[pl.BlockSpec((1, seq, 1), lambda b: (b, 0, 0)), pl.BlockSpec((1, LANES), lambda b: (0, 0))],
        out_specs=[spec] * 4,
        out_shape=[tab] * 4,
        compiler_params=_params(),
        name="rope_tables",
    )(positions.reshape(bsz, seq, 1), inv)


def _rope_heads(x, cos, sin):
    first = (lax.broadcasted_iota(jnp.int32, (1, LANES), 1) % HEAD_DIM) < HEAD_DIM // 2
    halves = []
    for half in range(GW // LANES):
        xh = x[:, half * LANES:(half + 1) * LANES]
        rot = jnp.where(first, pltpu.roll(xh, LANES - HEAD_DIM // 2, 1), pltpu.roll(xh, HEAD_DIM // 2, 1))
        halves.append(xh * cos + rot * sin)
    return jnp.concatenate(halves, axis=1)


def _project_in(xb_ref, w_ref, wm_ref, cos_ref, sin_ref, zd_ref, za_ref, zc_refs, zb_ref, zm_ref, stage_ref,
                r0, nrows):
    rs = slice(r0, r0 + nrows)
    groups = {id(zd_ref): 0, id(za_ref): ZD_W, id(zc_refs): ZD_W + ZA_W, id(zb_ref): ZD_W + ZA_W + ZC_W}
    for ref in (zc_refs, zd_ref, za_ref, zb_ref):
        off = groups[id(ref)]
        width = ZC_W if ref is zc_refs else ref.shape[1]
        for c in range(0, width, 256):
            cw = min(256, width - c)
            z = _dot(xb_ref[rs, :], w_ref[:, off + c:off + c + cw])
            if ref is not zc_refs:
                ref[rs, c:c + cw] = z.astype(BF16)
                continue
            if c < 2 * GW:
                z = _rope_heads(z, cos_ref[rs, :], sin_ref[rs, :])
                if c == 0:
                    z = z * (HEAD_DIM ** -0.5 * LOG2_E)
            zc_refs[0][rs, c:c + cw] = z.astype(BF16)
            for half in range(GW // LANES):
                stage_ref[c // 256, half, rs, :] = z[:, half * LANES:(half + 1) * LANES]
            for d, view in zip(DILATIONS[1:], zc_refs[1:]):
                for r in range(d):
                    for half in range(GW // LANES):
                        col = r * ZC_W + c + half * LANES
                        rows = stage_ref[c // 256, half, pl.ds(r0 + r, nrows // d, stride=d), :]
                        view[r0 // d:(r0 + nrows) // d, col:col + LANES] = rows.astype(BF16)
    zm_ref[rs, :] = _dot(xb_ref[rs, :], wm_ref[...])


def _inproj_kernel(x_ref, w_ref, wm_ref, cos_ref, sin_ref, zd_ref, za_ref, zc1_ref, zc4_ref, zc16_ref, zb_ref,
                   zm_ref, xb_ref, stage_ref):
    xb_ref[...] = x_ref[...].astype(BF16)
    _project_in(xb_ref, w_ref, wm_ref, cos_ref, sin_ref, zd_ref, za_ref, (zc1_ref, zc4_ref, zc16_ref),
                zb_ref, zm_ref, stage_ref, 0, x_ref.shape[0])


def _z_outputs(n, tm):
    row = lambda w: pl.BlockSpec((tm, w), lambda i: (i, 0))
    view = lambda d: pl.BlockSpec((tm // d, d * ZC_W), lambda i: (i, 0))
    specs = [row(ZD_W), row(ZA_W)] + [view(d) for d in DILATIONS] + [row(ZB_W), row(LANES)]
    shapes = ([jax.ShapeDtypeStruct((n, ZD_W), BF16), jax.ShapeDtypeStruct((n, ZA_W), BF16)]
              + [jax.ShapeDtypeStruct((n // d, d * ZC_W), BF16) for d in DILATIONS]
              + [jax.ShapeDtypeStruct((n, ZB_W), BF16), jax.ShapeDtypeStruct((n, LANES), F32)])
    stage = pltpu.VMEM((ZC_W // 256, GW // LANES, tm, LANES), F32)
    return specs, shapes, stage


def _inproj(x2, w_main, w_misc, cosd, sind, tm):
    n = x2.shape[0]
    row = lambda w: pl.BlockSpec((tm, w), lambda i: (i, 0))
    full = lambda a: pl.BlockSpec(a.shape, lambda i: (0, 0))
    z_specs, z_shapes, stage = _z_outputs(n, tm)
    return pl.pallas_call(
        _inproj_kernel,
        grid=(n // tm,),
        in_specs=[row(D_MODEL), full(w_main), full(w_misc), row(LANES), row(LANES)],
        out_specs=z_specs,
        out_shape=z_shapes,
        scratch_shapes=[pltpu.VMEM((tm, D_MODEL), BF16), stage],
        compiler_params=_params(),
        name="inproj",
    )(x2, w_main, w_misc, cosd, sind)


def _gmlp_kernel(za_ref, lng_ref, lnb_ref, ws_ref, bias_ref, y_ref):
    seq = za_ref.shape[0]
    r_i = lax.broadcasted_iota(jnp.int32, (BLOCK, BLOCK), 0)
    c_i = lax.broadcasted_iota(jnp.int32, (BLOCK, BLOCK), 1)
    w_causal = [jnp.where(c_i <= r_i, ws_ref[h], 0.0).astype(BF16) for h in range(HEADS)]
    lane_head = lax.broadcasted_iota(jnp.int32, (BLOCK, GW), 1) // HEAD_DIM

    def chunk(c, carry):
        r0 = pl.multiple_of(c * BLOCK, BLOCK)
        z = za_ref[pl.ds(r0, BLOCK), :].astype(F32)
        g = 0.5 * z * (1.0 + lax.erf(z * (0.5 ** 0.5)))
        u, v = g[:, :GW], g[:, GW:]
        vb = _layer_norm(v, lng_ref[...], lnb_ref[...]).astype(BF16)
        mixed = jnp.zeros((BLOCK, GW), F32)
        for h in range(HEADS):
            mixed = jnp.where(lane_head == h, _dot(w_causal[h], vb), mixed)
        y_ref[pl.ds(r0, BLOCK), :] = (u * (mixed + bias_ref[...])).astype(BF16)
        return carry

    lax.fori_loop(0, seq // BLOCK, chunk, 0, unroll=4)


def _gmlp(za, ln_g, ln_b, ws, bias, bsz, seq):
    full = lambda a: pl.BlockSpec(a.shape, lambda b: (0,) * a.ndim)
    return pl.pallas_call(
        _gmlp_kernel,
        grid=(bsz,),
        in_specs=[pl.BlockSpec((seq, ZA_W), lambda b: (b, 0)), full(ln_g), full(ln_b), full(ws),
                  full(bias)],
        out_specs=pl.BlockSpec((seq, GW), lambda b: (b, 0)),
        out_shape=jax.ShapeDtypeStruct((bsz * seq, GW), BF16),
        compiler_params=_params(),
        name="gmlp",
    )(za, ln_g, ln_b, ws, bias)


MLA_TQ = 256


def _mla_kernel(zb_ref, zm_ref, cos_ref, sin_ref, qg_ref, kvg_ref, wqm_ref, wqs_ref, wk_ref, wv_ref,
                y_ref, q_s, k_s, vt_s, acc_s, sc_a, sc_b, p_a, p_b):
    seq = zb_ref.shape[0]
    tq = MLA_TQ
    scale = (NOPE + ROPE) ** -0.5 * LOG2_E
    lane = lax.broadcasted_iota(jnp.int32, (1, LANES), 1)
    in_rope = (lane >= KR_LANE) & (lane < KR_LANE + ROPE)
    rows = 512
    for r in range(0, seq, rows):
        cos = cos_ref[0, r:r + rows, :]
        sin = sin_ref[0, r:r + rows, :]
        cq = zb_ref[r:r + rows, 0:256].astype(F32)
        ms = jnp.sum(cq * cq, axis=-1, keepdims=True) * (1.0 / Q_RANK)
        cqn = (cq * lax.rsqrt(ms + RMS_EPS) * qg_ref[...]).astype(BF16)
        qm = _dot(cqn, wqm_ref[...])
        qs = _dot(cqn, wqs_ref[...])
        ckv = zb_ref[r:r + rows, 256:384].astype(F32)
        ms = jnp.mean(ckv * ckv, axis=-1, keepdims=True)
        ckvn = (ckv * lax.rsqrt(ms + RMS_EPS) * kvg_ref[...]).astype(BF16)
        kn = _dot(ckvn, wk_ref[...])
        v = _dot(ckvn, wv_ref[...])
        for sub in range(rows // tq):
            vt_s[r // tq + sub] = v[sub * tq:(sub + 1) * tq, :].T.astype(BF16)
        zm = zm_ref[r:r + rows, :]
        kr = jnp.where(in_rope, zm * cos + pltpu.roll(zm, LANES - ROPE, 1) * sin, 0.0)
        for h in range(HEADS):
            sl = slice(h * LANES, (h + 1) * LANES)
            q_s[r:r + rows, sl] = ((qm[:, sl] * cos + qs[:, sl] * sin) * scale).astype(BF16)
            k_s[r:r + rows, sl] = (kn[:, sl] + kr).astype(BF16)

    key_i = lax.broadcasted_iota(jnp.int32, (tq, tq), 0)
    qry_i = lax.broadcasted_iota(jnp.int32, (tq, tq), 1)
    heads = range(HEADS)
    head_lanes = [slice(h * LANES, (h + 1) * LANES) for h in heads]

    def put_scores(qb, j, dst):
        q0 = pl.multiple_of(qb * tq, tq)
        k0 = pl.multiple_of(j * tq, tq)
        for h in heads:
            dst[h] = _dot_nt(k_s[pl.ds(k0, tq), head_lanes[h]], q_s[pl.ds(q0, tq), head_lanes[h]])

    def qblock(i, shift):
        q0 = pl.multiple_of(i * tq, tq)
        acc_s[...] = jnp.zeros(acc_s.shape, F32)

        def softmax(src, p_dst, m_old, l_old, masked):
            m_new, l_new, alpha = [], [], []
            for h in heads:
                s = src[h]
                if masked:
                    s = jnp.where(key_i <= qry_i, s, NEG_INF)
                m = jnp.maximum(m_old[h], jnp.max(s, axis=0, keepdims=True))
                a = jnp.exp2(m_old[h] - m)
                p = jnp.exp2(s - m)
                m_new.append(m)
                alpha.append(a)
                l_new.append(a * l_old[h] + jnp.sum(p, axis=0, keepdims=True))
                p_dst[h] = p.astype(BF16)
            return tuple(m_new), tuple(l_new), tuple(alpha)

        def add_values(j, p_src, alpha):
            for h in heads:
                vt = vt_s[j, h * HEAD_DIM:(h + 1) * HEAD_DIM, :]
                acc_s[h] = alpha[h] * acc_s[h] + _dot(vt, p_src[h])

        even, odd = (sc_a, p_a), (sc_b, p_b)

        def on_parity(j, fn, state):
            return lax.cond((j + shift) % 2 == 0, lambda st: fn(even, odd, st), lambda st: fn(odd, even, st),
                            state)

        def step(j, state):
            def run(cur, other, st):
                m_old, l_old, alpha_prev = st
                put_scores(i, j + 1, other[0])
                m_new, l_new, alpha = softmax(cur[0], cur[1], m_old, l_old, False)
                add_values(jnp.maximum(j - 1, 0), other[1], alpha_prev)
                return m_new, l_new, alpha
            return on_parity(j, run, state)

        def finish(cur, other, st):
            m_old, l_old, alpha_prev = st
            put_scores(jnp.minimum(i + 1, n_qblocks - 1), 0, other[0])
            _, l_new, alpha = softmax(cur[0], cur[1], m_old, l_old, True)
            add_values(jnp.maximum(i - 1, 0), other[1], alpha_prev)
            add_values(i, cur[1], alpha)
            return l_new

        p_a[...] = jnp.zeros(p_a.shape, BF16)
        p_b[...] = jnp.zeros(p_b.shape, BF16)
        init = (tuple(jnp.full((1, tq), NEG_INF, F32) for _ in heads),
                tuple(jnp.zeros((1, tq), F32) for _ in heads),
                tuple(jnp.ones((1, tq), F32) for _ in heads))
        state = lax.fori_loop(0, i, step, init)
        l_fin = on_parity(i, finish, state)
        out_t = jnp.concatenate([acc_s[h] / l_fin[h] for h in heads], axis=0)
        y_ref[pl.ds(q0, tq), :] = out_t.T.astype(BF16)
        return (i + shift + 1) % 2

    n_qblocks = seq // tq
    put_scores(0, 0, sc_a)
    lax.fori_loop(0, n_qblocks, qblock, jnp.int32(0))


def _mla(zb, zm, cosm, sinm, qg, kvg, wqm, wqs, wk, wv, bsz, seq):
    full = lambda a: pl.BlockSpec(a.shape, lambda b: (0,) * a.ndim)
    tab = pl.BlockSpec((1, seq, LANES), lambda b: (b, 0, 0))
    return pl.pallas_call(
        _mla_kernel,
        grid=(bsz,),
        in_specs=[pl.BlockSpec((seq, ZB_W), lambda b: (b, 0)),
                  pl.BlockSpec((seq, LANES), lambda b: (b, 0)), tab, tab,
                  full(qg), full(kvg), full(wqm), full(wqs), full(wk), full(wv)],
        out_specs=pl.BlockSpec((seq, GW), lambda b: (b, 0)),
        out_shape=jax.ShapeDtypeStruct((bsz * seq, GW), BF16),
        scratch_shapes=[pltpu.VMEM((seq, HEADS * LANES), BF16), pltpu.VMEM((seq, HEADS * LANES), BF16),
                        pltpu.VMEM((seq // MLA_TQ, GW, MLA_TQ), BF16),
                        pltpu.VMEM((HEADS, HEAD_DIM, MLA_TQ), F32),
                        pltpu.VMEM((HEADS, MLA_TQ, MLA_TQ), F32),
                        pltpu.VMEM((HEADS, MLA_TQ, MLA_TQ), F32),
                        pltpu.VMEM((HEADS, MLA_TQ, MLA_TQ), BF16),
                        pltpu.VMEM((HEADS, MLA_TQ, MLA_TQ), BF16)],
        compiler_params=_params(),
        name="mla",
    )(zb, zm, cosm, sinm, qg, kvg, wqm, wqs, wk, wv)


DILATIONS = (1, 4, 16)
HALVES = GW // LANES


def _head_columns(cols):
    low_half = lax.broadcasted_iota(jnp.int32, (1, LANES), 1) < HEAD_DIM
    return jnp.concatenate([jnp.where(low_half, cols[0], cols[1]),
                            jnp.where(low_half, cols[2], cols[3])], axis=1)


def _band_block(q, kh, vh, w0, nk, mask):
    pairs = range(HEADS // 2)
    scores = []
    for pr in pairs:
        k_win = jnp.concatenate([kh[2 * pr + e, pl.ds(w0, nk), :] for e in range(2)], axis=0)
        scores.append(_dot_nt(q[:, pr * LANES:(pr + 1) * LANES], k_win))
    m_cols, l_cols, probs = [], [], []
    for h in range(HEADS):
        s = scores[h // 2][:, (h % 2) * nk:(h % 2 + 1) * nk]
        s = jnp.where(mask, s, NEG_INF)
        m = jnp.max(s, axis=-1, keepdims=True)
        p = jnp.exp2(s - m)
        m_cols.append(m)
        l_cols.append(jnp.sum(p, axis=-1, keepdims=True))
        probs.append(p.astype(BF16))
    outs = []
    for pr in pairs:
        v_win = jnp.concatenate([vh[2 * pr + e, pl.ds(w0, nk), :] for e in range(2)], axis=0)
        outs.append(_dot(jnp.concatenate(probs[2 * pr:2 * pr + 2], axis=1), v_win))
    l_full = _head_columns(l_cols)
    o = jnp.concatenate(outs, axis=1) / l_full
    return o, _head_columns(m_cols) + jnp.log2(l_full)


def _dilated_kernel(z1_ref, z4_ref, z16_ref, o1_ref, o4_ref, o16_ref, l1_ref, l4_ref, l16_ref, qb, kh, vh):
    seq = z1_ref.shape[0]
    lane = lax.broadcasted_iota(jnp.int32, (1, LANES), 1)
    views = {1: (z1_ref, o1_ref, l1_ref), 4: (z4_ref, o4_ref, l4_ref), 16: (z16_ref, o16_ref, l16_ref)}

    r_i = lax.broadcasted_iota(jnp.int32, (BLOCK, 2 * BLOCK), 0)
    c_i = lax.broadcasted_iota(jnp.int32, (BLOCK, 2 * BLOCK), 1)
    band = (c_i >= r_i) & (c_i <= r_i + BLOCK)
    causal = (lax.broadcasted_iota(jnp.int32, (BLOCK, BLOCK), 1)
              <= lax.broadcasted_iota(jnp.int32, (BLOCK, BLOCK), 0))
    low_half = lane < HEAD_DIM
    for h in range(HEADS):
        kh[h, 0:BLOCK, :] = jnp.zeros((BLOCK, LANES), BF16)
        vh[h, 0:BLOCK, :] = jnp.zeros((BLOCK, LANES), BF16)

    for d in DILATIONS:
        sub = seq // d
        z_ref, o_ref, l_ref = views[d]
        for res in range(d):
            dst = slice(BLOCK + res * sub, BLOCK + (res + 1) * sub)
            qb[res * sub:(res + 1) * sub, :] = z_ref[:, res * ZC_W:res * ZC_W + GW]
            for half in range(HALVES):
                col = lambda base: slice(res * ZC_W + base + half * LANES,
                                         res * ZC_W + base + (half + 1) * LANES)
                k_half = z_ref[:, col(GW)]
                v_half = z_ref[:, col(2 * GW)]
                zero = jnp.zeros_like(k_half)
                kh[2 * half, dst, :] = jnp.where(low_half, k_half, zero)
                kh[2 * half + 1, dst, :] = jnp.where(low_half, zero, k_half)
                vh[2 * half, dst, :] = jnp.where(low_half, v_half, zero)
                vh[2 * half + 1, dst, :] = jnp.where(low_half, zero, v_half)
        blocks_per_class = sub // BLOCK

        for res in range(d):
            cols = slice(res * GW, (res + 1) * GW)

            def block(n, carry, res=res, cols=cols, o_ref=o_ref, l_ref=l_ref):
                r0 = pl.multiple_of(res * sub + n * BLOCK, BLOCK)
                q = qb[pl.ds(r0, BLOCK), :]
                if blocks_per_class == 1:
                    o, lse = _band_block(q, kh, vh, r0 + BLOCK, BLOCK, causal)
                else:
                    first_key = jnp.where(n > 0, 0, BLOCK)
                    o, lse = _band_block(q, kh, vh, r0, 2 * BLOCK, band & (c_i >= first_key))
                rows = pl.ds(pl.multiple_of(n * BLOCK, BLOCK), BLOCK)
                o_ref[rows, cols] = o.astype(BF16)
                l_ref[rows, cols] = lse
                return carry

            if blocks_per_class == 1:
                block(0, 0)
            else:
                lax.fori_loop(0, blocks_per_class, block, 0, unroll=True)


def _dilated(zc_views, bsz, seq):
    headbuf = pltpu.VMEM((HEADS, seq + BLOCK, LANES), BF16)
    n = bsz * seq
    view = lambda width, d: pl.BlockSpec((seq // d, d * width), lambda b: (b, 0))
    return pl.pallas_call(
        _dilated_kernel,
        grid=(bsz,),
        in_specs=[view(ZC_W, d) for d in DILATIONS],
        out_specs=[view(GW, d) for d in DILATIONS] * 2,
        out_shape=([jax.ShapeDtypeStruct((n // d, d * GW), BF16) for d in DILATIONS]
                   + [jax.ShapeDtypeStruct((n // d, d * GW), F32) for d in DILATIONS]),
        scratch_shapes=[pltpu.VMEM((seq, GW), BF16), headbuf, headbuf],
        compiler_params=_params(),
        name="dilated",
    )(*zc_views)


CONV_W = 4
CONV_PAD = 8


def _log_sigmoid(x):
    return jnp.minimum(x, 0.0) - jnp.log1p(jnp.exp(-jnp.abs(x)))


def _mlstm_kernel(zd_ref, zm_ref, gb_ref, cw_ref, cb_ref, y_ref, xpad, q_s, k_s, kt_s, a_r, b_r, cm_c,
                  b_c, b_f, c_s, n_s):
    seq = zd_ref.shape[0]
    rows = 512
    xpad[0:CONV_PAD, :] = jnp.zeros((CONV_PAD, 2 * GW), F32)
    for r in range(0, seq, rows):
        xpad[CONV_PAD + r:CONV_PAD + r + rows, :] = zd_ref[r:r + rows, 0:2 * GW].astype(F32)
    for r in range(0, seq, rows):
        win = xpad[r:r + rows + CONV_PAD, :]
        acc = cb_ref[...] + cw_ref[CONV_W - 1:CONV_W, :] * win[CONV_PAD:, :]
        for k in range(1, CONV_W):
            acc = acc + cw_ref[CONV_W - 1 - k:CONV_W - k, :] * pltpu.roll(win, k, 0)[CONV_PAD:, :]
        qk = acc * jax.nn.sigmoid(acc)
        k = qk[:, GW:] * (HEAD_DIM ** -0.5)
        q_s[r:r + rows, :] = qk[:, :GW].astype(BF16)
        k_s[r:r + rows, :] = k.astype(BF16)
        for sub in range(rows // BLOCK):
            kt_s[r // BLOCK + sub] = k[sub * BLOCK:(sub + 1) * BLOCK, :].T

    gt = (zm_ref[...] + gb_ref[...]).T[0:8, :]
    f_log = _log_sigmoid(pltpu.roll(gt, HEADS, 0))
    pos_in_chunk = lax.broadcasted_iota(jnp.int32, (8, seq), 1) % BLOCK
    steps = [1 << s for s in range(BLOCK.bit_length() - 1)]
    b = f_log
    for k in steps:
        b = b + jnp.where(pos_in_chunk >= k, pltpu.roll(b, k, 1), 0.0)
    a = gt - b
    cm = a
    for k in steps:
        cm = jnp.maximum(cm, jnp.where(pos_in_chunk >= k, pltpu.roll(cm, k, 1), NEG_INF))
    a_r[...] = a
    b_r[...] = b
    pad = jnp.zeros((LANES - 8, seq), F32)
    cm_c[...] = jnp.concatenate([cm, pad], axis=0).T
    b_c[...] = jnp.concatenate([b, pad], axis=0).T

    c_s[...] = jnp.zeros((GW, GW), F32)
    n_s[...] = jnp.zeros((GW, GW), F32)
    lane_head = lax.broadcasted_iota(jnp.int32, (1, GW), 1) // HEAD_DIM
    same_head = (lax.broadcasted_iota(jnp.int32, (GW, GW), 0) // HEAD_DIM
                 == lax.broadcasted_iota(jnp.int32, (GW, GW), 1) // HEAD_DIM)
    causal = (lax.broadcasted_iota(jnp.int32, (BLOCK, BLOCK), 1)
              <= lax.broadcasted_iota(jnp.int32, (BLOCK, BLOCK), 0))

    heads = range(HEADS)
    low_half = lax.broadcasted_iota(jnp.int32, (1, LANES), 1) < HEAD_DIM
    ones_blk = jnp.ones((BLOCK, LANES), BF16)
    head_sum = (lax.broadcasted_iota(jnp.int32, (HEADS * BLOCK, GW), 0) // BLOCK
                == lax.broadcasted_iota(jnp.int32, (HEADS * BLOCK, GW), 1) // HEAD_DIM).astype(BF16)

    def head_lanes(per_head):
        return jnp.concatenate([jnp.where(low_half, per_head[0], per_head[1]),
                                jnp.where(low_half, per_head[2], per_head[3])], axis=1)

    for r in range(0, seq, rows):
        b_cols = b_c[r:r + rows, :]
        b_f[r:r + rows, :] = head_lanes([b_cols[:, h:h + 1] for h in heads])

    def chunk(c, m_run):
        r0 = pl.multiple_of(c * BLOCK, BLOCK)
        qb = q_s[pl.ds(r0, BLOCK), :]
        kb = k_s[pl.ds(r0, BLOCK), :]
        v = zd_ref[pl.ds(r0, BLOCK), 2 * GW:3 * GW]
        og = zd_ref[pl.ds(r0, BLOCK), 3 * GW:4 * GW].astype(F32)
        a_rows = a_r[:, pl.ds(r0, BLOCK)]
        b_rows = b_r[:, pl.ds(r0, BLOCK)]
        cm_cols = cm_c[pl.ds(r0, BLOCK), :]

        k_heads = jnp.concatenate(
            [jnp.where(lane_head == h, kb, jnp.zeros_like(kb)) for h in heads], axis=0)
        v_heads = jnp.concatenate(
            [jnp.where(lane_head == h, v, jnp.zeros_like(v)) for h in heads], axis=0)
        scores = _dot_nt(qb, k_heads)
        carried = _dot(qb, jnp.concatenate([c_s[...], n_s[...]], axis=1).astype(BF16))

        w_intra, g_rep, wk_rows, decays, m_next = [], [], [], [], []
        for h in heads:
            a_row = a_rows[h:h + 1, :]
            g = jnp.maximum(jnp.broadcast_to(cm_cols[:, h:h + 1], (BLOCK, BLOCK)), m_run[h])
            g_rep.append(g)
            w_intra.append(jnp.where(causal, jnp.exp(a_row - g), 0.0))
            g_end = jnp.maximum(m_run[h], jnp.max(a_row, axis=-1, keepdims=True))
            m_next.append(b_rows[h:h + 1, BLOCK - 1:BLOCK] + g_end)
            decays.append(jnp.exp(m_run[h] - g_end))
            wk_rows.append(jnp.exp(a_row - g_end))
        g_full = head_lanes(g_rep)
        inter_scale = jnp.exp(head_lanes(m_run) - g_full)
        floor = jnp.exp(-(b_f[pl.ds(r0, BLOCK), :] + g_full))

        wk = jnp.concatenate([jnp.broadcast_to(w, (HEAD_DIM, BLOCK)) for w in wk_rows], axis=0)
        decay = jnp.concatenate([jnp.broadcast_to(dd, (HEAD_DIM, 1)) for dd in decays], axis=0)
        ktw = kt_s[c] * wk
        update = _dot(ktw.astype(BF16), jnp.concatenate([v, ones_blk], axis=1))

        sqk = (scores * jnp.concatenate(w_intra, axis=1)).astype(BF16)
        intra = _dot(sqk, jnp.concatenate([v_heads, head_sum], axis=1))
        num = intra[:, :GW] + inter_scale * carried[:, :GW]
        den = intra[:, GW:] + inter_scale * carried[:, GW:]
        h_out = num / jnp.maximum(jnp.abs(den), floor)
        y_ref[pl.ds(r0, BLOCK), :] = (jax.nn.sigmoid(og) * h_out).astype(BF16)

        k_sum = update[:, GW:]
        c_s[...] = decay * c_s[...] + jnp.where(same_head, update[:, :GW], 0.0)
        n_s[...] = decay * n_s[...] + jnp.where(same_head, jnp.concatenate([k_sum, k_sum], axis=1), 0.0)
        return tuple(m_next)

    lax.fori_loop(0, seq // BLOCK, chunk, tuple(jnp.zeros((1, 1), F32) for _ in range(HEADS)), unroll=4)


def _mlstm(zd, zm, gb, cw, cb, bsz, seq):
    full = lambda a: pl.BlockSpec(a.shape, lambda b: (0,) * a.ndim)
    return pl.pallas_call(
        _mlstm_kernel,
        grid=(bsz,),
        in_specs=[pl.BlockSpec((seq, ZD_W), lambda b: (b, 0)),
                  pl.BlockSpec((seq, LANES), lambda b: (b, 0)), full(gb), full(cw), full(cb)],
        out_specs=pl.BlockSpec((seq, GW), lambda b: (b, 0)),
        out_shape=jax.ShapeDtypeStruct((bsz * seq, GW), BF16),
        scratch_shapes=[pltpu.VMEM((seq + CONV_PAD, 2 * GW), F32),
                        pltpu.VMEM((seq, GW), BF16), pltpu.VMEM((seq, GW), BF16),
                        pltpu.VMEM((seq // BLOCK, GW, BLOCK), F32),
                        pltpu.VMEM((8, seq), F32), pltpu.VMEM((8, seq), F32),
                        pltpu.VMEM((seq, LANES), F32), pltpu.VMEM((seq, LANES), F32),
                        pltpu.VMEM((seq, GW), F32),
                        pltpu.VMEM((GW, GW), F32), pltpu.VMEM((GW, GW), F32)],
        compiler_params=_params(),
        name="mlstm",
    )(zd, zm, gb, cw, cb)


FF_CHUNK = 256


def _dense_kernel(with_next, ya_ref, yb_ref, yd_ref, oc1_ref, oc4_ref, oc16_ref, lc1_ref, lc4_ref, lc16_ref,
                  x_ref, wo_ref, g1_ref, b1_ref, wg_ref, wu_ref, wd_ref, g2_ref, b2_ref, *rest):
    if with_next:
        (w_ref, wm_ref, cos_ref, sin_ref, o_ref, zd_ref, za_ref, zc1_ref, zc4_ref, zc16_ref, zb_ref, zm_ref,
         xb_ref, acc_ref, nat_ref, stage_ref) = rest
    else:
        o_ref, xb_ref, acc_ref, nat_ref = rest
    tm = x_ref.shape[0]

    def natural(view, d, slot):
        if d == 1:
            return view[...].astype(F32)
        for r in range(d):
            for half in range(GW // LANES):
                col = r * GW + half * LANES
                nat_ref[slot, half, pl.ds(r, tm // d, stride=d), :] = view[:, col:col + LANES].astype(F32)
        return jnp.concatenate([nat_ref[slot, half] for half in range(GW // LANES)], axis=1)

    views = (oc1_ref, oc4_ref, oc16_ref, lc1_ref, lc4_ref, lc16_ref)
    slots = iter(range(nat_ref.shape[0]))
    nat = [natural(v, d, next(slots) if d > 1 else None) for v, d in zip(views, DILATIONS * 2)]
    outs, lse = nat[:3], nat[3:]
    top = jnp.maximum(jnp.maximum(lse[0], lse[1]), lse[2])
    wts = [jnp.exp2(l - top) for l in lse]
    mix = wts[0] * outs[0] + wts[1] * outs[1] + wts[2] * outs[2]
    yc = (mix / (wts[0] + wts[1] + wts[2])).astype(BF16)
    acc = ALPHA * x_ref[...]
    for i, y in ((0, ya_ref[...]), (1, yb_ref[...]), (3, yd_ref[...]), (2, yc)):
        acc = acc + _dot(y, wo_ref[i * GW:(i + 1) * GW, :])
    x1 = _layer_norm(acc, g1_ref[...], b1_ref[...])
    xb_ref[...] = x1.astype(BF16)
    acc_ref[...] = ALPHA * x1
    for c in range(0, D_FF, FF_CHUNK):
        gate = _dot(xb_ref[...], wg_ref[:, c:c + FF_CHUNK])
        up = _dot(xb_ref[...], wu_ref[:, c:c + FF_CHUNK])
        act = (gate * jax.nn.sigmoid(gate) * up).astype(BF16)
        acc_ref[...] += _dot(act, wd_ref[c:c + FF_CHUNK, :])
    x2 = _layer_norm(acc_ref[...], g2_ref[...], b2_ref[...])
    o_ref[...] = x2
    if with_next:
        xb_ref[...] = x2.astype(BF16)
        _project_in(xb_ref, w_ref, wm_ref, cos_ref, sin_ref, zd_ref, za_ref, (zc1_ref, zc4_ref, zc16_ref),
                    zb_ref, zm_ref, stage_ref, 0, tm)


def _dense(ys, dil, x2, wo, g1, b1, wg, wu, wd, g2, b2, next_w, tables, tm):
    n = x2.shape[0]
    row = lambda w_: pl.BlockSpec((tm, w_), lambda i: (i, 0))
    view = lambda d: pl.BlockSpec((tm // d, d * GW), lambda i: (i, 0))
    once = lambda a: pl.BlockSpec(a.shape, lambda i: (0, 0), pipeline_mode=pl.Buffered(1))
    weights = [wo, g1, b1, wg, wu, wd, g2, b2] + list(next_w)
    tables = list(tables) if next_w else []
    out_specs = [row(D_MODEL)]
    out_shape = [jax.ShapeDtypeStruct((n, D_MODEL), F32)]
    scratch = [pltpu.VMEM((tm, D_MODEL), BF16), pltpu.VMEM((tm, D_MODEL), F32),
               pltpu.VMEM((2 * (len(DILATIONS) - 1), GW // LANES, tm, LANES), F32)]
    if next_w:
        z_specs, z_shapes, stage = _z_outputs(n, tm)
        out_specs += z_specs
        out_shape += z_shapes
        scratch.append(stage)
    return pl.pallas_call(
        functools.partial(_dense_kernel, bool(next_w)),
        grid=(n // tm,),
        in_specs=([row(GW)] * len(ys) + [view(d) for d in DILATIONS] * 2 + [row(D_MODEL)]
                  + [once(w) for w in weights] + [row(LANES)] * len(tables)),
        out_specs=out_specs,
        out_shape=out_shape,
        scratch_shapes=scratch,
        compiler_params=_params(),
        name="dense_block",
    )(*ys, *dil, x2, *weights, *tables)


def _pack_weights(w_in, b_w_uq, b_w_ukv, b_q_norm, a_bs, d_igate_b, d_fgate_b):
    nl = w_in.shape[0]
    o_b = ZA_W
    o_c = o_b + Q_RANK + KV_RANK + ROPE
    o_d = o_c + ZC_W
    o_g = o_d + ZD_W
    w_a = w_in[..., :o_b]
    w_cq = w_in[..., o_b:o_b + Q_RANK]
    w_ckv = w_in[..., o_b + Q_RANK:o_b + Q_RANK + KV_RANK]
    w_kr = w_in[..., o_b + Q_RANK + KV_RANK:o_c]
    w_c = w_in[..., o_c:o_d]
    w_d = w_in[..., o_d:o_g]
    w_gates = w_in[..., o_g:]
    zeros = lambda *s: jnp.zeros((nl,) + s, F32)
    w_b = jnp.concatenate([w_cq, zeros(D_MODEL, 256 - Q_RANK), w_ckv], -1)
    w_main = jnp.concatenate([w_d, w_a, w_c, w_b], -1).astype(BF16)
    half = ROPE // 2
    w_kr_rot = jnp.concatenate([-w_kr[..., half:], w_kr[..., :half]], -1)
    w_misc = jnp.concatenate(
        [w_gates, zeros(D_MODEL, KR_LANE - 2 * HEADS), w_kr, w_kr_rot], -1).astype(BF16)

    wq = b_w_uq.reshape(nl, Q_RANK, HEADS, NOPE + ROPE)
    nope, x1, x2 = wq[..., :NOPE], wq[..., NOPE:NOPE + half], wq[..., NOPE + half:]
    tail = zeros(Q_RANK, HEADS, LANES - NOPE - ROPE)
    rowpad = ((0, 0), (0, 256 - Q_RANK), (0, 0))
    wqm = jnp.pad(jnp.concatenate([nope, x1, x2, tail], -1).reshape(nl, Q_RANK, HEADS * LANES), rowpad)
    wqs = jnp.pad(jnp.concatenate([zeros(Q_RANK, HEADS, NOPE), -x2, x1, tail], -1)
                  .reshape(nl, Q_RANK, HEADS * LANES), rowpad)
    wkv = b_w_ukv.reshape(nl, KV_RANK, HEADS, NOPE + HEAD_DIM)
    wk = jnp.concatenate([wkv[..., :NOPE], zeros(KV_RANK, HEADS, LANES - NOPE)], -1)
    wk = wk.reshape(nl, KV_RANK, HEADS * LANES)
    wv = wkv[..., NOPE:].reshape(nl, KV_RANK, GW)
    qg = jnp.pad(b_q_norm, ((0, 0), (0, 256 - Q_RANK)))[:, None, :]
    bias = jnp.repeat(jnp.swapaxes(a_bs, 1, 2), HEAD_DIM, axis=-1)
    gb = jnp.concatenate([d_igate_b, d_fgate_b, zeros(LANES - 2 * HEADS)], -1)[:, None, :]
    return dict(w_main=w_main, w_misc=w_misc, wqm=wqm.astype(BF16), wqs=wqs.astype(BF16),
                wk=wk.astype(BF16), wv=wv.astype(BF16), qg=qg, bias=bias, gb=gb)


@jax.jit
def _forward(x, positions, w_in, a_ln_g, a_ln_b, a_ws, a_bs, b_q_norm, b_kv_norm, b_w_uq, b_w_ukv,
             d_conv_w, d_conv_b, d_igate_b, d_fgate_b, w_out, ln1_g, ln1_b, w_gate, w_up, w_down,
             ln2_g, ln2_b):
    bsz, seq, _ = x.shape
    pk = _pack_weights(w_in, b_w_uq, b_w_ukv, b_q_norm, a_bs, d_igate_b, d_fgate_b)
    w_out_b, w_gate_b, w_up_b, w_down_b = (w.astype(BF16) for w in (w_out, w_gate, w_up, w_down))
    cosd, sind, cosm, sinm = _rope_tables(positions)
    tm = 512
    x2 = x.reshape(bsz * seq, D_MODEL)
    row = lambda a, l: a[l][None, :]
    rope_d = (cosd.reshape(bsz * seq, LANES), sind.reshape(bsz * seq, LANES))
    zd, za, *zc, zb, zm = _inproj(x2, pk["w_main"][0], pk["w_misc"][0], *rope_d, tm)
    for l in range(DEPTH):
        ya = _gmlp(za, row(a_ln_g, l), row(a_ln_b, l), a_ws[l], pk["bias"][l], bsz, seq)
        yb = _mla(zb, zm, cosm, sinm, pk["qg"][l], row(b_kv_norm, l), pk["wqm"][l], pk["wqs"][l],
                  pk["wk"][l], pk["wv"][l], bsz, seq)
        dil = _dilated(zc, bsz, seq)
        yd = _mlstm(zd, zm, pk["gb"][l], d_conv_w[l], row(d_conv_b, l), bsz, seq)
        next_w = (pk["w_main"][l + 1], pk["w_misc"][l + 1]) if l + 1 < DEPTH else ()
        outs = _dense((ya, yb, yd), dil, x2, w_out_b[l], row(ln1_g, l), row(ln1_b, l), w_gate_b[l],
                      w_up_b[l], w_down_b[l], row(ln2_g, l), row(ln2_b, l), next_w, rope_d, tm)
        if next_w:
            x2, zd, za, *zc, zb, zm = outs
        else:
            x2 = outs[0]
    return x2.reshape(bsz, seq, D_MODEL)


def kernel(x, positions, w_in, a_ln_g, a_ln_b, a_ws, a_bs, b_q_norm, b_kv_norm, b_w_uq, b_w_ukv,
           d_conv_w, d_conv_b, d_igate_b, d_fgate_b, w_out, ln1_g, ln1_b, w_gate, w_up, w_down,
           ln2_g, ln2_b):
    return _forward(x, positions, w_in, a_ln_g, a_ln_b, a_ws, a_bs, b_q_norm, b_kv_norm, b_w_uq,
                    b_w_ukv, d_conv_w, d_conv_b, d_igate_b, d_fgate_b, w_out, ln1_g, ln1_b, w_gate,
                    w_up, w_down, ln2_g, ln2_b)
```

```python
import functools

import jax
import jax.numpy as jnp
from jax import lax
from jax.experimental import pallas as pl
from jax.experimental.pallas import tpu as pltpu

F32 = jnp.float32
BF16 = jnp.bfloat16

D_MODEL = 1024
DEPTH = 4
HEAD_DIM = 64
HEADS = 4
GW = HEADS * HEAD_DIM
BLOCK = 128
Q_RANK = 192
KV_RANK = 128
NOPE = 64
ROPE = 32
D_FF = 2816
ROPE_THETA = 10000.0
LN_EPS = 1e-5
RMS_EPS = 1e-6
ALPHA = (2 * DEPTH) ** 0.25
LANES = 128
NEG_INF = float("-inf")
LOG2_E = 1.4426950408889634
LN_2 = 0.6931471805599453

ZD_W, ZA_W, ZC_W, ZB_W = 4 * GW, 2 * GW, 3 * GW, 384
ZMAIN_W = ZD_W + ZA_W + ZC_W + ZB_W
KR_LANE = 64

VMEM_LIMIT = 60000 * 1024


def _dot(a, b):
    return jnp.dot(a, b, preferred_element_type=F32)


def _dot_nt(a, b):
    return lax.dot_general(a, b, (((1,), (1,)), ((), ())), preferred_element_type=F32)


def _params(n_axes=1):
    return pltpu.CompilerParams(
        dimension_semantics=("arbitrary",) * n_axes, vmem_limit_bytes=VMEM_LIMIT)


def _layer_norm(r, g, b):
    mu = jnp.mean(r, axis=-1, keepdims=True)
    d = r - mu
    var = jnp.mean(d * d, axis=-1, keepdims=True)
    return d * lax.rsqrt(var + LN_EPS) * g + b


def _tables_kernel(pos_ref, inv_ref, cosd_ref, sind_ref, cosm_ref, sinm_ref):
    pos = pos_ref[0].astype(F32)
    lane = lax.broadcasted_iota(jnp.int32, (1, LANES), 1)
    ang = pos * inv_ref[...]
    cos, sin = jnp.cos(ang), jnp.sin(ang)
    low = lane < HEAD_DIM
    cosd_ref[0] = jnp.where(low, cos, pltpu.roll(cos, HEAD_DIM, 1))
    sign = jnp.where((lane % HEAD_DIM) < HEAD_DIM // 2, -1.0, 1.0)
    sind_ref[0] = jnp.where(low, sin, pltpu.roll(sin, HEAD_DIM, 1)) * sign
    in_rope = (lane >= KR_LANE) & (lane < KR_LANE + ROPE)
    cosm_ref[0] = jnp.where(in_rope, cos, jnp.where(lane < KR_LANE, 1.0, 0.0))
    sinm_ref[0] = jnp.where(in_rope, sin, 0.0)


def _rope_tables(positions):
    bsz, seq = positions.shape
    lane = jnp.arange(LANES)
    half_d = HEAD_DIM // 2
    inv_d = jnp.power(ROPE_THETA, -jnp.arange(half_d, dtype=F32) / half_d)
    half_m = ROPE // 2
    inv_m = jnp.power(ROPE_THETA, -jnp.arange(half_m, dtype=F32) / half_m)
    inv = jnp.where(lane < HEAD_DIM, inv_d[lane % half_d],
                    jnp.where(lane < KR_LANE + ROPE, inv_m[lane % half_m], 0.0))[None, :]
    tab = jax.ShapeDtypeStruct((bsz, seq, LANES), F32)
    spec = pl.BlockSpec((1, seq, LANES), lambda b: (b, 0, 0))
    return pl.pallas_call(
        _tables_kernel,
        grid=(bsz,),
        in_specs=[pl.BlockSpec((1, seq, 1), lambda b: (b, 0, 0)), pl.BlockSpec((1, LANES), lambda b: (0, 0))],
        out_specs=[spec] * 4,
        out_shape=[tab] * 4,
        compiler_params=_params(),
        name="rope_tables",
    )(positions.reshape(bsz, seq, 1), inv)


def _rope_heads(x, cos, sin):
    first = (lax.broadcasted_iota(jnp.int32, (1, LANES), 1) % HEAD_DIM) < HEAD_DIM // 2
    halves = []
    for half in range(GW // LANES):
        xh = x[:, half * LANES:(half + 1) * LANES]
        rot = jnp.where(first, pltpu.roll(xh, LANES - HEAD_DIM // 2, 1), pltpu.roll(xh, HEAD_DIM // 2, 1))
        halves.append(xh * cos + rot * sin)
    return jnp.concatenate(halves, axis=1)


def _project_in(xb_ref, w_ref, wm_ref, cos_ref, sin_ref, zd_ref, za_ref, zc_refs, zb_ref, zm_ref, stage_ref,
                r0, nrows):
    rs = slice(r0, r0 + nrows)
    groups = {id(zd_ref): 0, id(za_ref): ZD_W, id(zc_refs): ZD_W + ZA_W, id(zb_ref): ZD_W + ZA_W + ZC_W}
    for ref in (zc_refs, zd_ref, za_ref, zb_ref):
        off = groups[id(ref)]
        width = ZC_W if ref is zc_refs else ref.shape[1]
        for c in range(0, width, 256):
            cw = min(256, width - c)
            z = _dot(xb_ref[rs, :], w_ref[:, off + c:off + c + cw])
            if ref is not zc_refs:
                ref[rs, c:c + cw] = z.astype(BF16)
                continue
            if c < 2 * GW:
                z = _rope_heads(z, cos_ref[rs, :], sin_ref[rs, :])
                if c == 0:
                    z = z * (HEAD_DIM ** -0.5 * LOG2_E)
            zc_refs[0][rs, c:c + cw] = z.astype(BF16)
            for half in range(GW // LANES):
                stage_ref[c // 256, half, rs, :] = z[:, half * LANES:(half + 1) * LANES]
            for d, view in zip(DILATIONS[1:], zc_refs[1:]):
                for r in range(d):
                    for half in range(GW // LANES):
                        col = r * ZC_W + c + half * LANES
                        rows = stage_ref[c // 256, half, pl.ds(r0 + r, nrows // d, stride=d), :]
                        view[r0 // d:(r0 + nrows) // d, col:col + LANES] = rows.astype(BF16)
    zm_ref[rs, :] = _dot(xb_ref[rs, :], wm_ref[...])


def _inproj_kernel(x_ref, w_ref, wm_ref, cos_ref, sin_ref, zd_ref, za_ref, zc1_ref, zc4_ref, zc16_ref, zb_ref,
                   zm_ref, xb_ref, stage_ref):
    xb_ref[...] = x_ref[...].astype(BF16)
    _project_in(xb_ref, w_ref, wm_ref, cos_ref, sin_ref, zd_ref, za_ref, (zc1_ref, zc4_ref, zc16_ref),
                zb_ref, zm_ref, stage_ref, 0, x_ref.shape[0])


def _z_outputs(n, tm):
    row = lambda w: pl.BlockSpec((tm, w), lambda i: (i, 0))
    view = lambda d: pl.BlockSpec((tm // d, d * ZC_W), lambda i: (i, 0))
    specs = [row(ZD_W), row(ZA_W)] + [view(d) for d in DILATIONS] + [row(ZB_W), row(LANES)]
    shapes = ([jax.ShapeDtypeStruct((n, ZD_W), BF16), jax.ShapeDtypeStruct((n, ZA_W), BF16)]
              + [jax.ShapeDtypeStruct((n // d, d * ZC_W), BF16) for d in DILATIONS]
              + [jax.ShapeDtypeStruct((n, ZB_W), BF16), jax.ShapeDtypeStruct((n, LANES), F32)])
    stage = pltpu.VMEM((ZC_W // 256, GW // LANES, tm, LANES), F32)
    return specs, shapes, stage


def _inproj(x2, w_main, w_misc, cosd, sind, tm):
    n = x2.shape[0]
    row = lambda w: pl.BlockSpec((tm, w), lambda i: (i, 0))
    full = lambda a: pl.BlockSpec(a.shape, lambda i: (0, 0))
    z_specs, z_shapes, stage = _z_outputs(n, tm)
    return pl.pallas_call(
        _inproj_kernel,
        grid=(n // tm,),
        in_specs=[row(D_MODEL), full(w_main), full(w_misc), row(LANES), row(LANES)],
        out_specs=z_specs,
        out_shape=z_shapes,
        scratch_shapes=[pltpu.VMEM((tm, D_MODEL), BF16), stage],
        compiler_params=_params(),
        name="inproj",
    )(x2, w_main, w_misc, cosd, sind)


def _gmlp_kernel(za_ref, lng_ref, lnb_ref, ws_ref, bias_ref, y_ref):
    seq = za_ref.shape[0]
    r_i = lax.broadcasted_iota(jnp.int32, (BLOCK, BLOCK), 0)
    c_i = lax.broadcasted_iota(jnp.int32, (BLOCK, BLOCK), 1)
    w_causal = [jnp.where(c_i <= r_i, ws_ref[h], 0.0).astype(BF16) for h in range(HEADS)]
    lane_head = lax.broadcasted_iota(jnp.int32, (BLOCK, GW), 1) // HEAD_DIM

    def chunk(c, carry):
        r0 = pl.multiple_of(c * BLOCK, BLOCK)
        z = za_ref[pl.ds(r0, BLOCK), :].astype(F32)
        g = 0.5 * z * (1.0 + lax.erf(z * (0.5 ** 0.5)))
        u, v = g[:, :GW], g[:, GW:]
        vb = _layer_norm(v, lng_ref[...], lnb_ref[...]).astype(BF16)
        mixed = jnp.zeros((BLOCK, GW), F32)
        for h in range(HEADS):
            mixed = jnp.where(lane_head == h, _dot(w_causal[h], vb), mixed)
        y_ref[pl.ds(r0, BLOCK), :] = (u * (mixed + bias_ref[...])).astype(BF16)
        return carry

    lax.fori_loop(0, seq // BLOCK, chunk, 0, unroll=4)


def _gmlp(za, ln_g, ln_b, ws, bias, bsz, seq):
    full = lambda a: pl.BlockSpec(a.shape, lambda b: (0,) * a.ndim)
    return pl.pallas_call(
        _gmlp_kernel,
        grid=(bsz,),
        in_specs=[pl.BlockSpec((seq, ZA_W), lambda b: (b, 0)), full(ln_g), full(ln_b), full(ws),
                  full(bias)],
        out_specs=pl.BlockSpec((seq, GW), lambda b: (b, 0)),
        out_shape=jax.ShapeDtypeStruct((bsz * seq, GW), BF16),
        compiler_params=_params(),
        name="gmlp",
    )(za, ln_g, ln_b, ws, bias)


MLA_TQ = 256


def _mla_kernel(zb_ref, zm_ref, cos_ref, sin_ref, qg_ref, kvg_ref, wqm_ref, wqs_ref, wk_ref, wv_ref,
                y_ref, q_s, k_s, vt_s, acc_s, sc_a, sc_b, p_a, p_b):
    seq = zb_ref.shape[0]
    tq = MLA_TQ
    scale = (NOPE + ROPE) ** -0.5 * LOG2_E
    lane = lax.broadcasted_iota(jnp.int32, (1, LANES), 1)
    in_rope = (lane >= KR_LANE) & (lane < KR_LANE + ROPE)
    rows = 512
    for r in range(0, seq, rows):
        cos = cos_ref[0, r:r + rows, :]
        sin = sin_ref[0, r:r + rows, :]
        cq = zb_ref[r:r + rows, 0:256].astype(F32)
        ms = jnp.sum(cq * cq, axis=-1, keepdims=True) * (1.0 / Q_RANK)
        cqn = (cq * lax.rsqrt(ms + RMS_EPS) * qg_ref[...]).astype(BF16)
        qm = _dot(cqn, wqm_ref[...])
        qs = _dot(cqn, wqs_ref[...])
        ckv = zb_ref[r:r + rows, 256:384].astype(F32)
        ms = jnp.mean(ckv * ckv, axis=-1, keepdims=True)
        ckvn = (ckv * lax.rsqrt(ms + RMS_EPS) * kvg_ref[...]).astype(BF16)
        kn = _dot(ckvn, wk_ref[...])
        v = _dot(ckvn, wv_ref[...])
        for sub in range(rows // tq):
            vt_s[r // tq + sub] = v[sub * tq:(sub + 1) * tq, :].T.astype(BF16)
        zm = zm_ref[r:r + rows, :]
        kr = jnp.where(in_rope, zm * cos + pltpu.roll(zm, LANES - ROPE, 1) * sin, 0.0)
        for h in range(HEADS):
            sl = slice(h * LANES, (h + 1) * LANES)
            q_h = (qm[:, sl] * cos + qs[:, sl] * sin) * scale
            for sub in range(rows // tq):
                q_s[r // tq + sub, sl, :] = q_h[sub * tq:(sub + 1) * tq, :].T.astype(BF16)
            k_s[r:r + rows, sl] = (kn[:, sl] + kr).astype(BF16)

    key_i = lax.broadcasted_iota(jnp.int32, (tq, tq), 0)
    qry_i = lax.broadcasted_iota(jnp.int32, (tq, tq), 1)
    heads = range(HEADS)
    head_lanes = [slice(h * LANES, (h + 1) * LANES) for h in heads]

    def put_scores(qb, j, dst):
        k0 = pl.multiple_of(j * tq, tq)
        for h in heads:
            dst[h] = _dot(k_s[pl.ds(k0, tq), head_lanes[h]], q_s[qb, head_lanes[h], :])

    def qblock(i, shift):
        q0 = pl.multiple_of(i * tq, tq)
        acc_s[...] = jnp.zeros(acc_s.shape, F32)

        def softmax(src, p_dst, m_old, l_old, masked):
            m_new, l_new, alpha = [], [], []
            for h in heads:
                s = src[h]
                if masked:
                    s = jnp.where(key_i <= qry_i, s, NEG_INF)
                m = jnp.maximum(m_old[h], jnp.max(s, axis=0, keepdims=True))
                a = jnp.exp2(m_old[h] - m)
                p = jnp.exp2(s - m)
                m_new.append(m)
                alpha.append(a)
                l_new.append(a * l_old[h] + jnp.sum(p, axis=0, keepdims=True))
                p_dst[h] = p.astype(BF16)
            return tuple(m_new), tuple(l_new), tuple(alpha)

        def add_values(j, p_src, alpha):
            for h in heads:
                vt = vt_s[j, h * HEAD_DIM:(h + 1) * HEAD_DIM, :]
                acc_s[h] = alpha[h] * acc_s[h] + _dot(vt, p_src[h])

        even, odd = (sc_a, p_a), (sc_b, p_b)

        def on_parity(j, fn, state):
            return lax.cond((j + shift) % 2 == 0, lambda st: fn(even, odd, st), lambda st: fn(odd, even, st),
                            state)

        def step(j, state):
            def run(cur, other, st):
                m_old, l_old, alpha_prev = st
                put_scores(i, j + 1, other[0])
                m_new, l_new, alpha = softmax(cur[0], cur[1], m_old, l_old, False)
                add_values(jnp.maximum(j - 1, 0), other[1], alpha_prev)
                return m_new, l_new, alpha
            return on_parity(j, run, state)

        def finish(cur, other, st):
            m_old, l_old, alpha_prev = st
            put_scores(jnp.minimum(i + 1, n_qblocks - 1), 0, other[0])
            _, l_new, alpha = softmax(cur[0], cur[1], m_old, l_old, True)
            add_values(jnp.maximum(i - 1, 0), other[1], alpha_prev)
            add_values(i, cur[1], alpha)
            return l_new

        p_a[...] = jnp.zeros(p_a.shape, BF16)
        p_b[...] = jnp.zeros(p_b.shape, BF16)
        init = (tuple(jnp.full((1, tq), NEG_INF, F32) for _ in heads),
                tuple(jnp.zeros((1, tq), F32) for _ in heads),
                tuple(jnp.ones((1, tq), F32) for _ in heads))
        state = lax.fori_loop(0, i, step, init)
        l_fin = on_parity(i, finish, state)
        out_t = jnp.concatenate([acc_s[h] / l_fin[h] for h in heads], axis=0)
        y_ref[pl.ds(q0, tq), :] = out_t.T.astype(BF16)
        return (i + shift + 1) % 2

    n_qblocks = seq // tq
    put_scores(0, 0, sc_a)
    lax.fori_loop(0, n_qblocks, qblock, jnp.int32(0))


def _mla(zb, zm, cosm, sinm, qg, kvg, wqm, wqs, wk, wv, bsz, seq):
    full = lambda a: pl.BlockSpec(a.shape, lambda b: (0,) * a.ndim)
    tab = pl.BlockSpec((1, seq, LANES), lambda b: (b, 0, 0))
    return pl.pallas_call(
        _mla_kernel,
        grid=(bsz,),
        in_specs=[pl.BlockSpec((seq, ZB_W), lambda b: (b, 0)),
                  pl.BlockSpec((seq, LANES), lambda b: (b, 0)), tab, tab,
                  full(qg), full(kvg), full(wqm), full(wqs), full(wk), full(wv)],
        out_specs=pl.BlockSpec((seq, GW), lambda b: (b, 0)),
        out_shape=jax.ShapeDtypeStruct((bsz * seq, GW), BF16),
        scratch_shapes=[pltpu.VMEM((seq // MLA_TQ, HEADS * LANES, MLA_TQ), BF16),
                        pltpu.VMEM((seq, HEADS * LANES), BF16),
                        pltpu.VMEM((seq // MLA_TQ, GW, MLA_TQ), BF16),
                        pltpu.VMEM((HEADS, HEAD_DIM, MLA_TQ), F32),
                        pltpu.VMEM((HEADS, MLA_TQ, MLA_TQ), F32),
                        pltpu.VMEM((HEADS, MLA_TQ, MLA_TQ), F32),
                        pltpu.VMEM((HEADS, MLA_TQ, MLA_TQ), BF16),
                        pltpu.VMEM((HEADS, MLA_TQ, MLA_TQ), BF16)],
        compiler_params=_params(),
        name="mla",
    )(zb, zm, cosm, sinm, qg, kvg, wqm, wqs, wk, wv)


DILATIONS = (1, 4, 16)
HALVES = GW // LANES


def _head_columns(cols):
    low_half = lax.broadcasted_iota(jnp.int32, (1, LANES), 1) < HEAD_DIM
    return jnp.concatenate([jnp.where(low_half, cols[0], cols[1]),
                            jnp.where(low_half, cols[2], cols[3])], axis=1)


def _band_block(q, kh, vh, w0, nk, mask):
    pairs = range(HEADS // 2)
    scores = []
    for pr in pairs:
        k_win = jnp.concatenate([kh[2 * pr + e, pl.ds(w0, nk), :] for e in range(2)], axis=0)
        scores.append(_dot_nt(q[:, pr * LANES:(pr + 1) * LANES], k_win))
    m_cols, l_cols, probs = [], [], []
    for h in range(HEADS):
        s = scores[h // 2][:, (h % 2) * nk:(h % 2 + 1) * nk]
        s = jnp.where(mask, s, NEG_INF)
        m = jnp.max(s, axis=-1, keepdims=True)
        p = jnp.exp2(s - m)
        m_cols.append(m)
        l_cols.append(jnp.sum(p, axis=-1, keepdims=True))
        probs.append(p.astype(BF16))
    outs = []
    for pr in pairs:
        v_win = jnp.concatenate([vh[2 * pr + e, pl.ds(w0, nk), :] for e in range(2)], axis=0)
        outs.append(_dot(jnp.concatenate(probs[2 * pr:2 * pr + 2], axis=1), v_win))
    l_full = _head_columns(l_cols)
    o = jnp.concatenate(outs, axis=1) / l_full
    return o, _head_columns(m_cols) + jnp.log2(l_full)


def _dilated_kernel(z1_ref, z4_ref, z16_ref, o1_ref, o4_ref, o16_ref, l1_ref, l4_ref, l16_ref, qb, kh, vh):
    seq = z1_ref.shape[0]
    lane = lax.broadcasted_iota(jnp.int32, (1, LANES), 1)
    views = {1: (z1_ref, o1_ref, l1_ref), 4: (z4_ref, o4_ref, l4_ref), 16: (z16_ref, o16_ref, l16_ref)}

    r_i = lax.broadcasted_iota(jnp.int32, (BLOCK, 2 * BLOCK), 0)
    c_i = lax.broadcasted_iota(jnp.int32, (BLOCK, 2 * BLOCK), 1)
    band = (c_i >= r_i) & (c_i <= r_i + BLOCK)
    causal = (lax.broadcasted_iota(jnp.int32, (BLOCK, BLOCK), 1)
              <= lax.broadcasted_iota(jnp.int32, (BLOCK, BLOCK), 0))
    low_half = lane < HEAD_DIM
    for h in range(HEADS):
        kh[h, 0:BLOCK, :] = jnp.zeros((BLOCK, LANES), BF16)
        vh[h, 0:BLOCK, :] = jnp.zeros((BLOCK, LANES), BF16)

    for d in DILATIONS:
        sub = seq // d
        z_ref, o_ref, l_ref = views[d]
        for res in range(d):
            dst = slice(BLOCK + res * sub, BLOCK + (res + 1) * sub)
            qb[res * sub:(res + 1) * sub, :] = z_ref[:, res * ZC_W:res * ZC_W + GW]
            for half in range(HALVES):
                col = lambda base: slice(res * ZC_W + base + half * LANES,
                                         res * ZC_W + base + (half + 1) * LANES)
                k_half = z_ref[:, col(GW)]
                v_half = z_ref[:, col(2 * GW)]
                zero = jnp.zeros_like(k_half)
                kh[2 * half, dst, :] = jnp.where(low_half, k_half, zero)
                kh[2 * half + 1, dst, :] = jnp.where(low_half, zero, k_half)
                vh[2 * half, dst, :] = jnp.where(low_half, v_half, zero)
                vh[2 * half + 1, dst, :] = jnp.where(low_half, zero, v_half)
        blocks_per_class = sub // BLOCK

        for res in range(d):
            cols = slice(res * GW, (res + 1) * GW)

            def block(n, carry, res=res, cols=cols, o_ref=o_ref, l_ref=l_ref):
                r0 = pl.multiple_of(res * sub + n * BLOCK, BLOCK)
                q = qb[pl.ds(r0, BLOCK), :]
                if blocks_per_class == 1:
                    o, lse = _band_block(q, kh, vh, r0 + BLOCK, BLOCK, causal)
                else:
                    first_key = jnp.where(n > 0, 0, BLOCK)
                    o, lse = _band_block(q, kh, vh, r0, 2 * BLOCK, band & (c_i >= first_key))
                rows = pl.ds(pl.multiple_of(n * BLOCK, BLOCK), BLOCK)
                o_ref[rows, cols] = o.astype(BF16)
                l_ref[rows, cols] = lse
                return carry

            if blocks_per_class == 1:
                block(0, 0)
            else:
                lax.fori_loop(0, blocks_per_class, block, 0, unroll=True)


def _dilated(zc_views, bsz, seq):
    headbuf = pltpu.VMEM((HEADS, seq + BLOCK, LANES), BF16)
    n = bsz * seq
    view = lambda width, d: pl.BlockSpec((seq // d, d * width), lambda b: (b, 0))
    return pl.pallas_call(
        _dilated_kernel,
        grid=(bsz,),
        in_specs=[view(ZC_W, d) for d in DILATIONS],
        out_specs=[view(GW, d) for d in DILATIONS] * 2,
        out_shape=([jax.ShapeDtypeStruct((n // d, d * GW), BF16) for d in DILATIONS]
                   + [jax.ShapeDtypeStruct((n // d, d * GW), F32) for d in DILATIONS]),
        scratch_shapes=[pltpu.VMEM((seq, GW), BF16), headbuf, headbuf],
        compiler_params=_params(),
        name="dilated",
    )(*zc_views)


CONV_W = 4
CONV_PAD = 8


def _log_sigmoid(x):
    return jnp.minimum(x, 0.0) - jnp.log1p(jnp.exp(-jnp.abs(x)))


def _mlstm_kernel(zd_ref, zm_ref, gb_ref, cw_ref, cb_ref, y_ref, xpad, q_s, k_s, kt_s, a_r, b_r, cm_c,
                  b_c, b_f, c_s, n_s):
    seq = zd_ref.shape[0]
    rows = 512
    xpad[0:CONV_PAD, :] = jnp.zeros((CONV_PAD, 2 * GW), F32)
    for r in range(0, seq, rows):
        xpad[CONV_PAD + r:CONV_PAD + r + rows, :] = zd_ref[r:r + rows, 0:2 * GW].astype(F32)
    def conv_chunk(r):
        win = xpad[r:r + rows + CONV_PAD, :]
        acc = cb_ref[...] + cw_ref[CONV_W - 1:CONV_W, :] * win[CONV_PAD:, :]
        for k in range(1, CONV_W):
            acc = acc + cw_ref[CONV_W - 1 - k:CONV_W - k, :] * pltpu.roll(win, k, 0)[CONV_PAD:, :]
        qk = acc * jax.nn.sigmoid(acc)
        k = qk[:, GW:] * (HEAD_DIM ** -0.5)
        q_s[r:r + rows, :] = qk[:, :GW].astype(BF16)
        k_s[r:r + rows, :] = k.astype(BF16)
        for sub in range(rows // BLOCK):
            kt_s[r // BLOCK + sub] = k[sub * BLOCK:(sub + 1) * BLOCK, :].T

    heads = range(HEADS)
    low_half = lax.broadcasted_iota(jnp.int32, (1, LANES), 1) < HEAD_DIM

    def head_lanes(per_head):
        return jnp.concatenate([jnp.where(low_half, per_head[0], per_head[1]),
                                jnp.where(low_half, per_head[2], per_head[3])], axis=1)

    def decay_lanes(r):
        b_cols = b_c[r:r + rows, :]
        b_f[r:r + rows, :] = head_lanes([b_cols[:, h:h + 1] for h in heads])

    conv_chunks = iter(range(0, seq, rows))

    gt = (zm_ref[...] + gb_ref[...]).T[0:8, :]
    f_log = _log_sigmoid(pltpu.roll(gt, HEADS, 0))
    pos_in_chunk = lax.broadcasted_iota(jnp.int32, (8, seq), 1) % BLOCK
    steps = [1 << s for s in range(BLOCK.bit_length() - 1)]
    conv_chunk(next(conv_chunks))
    b = f_log
    for k in steps:
        b = b + jnp.where(pos_in_chunk >= k, pltpu.roll(b, k, 1), 0.0)
    conv_chunk(next(conv_chunks))
    a = gt - b
    cm = a
    for k in steps:
        cm = jnp.maximum(cm, jnp.where(pos_in_chunk >= k, pltpu.roll(cm, k, 1), NEG_INF))
    conv_chunk(next(conv_chunks))
    a_r[...] = a
    b_r[...] = b
    pad = jnp.zeros((LANES - 8, seq), F32)
    cm_c[...] = jnp.concatenate([cm, pad], axis=0).T
    b_c[...] = jnp.concatenate([b, pad], axis=0).T
    conv_chunk(next(conv_chunks))
    for r in range(0, seq, rows):
        decay_lanes(r)

    c_s[...] = jnp.zeros((GW, GW), F32)
    n_s[...] = jnp.zeros((GW, GW), F32)
    lane_head = lax.broadcasted_iota(jnp.int32, (1, GW), 1) // HEAD_DIM
    same_head = (lax.broadcasted_iota(jnp.int32, (GW, GW), 0) // HEAD_DIM
                 == lax.broadcasted_iota(jnp.int32, (GW, GW), 1) // HEAD_DIM)
    causal = (lax.broadcasted_iota(jnp.int32, (BLOCK, BLOCK), 1)
              <= lax.broadcasted_iota(jnp.int32, (BLOCK, BLOCK), 0))

    ones_blk = jnp.ones((BLOCK, LANES), BF16)
    head_sum = (lax.broadcasted_iota(jnp.int32, (HEADS * BLOCK, GW), 0) // BLOCK
                == lax.broadcasted_iota(jnp.int32, (HEADS * BLOCK, GW), 1) // HEAD_DIM).astype(BF16)

    def chunk(c, m_run):
        r0 = pl.multiple_of(c * BLOCK, BLOCK)
        qb = q_s[pl.ds(r0, BLOCK), :]
        kb = k_s[pl.ds(r0, BLOCK), :]
        v = zd_ref[pl.ds(r0, BLOCK), 2 * GW:3 * GW]
        og = zd_ref[pl.ds(r0, BLOCK), 3 * GW:4 * GW].astype(F32)
        a_rows = a_r[:, pl.ds(r0, BLOCK)]
        b_rows = b_r[:, pl.ds(r0, BLOCK)]
        cm_cols = cm_c[pl.ds(r0, BLOCK), :]

        k_heads = jnp.concatenate(
            [jnp.where(lane_head == h, kb, jnp.zeros_like(kb)) for h in heads], axis=0)
        v_heads = jnp.concatenate(
            [jnp.where(lane_head == h, v, jnp.zeros_like(v)) for h in heads], axis=0)
        scores = _dot_nt(qb, k_heads)
        carried = _dot(qb, jnp.concatenate([c_s[...], n_s[...]], axis=1).astype(BF16))

        w_intra, g_rep, wk_rows, decays, m_next = [], [], [], [], []
        for h in heads:
            a_row = a_rows[h:h + 1, :]
            g = jnp.maximum(jnp.broadcast_to(cm_cols[:, h:h + 1], (BLOCK, BLOCK)), m_run[h])
            g_rep.append(g)
            w_intra.append(jnp.where(causal, jnp.exp(a_row - g), 0.0))
            g_end = jnp.maximum(m_run[h], jnp.max(a_row, axis=-1, keepdims=True))
            m_next.append(b_rows[h:h + 1, BLOCK - 1:BLOCK] + g_end)
            decays.append(jnp.exp(m_run[h] - g_end))
            wk_rows.append(jnp.exp(a_row - g_end))
        g_full = head_lanes(g_rep)
        inter_scale = jnp.exp(head_lanes(m_run) - g_full)
        floor = jnp.exp(-(b_f[pl.ds(r0, BLOCK), :] + g_full))

        wk = jnp.concatenate([jnp.broadcast_to(w, (HEAD_DIM, BLOCK)) for w in wk_rows], axis=0)
        decay = jnp.concatenate([jnp.broadcast_to(dd, (HEAD_DIM, 1)) for dd in decays], axis=0)
        ktw = kt_s[c] * wk
        update = _dot(ktw.astype(BF16), jnp.concatenate([v, ones_blk], axis=1))

        sqk = (scores * jnp.concatenate(w_intra, axis=1)).astype(BF16)
        intra = _dot(sqk, jnp.concatenate([v_heads, head_sum], axis=1))
        num = intra[:, :GW] + inter_scale * carried[:, :GW]
        den = intra[:, GW:] + inter_scale * carried[:, GW:]
        h_out = num / jnp.maximum(jnp.abs(den), floor)
        y_ref[pl.ds(r0, BLOCK), :] = (jax.nn.sigmoid(og) * h_out).astype(BF16)

        k_sum = update[:, GW:]
        c_s[...] = decay * c_s[...] + jnp.where(same_head, update[:, :GW], 0.0)
        n_s[...] = decay * n_s[...] + jnp.where(same_head, jnp.concatenate([k_sum, k_sum], axis=1), 0.0)
        return tuple(m_next)

    lax.fori_loop(0, seq // BLOCK, chunk, tuple(jnp.zeros((1, 1), F32) for _ in range(HEADS)), unroll=4)


def _mlstm(zd, zm, gb, cw, cb, bsz, seq):
    full = lambda a: pl.BlockSpec(a.shape, lambda b: (0,) * a.ndim)
    return pl.pallas_call(
        _mlstm_kernel,
        grid=(bsz,),
        in_specs=[pl.BlockSpec((seq, ZD_W), lambda b: (b, 0)),
                  pl.BlockSpec((seq, LANES), lambda b: (b, 0)), full(gb), full(cw), full(cb)],
        out_specs=pl.BlockSpec((seq, GW), lambda b: (b, 0)),
        out_shape=jax.ShapeDtypeStruct((bsz * seq, GW), BF16),
        scratch_shapes=[pltpu.VMEM((seq + CONV_PAD, 2 * GW), F32),
                        pltpu.VMEM((seq, GW), BF16), pltpu.VMEM((seq, GW), BF16),
                        pltpu.VMEM((seq // BLOCK, GW, BLOCK), F32),
                        pltpu.VMEM((8, seq), F32), pltpu.VMEM((8, seq), F32),
                        pltpu.VMEM((seq, LANES), F32), pltpu.VMEM((seq, LANES), F32),
                        pltpu.VMEM((seq, GW), F32),
                        pltpu.VMEM((GW, GW), F32), pltpu.VMEM((GW, GW), F32)],
        compiler_params=_params(),
        name="mlstm",
    )(zd, zm, gb, cw, cb)


FF_CHUNK = 256


def _dense_kernel(with_next, ya_ref, yb_ref, yd_ref, oc1_ref, oc4_ref, oc16_ref, lc1_ref, lc4_ref, lc16_ref,
                  x_ref, wo_ref, g1_ref, b1_ref, wg_ref, wu_ref, wd_ref, g2_ref, b2_ref, *rest):
    if with_next:
        (w_ref, wm_ref, cos_ref, sin_ref, o_ref, zd_ref, za_ref, zc1_ref, zc4_ref, zc16_ref, zb_ref, zm_ref,
         xb_ref, acc_ref, nat_ref, stage_ref) = rest
    else:
        o_ref, xb_ref, acc_ref, nat_ref = rest
    tm = x_ref.shape[0]

    def natural(view, d, slot):
        if d == 1:
            return view[...].astype(F32)
        for r in range(d):
            for half in range(GW // LANES):
                col = r * GW + half * LANES
                nat_ref[slot, half, pl.ds(r, tm // d, stride=d), :] = view[:, col:col + LANES].astype(F32)
        return jnp.concatenate([nat_ref[slot, half] for half in range(GW // LANES)], axis=1)

    views = (oc1_ref, oc4_ref, oc16_ref, lc1_ref, lc4_ref, lc16_ref)
    slots = iter(range(nat_ref.shape[0]))
    nat = [natural(v, d, next(slots) if d > 1 else None) for v, d in zip(views, DILATIONS * 2)]
    outs, lse = nat[:3], nat[3:]
    top = jnp.maximum(jnp.maximum(lse[0], lse[1]), lse[2])
    wts = [jnp.exp2(l - top) for l in lse]
    mix = wts[0] * outs[0] + wts[1] * outs[1] + wts[2] * outs[2]
    yc = (mix / (wts[0] + wts[1] + wts[2])).astype(BF16)
    acc = ALPHA * x_ref[...]
    for i, y in ((0, ya_ref[...]), (1, yb_ref[...]), (3, yd_ref[...]), (2, yc)):
        acc = acc + _dot(y, wo_ref[i * GW:(i + 1) * GW, :])
    x1 = _layer_norm(acc, g1_ref[...], b1_ref[...])
    xb_ref[...] = x1.astype(BF16)
    acc_ref[...] = ALPHA * x1
    for c in range(0, D_FF, FF_CHUNK):
        gate = _dot(xb_ref[...], wg_ref[:, c:c + FF_CHUNK])
        up = _dot(xb_ref[...], wu_ref[:, c:c + FF_CHUNK])
        act = (gate * jax.nn.sigmoid(gate) * up).astype(BF16)
        acc_ref[...] += _dot(act, wd_ref[c:c + FF_CHUNK, :])
    x2 = _layer_norm(acc_ref[...], g2_ref[...], b2_ref[...])
    o_ref[...] = x2
    if with_next:
        xb_ref[...] = x2.astype(BF16)
        _project_in(xb_ref, w_ref, wm_ref, cos_ref, sin_ref, zd_ref, za_ref, (zc1_ref, zc4_ref, zc16_ref),
                    zb_ref, zm_ref, stage_ref, 0, tm)


def _dense(ys, dil, x2, wo, g1, b1, wg, wu, wd, g2, b2, next_w, tables, tm):
    n = x2.shape[0]
    row = lambda w_: pl.BlockSpec((tm, w_), lambda i: (i, 0))
    view = lambda d: pl.BlockSpec((tm // d, d * GW), lambda i: (i, 0))
    once = lambda a: pl.BlockSpec(a.shape, lambda i: (0, 0), pipeline_mode=pl.Buffered(1))
    weights = [wo, g1, b1, wg, wu, wd, g2, b2] + list(next_w)
    tables = list(tables) if next_w else []
    out_specs = [row(D_MODEL)]
    out_shape = [jax.ShapeDtypeStruct((n, D_MODEL), F32)]
    scratch = [pltpu.VMEM((tm, D_MODEL), BF16), pltpu.VMEM((tm, D_MODEL), F32),
               pltpu.VMEM((2 * (len(DILATIONS) - 1), GW // LANES, tm, LANES), F32)]
    if next_w:
        z_specs, z_shapes, stage = _z_outputs(n, tm)
        out_specs += z_specs
        out_shape += z_shapes
        scratch.append(stage)
    return pl.pallas_call(
        functools.partial(_dense_kernel, bool(next_w)),
        grid=(n // tm,),
        in_specs=([row(GW)] * len(ys) + [view(d) for d in DILATIONS] * 2 + [row(D_MODEL)]
                  + [once(w) for w in weights] + [row(LANES)] * len(tables)),
        out_specs=out_specs,
        out_shape=out_shape,
        scratch_shapes=scratch,
        compiler_params=_params(),
        name="dense_block",
    )(*ys, *dil, x2, *weights, *tables)


def _pack_weights(w_in, b_w_uq, b_w_ukv, b_q_norm, a_bs, d_igate_b, d_fgate_b):
    nl = w_in.shape[0]
    o_b = ZA_W
    o_c = o_b + Q_RANK + KV_RANK + ROPE
    o_d = o_c + ZC_W
    o_g = o_d + ZD_W
    w_a = w_in[..., :o_b]
    w_cq = w_in[..., o_b:o_b + Q_RANK]
    w_ckv = w_in[..., o_b + Q_RANK:o_b + Q_RANK + KV_RANK]
    w_kr = w_in[..., o_b + Q_RANK + KV_RANK:o_c]
    w_c = w_in[..., o_c:o_d]
    w_d = w_in[..., o_d:o_g]
    w_gates = w_in[..., o_g:]
    zeros = lambda *s: jnp.zeros((nl,) + s, F32)
    w_b = jnp.concatenate([w_cq, zeros(D_MODEL, 256 - Q_RANK), w_ckv], -1)
    w_main = jnp.concatenate([w_d, w_a, w_c, w_b], -1).astype(BF16)
    half = ROPE // 2
    w_kr_rot = jnp.concatenate([-w_kr[..., half:], w_kr[..., :half]], -1)
    w_misc = jnp.concatenate(
        [w_gates, zeros(D_MODEL, KR_LANE - 2 * HEADS), w_kr, w_kr_rot], -1).astype(BF16)

    wq = b_w_uq.reshape(nl, Q_RANK, HEADS, NOPE + ROPE)
    nope, x1, x2 = wq[..., :NOPE], wq[..., NOPE:NOPE + half], wq[..., NOPE + half:]
    tail = zeros(Q_RANK, HEADS, LANES - NOPE - ROPE)
    rowpad = ((0, 0), (0, 256 - Q_RANK), (0, 0))
    wqm = jnp.pad(jnp.concatenate([nope, x1, x2, tail], -1).reshape(nl, Q_RANK, HEADS * LANES), rowpad)
    wqs = jnp.pad(jnp.concatenate([zeros(Q_RANK, HEADS, NOPE), -x2, x1, tail], -1)
                  .reshape(nl, Q_RANK, HEADS * LANES), rowpad)
    wkv = b_w_ukv.reshape(nl, KV_RANK, HEADS, NOPE + HEAD_DIM)
    wk = jnp.concatenate([wkv[..., :NOPE], zeros(KV_RANK, HEADS, LANES - NOPE)], -1)
    wk = wk.reshape(nl, KV_RANK, HEADS * LANES)
    wv = wkv[..., NOPE:].reshape(nl, KV_RANK, GW)
    qg = jnp.pad(b_q_norm, ((0, 0), (0, 256 - Q_RANK)))[:, None, :]
    bias = jnp.repeat(jnp.swapaxes(a_bs, 1, 2), HEAD_DIM, axis=-1)
    gb = jnp.concatenate([d_igate_b, d_fgate_b, zeros(LANES - 2 * HEADS)], -1)[:, None, :]
    return dict(w_main=w_main, w_misc=w_misc, wqm=wqm.astype(BF16), wqs=wqs.astype(BF16),
                wk=wk.astype(BF16), wv=wv.astype(BF16), qg=qg, bias=bias, gb=gb)


@jax.jit
def _forward(x, positions, w_in, a_ln_g, a_ln_b, a_ws, a_bs, b_q_norm, b_kv_norm, b_w_uq, b_w_ukv,
             d_conv_w, d_conv_b, d_igate_b, d_fgate_b, w_out, ln1_g, ln1_b, w_gate, w_up, w_down,
             ln2_g, ln2_b):
    bsz, seq, _ = x.shape
    pk = _pack_weights(w_in, b_w_uq, b_w_ukv, b_q_norm, a_bs, d_igate_b, d_fgate_b)
    w_out_b, w_gate_b, w_up_b, w_down_b = (w.astype(BF16) for w in (w_out, w_gate, w_up, w_down))
    cosd, sind, cosm, sinm = _rope_tables(positions)
    tm = 512
    x2 = x.reshape(bsz * seq, D_MODEL)
    row = lambda a, l: a[l][None, :]
    rope_d = (cosd.reshape(bsz * seq, LANES), sind.reshape(bsz * seq, LANES))
    zd, za, *zc, zb, zm = _inproj(x2, pk["w_main"][0], pk["w_misc"][0], *rope_d, tm)
    for l in range(DEPTH):
        ya = _gmlp(za, row(a_ln_g, l), row(a_ln_b, l), a_ws[l], pk["bias"][l], bsz, seq)
        yb = _mla(zb, zm, cosm, sinm, pk["qg"][l], row(b_kv_norm, l), pk["wqm"][l], pk["wqs"][l],
                  pk["wk"][l], pk["wv"][l], bsz, seq)
        dil = _dilated(zc, bsz, seq)
        yd = _mlstm(zd, zm, pk["gb"][l], d_conv_w[l], row(d_conv_b, l), bsz, seq)
        next_w = (pk["w_main"][l + 1], pk["w_misc"][l + 1]) if l + 1 < DEPTH else ()
        outs = _dense((ya, yb, yd), dil, x2, w_out_b[l], row(ln1_g, l), row(ln1_b, l), w_gate_b[l],
                      w_up_b[l], w_down_b[l], row(ln2_g, l), row(ln2_b, l), next_w, rope_d, tm)
        if next_w:
            x2, zd, za, *zc, zb, zm = outs
        else:
            x2 = outs[0]
    return x2.reshape(bsz, seq, D_MODEL)


def kernel(x, positions, w_in, a_ln_g, a_ln_b, a_ws, a_bs, b_q_norm, b_kv_norm, b_w_uq, b_w_ukv,
           d_conv_w, d_conv_b, d_igate_b, d_fgate_b, w_out, ln1_g, ln1_b, w_gate, w_up, w_down,
           ln2_g, ln2_b):
    return _forward(x, positions, w_in, a_ln_g, a_ln_b, a_ws, a_bs, b_q_norm, b_kv_norm, b_w_uq,
                    b_w_ukv, d_conv_w, d_conv_b, d_igate_b, d_fgate_b, w_out, ln1_g, ln1_b, w_gate,
                    w_up, w_down, ln2_g, ln2_b)
```

```python
import functools

import jax
import jax.numpy as jnp
from jax import lax
from jax.experimental import pallas as pl
from jax.experimental.pallas import tpu as pltpu

F32 = jnp.float32
BF16 = jnp.bfloat16

D_MODEL = 1024
DEPTH = 4
HEAD_DIM = 64
HEADS = 4
GW = HEADS * HEAD_DIM
BLOCK = 128
Q_RANK = 192
KV_RANK = 128
NOPE = 64
ROPE = 32
D_FF = 2816
ROPE_THETA = 10000.0
LN_EPS = 1e-5
RMS_EPS = 1e-6
ALPHA = (2 * DEPTH) ** 0.25
LANES = 128
NEG_INF = float("-inf")
LOG2_E = 1.4426950408889634

Q_PAD = 256
ZD_W, ZA_W, ZC_W, ZB_W = 4 * GW, 2 * GW, 3 * GW, Q_PAD + KV_RANK
ZMAIN_W = ZD_W + ZA_W + ZC_W + ZB_W
KR_LANE = 64

VMEM_LIMIT = 60000 * 1024


def _dot(a, b):
    return jnp.dot(a, b, preferred_element_type=F32)


def _dot_nt(a, b):
    return lax.dot_general(a, b, (((1,), (1,)), ((), ())), preferred_element_type=F32)


def _params(n_axes=1):
    return pltpu.CompilerParams(
        dimension_semantics=("arbitrary",) * n_axes, vmem_limit_bytes=VMEM_LIMIT)


def _layer_norm(r, g, b):
    mu = jnp.mean(r, axis=-1, keepdims=True)
    d = r - mu
    var = jnp.mean(d * d, axis=-1, keepdims=True)
    return d * lax.rsqrt(var + LN_EPS) * g + b


def _tables_kernel(pos_ref, inv_ref, cosd_ref, sind_ref, cosm_ref, sinm_ref):
    pos = pos_ref[0].astype(F32)
    lane = lax.broadcasted_iota(jnp.int32, (1, LANES), 1)
    ang = pos * inv_ref[...]
    cos, sin = jnp.cos(ang), jnp.sin(ang)
    low = lane < HEAD_DIM
    cosd_ref[0] = jnp.where(low, cos, pltpu.roll(cos, HEAD_DIM, 1))
    sign = jnp.where((lane % HEAD_DIM) < HEAD_DIM // 2, -1.0, 1.0)
    sind_ref[0] = jnp.where(low, sin, pltpu.roll(sin, HEAD_DIM, 1)) * sign
    in_rope = (lane >= KR_LANE) & (lane < KR_LANE + ROPE)
    cosm_ref[0] = jnp.where(in_rope, cos, jnp.where(lane < KR_LANE, 1.0, 0.0))
    sinm_ref[0] = jnp.where(in_rope, sin, 0.0)


def _rope_tables(positions):
    bsz, seq = positions.shape
    lane = jnp.arange(LANES)
    half_d = HEAD_DIM // 2
    inv_d = jnp.power(ROPE_THETA, -jnp.arange(half_d, dtype=F32) / half_d)
    half_m = ROPE // 2
    inv_m = jnp.power(ROPE_THETA, -jnp.arange(half_m, dtype=F32) / half_m)
    inv = jnp.where(lane < HEAD_DIM, inv_d[lane % half_d],
                    jnp.where(lane < KR_LANE + ROPE, inv_m[lane % half_m], 0.0))[None, :]
    tab = jax.ShapeDtypeStruct((bsz, seq, LANES), F32)
    spec = pl.BlockSpec((1, seq, LANES), lambda b: (b, 0, 0))
    return pl.pallas_call(
        _tables_kernel,
        grid=(bsz,),
        in_specs=[pl.BlockSpec((1, seq, 1), lambda b: (b, 0, 0)), pl.BlockSpec((1, LANES), lambda b: (0, 0))],
        out_specs=[spec] * 4,
        out_shape=[tab] * 4,
        compiler_params=_params(),
        name="rope_tables",
    )(positions.reshape(bsz, seq, 1), inv)


def _rope_heads(x, cos, sin):
    first = (lax.broadcasted_iota(jnp.int32, (1, LANES), 1) % HEAD_DIM) < HEAD_DIM // 2
    halves = []
    for half in range(GW // LANES):
        xh = x[:, half * LANES:(half + 1) * LANES]
        rot = jnp.where(first, pltpu.roll(xh, LANES - HEAD_DIM // 2, 1), pltpu.roll(xh, HEAD_DIM // 2, 1))
        halves.append(xh * cos + rot * sin)
    return jnp.concatenate(halves, axis=1)


def _project_in(xb_ref, w_ref, wm_ref, cos_ref, sin_ref, zd_ref, za_ref, zc_refs, zb_ref, zm_ref, stage_ref,
                r0, nrows):
    rs = slice(r0, r0 + nrows)
    groups = {id(zd_ref): 0, id(za_ref): ZD_W, id(zc_refs): ZD_W + ZA_W, id(zb_ref): ZD_W + ZA_W + ZC_W}
    for ref in (zc_refs, zd_ref, za_ref, zb_ref):
        off = groups[id(ref)]
        width = ZC_W if ref is zc_refs else ref.shape[1]
        for c in range(0, width, GW):
            cw = min(GW, width - c)
            z = _dot(xb_ref[rs, :], w_ref[:, off + c:off + c + cw])
            if ref is not zc_refs:
                ref[rs, c:c + cw] = z.astype(BF16)
                continue
            if c < 2 * GW:
                z = _rope_heads(z, cos_ref[rs, :], sin_ref[rs, :])
                if c == 0:
                    z = z * (HEAD_DIM ** -0.5 * LOG2_E)
            zc_refs[0][rs, c:c + cw] = z.astype(BF16)
            for half in range(GW // LANES):
                stage_ref[c // GW, half, rs, :] = z[:, half * LANES:(half + 1) * LANES]
            for d, view in zip(DILATIONS[1:], zc_refs[1:]):
                for r in range(d):
                    for half in range(GW // LANES):
                        col = r * ZC_W + c + half * LANES
                        rows = stage_ref[c // GW, half, pl.ds(r0 + r, nrows // d, stride=d), :]
                        view[r0 // d:(r0 + nrows) // d, col:col + LANES] = rows.astype(BF16)
    zm_ref[rs, :] = _dot(xb_ref[rs, :], wm_ref[...])


def _inproj_kernel(x_ref, w_ref, wm_ref, cos_ref, sin_ref, zd_ref, za_ref, zc1_ref, zc4_ref, zc16_ref, zb_ref,
                   zm_ref, xb_ref, stage_ref):
    xb_ref[...] = x_ref[...].astype(BF16)
    _project_in(xb_ref, w_ref, wm_ref, cos_ref, sin_ref, zd_ref, za_ref, (zc1_ref, zc4_ref, zc16_ref),
                zb_ref, zm_ref, stage_ref, 0, x_ref.shape[0])


def _z_outputs(n, tm):
    row = lambda w: pl.BlockSpec((tm, w), lambda i: (i, 0))
    view = lambda d: pl.BlockSpec((tm // d, d * ZC_W), lambda i: (i, 0))
    specs = [row(ZD_W), row(ZA_W)] + [view(d) for d in DILATIONS] + [row(ZB_W), row(LANES)]
    shapes = ([jax.ShapeDtypeStruct((n, ZD_W), BF16), jax.ShapeDtypeStruct((n, ZA_W), BF16)]
              + [jax.ShapeDtypeStruct((n // d, d * ZC_W), BF16) for d in DILATIONS]
              + [jax.ShapeDtypeStruct((n, ZB_W), BF16), jax.ShapeDtypeStruct((n, LANES), F32)])
    stage = pltpu.VMEM((ZC_W // GW, GW // LANES, tm, LANES), F32)
    return specs, shapes, stage


def _inproj(x2, w_main, w_misc, cosd, sind, tm):
    n = x2.shape[0]
    row = lambda w: pl.BlockSpec((tm, w), lambda i: (i, 0))
    full = lambda a: pl.BlockSpec(a.shape, lambda i: (0, 0))
    z_specs, z_shapes, stage = _z_outputs(n, tm)
    return pl.pallas_call(
        _inproj_kernel,
        grid=(n // tm,),
        in_specs=[row(D_MODEL), full(w_main), full(w_misc), row(LANES), row(LANES)],
        out_specs=z_specs,
        out_shape=z_shapes,
        scratch_shapes=[pltpu.VMEM((tm, D_MODEL), BF16), stage],
        compiler_params=_params(),
        name="inproj",
    )(x2, w_main, w_misc, cosd, sind)


def _gmlp_kernel(za_ref, lng_ref, lnb_ref, ws_ref, bias_ref, y_ref):
    seq = za_ref.shape[0]
    r_i = lax.broadcasted_iota(jnp.int32, (BLOCK, BLOCK), 0)
    c_i = lax.broadcasted_iota(jnp.int32, (BLOCK, BLOCK), 1)
    w_causal = [jnp.where(c_i <= r_i, ws_ref[h], 0.0).astype(BF16) for h in range(HEADS)]
    lane_head = lax.broadcasted_iota(jnp.int32, (BLOCK, GW), 1) // HEAD_DIM

    def chunk(c, carry):
        r0 = pl.multiple_of(c * BLOCK, BLOCK)
        z = za_ref[pl.ds(r0, BLOCK), :].astype(F32)
        g = 0.5 * z * (1.0 + lax.erf(z * (0.5 ** 0.5)))
        u, v = g[:, :GW], g[:, GW:]
        vb = _layer_norm(v, lng_ref[...], lnb_ref[...]).astype(BF16)
        mixed = jnp.zeros((BLOCK, GW), F32)
        for h in range(HEADS):
            mixed = jnp.where(lane_head == h, _dot(w_causal[h], vb), mixed)
        y_ref[pl.ds(r0, BLOCK), :] = (u * (mixed + bias_ref[...])).astype(BF16)
        return carry

    lax.fori_loop(0, seq // BLOCK, chunk, 0, unroll=8)


def _gmlp(za, ln_g, ln_b, ws, bias, bsz, seq):
    full = lambda a: pl.BlockSpec(a.shape, lambda b: (0,) * a.ndim)
    return pl.pallas_call(
        _gmlp_kernel,
        grid=(bsz,),
        in_specs=[pl.BlockSpec((seq, ZA_W), lambda b: (b, 0)), full(ln_g), full(ln_b), full(ws),
                  full(bias)],
        out_specs=pl.BlockSpec((seq, GW), lambda b: (b, 0)),
        out_shape=jax.ShapeDtypeStruct((bsz * seq, GW), BF16),
        compiler_params=_params(),
        name="gmlp",
    )(za, ln_g, ln_b, ws, bias)


MLA_TQ = 256


def _mla_kernel(zb_ref, zm_ref, cos_ref, sin_ref, qg_ref, kvg_ref, wqm_ref, wqs_ref, wk_ref, wv_ref,
                y_ref, q_s, k_s, vt_s, acc_s, sc_a, sc_b, p_a, p_b):
    seq = zb_ref.shape[0]
    tq = MLA_TQ
    scale = (NOPE + ROPE) ** -0.5 * LOG2_E
    lane = lax.broadcasted_iota(jnp.int32, (1, LANES), 1)
    in_rope = (lane >= KR_LANE) & (lane < KR_LANE + ROPE)
    rows = 512
    for r in range(0, seq, rows):
        cos = cos_ref[0, r:r + rows, :]
        sin = sin_ref[0, r:r + rows, :]
        cq = zb_ref[r:r + rows, 0:Q_PAD].astype(F32)
        ms = jnp.sum(cq * cq, axis=-1, keepdims=True) * (1.0 / Q_RANK)
        cqn = (cq * lax.rsqrt(ms + RMS_EPS) * qg_ref[...]).astype(BF16)
        qm = _dot(cqn, wqm_ref[...])
        qs = _dot(cqn, wqs_ref[...])
        ckv = zb_ref[r:r + rows, Q_PAD:ZB_W].astype(F32)
        ms = jnp.mean(ckv * ckv, axis=-1, keepdims=True)
        ckvn = (ckv * lax.rsqrt(ms + RMS_EPS) * kvg_ref[...]).astype(BF16)
        kn = _dot(ckvn, wk_ref[...])
        v = _dot(ckvn, wv_ref[...])
        for sub in range(rows // tq):
            vt_s[r // tq + sub] = v[sub * tq:(sub + 1) * tq, :].T.astype(BF16)
        zm = zm_ref[r:r + rows, :]
        kr = jnp.where(in_rope, zm * cos + pltpu.roll(zm, LANES - ROPE, 1) * sin, 0.0)
        for h in range(HEADS):
            sl = slice(h * LANES, (h + 1) * LANES)
            q_h = (qm[:, sl] * cos + qs[:, sl] * sin) * scale
            for sub in range(rows // tq):
                q_s[r // tq + sub, sl, :] = q_h[sub * tq:(sub + 1) * tq, :].T.astype(BF16)
            k_s[r:r + rows, sl] = (kn[:, sl] + kr).astype(BF16)

    key_i = lax.broadcasted_iota(jnp.int32, (tq, tq), 0)
    qry_i = lax.broadcasted_iota(jnp.int32, (tq, tq), 1)
    heads = range(HEADS)
    head_lanes = [slice(h * LANES, (h + 1) * LANES) for h in heads]

    def put_scores(qb, j, dst):
        k0 = pl.multiple_of(j * tq, tq)
        for h in heads:
            dst[h] = _dot(k_s[pl.ds(k0, tq), head_lanes[h]], q_s[qb, head_lanes[h], :])

    def qblock(i, shift):
        q0 = pl.multiple_of(i * tq, tq)
        acc_s[...] = jnp.zeros(acc_s.shape, F32)

        def softmax(src, p_dst, m_old, l_old, masked):
            m_new, l_new, alpha = [], [], []
            for h in heads:
                s = src[h]
                if masked:
                    s = jnp.where(key_i <= qry_i, s, NEG_INF)
                m = jnp.maximum(m_old[h], jnp.max(s, axis=0, keepdims=True))
                a = jnp.exp2(m_old[h] - m)
                p = jnp.exp2(s - m)
                m_new.append(m)
                alpha.append(a)
                l_new.append(a * l_old[h] + jnp.sum(p, axis=0, keepdims=True))
                p_dst[h] = p.astype(BF16)
            return tuple(m_new), tuple(l_new), tuple(alpha)

        def add_values(j, p_src, alpha):
            for h in heads:
                vt = vt_s[j, h * HEAD_DIM:(h + 1) * HEAD_DIM, :]
                acc_s[h] = alpha[h] * acc_s[h] + _dot(vt, p_src[h])

        even, odd = (sc_a, p_a), (sc_b, p_b)

        def on_parity(j, fn, state):
            return lax.cond((j + shift) % 2 == 0, lambda st: fn(even, odd, st), lambda st: fn(odd, even, st),
                            state)

        def step(j, state):
            def run(cur, other, st):
                m_old, l_old, alpha_prev = st
                put_scores(i, j + 1, other[0])
                m_new, l_new, alpha = softmax(cur[0], cur[1], m_old, l_old, False)
                add_values(jnp.maximum(j - 1, 0), other[1], alpha_prev)
                return m_new, l_new, alpha
            return on_parity(j, run, state)

        def finish(cur, other, st):
            m_old, l_old, alpha_prev = st
            put_scores(jnp.minimum(i + 1, n_qblocks - 1), 0, other[0])
            _, l_new, alpha = softmax(cur[0], cur[1], m_old, l_old, True)
            add_values(jnp.maximum(i - 1, 0), other[1], alpha_prev)
            add_values(i, cur[1], alpha)
            return l_new

        p_a[...] = jnp.zeros(p_a.shape, BF16)
        p_b[...] = jnp.zeros(p_b.shape, BF16)
        init = (tuple(jnp.full((1, tq), NEG_INF, F32) for _ in heads),
                tuple(jnp.zeros((1, tq), F32) for _ in heads),
                tuple(jnp.ones((1, tq), F32) for _ in heads))
        state = lax.fori_loop(0, i, step, init)
        l_fin = on_parity(i, finish, state)
        out_t = jnp.concatenate([acc_s[h] / l_fin[h] for h in heads], axis=0)
        y_ref[pl.ds(q0, tq), :] = out_t.T.astype(BF16)
        return (i + shift + 1) % 2

    n_qblocks = seq // tq
    put_scores(0, 0, sc_a)
    lax.fori_loop(0, n_qblocks, qblock, jnp.int32(0))


def _mla(zb, zm, cosm, sinm, qg, kvg, wqm, wqs, wk, wv, bsz, seq):
    full = lambda a: pl.BlockSpec(a.shape, lambda b: (0,) * a.ndim)
    tab = pl.BlockSpec((1, seq, LANES), lambda b: (b, 0, 0))
    return pl.pallas_call(
        _mla_kernel,
        grid=(bsz,),
        in_specs=[pl.BlockSpec((seq, ZB_W), lambda b: (b, 0)),
                  pl.BlockSpec((seq, LANES), lambda b: (b, 0)), tab, tab,
                  full(qg), full(kvg), full(wqm), full(wqs), full(wk), full(wv)],
        out_specs=pl.BlockSpec((seq, GW), lambda b: (b, 0)),
        out_shape=jax.ShapeDtypeStruct((bsz * seq, GW), BF16),
        scratch_shapes=[pltpu.VMEM((seq // MLA_TQ, HEADS * LANES, MLA_TQ), BF16),
                        pltpu.VMEM((seq, HEADS * LANES), BF16),
                        pltpu.VMEM((seq // MLA_TQ, GW, MLA_TQ), BF16),
                        pltpu.VMEM((HEADS, HEAD_DIM, MLA_TQ), F32),
                        pltpu.VMEM((HEADS, MLA_TQ, MLA_TQ), F32),
                        pltpu.VMEM((HEADS, MLA_TQ, MLA_TQ), F32),
                        pltpu.VMEM((HEADS, MLA_TQ, MLA_TQ), BF16),
                        pltpu.VMEM((HEADS, MLA_TQ, MLA_TQ), BF16)],
        compiler_params=_params(),
        name="mla",
    )(zb, zm, cosm, sinm, qg, kvg, wqm, wqs, wk, wv)


DILATIONS = (1, 4, 16)
HALVES = GW // LANES


def _head_columns(cols):
    low_half = lax.broadcasted_iota(jnp.int32, (1, LANES), 1) < HEAD_DIM
    return jnp.concatenate([jnp.where(low_half, cols[0], cols[1]),
                            jnp.where(low_half, cols[2], cols[3])], axis=1)


def _band_block(q, kh, vh, w0, nk, mask):
    pairs = range(HEADS // 2)
    scores = []
    for pr in pairs:
        k_win = jnp.concatenate([kh[2 * pr + e, pl.ds(w0, nk), :] for e in range(2)], axis=0)
        scores.append(_dot_nt(q[:, pr * LANES:(pr + 1) * LANES], k_win))
    m_cols, l_cols, probs = [], [], []
    for h in range(HEADS):
        s = scores[h // 2][:, (h % 2) * nk:(h % 2 + 1) * nk]
        s = jnp.where(mask, s, NEG_INF)
        m = jnp.max(s, axis=-1, keepdims=True)
        p = jnp.exp2(s - m)
        m_cols.append(m)
        l_cols.append(jnp.sum(p, axis=-1, keepdims=True))
        probs.append(p.astype(BF16))
    outs = []
    for pr in pairs:
        v_win = jnp.concatenate([vh[2 * pr + e, pl.ds(w0, nk), :] for e in range(2)], axis=0)
        outs.append(_dot(jnp.concatenate(probs[2 * pr:2 * pr + 2], axis=1), v_win))
    l_full = _head_columns(l_cols)
    o = jnp.concatenate(outs, axis=1) / l_full
    return o, _head_columns(m_cols) + jnp.log2(l_full)


def _dilated_kernel(z1_ref, z4_ref, z16_ref, o1_ref, o4_ref, o16_ref, l1_ref, l4_ref, l16_ref, qb, kh, vh):
    seq = z1_ref.shape[0]
    lane = lax.broadcasted_iota(jnp.int32, (1, LANES), 1)
    views = {1: (z1_ref, o1_ref, l1_ref), 4: (z4_ref, o4_ref, l4_ref), 16: (z16_ref, o16_ref, l16_ref)}

    r_i = lax.broadcasted_iota(jnp.int32, (BLOCK, 2 * BLOCK), 0)
    c_i = lax.broadcasted_iota(jnp.int32, (BLOCK, 2 * BLOCK), 1)
    band = (c_i >= r_i) & (c_i <= r_i + BLOCK)
    causal = (lax.broadcasted_iota(jnp.int32, (BLOCK, BLOCK), 1)
              <= lax.broadcasted_iota(jnp.int32, (BLOCK, BLOCK), 0))
    low_half = lane < HEAD_DIM
    for h in range(HEADS):
        kh[h, 0:BLOCK, :] = jnp.zeros((BLOCK, LANES), BF16)
        vh[h, 0:BLOCK, :] = jnp.zeros((BLOCK, LANES), BF16)

    for d in DILATIONS:
        sub = seq // d
        z_ref, o_ref, l_ref = views[d]
        for res in range(d):
            dst = slice(BLOCK + res * sub, BLOCK + (res + 1) * sub)
            qb[res * sub:(res + 1) * sub, :] = z_ref[:, res * ZC_W:res * ZC_W + GW]
            for half in range(HALVES):
                col = lambda base: slice(res * ZC_W + base + half * LANES,
                                         res * ZC_W + base + (half + 1) * LANES)
                k_half = z_ref[:, col(GW)]
                v_half = z_ref[:, col(2 * GW)]
                zero = jnp.zeros_like(k_half)
                kh[2 * half, dst, :] = jnp.where(low_half, k_half, zero)
                kh[2 * half + 1, dst, :] = jnp.where(low_half, zero, k_half)
                vh[2 * half, dst, :] = jnp.where(low_half, v_half, zero)
                vh[2 * half + 1, dst, :] = jnp.where(low_half, zero, v_half)
        blocks_per_class = sub // BLOCK

        for res in range(d):
            cols = slice(res * GW, (res + 1) * GW)

            def block(n, carry, res=res, cols=cols, o_ref=o_ref, l_ref=l_ref):
                r0 = pl.multiple_of(res * sub + n * BLOCK, BLOCK)
                q = qb[pl.ds(r0, BLOCK), :]
                if blocks_per_class == 1:
                    o, lse = _band_block(q, kh, vh, r0 + BLOCK, BLOCK, causal)
                else:
                    first_key = jnp.where(n > 0, 0, BLOCK)
                    o, lse = _band_block(q, kh, vh, r0, 2 * BLOCK, band & (c_i >= first_key))
                rows = pl.ds(pl.multiple_of(n * BLOCK, BLOCK), BLOCK)
                o_ref[rows, cols] = o.astype(BF16)
                l_ref[rows, cols] = lse
                return carry

            if blocks_per_class == 1:
                block(0, 0)
            else:
                lax.fori_loop(0, blocks_per_class, block, 0, unroll=True)


def _dilated(zc_views, bsz, seq):
    headbuf = pltpu.VMEM((HEADS, seq + BLOCK, LANES), BF16)
    n = bsz * seq
    view = lambda width, d: pl.BlockSpec((seq // d, d * width), lambda b: (b, 0))
    return pl.pallas_call(
        _dilated_kernel,
        grid=(bsz,),
        in_specs=[view(ZC_W, d) for d in DILATIONS],
        out_specs=[view(GW, d) for d in DILATIONS] * 2,
        out_shape=([jax.ShapeDtypeStruct((n // d, d * GW), BF16) for d in DILATIONS]
                   + [jax.ShapeDtypeStruct((n // d, d * GW), F32) for d in DILATIONS]),
        scratch_shapes=[pltpu.VMEM((seq, GW), BF16), headbuf, headbuf],
        compiler_params=_params(),
        name="dilated",
    )(*zc_views)


CONV_W = 4
CONV_PAD = 8


def _log_sigmoid(x):
    return jnp.minimum(x, 0.0) - jnp.log1p(jnp.exp(-jnp.abs(x)))


def _mlstm_kernel(zd_ref, zm_ref, gb_ref, cw_ref, cb_ref, y_ref, xpad, q_s, k_s, kt_s, a_r, b_r, cm_c,
                  b_c, b_f, c_s, n_s):
    seq = zd_ref.shape[0]
    rows = 512
    xpad[0:CONV_PAD, :] = jnp.zeros((CONV_PAD, 2 * GW), F32)
    for r in range(0, seq, rows):
        xpad[CONV_PAD + r:CONV_PAD + r + rows, :] = zd_ref[r:r + rows, 0:2 * GW].astype(F32)
    def conv_chunk(r):
        win = xpad[r:r + rows + CONV_PAD, :]
        acc = cb_ref[...] + cw_ref[CONV_W - 1:CONV_W, :] * win[CONV_PAD:, :]
        for k in range(1, CONV_W):
            acc = acc + cw_ref[CONV_W - 1 - k:CONV_W - k, :] * pltpu.roll(win, k, 0)[CONV_PAD:, :]
        qk = acc * jax.nn.sigmoid(acc)
        k = qk[:, GW:] * (HEAD_DIM ** -0.5)
        q_s[r:r + rows, :] = qk[:, :GW].astype(BF16)
        k_s[r:r + rows, :] = k.astype(BF16)
        for sub in range(rows // BLOCK):
            kt_s[r // BLOCK + sub] = k[sub * BLOCK:(sub + 1) * BLOCK, :].T

    heads = range(HEADS)
    low_half = lax.broadcasted_iota(jnp.int32, (1, LANES), 1) < HEAD_DIM

    def head_lanes(per_head):
        return jnp.concatenate([jnp.where(low_half, per_head[0], per_head[1]),
                                jnp.where(low_half, per_head[2], per_head[3])], axis=1)

    def decay_lanes(r):
        b_cols = b_c[r:r + rows, :]
        b_f[r:r + rows, :] = head_lanes([b_cols[:, h:h + 1] for h in heads])

    conv_chunks = iter(range(0, seq, rows))

    gt = (zm_ref[...] + gb_ref[...]).T[0:8, :]
    f_log = _log_sigmoid(pltpu.roll(gt, HEADS, 0))
    pos_in_chunk = lax.broadcasted_iota(jnp.int32, (8, seq), 1) % BLOCK
    steps = [1 << s for s in range(BLOCK.bit_length() - 1)]
    conv_chunk(next(conv_chunks))
    b = f_log
    for k in steps:
        b = b + jnp.where(pos_in_chunk >= k, pltpu.roll(b, k, 1), 0.0)
    conv_chunk(next(conv_chunks))
    a = gt - b
    cm = a
    for k in steps:
        cm = jnp.maximum(cm, jnp.where(pos_in_chunk >= k, pltpu.roll(cm, k, 1), NEG_INF))
    conv_chunk(next(conv_chunks))
    a_r[...] = a
    b_r[...] = b
    pad = jnp.zeros((LANES - 8, seq), F32)
    cm_c[...] = jnp.concatenate([cm, pad], axis=0).T
    b_c[...] = jnp.concatenate([b, pad], axis=0).T
    conv_chunk(next(conv_chunks))
    for r in range(0, seq, rows):
        decay_lanes(r)

    c_s[...] = jnp.zeros((GW, GW), F32)
    n_s[...] = jnp.zeros((GW, GW), F32)
    lane_head = lax.broadcasted_iota(jnp.int32, (1, GW), 1) // HEAD_DIM
    same_head = (lax.broadcasted_iota(jnp.int32, (GW, GW), 0) // HEAD_DIM
                 == lax.broadcasted_iota(jnp.int32, (GW, GW), 1) // HEAD_DIM)
    causal = (lax.broadcasted_iota(jnp.int32, (BLOCK, BLOCK), 1)
              <= lax.broadcasted_iota(jnp.int32, (BLOCK, BLOCK), 0))

    ones_blk = jnp.ones((BLOCK, LANES), BF16)
    head_sum = (lax.broadcasted_iota(jnp.int32, (HEADS * BLOCK, GW), 0) // BLOCK
                == lax.broadcasted_iota(jnp.int32, (HEADS * BLOCK, GW), 1) // HEAD_DIM).astype(BF16)

    def chunk(c, m_run):
        r0 = pl.multiple_of(c * BLOCK, BLOCK)
        qb = q_s[pl.ds(r0, BLOCK), :]
        kb = k_s[pl.ds(r0, BLOCK), :]
        v = zd_ref[pl.ds(r0, BLOCK), 2 * GW:3 * GW]
        og = zd_ref[pl.ds(r0, BLOCK), 3 * GW:4 * GW].astype(F32)
        a_rows = a_r[:, pl.ds(r0, BLOCK)]
        b_rows = b_r[:, pl.ds(r0, BLOCK)]
        cm_cols = cm_c[pl.ds(r0, BLOCK), :]

        k_heads = jnp.concatenate(
            [jnp.where(lane_head == h, kb, jnp.zeros_like(kb)) for h in heads], axis=0)
        v_heads = jnp.concatenate(
            [jnp.where(lane_head == h, v, jnp.zeros_like(v)) for h in heads], axis=0)
        scores = _dot_nt(qb, k_heads)
        carried = _dot(qb, jnp.concatenate([c_s[...], n_s[...]], axis=1).astype(BF16))

        w_intra, g_rep, wk_rows, decays, m_next = [], [], [], [], []
        for h in heads:
            a_row = a_rows[h:h + 1, :]
            g = jnp.maximum(jnp.broadcast_to(cm_cols[:, h:h + 1], (BLOCK, BLOCK)), m_run[h])
            g_rep.append(g)
            w_intra.append(jnp.where(causal, jnp.exp(a_row - g), 0.0))
            g_end = jnp.maximum(m_run[h], jnp.max(a_row, axis=-1, keepdims=True))
            m_next.append(b_rows[h:h + 1, BLOCK - 1:BLOCK] + g_end)
            decays.append(jnp.exp(m_run[h] - g_end))
            wk_rows.append(jnp.exp(a_row - g_end))
        g_full = head_lanes(g_rep)
        inter_scale = jnp.exp(head_lanes(m_run) - g_full)
        floor = jnp.exp(-(b_f[pl.ds(r0, BLOCK), :] + g_full))

        wk = jnp.concatenate([jnp.broadcast_to(w, (HEAD_DIM, BLOCK)) for w in wk_rows], axis=0)
        decay = jnp.concatenate([jnp.broadcast_to(dd, (HEAD_DIM, 1)) for dd in decays], axis=0)
        ktw = kt_s[c] * wk
        update = _dot(ktw.astype(BF16), jnp.concatenate([v, ones_blk], axis=1))

        sqk = (scores * jnp.concatenate(w_intra, axis=1)).astype(BF16)
        intra = _dot(sqk, jnp.concatenate([v_heads, head_sum], axis=1))
        num = intra[:, :GW] + inter_scale * carried[:, :GW]
        den = intra[:, GW:] + inter_scale * carried[:, GW:]
        h_out = num / jnp.maximum(jnp.abs(den), floor)
        y_ref[pl.ds(r0, BLOCK), :] = (jax.nn.sigmoid(og) * h_out).astype(BF16)

        k_sum = update[:, GW:]
        c_s[...] = decay * c_s[...] + jnp.where(same_head, update[:, :GW], 0.0)
        n_s[...] = decay * n_s[...] + jnp.where(same_head, jnp.concatenate([k_sum, k_sum], axis=1), 0.0)
        return tuple(m_next)

    lax.fori_loop(0, seq // BLOCK, chunk, tuple(jnp.zeros((1, 1), F32) for _ in range(HEADS)), unroll=4)


def _mlstm(zd, zm, gb, cw, cb, bsz, seq):
    full = lambda a: pl.BlockSpec(a.shape, lambda b: (0,) * a.ndim)
    return pl.pallas_call(
        _mlstm_kernel,
        grid=(bsz,),
        in_specs=[pl.BlockSpec((seq, ZD_W), lambda b: (b, 0)),
                  pl.BlockSpec((seq, LANES), lambda b: (b, 0)), full(gb), full(cw), full(cb)],
        out_specs=pl.BlockSpec((seq, GW), lambda b: (b, 0)),
        out_shape=jax.ShapeDtypeStruct((bsz * seq, GW), BF16),
        scratch_shapes=[pltpu.VMEM((seq + CONV_PAD, 2 * GW), F32),
                        pltpu.VMEM((seq, GW), BF16), pltpu.VMEM((seq, GW), BF16),
                        pltpu.VMEM((seq // BLOCK, GW, BLOCK), F32),
                        pltpu.VMEM((8, seq), F32), pltpu.VMEM((8, seq), F32),
                        pltpu.VMEM((seq, LANES), F32), pltpu.VMEM((seq, LANES), F32),
                        pltpu.VMEM((seq, GW), F32),
                        pltpu.VMEM((GW, GW), F32), pltpu.VMEM((GW, GW), F32)],
        compiler_params=_params(),
        name="mlstm",
    )(zd, zm, gb, cw, cb)


FF_CHUNK = 256


def _dense_kernel(with_next, ya_ref, yb_ref, yd_ref, oc1_ref, oc4_ref, oc16_ref, lc1_ref, lc4_ref, lc16_ref,
                  x_ref, wo_ref, g1_ref, b1_ref, wg_ref, wu_ref, wd_ref, g2_ref, b2_ref, *rest):
    if with_next:
        (w_ref, wm_ref, cos_ref, sin_ref, o_ref, zd_ref, za_ref, zc1_ref, zc4_ref, zc16_ref, zb_ref, zm_ref,
         xb_ref, acc_ref, nat_ref, stage_ref) = rest
    else:
        o_ref, xb_ref, acc_ref, nat_ref = rest
    tm = x_ref.shape[0]

    def natural(view, d, slot):
        if d == 1:
            return view[...].astype(F32)
        for r in range(d):
            for half in range(GW // LANES):
                col = r * GW + half * LANES
                nat_ref[slot, half, pl.ds(r, tm // d, stride=d), :] = view[:, col:col + LANES].astype(F32)
        return jnp.concatenate([nat_ref[slot, half] for half in range(GW // LANES)], axis=1)

    views = (oc1_ref, oc4_ref, oc16_ref, lc1_ref, lc4_ref, lc16_ref)
    slots = iter(range(nat_ref.shape[0]))
    nat = [natural(v, d, next(slots) if d > 1 else None) for v, d in zip(views, DILATIONS * 2)]
    outs, lse = nat[:3], nat[3:]
    top = jnp.maximum(jnp.maximum(lse[0], lse[1]), lse[2])
    wts = [jnp.exp2(l - top) for l in lse]
    mix = wts[0] * outs[0] + wts[1] * outs[1] + wts[2] * outs[2]
    yc = (mix / (wts[0] + wts[1] + wts[2])).astype(BF16)
    acc = ALPHA * x_ref[...]
    for i, y in ((0, ya_ref[...]), (1, yb_ref[...]), (3, yd_ref[...]), (2, yc)):
        acc = acc + _dot(y, wo_ref[i * GW:(i + 1) * GW, :])
    x1 = _layer_norm(acc, g1_ref[...], b1_ref[...])
    xb_ref[...] = x1.astype(BF16)
    acc_ref[...] = ALPHA * x1
    for c in range(0, D_FF, FF_CHUNK):
        gate = _dot(xb_ref[...], wg_ref[:, c:c + FF_CHUNK])
        up = _dot(xb_ref[...], wu_ref[:, c:c + FF_CHUNK])
        act = (gate * jax.nn.sigmoid(gate) * up).astype(BF16)
        acc_ref[...] += _dot(act, wd_ref[c:c + FF_CHUNK, :])
    x2 = _layer_norm(acc_ref[...], g2_ref[...], b2_ref[...])
    o_ref[...] = x2
    if with_next:
        xb_ref[...] = x2.astype(BF16)
        _project_in(xb_ref, w_ref, wm_ref, cos_ref, sin_ref, zd_ref, za_ref, (zc1_ref, zc4_ref, zc16_ref),
                    zb_ref, zm_ref, stage_ref, 0, tm)


def _dense(ys, dil, x2, wo, g1, b1, wg, wu, wd, g2, b2, next_w, tables, tm):
    n = x2.shape[0]
    row = lambda w_: pl.BlockSpec((tm, w_), lambda i: (i, 0))
    view = lambda d: pl.BlockSpec((tm // d, d * GW), lambda i: (i, 0))
    once = lambda a: pl.BlockSpec(a.shape, lambda i: (0, 0), pipeline_mode=pl.Buffered(1))
    weights = [wo, g1, b1, wg, wu, wd, g2, b2] + list(next_w)
    tables = list(tables) if next_w else []
    out_specs = [row(D_MODEL)]
    out_shape = [jax.ShapeDtypeStruct((n, D_MODEL), F32)]
    scratch = [pltpu.VMEM((tm, D_MODEL), BF16), pltpu.VMEM((tm, D_MODEL), F32),
               pltpu.VMEM((2 * (len(DILATIONS) - 1), GW // LANES, tm, LANES), F32)]
    if next_w:
        z_specs, z_shapes, stage = _z_outputs(n, tm)
        out_specs += z_specs
        out_shape += z_shapes
        scratch.append(stage)
    return pl.pallas_call(
        functools.partial(_dense_kernel, bool(next_w)),
        grid=(n // tm,),
        in_specs=([row(GW)] * len(ys) + [view(d) for d in DILATIONS] * 2 + [row(D_MODEL)]
                  + [once(w) for w in weights] + [row(LANES)] * len(tables)),
        out_specs=out_specs,
        out_shape=out_shape,
        scratch_shapes=scratch,
        compiler_params=_params(),
        name="dense_block",
    )(*ys, *dil, x2, *weights, *tables)


def _pack_weights(w_in, b_w_uq, b_w_ukv, b_q_norm, a_bs, d_igate_b, d_fgate_b):
    nl = w_in.shape[0]
    o_b = ZA_W
    o_c = o_b + Q_RANK + KV_RANK + ROPE
    o_d = o_c + ZC_W
    o_g = o_d + ZD_W
    w_a = w_in[..., :o_b]
    w_cq = w_in[..., o_b:o_b + Q_RANK]
    w_ckv = w_in[..., o_b + Q_RANK:o_b + Q_RANK + KV_RANK]
    w_kr = w_in[..., o_b + Q_RANK + KV_RANK:o_c]
    w_c = w_in[..., o_c:o_d]
    w_d = w_in[..., o_d:o_g]
    w_gates = w_in[..., o_g:]
    zeros = lambda *s: jnp.zeros((nl,) + s, F32)
    w_b = jnp.concatenate([w_cq, zeros(D_MODEL, Q_PAD - Q_RANK), w_ckv], -1)
    w_main = jnp.concatenate([w_d, w_a, w_c, w_b], -1).astype(BF16)
    half = ROPE // 2
    w_kr_rot = jnp.concatenate([-w_kr[..., half:], w_kr[..., :half]], -1)
    w_misc = jnp.concatenate(
        [w_gates, zeros(D_MODEL, KR_LANE - 2 * HEADS), w_kr, w_kr_rot], -1).astype(BF16)

    wq = b_w_uq.reshape(nl, Q_RANK, HEADS, NOPE + ROPE)
    nope, x1, x2 = wq[..., :NOPE], wq[..., NOPE:NOPE + half], wq[..., NOPE + half:]
    tail = zeros(Q_RANK, HEADS, LANES - NOPE - ROPE)
    rowpad = ((0, 0), (0, Q_PAD - Q_RANK), (0, 0))
    wqm = jnp.pad(jnp.concatenate([nope, x1, x2, tail], -1).reshape(nl, Q_RANK, HEADS * LANES), rowpad)
    wqs = jnp.pad(jnp.concatenate([zeros(Q_RANK, HEADS, NOPE), -x2, x1, tail], -1)
                  .reshape(nl, Q_RANK, HEADS * LANES), rowpad)
    wkv = b_w_ukv.reshape(nl, KV_RANK, HEADS, NOPE + HEAD_DIM)
    wk = jnp.concatenate([wkv[..., :NOPE], zeros(KV_RANK, HEADS, LANES - NOPE)], -1)
    wk = wk.reshape(nl, KV_RANK, HEADS * LANES)
    wv = wkv[..., NOPE:].reshape(nl, KV_RANK, GW)
    qg = jnp.pad(b_q_norm, ((0, 0), (0, Q_PAD - Q_RANK)))[:, None, :]
    bias = jnp.repeat(jnp.swapaxes(a_bs, 1, 2), HEAD_DIM, axis=-1)
    gb = jnp.concatenate([d_igate_b, d_fgate_b, zeros(LANES - 2 * HEADS)], -1)[:, None, :]
    return dict(w_main=w_main, w_misc=w_misc, wqm=wqm.astype(BF16), wqs=wqs.astype(BF16),
                wk=wk.astype(BF16), wv=wv.astype(BF16), qg=qg, bias=bias, gb=gb)


@jax.jit
def _forward(x, positions, w_in, a_ln_g, a_ln_b, a_ws, a_bs, b_q_norm, b_kv_norm, b_w_uq, b_w_ukv,
             d_conv_w, d_conv_b, d_igate_b, d_fgate_b, w_out, ln1_g, ln1_b, w_gate, w_up, w_down,
             ln2_g, ln2_b):
    bsz, seq, _ = x.shape
    pk = _pack_weights(w_in, b_w_uq, b_w_ukv, b_q_norm, a_bs, d_igate_b, d_fgate_b)
    w_out_b, w_gate_b, w_up_b, w_down_b = (w.astype(BF16) for w in (w_out, w_gate, w_up, w_down))
    cosd, sind, cosm, sinm = _rope_tables(positions)
    tm = 512
    x2 = x.reshape(bsz * seq, D_MODEL)
    row = lambda a, l: a[l][None, :]
    rope_d = (cosd.reshape(bsz * seq, LANES), sind.reshape(bsz * seq, LANES))
    zd, za, *zc, zb, zm = _inproj(x2, pk["w_main"][0], pk["w_misc"][0], *rope_d, tm)
    for l in range(DEPTH):
        ya = _gmlp(za, row(a_ln_g, l), row(a_ln_b, l), a_ws[l], pk["bias"][l], bsz, seq)
        yb = _mla(zb, zm, cosm, sinm, pk["qg"][l], row(b_kv_norm, l), pk["wqm"][l], pk["wqs"][l],
                  pk["wk"][l], pk["wv"][l], bsz, seq)
        dil = _dilated(zc, bsz, seq)
        yd = _mlstm(zd, zm, pk["gb"][l], d_conv_w[l], row(d_conv_b, l), bsz, seq)
        next_w = (pk["w_main"][l + 1], pk["w_misc"][l + 1]) if l + 1 < DEPTH else ()
        outs = _dense((ya, yb, yd), dil, x2, w_out_b[l], row(ln1_g, l), row(ln1_b, l), w_gate_b[l],
                      w_up_b[l], w_down_b[l], row(ln2_g, l), row(ln2_b, l), next_w, rope_d, tm)
        if next_w:
            x2, zd, za, *zc, zb, zm = outs
        else:
            x2 = outs[0]
    return x2.reshape(bsz, seq, D_MODEL)


def kernel(x, positions, w_in, a_ln_g, a_ln_b, a_ws, a_bs, b_q_norm, b_kv_norm, b_w_uq, b_w_ukv,
           d_conv_w, d_conv_b, d_igate_b, d_fgate_b, w_out, ln1_g, ln1_b, w_gate, w_up, w_down,
           ln2_g, ln2_b):
    return _forward(x, positions, w_in, a_ln_g, a_ln_b, a_ws, a_bs, b_q_norm, b_kv_norm, b_w_uq,
                    b_w_ukv, d_conv_w, d_conv_b, d_igate_b, d_fgate_b, w_out, ln1_g, ln1_b, w_gate,
                    w_up, w_down, ln2_g, ln2_b)
```

```python
import functools

import jax
import jax.numpy as jnp
from jax import lax
from jax.experimental import pallas as pl
from jax.experimental.pallas import tpu as pltpu

F32 = jnp.float32
BF16 = jnp.bfloat16

D_MODEL = 1024
DEPTH = 4
HEAD_DIM = 64
HEADS = 4
GW = HEADS * HEAD_DIM
BLOCK = 128
Q_RANK = 192
KV_RANK = 128
NOPE = 64
ROPE = 32
D_FF = 2816
ROPE_THETA = 10000.0
LN_EPS = 1e-5
RMS_EPS = 1e-6
ALPHA = (2 * DEPTH) ** 0.25
LANES = 128
NEG_INF = float("-inf")
LOG2_E = 1.4426950408889634

Q_PAD = 256
ZD_W, ZA_W, ZC_W, ZB_W = 4 * GW, 2 * GW, 3 * GW, Q_PAD + KV_RANK
ZMAIN_W = ZD_W + ZA_W + ZC_W + ZB_W
KR_LANE = 64

VMEM_LIMIT = 60000 * 1024


def _dot(a, b):
    return jnp.dot(a, b, preferred_element_type=F32)


def _dot_nt(a, b):
    return lax.dot_general(a, b, (((1,), (1,)), ((), ())), preferred_element_type=F32)


def _params(n_axes=1):
    return pltpu.CompilerParams(
        dimension_semantics=("arbitrary",) * n_axes, vmem_limit_bytes=VMEM_LIMIT)


def _layer_norm(r, g, b):
    mu = jnp.mean(r, axis=-1, keepdims=True)
    d = r - mu
    var = jnp.mean(d * d, axis=-1, keepdims=True)
    return d * lax.rsqrt(var + LN_EPS) * g + b


def _tables_kernel(pos_ref, inv_ref, cosd_ref, sind_ref, cosm_ref, sinm_ref):
    pos = pos_ref[0].astype(F32)
    lane = lax.broadcasted_iota(jnp.int32, (1, LANES), 1)
    ang = pos * inv_ref[...]
    cos, sin = jnp.cos(ang), jnp.sin(ang)
    low = lane < HEAD_DIM
    cosd_ref[0] = jnp.where(low, cos, pltpu.roll(cos, HEAD_DIM, 1))
    sign = jnp.where((lane % HEAD_DIM) < HEAD_DIM // 2, -1.0, 1.0)
    sind_ref[0] = jnp.where(low, sin, pltpu.roll(sin, HEAD_DIM, 1)) * sign
    in_rope = (lane >= KR_LANE) & (lane < KR_LANE + ROPE)
    cosm_ref[0] = jnp.where(in_rope, cos, jnp.where(lane < KR_LANE, 1.0, 0.0))
    sinm_ref[0] = jnp.where(in_rope, sin, 0.0)


def _rope_tables(positions):
    bsz, seq = positions.shape
    lane = jnp.arange(LANES)
    half_d = HEAD_DIM // 2
    inv_d = jnp.power(ROPE_THETA, -jnp.arange(half_d, dtype=F32) / half_d)
    half_m = ROPE // 2
    inv_m = jnp.power(ROPE_THETA, -jnp.arange(half_m, dtype=F32) / half_m)
    inv = jnp.where(lane < HEAD_DIM, inv_d[lane % half_d],
                    jnp.where(lane < KR_LANE + ROPE, inv_m[lane % half_m], 0.0))[None, :]
    tab = jax.ShapeDtypeStruct((bsz, seq, LANES), F32)
    spec = pl.BlockSpec((1, seq, LANES), lambda b: (b, 0, 0))
    return pl.pallas_call(
        _tables_kernel,
        grid=(bsz,),
        in_specs=[pl.BlockSpec((1, seq, 1), lambda b: (b, 0, 0)), pl.BlockSpec((1, LANES), lambda b: (0, 0))],
        out_specs=[spec] * 4,
        out_shape=[tab] * 4,
        compiler_params=_params(),
        name="rope_tables",
    )(positions.reshape(bsz, seq, 1), inv)


def _rope_heads(x, cos, sin):
    first = (lax.broadcasted_iota(jnp.int32, (1, LANES), 1) % HEAD_DIM) < HEAD_DIM // 2
    halves = []
    for half in range(GW // LANES):
        xh = x[:, half * LANES:(half + 1) * LANES]
        rot = jnp.where(first, pltpu.roll(xh, LANES - HEAD_DIM // 2, 1), pltpu.roll(xh, HEAD_DIM // 2, 1))
        halves.append(xh * cos + rot * sin)
    return jnp.concatenate(halves, axis=1)


def _project_in(xb_ref, w_ref, wm_ref, cos_ref, sin_ref, zd_ref, za_ref, zc_refs, zb_ref, zm_ref, stage_ref,
                r0, nrows):
    rs = slice(r0, r0 + nrows)
    groups = {id(zd_ref): 0, id(za_ref): ZD_W, id(zc_refs): ZD_W + ZA_W, id(zb_ref): ZD_W + ZA_W + ZC_W}
    for ref in (zc_refs, zd_ref, za_ref, zb_ref):
        off = groups[id(ref)]
        width = ZC_W if ref is zc_refs else ref.shape[1]
        for c in range(0, width, GW):
            cw = min(GW, width - c)
            z = _dot(xb_ref[rs, :], w_ref[:, off + c:off + c + cw])
            if ref is not zc_refs:
                ref[rs, c:c + cw] = z.astype(BF16)
                continue
            if c < 2 * GW:
                z = _rope_heads(z, cos_ref[rs, :], sin_ref[rs, :])
                if c == 0:
                    z = z * (HEAD_DIM ** -0.5 * LOG2_E)
            zc_refs[0][rs, c:c + cw] = z.astype(BF16)
            for half in range(GW // LANES):
                stage_ref[c // GW, half, rs, :] = z[:, half * LANES:(half + 1) * LANES]
            for d, view in zip(DILATIONS[1:], zc_refs[1:]):
                for r in range(d):
                    for half in range(GW // LANES):
                        col = r * ZC_W + c + half * LANES
                        rows = stage_ref[c // GW, half, pl.ds(r0 + r, nrows // d, stride=d), :]
                        view[r0 // d:(r0 + nrows) // d, col:col + LANES] = rows.astype(BF16)
    zm_ref[rs, :] = _dot(xb_ref[rs, :], wm_ref[...])


def _inproj_kernel(x_ref, w_ref, wm_ref, cos_ref, sin_ref, zd_ref, za_ref, zc1_ref, zc4_ref, zc16_ref, zb_ref,
                   zm_ref, xb_ref, stage_ref):
    xb_ref[...] = x_ref[...].astype(BF16)
    _project_in(xb_ref, w_ref, wm_ref, cos_ref, sin_ref, zd_ref, za_ref, (zc1_ref, zc4_ref, zc16_ref),
                zb_ref, zm_ref, stage_ref, 0, x_ref.shape[0])


def _z_outputs(n, tm):
    row = lambda w: pl.BlockSpec((tm, w), lambda i: (i, 0))
    view = lambda d: pl.BlockSpec((tm // d, d * ZC_W), lambda i: (i, 0))
    specs = [row(ZD_W), row(ZA_W)] + [view(d) for d in DILATIONS] + [row(ZB_W), row(LANES)]
    shapes = ([jax.ShapeDtypeStruct((n, ZD_W), BF16), jax.ShapeDtypeStruct((n, ZA_W), BF16)]
              + [jax.ShapeDtypeStruct((n // d, d * ZC_W), BF16) for d in DILATIONS]
              + [jax.ShapeDtypeStruct((n, ZB_W), BF16), jax.ShapeDtypeStruct((n, LANES), F32)])
    stage = pltpu.VMEM((ZC_W // GW, GW // LANES, tm, LANES), F32)
    return specs, shapes, stage


def _inproj(x2, w_main, w_misc, cosd, sind, tm):
    n = x2.shape[0]
    row = lambda w: pl.BlockSpec((tm, w), lambda i: (i, 0))
    full = lambda a: pl.BlockSpec(a.shape, lambda i: (0, 0))
    z_specs, z_shapes, stage = _z_outputs(n, tm)
    return pl.pallas_call(
        _inproj_kernel,
        grid=(n // tm,),
        in_specs=[row(D_MODEL), full(w_main), full(w_misc), row(LANES), row(LANES)],
        out_specs=z_specs,
        out_shape=z_shapes,
        scratch_shapes=[pltpu.VMEM((tm, D_MODEL), BF16), stage],
        compiler_params=_params(),
        name="inproj",
    )(x2, w_main, w_misc, cosd, sind)


def _gmlp_kernel(za_ref, lng_ref, lnb_ref, ws_ref, bias_ref, y_ref):
    seq = za_ref.shape[0]
    r_i = lax.broadcasted_iota(jnp.int32, (BLOCK, BLOCK), 0)
    c_i = lax.broadcasted_iota(jnp.int32, (BLOCK, BLOCK), 1)
    w_causal = [jnp.where(c_i <= r_i, ws_ref[h], 0.0).astype(BF16) for h in range(HEADS)]
    lane_head = lax.broadcasted_iota(jnp.int32, (BLOCK, GW), 1) // HEAD_DIM

    def chunk(c, carry):
        r0 = pl.multiple_of(c * BLOCK, BLOCK)
        z = za_ref[pl.ds(r0, BLOCK), :].astype(F32)
        g = 0.5 * z * (1.0 + lax.erf(z * (0.5 ** 0.5)))
        u, v = g[:, :GW], g[:, GW:]
        vb = _layer_norm(v, lng_ref[...], lnb_ref[...]).astype(BF16)
        mixed = jnp.zeros((BLOCK, GW), F32)
        for h in range(HEADS):
            mixed = jnp.where(lane_head == h, _dot(w_causal[h], vb), mixed)
        y_ref[pl.ds(r0, BLOCK), :] = (u * (mixed + bias_ref[...])).astype(BF16)
        return carry

    lax.fori_loop(0, seq // BLOCK, chunk, 0, unroll=8)


def _gmlp(za, ln_g, ln_b, ws, bias, bsz, seq):
    full = lambda a: pl.BlockSpec(a.shape, lambda b: (0,) * a.ndim)
    return pl.pallas_call(
        _gmlp_kernel,
        grid=(bsz,),
        in_specs=[pl.BlockSpec((seq, ZA_W), lambda b: (b, 0)), full(ln_g), full(ln_b), full(ws),
                  full(bias)],
        out_specs=pl.BlockSpec((seq, GW), lambda b: (b, 0)),
        out_shape=jax.ShapeDtypeStruct((bsz * seq, GW), BF16),
        compiler_params=_params(),
        name="gmlp",
    )(za, ln_g, ln_b, ws, bias)


MLA_TQ = 256


def _mla_kernel(zb_ref, zm_ref, cos_ref, sin_ref, qg_ref, kvg_ref, wqm_ref, wqs_ref, wk_ref, wv_ref,
                y_ref, q_s, k_s, vt_s, acc_s, sc_a, sc_b, p_a, p_b):
    seq = zb_ref.shape[0]
    tq = MLA_TQ
    scale = (NOPE + ROPE) ** -0.5 * LOG2_E
    lane = lax.broadcasted_iota(jnp.int32, (1, LANES), 1)
    in_rope = (lane >= KR_LANE) & (lane < KR_LANE + ROPE)
    rows = 512
    for r in range(0, seq, rows):
        cos = cos_ref[0, r:r + rows, :]
        sin = sin_ref[0, r:r + rows, :]
        cq = zb_ref[r:r + rows, 0:Q_PAD].astype(F32)
        ms = jnp.sum(cq * cq, axis=-1, keepdims=True) * (1.0 / Q_RANK)
        cqn = (cq * lax.rsqrt(ms + RMS_EPS) * qg_ref[...]).astype(BF16)
        qm = _dot(cqn, wqm_ref[...])
        qs = _dot(cqn, wqs_ref[...])
        ckv = zb_ref[r:r + rows, Q_PAD:ZB_W].astype(F32)
        ms = jnp.mean(ckv * ckv, axis=-1, keepdims=True)
        ckvn = (ckv * lax.rsqrt(ms + RMS_EPS) * kvg_ref[...]).astype(BF16)
        kn = _dot(ckvn, wk_ref[...])
        v = _dot(ckvn, wv_ref[...])
        for sub in range(rows // tq):
            vt_s[r // tq + sub] = v[sub * tq:(sub + 1) * tq, :].T.astype(BF16)
        zm = zm_ref[r:r + rows, :]
        kr = jnp.where(in_rope, zm * cos + pltpu.roll(zm, LANES - ROPE, 1) * sin, 0.0)
        for h in range(HEADS):
            sl = slice(h * LANES, (h + 1) * LANES)
            q_h = (qm[:, sl] * cos + qs[:, sl] * sin) * scale
            for sub in range(rows // tq):
                q_s[r // tq + sub, sl, :] = q_h[sub * tq:(sub + 1) * tq, :].T.astype(BF16)
            k_s[r:r + rows, sl] = (kn[:, sl] + kr).astype(BF16)

    key_i = lax.broadcasted_iota(jnp.int32, (tq, tq), 0)
    qry_i = lax.broadcasted_iota(jnp.int32, (tq, tq), 1)
    heads = range(HEADS)
    head_lanes = [slice(h * LANES, (h + 1) * LANES) for h in heads]

    def put_scores(qb, j, dst):
        k0 = pl.multiple_of(j * tq, tq)
        for h in heads:
            dst[h] = _dot(k_s[pl.ds(k0, tq), head_lanes[h]], q_s[qb, head_lanes[h], :])

    def qblock(i, shift):
        q0 = pl.multiple_of(i * tq, tq)
        acc_s[...] = jnp.zeros(acc_s.shape, F32)

        def softmax(src, p_dst, m_old, l_old, masked):
            m_new, l_new, alpha = [], [], []
            for h in heads:
                s = src[h]
                if masked:
                    s = jnp.where(key_i <= qry_i, s, NEG_INF)
                m = jnp.maximum(m_old[h], jnp.max(s, axis=0, keepdims=True))
                a = jnp.exp2(m_old[h] - m)
                p = jnp.exp2(s - m)
                m_new.append(m)
                alpha.append(a)
                l_new.append(a * l_old[h] + jnp.sum(p, axis=0, keepdims=True))
                p_dst[h] = p.astype(BF16)
            return tuple(m_new), tuple(l_new), tuple(alpha)

        def add_values(j, p_src, alpha):
            for h in heads:
                vt = vt_s[j, h * HEAD_DIM:(h + 1) * HEAD_DIM, :]
                acc_s[h] = alpha[h] * acc_s[h] + _dot(vt, p_src[h])

        even, odd = (sc_a, p_a), (sc_b, p_b)

        def on_parity(j, fn, state):
            return lax.cond((j + shift) % 2 == 0, lambda st: fn(even, odd, st), lambda st: fn(odd, even, st),
                            state)

        def step(j, state):
            def run(cur, other, st):
                m_old, l_old, alpha_prev = st
                put_scores(i, j + 1, other[0])
                m_new, l_new, alpha = softmax(cur[0], cur[1], m_old, l_old, False)
                add_values(jnp.maximum(j - 1, 0), other[1], alpha_prev)
                return m_new, l_new, alpha
            return on_parity(j, run, state)

        def finish(cur, other, st):
            m_old, l_old, alpha_prev = st
            put_scores(jnp.minimum(i + 1, n_qblocks - 1), 0, other[0])
            _, l_new, alpha = softmax(cur[0], cur[1], m_old, l_old, True)
            add_values(jnp.maximum(i - 1, 0), other[1], alpha_prev)
            add_values(i, cur[1], alpha)
            return l_new

        p_a[...] = jnp.zeros(p_a.shape, BF16)
        p_b[...] = jnp.zeros(p_b.shape, BF16)
        init = (tuple(jnp.full((1, tq), NEG_INF, F32) for _ in heads),
                tuple(jnp.zeros((1, tq), F32) for _ in heads),
                tuple(jnp.ones((1, tq), F32) for _ in heads))
        state = lax.fori_loop(0, i, step, init)
        l_fin = on_parity(i, finish, state)
        out_t = jnp.concatenate([acc_s[h] / l_fin[h] for h in heads], axis=0)
        y_ref[pl.ds(q0, tq), :] = out_t.T.astype(BF16)
        return (i + shift + 1) % 2

    n_qblocks = seq // tq
    put_scores(0, 0, sc_a)
    lax.fori_loop(0, n_qblocks, qblock, jnp.int32(0))


def _mla(zb, zm, cosm, sinm, qg, kvg, wqm, wqs, wk, wv, bsz, seq):
    full = lambda a: pl.BlockSpec(a.shape, lambda b: (0,) * a.ndim)
    tab = pl.BlockSpec((1, seq, LANES), lambda b: (b, 0, 0))
    return pl.pallas_call(
        _mla_kernel,
        grid=(bsz,),
        in_specs=[pl.BlockSpec((seq, ZB_W), lambda b: (b, 0)),
                  pl.BlockSpec((seq, LANES), lambda b: (b, 0)), tab, tab,
                  full(qg), full(kvg), full(wqm), full(wqs), full(wk), full(wv)],
        out_specs=pl.BlockSpec((seq, GW), lambda b: (b, 0)),
        out_shape=jax.ShapeDtypeStruct((bsz * seq, GW), BF16),
        scratch_shapes=[pltpu.VMEM((seq // MLA_TQ, HEADS * LANES, MLA_TQ), BF16),
                        pltpu.VMEM((seq, HEADS * LANES), BF16),
                        pltpu.VMEM((seq // MLA_TQ, GW, MLA_TQ), BF16),
                        pltpu.VMEM((HEADS, HEAD_DIM, MLA_TQ), F32),
                        pltpu.VMEM((HEADS, MLA_TQ, MLA_TQ), F32),
                        pltpu.VMEM((HEADS, MLA_TQ, MLA_TQ), F32),
                        pltpu.VMEM((HEADS, MLA_TQ, MLA_TQ), BF16),
                        pltpu.VMEM((HEADS, MLA_TQ, MLA_TQ), BF16)],
        compiler_params=_params(),
        name="mla",
    )(zb, zm, cosm, sinm, qg, kvg, wqm, wqs, wk, wv)


DILATIONS = (1, 4, 16)
HALVES = GW // LANES


def _head_columns(cols):
    low_half = lax.broadcasted_iota(jnp.int32, (1, LANES), 1) < HEAD_DIM
    return jnp.concatenate([jnp.where(low_half, cols[0], cols[1]),
                            jnp.where(low_half, cols[2], cols[3])], axis=1)


def _band_block(q, kh, vh, w0, nk, mask):
    pairs = range(HEADS // 2)
    scores = []
    for pr in pairs:
        k_win = jnp.concatenate([kh[2 * pr + e, pl.ds(w0, nk), :] for e in range(2)], axis=0)
        scores.append(_dot_nt(q[:, pr * LANES:(pr + 1) * LANES], k_win))
    m_cols, probs = [], []
    for h in range(HEADS):
        s = scores[h // 2][:, (h % 2) * nk:(h % 2 + 1) * nk]
        s = jnp.where(mask, s, NEG_INF)
        m = jnp.max(s, axis=-1, keepdims=True)
        m_cols.append(m)
        probs.append(jnp.exp2(s - m).astype(BF16))
    spread = (lax.broadcasted_iota(jnp.int32, (2 * nk, LANES), 0) // nk
              == lax.broadcasted_iota(jnp.int32, (2 * nk, LANES), 1) // HEAD_DIM).astype(BF16)
    outs, sums = [], []
    for pr in pairs:
        v_win = jnp.concatenate([vh[2 * pr + e, pl.ds(w0, nk), :] for e in range(2)], axis=0)
        both = _dot(jnp.concatenate(probs[2 * pr:2 * pr + 2], axis=1), jnp.concatenate([v_win, spread], axis=1))
        outs.append(both[:, :LANES])
        sums.append(both[:, LANES:])
    l_full = jnp.concatenate(sums, axis=1)
    o = jnp.concatenate(outs, axis=1) / l_full
    return o, _head_columns(m_cols) + jnp.log2(l_full)


def _dilated_kernel(z1_ref, z4_ref, z16_ref, o1_ref, o4_ref, o16_ref, l1_ref, l4_ref, l16_ref, qb, kh, vh):
    seq = z1_ref.shape[0]
    lane = lax.broadcasted_iota(jnp.int32, (1, LANES), 1)
    views = {1: (z1_ref, o1_ref, l1_ref), 4: (z4_ref, o4_ref, l4_ref), 16: (z16_ref, o16_ref, l16_ref)}

    r_i = lax.broadcasted_iota(jnp.int32, (BLOCK, 2 * BLOCK), 0)
    c_i = lax.broadcasted_iota(jnp.int32, (BLOCK, 2 * BLOCK), 1)
    band = (c_i >= r_i) & (c_i <= r_i + BLOCK)
    causal = (lax.broadcasted_iota(jnp.int32, (BLOCK, BLOCK), 1)
              <= lax.broadcasted_iota(jnp.int32, (BLOCK, BLOCK), 0))
    low_half = lane < HEAD_DIM
    for h in range(HEADS):
        kh[h, 0:BLOCK, :] = jnp.zeros((BLOCK, LANES), BF16)
        vh[h, 0:BLOCK, :] = jnp.zeros((BLOCK, LANES), BF16)

    for d in DILATIONS:
        sub = seq // d
        z_ref, o_ref, l_ref = views[d]
        for res in range(d):
            dst = slice(BLOCK + res * sub, BLOCK + (res + 1) * sub)
            qb[res * sub:(res + 1) * sub, :] = z_ref[:, res * ZC_W:res * ZC_W + GW]
            for half in range(HALVES):
                col = lambda base: slice(res * ZC_W + base + half * LANES,
                                         res * ZC_W + base + (half + 1) * LANES)
                k_half = z_ref[:, col(GW)]
                v_half = z_ref[:, col(2 * GW)]
                zero = jnp.zeros_like(k_half)
                kh[2 * half, dst, :] = jnp.where(low_half, k_half, zero)
                kh[2 * half + 1, dst, :] = jnp.where(low_half, zero, k_half)
                vh[2 * half, dst, :] = jnp.where(low_half, v_half, zero)
                vh[2 * half + 1, dst, :] = jnp.where(low_half, zero, v_half)
        blocks_per_class = sub // BLOCK

        for res in range(d):
            cols = slice(res * GW, (res + 1) * GW)

            def block(n, carry, res=res, cols=cols, o_ref=o_ref, l_ref=l_ref):
                r0 = pl.multiple_of(res * sub + n * BLOCK, BLOCK)
                q = qb[pl.ds(r0, BLOCK), :]
                if blocks_per_class == 1:
                    o, lse = _band_block(q, kh, vh, r0 + BLOCK, BLOCK, causal)
                else:
                    first_key = jnp.where(n > 0, 0, BLOCK)
                    o, lse = _band_block(q, kh, vh, r0, 2 * BLOCK, band & (c_i >= first_key))
                rows = pl.ds(pl.multiple_of(n * BLOCK, BLOCK), BLOCK)
                o_ref[rows, cols] = o.astype(BF16)
                l_ref[rows, cols] = lse
                return carry

            if blocks_per_class == 1:
                block(0, 0)
            else:
                lax.fori_loop(0, blocks_per_class, block, 0, unroll=True)


def _dilated(zc_views, bsz, seq):
    headbuf = pltpu.VMEM((HEADS, seq + BLOCK, LANES), BF16)
    n = bsz * seq
    view = lambda width, d: pl.BlockSpec((seq // d, d * width), lambda b: (b, 0))
    return pl.pallas_call(
        _dilated_kernel,
        grid=(bsz,),
        in_specs=[view(ZC_W, d) for d in DILATIONS],
        out_specs=[view(GW, d) for d in DILATIONS] * 2,
        out_shape=([jax.ShapeDtypeStruct((n // d, d * GW), BF16) for d in DILATIONS]
                   + [jax.ShapeDtypeStruct((n // d, d * GW), F32) for d in DILATIONS]),
        scratch_shapes=[pltpu.VMEM((seq, GW), BF16), headbuf, headbuf],
        compiler_params=_params(),
        name="dilated",
    )(*zc_views)


CONV_W = 4
CONV_PAD = 8


def _log_sigmoid(x):
    return jnp.minimum(x, 0.0) - jnp.log1p(jnp.exp(-jnp.abs(x)))


def _mlstm_kernel(zd_ref, zm_ref, gb_ref, cw_ref, cb_ref, y_ref, xpad, q_s, k_s, kt_s, a_r, b_r, cm_c,
                  b_c, b_f, c_s, n_s):
    seq = zd_ref.shape[0]
    rows = 512
    xpad[0:CONV_PAD, :] = jnp.zeros((CONV_PAD, 2 * GW), F32)
    for r in range(0, seq, rows):
        xpad[CONV_PAD + r:CONV_PAD + r + rows, :] = zd_ref[r:r + rows, 0:2 * GW].astype(F32)
    def conv_chunk(r):
        win = xpad[r:r + rows + CONV_PAD, :]
        acc = cb_ref[...] + cw_ref[CONV_W - 1:CONV_W, :] * win[CONV_PAD:, :]
        for k in range(1, CONV_W):
            acc = acc + cw_ref[CONV_W - 1 - k:CONV_W - k, :] * pltpu.roll(win, k, 0)[CONV_PAD:, :]
        qk = acc * jax.nn.sigmoid(acc)
        k = qk[:, GW:] * (HEAD_DIM ** -0.5)
        q_s[r:r + rows, :] = qk[:, :GW].astype(BF16)
        k_s[r:r + rows, :] = k.astype(BF16)
        for sub in range(rows // BLOCK):
            kt_s[r // BLOCK + sub] = k[sub * BLOCK:(sub + 1) * BLOCK, :].T

    heads = range(HEADS)
    low_half = lax.broadcasted_iota(jnp.int32, (1, LANES), 1) < HEAD_DIM

    def head_lanes(per_head):
        return jnp.concatenate([jnp.where(low_half, per_head[0], per_head[1]),
                                jnp.where(low_half, per_head[2], per_head[3])], axis=1)

    def decay_lanes(r):
        b_cols = b_c[r:r + rows, :]
        b_f[r:r + rows, :] = head_lanes([b_cols[:, h:h + 1] for h in heads])

    conv_chunks = iter(range(0, seq, rows))

    gt = (zm_ref[...] + gb_ref[...]).T[0:8, :]
    f_log = _log_sigmoid(pltpu.roll(gt, HEADS, 0))
    pos_in_chunk = lax.broadcasted_iota(jnp.int32, (8, seq), 1) % BLOCK
    steps = [1 << s for s in range(BLOCK.bit_length() - 1)]
    conv_chunk(next(conv_chunks))
    b = f_log
    for k in steps:
        b = b + jnp.where(pos_in_chunk >= k, pltpu.roll(b, k, 1), 0.0)
    conv_chunk(next(conv_chunks))
    a = gt - b
    cm = a
    for k in steps:
        cm = jnp.maximum(cm, jnp.where(pos_in_chunk >= k, pltpu.roll(cm, k, 1), NEG_INF))
    conv_chunk(next(conv_chunks))
    a_r[...] = a
    b_r[...] = b
    pad = jnp.zeros((LANES - 8, seq), F32)
    cm_c[...] = jnp.concatenate([cm, pad], axis=0).T
    b_c[...] = jnp.concatenate([b, pad], axis=0).T
    conv_chunk(next(conv_chunks))
    for r in range(0, seq, rows):
        decay_lanes(r)

    c_s[...] = jnp.zeros((GW, GW), F32)
    n_s[...] = jnp.zeros((GW, GW), F32)
    lane_head = lax.broadcasted_iota(jnp.int32, (1, GW), 1) // HEAD_DIM
    same_head = (lax.broadcasted_iota(jnp.int32, (GW, GW), 0) // HEAD_DIM
                 == lax.broadcasted_iota(jnp.int32, (GW, GW), 1) // HEAD_DIM)
    causal = (lax.broadcasted_iota(jnp.int32, (BLOCK, BLOCK), 1)
              <= lax.broadcasted_iota(jnp.int32, (BLOCK, BLOCK), 0))

    ones_blk = jnp.ones((BLOCK, LANES), BF16)
    head_sum = (lax.broadcasted_iota(jnp.int32, (HEADS * BLOCK, GW), 0) // BLOCK
                == lax.broadcasted_iota(jnp.int32, (HEADS * BLOCK, GW), 1) // HEAD_DIM).astype(BF16)

    def chunk(c, m_run):
        r0 = pl.multiple_of(c * BLOCK, BLOCK)
        qb = q_s[pl.ds(r0, BLOCK), :]
        kb = k_s[pl.ds(r0, BLOCK), :]
        v = zd_ref[pl.ds(r0, BLOCK), 2 * GW:3 * GW]
        og = zd_ref[pl.ds(r0, BLOCK), 3 * GW:4 * GW].astype(F32)
        a_rows = a_r[:, pl.ds(r0, BLOCK)]
        b_rows = b_r[:, pl.ds(r0, BLOCK)]
        cm_cols = cm_c[pl.ds(r0, BLOCK), :]

        k_heads = jnp.concatenate(
            [jnp.where(lane_head == h, kb, jnp.zeros_like(kb)) for h in heads], axis=0)
        v_heads = jnp.concatenate(
            [jnp.where(lane_head == h, v, jnp.zeros_like(v)) for h in heads], axis=0)
        scores = _dot_nt(qb, k_heads)
        carried = _dot(qb, jnp.concatenate([c_s[...], n_s[...]], axis=1).astype(BF16))

        w_intra, g_rep, wk_rows, decays, m_next = [], [], [], [], []
        for h in heads:
            a_row = a_rows[h:h + 1, :]
            g = jnp.maximum(jnp.broadcast_to(cm_cols[:, h:h + 1], (BLOCK, BLOCK)), m_run[h])
            g_rep.append(g)
            w_intra.append(jnp.where(causal, jnp.exp(a_row - g), 0.0))
            g_end = jnp.maximum(m_run[h], jnp.max(a_row, axis=-1, keepdims=True))
            m_next.append(b_rows[h:h + 1, BLOCK - 1:BLOCK] + g_end)
            decays.append(jnp.exp(m_run[h] - g_end))
            wk_rows.append(jnp.exp(a_row - g_end))
        g_full = head_lanes(g_rep)
        inter_scale = jnp.exp(head_lanes(m_run) - g_full)
        floor = jnp.exp(-(b_f[pl.ds(r0, BLOCK), :] + g_full))

        wk = jnp.concatenate([jnp.broadcast_to(w, (HEAD_DIM, BLOCK)) for w in wk_rows], axis=0)
        decay = jnp.concatenate([jnp.broadcast_to(dd, (HEAD_DIM, 1)) for dd in decays], axis=0)
        ktw = kt_s[c] * wk
        update = _dot(ktw.astype(BF16), jnp.concatenate([v, ones_blk], axis=1))

        sqk = (scores * jnp.concatenate(w_intra, axis=1)).astype(BF16)
        intra = _dot(sqk, jnp.concatenate([v_heads, head_sum], axis=1))
        num = intra[:, :GW] + inter_scale * carried[:, :GW]
        den = intra[:, GW:] + inter_scale * carried[:, GW:]
        h_out = num / jnp.maximum(jnp.abs(den), floor)
        y_ref[pl.ds(r0, BLOCK), :] = (jax.nn.sigmoid(og) * h_out).astype(BF16)

        k_sum = update[:, GW:]
        c_s[...] = decay * c_s[...] + jnp.where(same_head, update[:, :GW], 0.0)
        n_s[...] = decay * n_s[...] + jnp.where(same_head, jnp.concatenate([k_sum, k_sum], axis=1), 0.0)
        return tuple(m_next)

    lax.fori_loop(0, seq // BLOCK, chunk, tuple(jnp.zeros((1, 1), F32) for _ in range(HEADS)), unroll=4)


def _mlstm(zd, zm, gb, cw, cb, bsz, seq):
    full = lambda a: pl.BlockSpec(a.shape, lambda b: (0,) * a.ndim)
    return pl.pallas_call(
        _mlstm_kernel,
        grid=(bsz,),
        in_specs=[pl.BlockSpec((seq, ZD_W), lambda b: (b, 0)),
                  pl.BlockSpec((seq, LANES), lambda b: (b, 0)), full(gb), full(cw), full(cb)],
        out_specs=pl.BlockSpec((seq, GW), lambda b: (b, 0)),
        out_shape=jax.ShapeDtypeStruct((bsz * seq, GW), BF16),
        scratch_shapes=[pltpu.VMEM((seq + CONV_PAD, 2 * GW), F32),
                        pltpu.VMEM((seq, GW), BF16), pltpu.VMEM((seq, GW), BF16),
                        pltpu.VMEM((seq // BLOCK, GW, BLOCK), F32),
                        pltpu.VMEM((8, seq), F32), pltpu.VMEM((8, seq), F32),
                        pltpu.VMEM((seq, LANES), F32), pltpu.VMEM((seq, LANES), F32),
                        pltpu.VMEM((seq, GW), F32),
                        pltpu.VMEM((GW, GW), F32), pltpu.VMEM((GW, GW), F32)],
        compiler_params=_params(),
        name="mlstm",
    )(zd, zm, gb, cw, cb)


FF_CHUNK = 256


def _dense_kernel(with_next, ya_ref, yb_ref, yd_ref, oc1_ref, oc4_ref, oc16_ref, lc1_ref, lc4_ref, lc16_ref,
                  x_ref, wo_ref, g1_ref, b1_ref, wg_ref, wu_ref, wd_ref, g2_ref, b2_ref, *rest):
    if with_next:
        (w_ref, wm_ref, cos_ref, sin_ref, o_ref, zd_ref, za_ref, zc1_ref, zc4_ref, zc16_ref, zb_ref, zm_ref,
         xb_ref, acc_ref, nat_ref, stage_ref) = rest
    else:
        o_ref, xb_ref, acc_ref, nat_ref = rest
    tm = x_ref.shape[0]

    def natural(view, d, slot):
        if d == 1:
            return view[...].astype(F32)
        for r in range(d):
            for half in range(GW // LANES):
                col = r * GW + half * LANES
                nat_ref[slot, half, pl.ds(r, tm // d, stride=d), :] = view[:, col:col + LANES].astype(F32)
        return jnp.concatenate([nat_ref[slot, half] for half in range(GW // LANES)], axis=1)

    views = (oc1_ref, oc4_ref, oc16_ref, lc1_ref, lc4_ref, lc16_ref)
    slots = iter(range(nat_ref.shape[0]))
    nat = [natural(v, d, next(slots) if d > 1 else None) for v, d in zip(views, DILATIONS * 2)]
    outs, lse = nat[:3], nat[3:]
    top = jnp.maximum(jnp.maximum(lse[0], lse[1]), lse[2])
    wts = [jnp.exp2(l - top) for l in lse]
    mix = wts[0] * outs[0] + wts[1] * outs[1] + wts[2] * outs[2]
    yc = (mix / (wts[0] + wts[1] + wts[2])).astype(BF16)
    acc = ALPHA * x_ref[...]
    for i, y in ((0, ya_ref[...]), (1, yb_ref[...]), (3, yd_ref[...]), (2, yc)):
        acc = acc + _dot(y, wo_ref[i * GW:(i + 1) * GW, :])
    x1 = _layer_norm(acc, g1_ref[...], b1_ref[...])
    xb_ref[...] = x1.astype(BF16)
    acc_ref[...] = ALPHA * x1
    for c in range(0, D_FF, FF_CHUNK):
        gate = _dot(xb_ref[...], wg_ref[:, c:c + FF_CHUNK])
        up = _dot(xb_ref[...], wu_ref[:, c:c + FF_CHUNK])
        act = (gate * jax.nn.sigmoid(gate) * up).astype(BF16)
        acc_ref[...] += _dot(act, wd_ref[c:c + FF_CHUNK, :])
    x2 = _layer_norm(acc_ref[...], g2_ref[...], b2_ref[...])
    o_ref[...] = x2
    if with_next:
        xb_ref[...] = x2.astype(BF16)
        _project_in(xb_ref, w_ref, wm_ref, cos_ref, sin_ref, zd_ref, za_ref, (zc1_ref, zc4_ref, zc16_ref),
                    zb_ref, zm_ref, stage_ref, 0, tm)


def _dense(ys, dil, x2, wo, g1, b1, wg, wu, wd, g2, b2, next_w, tables, tm):
    n = x2.shape[0]
    row = lambda w_: pl.BlockSpec((tm, w_), lambda i: (i, 0))
    view = lambda d: pl.BlockSpec((tm // d, d * GW), lambda i: (i, 0))
    once = lambda a: pl.BlockSpec(a.shape, lambda i: (0, 0), pipeline_mode=pl.Buffered(1))
    weights = [wo, g1, b1, wg, wu, wd, g2, b2] + list(next_w)
    tables = list(tables) if next_w else []
    out_specs = [row(D_MODEL)]
    out_shape = [jax.ShapeDtypeStruct((n, D_MODEL), F32)]
    scratch = [pltpu.VMEM((tm, D_MODEL), BF16), pltpu.VMEM((tm, D_MODEL), F32),
               pltpu.VMEM((2 * (len(DILATIONS) - 1), GW // LANES, tm, LANES), F32)]
    if next_w:
        z_specs, z_shapes, stage = _z_outputs(n, tm)
        out_specs += z_specs
        out_shape += z_shapes
        scratch.append(stage)
    return pl.pallas_call(
        functools.partial(_dense_kernel, bool(next_w)),
        grid=(n // tm,),
        in_specs=([row(GW)] * len(ys) + [view(d) for d in DILATIONS] * 2 + [row(D_MODEL)]
                  + [once(w) for w in weights] + [row(LANES)] * len(tables)),
        out_specs=out_specs,
        out_shape=out_shape,
        scratch_shapes=scratch,
        compiler_params=_params(),
        name="dense_block",
    )(*ys, *dil, x2, *weights, *tables)


def _pack_weights(w_in, b_w_uq, b_w_ukv, b_q_norm, a_bs, d_igate_b, d_fgate_b):
    nl = w_in.shape[0]
    o_b = ZA_W
    o_c = o_b + Q_RANK + KV_RANK + ROPE
    o_d = o_c + ZC_W
    o_g = o_d + ZD_W
    w_a = w_in[..., :o_b]
    w_cq = w_in[..., o_b:o_b + Q_RANK]
    w_ckv = w_in[..., o_b + Q_RANK:o_b + Q_RANK + KV_RANK]
    w_kr = w_in[..., o_b + Q_RANK + KV_RANK:o_c]
    w_c = w_in[..., o_c:o_d]
    w_d = w_in[..., o_d:o_g]
    w_gates = w_in[..., o_g:]
    zeros = lambda *s: jnp.zeros((nl,) + s, F32)
    w_b = jnp.concatenate([w_cq, zeros(D_MODEL, Q_PAD - Q_RANK), w_ckv], -1)
    w_main = jnp.concatenate([w_d, w_a, w_c, w_b], -1).astype(BF16)
    half = ROPE // 2
    w_kr_rot = jnp.concatenate([-w_kr[..., half:], w_kr[..., :half]], -1)
    w_misc = jnp.concatenate(
        [w_gates, zeros(D_MODEL, KR_LANE - 2 * HEADS), w_kr, w_kr_rot], -1).astype(BF16)

    wq = b_w_uq.reshape(nl, Q_RANK, HEADS, NOPE + ROPE)
    nope, x1, x2 = wq[..., :NOPE], wq[..., NOPE:NOPE + half], wq[..., NOPE + half:]
    tail = zeros(Q_RANK, HEADS, LANES - NOPE - ROPE)
    rowpad = ((0, 0), (0, Q_PAD - Q_RANK), (0, 0))
    wqm = jnp.pad(jnp.concatenate([nope, x1, x2, tail], -1).reshape(nl, Q_RANK, HEADS * LANES), rowpad)
    wqs = jnp.pad(jnp.concatenate([zeros(Q_RANK, HEADS, NOPE), -x2, x1, tail], -1)
                  .reshape(nl, Q_RANK, HEADS * LANES), rowpad)
    wkv = b_w_ukv.reshape(nl, KV_RANK, HEADS, NOPE + HEAD_DIM)
    wk = jnp.concatenate([wkv[..., :NOPE], zeros(KV_RANK, HEADS, LANES - NOPE)], -1)
    wk = wk.reshape(nl, KV_RANK, HEADS * LANES)
    wv = wkv[..., NOPE:].reshape(nl, KV_RANK, GW)
    qg = jnp.pad(b_q_norm, ((0, 0), (0, Q_PAD - Q_RANK)))[:, None, :]
    bias = jnp.repeat(jnp.swapaxes(a_bs, 1, 2), HEAD_DIM, axis=-1)
    gb = jnp.concatenate([d_igate_b, d_fgate_b, zeros(LANES - 2 * HEADS)], -1)[:, None, :]
    return dict(w_main=w_main, w_misc=w_misc, wqm=wqm.astype(BF16), wqs=wqs.astype(BF16),
                wk=wk.astype(BF16), wv=wv.astype(BF16), qg=qg, bias=bias, gb=gb)


@jax.jit
def _forward(x, positions, w_in, a_ln_g, a_ln_b, a_ws, a_bs, b_q_norm, b_kv_norm, b_w_uq, b_w_ukv,
             d_conv_w, d_conv_b, d_igate_b, d_fgate_b, w_out, ln1_g, ln1_b, w_gate, w_up, w_down,
             ln2_g, ln2_b):
    bsz, seq, _ = x.shape
    pk = _pack_weights(w_in, b_w_uq, b_w_ukv, b_q_norm, a_bs, d_igate_b, d_fgate_b)
    w_out_b, w_gate_b, w_up_b, w_down_b = (w.astype(BF16) for w in (w_out, w_gate, w_up, w_down))
    cosd, sind, cosm, sinm = _rope_tables(positions)
    tm = 512
    x2 = x.reshape(bsz * seq, D_MODEL)
    row = lambda a, l: a[l][None, :]
    rope_d = (cosd.reshape(bsz * seq, LANES), sind.reshape(bsz * seq, LANES))
    zd, za, *zc, zb, zm = _inproj(x2, pk["w_main"][0], pk["w_misc"][0], *rope_d, tm)
    for l in range(DEPTH):
        ya = _gmlp(za, row(a_ln_g, l), row(a_ln_b, l), a_ws[l], pk["bias"][l], bsz, seq)
        yb = _mla(zb, zm, cosm, sinm, pk["qg"][l], row(b_kv_norm, l), pk["wqm"][l], pk["wqs"][l],
                  pk["wk"][l], pk["wv"][l], bsz, seq)
        dil = _dilated(zc, bsz, seq)
        yd = _mlstm(zd, zm, pk["gb"][l], d_conv_w[l], row(d_conv_b, l), bsz, seq)
        next_w = (pk["w_main"][l + 1], pk["w_misc"][l + 1]) if l + 1 < DEPTH else ()
        outs = _dense((ya, yb, yd), dil, x2, w_out_b[l], row(ln1_g, l), row(ln1_b, l), w_gate_b[l],
                      w_up_b[l], w_down_b[l], row(ln2_g, l), row(ln2_b, l), next_w, rope_d, tm)
        if next_w:
            x2, zd, za, *zc, zb, zm = outs
        else:
            x2 = outs[0]
    return x2.reshape(bsz, seq, D_MODEL)


def kernel(x, positions, w_in, a_ln_g, a_ln_b, a_ws, a_bs, b_q_norm, b_kv_norm, b_w_uq, b_w_ukv,
           d_conv_w, d_conv_b, d_igate_b, d_fgate_b, w_out, ln1_g, ln1_b, w_gate, w_up, w_down,
           ln2_g, ln2_b):
    return _forward(x, positions, w_in, a_ln_g, a_ln_b, a_ws, a_bs, b_q_norm, b_kv_norm, b_w_uq,
                    b_w_ukv, d_conv_w, d_conv_b, d_igate_b, d_fgate_b, w_out, ln1_g, ln1_b, w_gate,
                    w_up, w_down, ln2_g, ln2_b)
```

```python
import functools

import jax
import jax.numpy as jnp
from jax import lax
from jax.experimental import pallas as pl
from jax.experimental.pallas import tpu as pltpu

F32 = jnp.float32
BF16 = jnp.bfloat16

D_MODEL = 1024
DEPTH = 4
HEAD_DIM = 64
HEADS = 4
GW = HEADS * HEAD_DIM
BLOCK = 128
Q_RANK = 192
KV_RANK = 128
NOPE = 64
ROPE = 32
D_FF = 2816
ROPE_THETA = 10000.0
LN_EPS = 1e-5
RMS_EPS = 1e-6
ALPHA = (2 * DEPTH) ** 0.25
LANES = 128
NEG_INF = float("-inf")
LOG2_E = 1.4426950408889634

Q_PAD = 256
ZD_W, ZA_W, ZC_W, ZB_W = 4 * GW, 2 * GW, 3 * GW, Q_PAD + KV_RANK
ZMAIN_W = ZD_W + ZA_W + ZC_W + ZB_W
KR_LANE = 64

VMEM_LIMIT = 60000 * 1024


def _dot(a, b):
    return jnp.dot(a, b, preferred_element_type=F32)


def _dot_nt(a, b):
    return lax.dot_general(a, b, (((1,), (1,)), ((), ())), preferred_element_type=F32)


def _params(n_axes=1):
    return pltpu.CompilerParams(
        dimension_semantics=("arbitrary",) * n_axes, vmem_limit_bytes=VMEM_LIMIT)


def _layer_norm(r, g, b):
    mu = jnp.mean(r, axis=-1, keepdims=True)
    d = r - mu
    var = jnp.mean(d * d, axis=-1, keepdims=True)
    return d * lax.rsqrt(var + LN_EPS) * g + b


def _rope_table_values(pos, inv):
    lane = lax.broadcasted_iota(jnp.int32, (1, LANES), 1)
    ang = pos * inv
    cos, sin = jnp.cos(ang), jnp.sin(ang)
    low = lane < HEAD_DIM
    cosd = jnp.where(low, cos, pltpu.roll(cos, HEAD_DIM, 1))
    sign = jnp.where((lane % HEAD_DIM) < HEAD_DIM // 2, -1.0, 1.0)
    sind = jnp.where(low, sin, pltpu.roll(sin, HEAD_DIM, 1)) * sign
    in_rope = (lane >= KR_LANE) & (lane < KR_LANE + ROPE)
    cosm = jnp.where(in_rope, cos, jnp.where(lane < KR_LANE, 1.0, 0.0))
    sinm = jnp.where(in_rope, sin, 0.0)
    return cosd, sind, cosm, sinm


def _rope_frequencies():
    lane = jnp.arange(LANES)
    half_d = HEAD_DIM // 2
    inv_d = jnp.power(ROPE_THETA, -jnp.arange(half_d, dtype=F32) / half_d)
    half_m = ROPE // 2
    inv_m = jnp.power(ROPE_THETA, -jnp.arange(half_m, dtype=F32) / half_m)
    return jnp.where(lane < HEAD_DIM, inv_d[lane % half_d],
                     jnp.where(lane < KR_LANE + ROPE, inv_m[lane % half_m], 0.0))[None, :]


def _rope_heads(x, cos, sin):
    first = (lax.broadcasted_iota(jnp.int32, (1, LANES), 1) % HEAD_DIM) < HEAD_DIM // 2
    halves = []
    for half in range(GW // LANES):
        xh = x[:, half * LANES:(half + 1) * LANES]
        rot = jnp.where(first, pltpu.roll(xh, LANES - HEAD_DIM // 2, 1), pltpu.roll(xh, HEAD_DIM // 2, 1))
        halves.append(xh * cos + rot * sin)
    return jnp.concatenate(halves, axis=1)


def _project_in(xb_ref, w_ref, wm_ref, cos_ref, sin_ref, zd_ref, za_ref, zc_refs, zb_ref, zm_ref, stage_ref,
                r0, nrows):
    rs = slice(r0, r0 + nrows)
    groups = {id(zd_ref): 0, id(za_ref): ZD_W, id(zc_refs): ZD_W + ZA_W, id(zb_ref): ZD_W + ZA_W + ZC_W}
    for ref in (zc_refs, zd_ref, za_ref, zb_ref):
        off = groups[id(ref)]
        width = ZC_W if ref is zc_refs else ref.shape[1]
        for c in range(0, width, GW):
            cw = min(GW, width - c)
            z = _dot(xb_ref[rs, :], w_ref[:, off + c:off + c + cw])
            if ref is not zc_refs:
                ref[rs, c:c + cw] = z.astype(BF16)
                continue
            if c < 2 * GW:
                z = _rope_heads(z, cos_ref[rs, :], sin_ref[rs, :])
                if c == 0:
                    z = z * (HEAD_DIM ** -0.5 * LOG2_E)
            zc_refs[0][rs, c:c + cw] = z.astype(BF16)
            for half in range(GW // LANES):
                stage_ref[c // GW, half, rs, :] = z[:, half * LANES:(half + 1) * LANES]
            for d, view in zip(DILATIONS[1:], zc_refs[1:]):
                for r in range(d):
                    for half in range(GW // LANES):
                        col = r * ZC_W + c + half * LANES
                        rows = stage_ref[c // GW, half, pl.ds(r0 + r, nrows // d, stride=d), :]
                        view[r0 // d:(r0 + nrows) // d, col:col + LANES] = rows.astype(BF16)
    zm_ref[rs, :] = _dot(xb_ref[rs, :], wm_ref[...])


def _inproj_kernel(x_ref, w_ref, wm_ref, pos_ref, inv_ref, zd_ref, za_ref, zc1_ref, zc4_ref, zc16_ref, zb_ref,
                   zm_ref, cosd_ref, sind_ref, cosm_ref, sinm_ref, xb_ref, stage_ref):
    tables = _rope_table_values(pos_ref[...].astype(F32), inv_ref[...])
    for ref, tab in zip((cosd_ref, sind_ref, cosm_ref, sinm_ref), tables):
        ref[...] = tab
    xb_ref[...] = x_ref[...].astype(BF16)
    _project_in(xb_ref, w_ref, wm_ref, cosd_ref, sind_ref, zd_ref, za_ref, (zc1_ref, zc4_ref, zc16_ref),
                zb_ref, zm_ref, stage_ref, 0, x_ref.shape[0])


def _z_outputs(n, tm):
    row = lambda w: pl.BlockSpec((tm, w), lambda i: (i, 0))
    view = lambda d: pl.BlockSpec((tm // d, d * ZC_W), lambda i: (i, 0))
    specs = [row(ZD_W), row(ZA_W)] + [view(d) for d in DILATIONS] + [row(ZB_W), row(LANES)]
    shapes = ([jax.ShapeDtypeStruct((n, ZD_W), BF16), jax.ShapeDtypeStruct((n, ZA_W), BF16)]
              + [jax.ShapeDtypeStruct((n // d, d * ZC_W), BF16) for d in DILATIONS]
              + [jax.ShapeDtypeStruct((n, ZB_W), BF16), jax.ShapeDtypeStruct((n, LANES), F32)])
    stage = pltpu.VMEM((ZC_W // GW, GW // LANES, tm, LANES), F32)
    return specs, shapes, stage


def _inproj(x2, w_main, w_misc, positions, tm):
    n = x2.shape[0]
    row = lambda w: pl.BlockSpec((tm, w), lambda i: (i, 0))
    full = lambda a: pl.BlockSpec(a.shape, lambda i: (0, 0))
    z_specs, z_shapes, stage = _z_outputs(n, tm)
    inv = _rope_frequencies()
    return pl.pallas_call(
        _inproj_kernel,
        grid=(n // tm,),
        in_specs=[row(D_MODEL), full(w_main), full(w_misc), row(1), full(inv)],
        out_specs=z_specs + [row(LANES)] * 4,
        out_shape=z_shapes + [jax.ShapeDtypeStruct((n, LANES), F32)] * 4,
        scratch_shapes=[pltpu.VMEM((tm, D_MODEL), BF16), stage],
        compiler_params=_params(),
        name="inproj",
    )(x2, w_main, w_misc, positions.reshape(n, 1), inv)


def _gmlp_kernel(za_ref, lng_ref, lnb_ref, ws_ref, bias_ref, y_ref):
    seq = za_ref.shape[0]
    r_i = lax.broadcasted_iota(jnp.int32, (BLOCK, BLOCK), 0)
    c_i = lax.broadcasted_iota(jnp.int32, (BLOCK, BLOCK), 1)
    w_causal = [jnp.where(c_i <= r_i, ws_ref[h], 0.0).astype(BF16) for h in range(HEADS)]
    lane_head = lax.broadcasted_iota(jnp.int32, (BLOCK, GW), 1) // HEAD_DIM

    def chunk(c, carry):
        r0 = pl.multiple_of(c * BLOCK, BLOCK)
        z = za_ref[pl.ds(r0, BLOCK), :].astype(F32)
        g = 0.5 * z * (1.0 + lax.erf(z * (0.5 ** 0.5)))
        u, v = g[:, :GW], g[:, GW:]
        vb = _layer_norm(v, lng_ref[...], lnb_ref[...]).astype(BF16)
        mixed = jnp.zeros((BLOCK, GW), F32)
        for h in range(HEADS):
            mixed = jnp.where(lane_head == h, _dot(w_causal[h], vb), mixed)
        y_ref[pl.ds(r0, BLOCK), :] = (u * (mixed + bias_ref[...])).astype(BF16)
        return carry

    lax.fori_loop(0, seq // BLOCK, chunk, 0, unroll=8)


def _gmlp(za, ln_g, ln_b, ws, bias, bsz, seq):
    full = lambda a: pl.BlockSpec(a.shape, lambda b: (0,) * a.ndim)
    return pl.pallas_call(
        _gmlp_kernel,
        grid=(bsz,),
        in_specs=[pl.BlockSpec((seq, ZA_W), lambda b: (b, 0)), full(ln_g), full(ln_b), full(ws),
                  full(bias)],
        out_specs=pl.BlockSpec((seq, GW), lambda b: (b, 0)),
        out_shape=jax.ShapeDtypeStruct((bsz * seq, GW), BF16),
        compiler_params=_params(),
        name="gmlp",
    )(za, ln_g, ln_b, ws, bias)


MLA_TQ = 256


def _mla_kernel(zb_ref, zm_ref, cos_ref, sin_ref, qg_ref, kvg_ref, wqm_ref, wqs_ref, wk_ref, wv_ref,
                y_ref, q_s, k_s, vt_s, acc_s, sc_a, sc_b, p_a, p_b):
    seq = zb_ref.shape[0]
    tq = MLA_TQ
    scale = (NOPE + ROPE) ** -0.5 * LOG2_E
    lane = lax.broadcasted_iota(jnp.int32, (1, LANES), 1)
    in_rope = (lane >= KR_LANE) & (lane < KR_LANE + ROPE)
    rows = 512
    for r in range(0, seq, rows):
        cos = cos_ref[0, r:r + rows, :]
        sin = sin_ref[0, r:r + rows, :]
        cq = zb_ref[r:r + rows, 0:Q_PAD].astype(F32)
        ms = jnp.sum(cq * cq, axis=-1, keepdims=True) * (1.0 / Q_RANK)
        cqn = (cq * lax.rsqrt(ms + RMS_EPS) * qg_ref[...]).astype(BF16)
        qm = _dot(cqn, wqm_ref[...])
        qs = _dot(cqn, wqs_ref[...])
        ckv = zb_ref[r:r + rows, Q_PAD:ZB_W].astype(F32)
        ms = jnp.mean(ckv * ckv, axis=-1, keepdims=True)
        ckvn = (ckv * lax.rsqrt(ms + RMS_EPS) * kvg_ref[...]).astype(BF16)
        kn = _dot(ckvn, wk_ref[...])
        v = _dot(ckvn, wv_ref[...])
        for sub in range(rows // tq):
            vt_s[r // tq + sub] = v[sub * tq:(sub + 1) * tq, :].T.astype(BF16)
        zm = zm_ref[r:r + rows, :]
        kr = jnp.where(in_rope, zm * cos + pltpu.roll(zm, LANES - ROPE, 1) * sin, 0.0)
        for h in range(HEADS):
            sl = slice(h * LANES, (h + 1) * LANES)
            q_h = (qm[:, sl] * cos + qs[:, sl] * sin) * scale
            for sub in range(rows // tq):
                q_s[r // tq + sub, sl, :] = q_h[sub * tq:(sub + 1) * tq, :].T.astype(BF16)
            k_s[r:r + rows, sl] = (kn[:, sl] + kr).astype(BF16)

    key_i = lax.broadcasted_iota(jnp.int32, (tq, tq), 0)
    qry_i = lax.broadcasted_iota(jnp.int32, (tq, tq), 1)
    heads = range(HEADS)
    head_lanes = [slice(h * LANES, (h + 1) * LANES) for h in heads]

    def put_scores(qb, j, dst):
        k0 = pl.multiple_of(j * tq, tq)
        for h in heads:
            dst[h] = _dot(k_s[pl.ds(k0, tq), head_lanes[h]], q_s[qb, head_lanes[h], :])

    def qblock(i, shift):
        q0 = pl.multiple_of(i * tq, tq)
        acc_s[...] = jnp.zeros(acc_s.shape, F32)

        def softmax(src, p_dst, m_old, l_old, masked):
            m_new, l_new, alpha = [], [], []
            for h in heads:
                s = src[h]
                if masked:
                    s = jnp.where(key_i <= qry_i, s, NEG_INF)
                m = jnp.maximum(m_old[h], jnp.max(s, axis=0, keepdims=True))
                a = jnp.exp2(m_old[h] - m)
                p = jnp.exp2(s - m)
                m_new.append(m)
                alpha.append(a)
                l_new.append(a * l_old[h] + jnp.sum(p, axis=0, keepdims=True))
                p_dst[h] = p.astype(BF16)
            return tuple(m_new), tuple(l_new), tuple(alpha)

        def add_values(j, p_src, alpha):
            for h in heads:
                vt = vt_s[j, h * HEAD_DIM:(h + 1) * HEAD_DIM, :]
                acc_s[h] = alpha[h] * acc_s[h] + _dot(vt, p_src[h])

        even, odd = (sc_a, p_a), (sc_b, p_b)

        def on_parity(j, fn, state):
            return lax.cond((j + shift) % 2 == 0, lambda st: fn(even, odd, st), lambda st: fn(odd, even, st),
                            state)

        def step(j, state):
            def run(cur, other, st):
                m_old, l_old, alpha_prev = st
                put_scores(i, j + 1, other[0])
                m_new, l_new, alpha = softmax(cur[0], cur[1], m_old, l_old, False)
                add_values(jnp.maximum(j - 1, 0), other[1], alpha_prev)
                return m_new, l_new, alpha
            return on_parity(j, run, state)

        def finish(cur, other, st):
            m_old, l_old, alpha_prev = st
            put_scores(jnp.minimum(i + 1, n_qblocks - 1), 0, other[0])
            _, l_new, alpha = softmax(cur[0], cur[1], m_old, l_old, True)
            add_values(jnp.maximum(i - 1, 0), other[1], alpha_prev)
            add_values(i, cur[1], alpha)
            return l_new

        p_a[...] = jnp.zeros(p_a.shape, BF16)
        p_b[...] = jnp.zeros(p_b.shape, BF16)
        init = (tuple(jnp.full((1, tq), NEG_INF, F32) for _ in heads),
                tuple(jnp.zeros((1, tq), F32) for _ in heads),
                tuple(jnp.ones((1, tq), F32) for _ in heads))
        state = lax.fori_loop(0, i, step, init)
        l_fin = on_parity(i, finish, state)
        out_t = jnp.concatenate([acc_s[h] / l_fin[h] for h in heads], axis=0)
        y_ref[pl.ds(q0, tq), :] = out_t.T.astype(BF16)
        return (i + shift + 1) % 2

    n_qblocks = seq // tq
    put_scores(0, 0, sc_a)
    lax.fori_loop(0, n_qblocks, qblock, jnp.int32(0))


def _mla(zb, zm, cosm, sinm, qg, kvg, wqm, wqs, wk, wv, bsz, seq):
    full = lambda a: pl.BlockSpec(a.shape, lambda b: (0,) * a.ndim)
    tab = pl.BlockSpec((1, seq, LANES), lambda b: (b, 0, 0))
    return pl.pallas_call(
        _mla_kernel,
        grid=(bsz,),
        in_specs=[pl.BlockSpec((seq, ZB_W), lambda b: (b, 0)),
                  pl.BlockSpec((seq, LANES), lambda b: (b, 0)), tab, tab,
                  full(qg), full(kvg), full(wqm), full(wqs), full(wk), full(wv)],
        out_specs=pl.BlockSpec((seq, GW), lambda b: (b, 0)),
        out_shape=jax.ShapeDtypeStruct((bsz * seq, GW), BF16),
        scratch_shapes=[pltpu.VMEM((seq // MLA_TQ, HEADS * LANES, MLA_TQ), BF16),
                        pltpu.VMEM((seq, HEADS * LANES), BF16),
                        pltpu.VMEM((seq // MLA_TQ, GW, MLA_TQ), BF16),
                        pltpu.VMEM((HEADS, HEAD_DIM, MLA_TQ), F32),
                        pltpu.VMEM((HEADS, MLA_TQ, MLA_TQ), F32),
                        pltpu.VMEM((HEADS, MLA_TQ, MLA_TQ), F32),
                        pltpu.VMEM((HEADS, MLA_TQ, MLA_TQ), BF16),
                        pltpu.VMEM((HEADS, MLA_TQ, MLA_TQ), BF16)],
        compiler_params=_params(),
        name="mla",
    )(zb, zm, cosm, sinm, qg, kvg, wqm, wqs, wk, wv)


DILATIONS = (1, 4, 16)
HALVES = GW // LANES


def _head_columns(cols):
    low_half = lax.broadcasted_iota(jnp.int32, (1, LANES), 1) < HEAD_DIM
    return jnp.concatenate([jnp.where(low_half, cols[0], cols[1]),
                            jnp.where(low_half, cols[2], cols[3])], axis=1)


def _band_block(q, kh, vh, w0, nk, mask):
    pairs = range(HEADS // 2)
    scores = []
    for pr in pairs:
        k_win = jnp.concatenate([kh[2 * pr + e, pl.ds(w0, nk), :] for e in range(2)], axis=0)
        scores.append(_dot_nt(q[:, pr * LANES:(pr + 1) * LANES], k_win))
    m_cols, l_cols, probs = [], [], []
    for h in range(HEADS):
        s = scores[h // 2][:, (h % 2) * nk:(h % 2 + 1) * nk]
        s = jnp.where(mask, s, NEG_INF)
        m = jnp.max(s, axis=-1, keepdims=True)
        p = jnp.exp2(s - m)
        m_cols.append(m)
        l_cols.append(jnp.sum(p, axis=-1, keepdims=True))
        probs.append(p.astype(BF16))
    outs = []
    for pr in pairs:
        v_win = jnp.concatenate([vh[2 * pr + e, pl.ds(w0, nk), :] for e in range(2)], axis=0)
        outs.append(_dot(jnp.concatenate(probs[2 * pr:2 * pr + 2], axis=1), v_win))
    l_full = _head_columns(l_cols)
    o = jnp.concatenate(outs, axis=1) / l_full
    return o, _head_columns(m_cols) + jnp.log2(l_full)


def _dilated_kernel(z1_ref, z4_ref, z16_ref, o1_ref, o4_ref, o16_ref, l1_ref, l4_ref, l16_ref, qb, kh, vh):
    seq = z1_ref.shape[0]
    lane = lax.broadcasted_iota(jnp.int32, (1, LANES), 1)
    views = {1: (z1_ref, o1_ref, l1_ref), 4: (z4_ref, o4_ref, l4_ref), 16: (z16_ref, o16_ref, l16_ref)}

    r_i = lax.broadcasted_iota(jnp.int32, (BLOCK, 2 * BLOCK), 0)
    c_i = lax.broadcasted_iota(jnp.int32, (BLOCK, 2 * BLOCK), 1)
    band = (c_i >= r_i) & (c_i <= r_i + BLOCK)
    causal = (lax.broadcasted_iota(jnp.int32, (BLOCK, BLOCK), 1)
              <= lax.broadcasted_iota(jnp.int32, (BLOCK, BLOCK), 0))
    low_half = lane < HEAD_DIM
    for h in range(HEADS):
        kh[h, 0:BLOCK, :] = jnp.zeros((BLOCK, LANES), BF16)
        vh[h, 0:BLOCK, :] = jnp.zeros((BLOCK, LANES), BF16)

    for d in DILATIONS:
        sub = seq // d
        z_ref, o_ref, l_ref = views[d]
        for res in range(d):
            dst = slice(BLOCK + res * sub, BLOCK + (res + 1) * sub)
            qb[res * sub:(res + 1) * sub, :] = z_ref[:, res * ZC_W:res * ZC_W + GW]
            for half in range(HALVES):
                col = lambda base: slice(res * ZC_W + base + half * LANES,
                                         res * ZC_W + base + (half + 1) * LANES)
                k_half = z_ref[:, col(GW)]
                v_half = z_ref[:, col(2 * GW)]
                zero = jnp.zeros_like(k_half)
                kh[2 * half, dst, :] = jnp.where(low_half, k_half, zero)
                kh[2 * half + 1, dst, :] = jnp.where(low_half, zero, k_half)
                vh[2 * half, dst, :] = jnp.where(low_half, v_half, zero)
                vh[2 * half + 1, dst, :] = jnp.where(low_half, zero, v_half)
        blocks_per_class = sub // BLOCK

        for res in range(d):
            cols = slice(res * GW, (res + 1) * GW)

            def block(n, carry, res=res, cols=cols, o_ref=o_ref, l_ref=l_ref):
                r0 = pl.multiple_of(res * sub + n * BLOCK, BLOCK)
                q = qb[pl.ds(r0, BLOCK), :]
                if blocks_per_class == 1:
                    o, lse = _band_block(q, kh, vh, r0 + BLOCK, BLOCK, causal)
                else:
                    first_key = jnp.where(n > 0, 0, BLOCK)
                    o, lse = _band_block(q, kh, vh, r0, 2 * BLOCK, band & (c_i >= first_key))
                rows = pl.ds(pl.multiple_of(n * BLOCK, BLOCK), BLOCK)
                o_ref[rows, cols] = o.astype(BF16)
                l_ref[rows, cols] = lse
                return carry

            if blocks_per_class == 1:
                block(0, 0)
            else:
                lax.fori_loop(0, blocks_per_class, block, 0, unroll=True)


def _dilated(zc_views, bsz, seq):
    headbuf = pltpu.VMEM((HEADS, seq + BLOCK, LANES), BF16)
    n = bsz * seq
    view = lambda width, d: pl.BlockSpec((seq // d, d * width), lambda b: (b, 0))
    return pl.pallas_call(
        _dilated_kernel,
        grid=(bsz,),
        in_specs=[view(ZC_W, d) for d in DILATIONS],
        out_specs=[view(GW, d) for d in DILATIONS] * 2,
        out_shape=([jax.ShapeDtypeStruct((n // d, d * GW), BF16) for d in DILATIONS]
                   + [jax.ShapeDtypeStruct((n // d, d * GW), F32) for d in DILATIONS]),
        scratch_shapes=[pltpu.VMEM((seq, GW), BF16), headbuf, headbuf],
        compiler_params=_params(),
        name="dilated",
    )(*zc_views)


CONV_W = 4
CONV_PAD = 8


def _log_sigmoid(x):
    return jnp.minimum(x, 0.0) - jnp.log1p(jnp.exp(-jnp.abs(x)))


def _mlstm_kernel(zd_ref, zm_ref, gb_ref, cw_ref, cb_ref, y_ref, xpad, q_s, k_s, kt_s, a_r, b_r, cm_c,
                  b_c, b_f, c_s, n_s):
    seq = zd_ref.shape[0]
    rows = 512
    xpad[0:CONV_PAD, :] = jnp.zeros((CONV_PAD, 2 * GW), F32)
    for r in range(0, seq, rows):
        xpad[CONV_PAD + r:CONV_PAD + r + rows, :] = zd_ref[r:r + rows, 0:2 * GW].astype(F32)
    def conv_chunk(r):
        win = xpad[r:r + rows + CONV_PAD, :]
        acc = cb_ref[...] + cw_ref[CONV_W - 1:CONV_W, :] * win[CONV_PAD:, :]
        for k in range(1, CONV_W):
            acc = acc + cw_ref[CONV_W - 1 - k:CONV_W - k, :] * pltpu.roll(win, k, 0)[CONV_PAD:, :]
        qk = acc * jax.nn.sigmoid(acc)
        k = qk[:, GW:] * (HEAD_DIM ** -0.5)
        q_s[r:r + rows, :] = qk[:, :GW].astype(BF16)
        k_s[r:r + rows, :] = k.astype(BF16)
        for sub in range(rows // BLOCK):
            kt_s[r // BLOCK + sub] = k[sub * BLOCK:(sub + 1) * BLOCK, :].T

    heads = range(HEADS)
    low_half = lax.broadcasted_iota(jnp.int32, (1, LANES), 1) < HEAD_DIM

    def head_lanes(per_head):
        return jnp.concatenate([jnp.where(low_half, per_head[0], per_head[1]),
                                jnp.where(low_half, per_head[2], per_head[3])], axis=1)

    def decay_lanes(r):
        b_cols = b_c[r:r + rows, :]
        b_f[r:r + rows, :] = head_lanes([b_cols[:, h:h + 1] for h in heads])

    conv_chunks = iter(range(0, seq, rows))

    gt = (zm_ref[...] + gb_ref[...]).T[0:8, :]
    f_log = _log_sigmoid(pltpu.roll(gt, HEADS, 0))
    pos_in_chunk = lax.broadcasted_iota(jnp.int32, (8, seq), 1) % BLOCK
    steps = [1 << s for s in range(BLOCK.bit_length() - 1)]
    conv_chunk(next(conv_chunks))
    b = f_log
    for k in steps:
        b = b + jnp.where(pos_in_chunk >= k, pltpu.roll(b, k, 1), 0.0)
    conv_chunk(next(conv_chunks))
    a = gt - b
    cm = a
    for k in steps:
        cm = jnp.maximum(cm, jnp.where(pos_in_chunk >= k, pltpu.roll(cm, k, 1), NEG_INF))
    conv_chunk(next(conv_chunks))
    a_r[...] = a
    b_r[...] = b
    pad = jnp.zeros((LANES - 8, seq), F32)
    cm_c[...] = jnp.concatenate([cm, pad], axis=0).T
    b_c[...] = jnp.concatenate([b, pad], axis=0).T
    conv_chunk(next(conv_chunks))
    for r in range(0, seq, rows):
        decay_lanes(r)

    c_s[...] = jnp.zeros((GW, GW), F32)
    n_s[...] = jnp.zeros((GW, GW), F32)
    lane_head = lax.broadcasted_iota(jnp.int32, (1, GW), 1) // HEAD_DIM
    same_head = (lax.broadcasted_iota(jnp.int32, (GW, GW), 0) // HEAD_DIM
                 == lax.broadcasted_iota(jnp.int32, (GW, GW), 1) // HEAD_DIM)
    causal = (lax.broadcasted_iota(jnp.int32, (BLOCK, BLOCK), 1)
              <= lax.broadcasted_iota(jnp.int32, (BLOCK, BLOCK), 0))

    ones_blk = jnp.ones((BLOCK, LANES), BF16)
    head_sum = (lax.broadcasted_iota(jnp.int32, (HEADS * BLOCK, GW), 0) // BLOCK
                == lax.broadcasted_iota(jnp.int32, (HEADS * BLOCK, GW), 1) // HEAD_DIM).astype(BF16)

    def chunk(c, m_run):
        r0 = pl.multiple_of(c * BLOCK, BLOCK)
        qb = q_s[pl.ds(r0, BLOCK), :]
        kb = k_s[pl.ds(r0, BLOCK), :]
        v = zd_ref[pl.ds(r0, BLOCK), 2 * GW:3 * GW]
        og = zd_ref[pl.ds(r0, BLOCK), 3 * GW:4 * GW].astype(F32)
        a_rows = a_r[:, pl.ds(r0, BLOCK)]
        b_rows = b_r[:, pl.ds(r0, BLOCK)]
        cm_cols = cm_c[pl.ds(r0, BLOCK), :]

        k_heads = jnp.concatenate(
            [jnp.where(lane_head == h, kb, jnp.zeros_like(kb)) for h in heads], axis=0)
        v_heads = jnp.concatenate(
            [jnp.where(lane_head == h, v, jnp.zeros_like(v)) for h in heads], axis=0)
        scores = _dot_nt(qb, k_heads)
        carried = _dot(qb, jnp.concatenate([c_s[...], n_s[...]], axis=1).astype(BF16))

        w_intra, g_rep, wk_rows, decays, m_next = [], [], [], [], []
        for h in heads:
            a_row = a_rows[h:h + 1, :]
            g = jnp.maximum(jnp.broadcast_to(cm_cols[:, h:h + 1], (BLOCK, BLOCK)), m_run[h])
            g_rep.append(g)
            w_intra.append(jnp.where(causal, jnp.exp(a_row - g), 0.0))
            g_end = jnp.maximum(m_run[h], jnp.max(a_row, axis=-1, keepdims=True))
            m_next.append(b_rows[h:h + 1, BLOCK - 1:BLOCK] + g_end)
            decays.append(jnp.exp(m_run[h] - g_end))
            wk_rows.append(jnp.exp(a_row - g_end))
        g_full = head_lanes(g_rep)
        inter_scale = jnp.exp(head_lanes(m_run) - g_full)
        floor = jnp.exp(-(b_f[pl.ds(r0, BLOCK), :] + g_full))

        wk = jnp.concatenate([jnp.broadcast_to(w, (HEAD_DIM, BLOCK)) for w in wk_rows], axis=0)
        decay = jnp.concatenate([jnp.broadcast_to(dd, (HEAD_DIM, 1)) for dd in decays], axis=0)
        ktw = kt_s[c] * wk
        update = _dot(ktw.astype(BF16), jnp.concatenate([v, ones_blk], axis=1))

        sqk = (scores * jnp.concatenate(w_intra, axis=1)).astype(BF16)
        intra = _dot(sqk, jnp.concatenate([v_heads, head_sum], axis=1))
        num = intra[:, :GW] + inter_scale * carried[:, :GW]
        den = intra[:, GW:] + inter_scale * carried[:, GW:]
        h_out = num / jnp.maximum(jnp.abs(den), floor)
        y_ref[pl.ds(r0, BLOCK), :] = (jax.nn.sigmoid(og) * h_out).astype(BF16)

        k_sum = update[:, GW:]
        c_s[...] = decay * c_s[...] + jnp.where(same_head, update[:, :GW], 0.0)
        n_s[...] = decay * n_s[...] + jnp.where(same_head, jnp.concatenate([k_sum, k_sum], axis=1), 0.0)
        return tuple(m_next)

    lax.fori_loop(0, seq // BLOCK, chunk, tuple(jnp.zeros((1, 1), F32) for _ in range(HEADS)), unroll=4)


def _mlstm(zd, zm, gb, cw, cb, bsz, seq):
    full = lambda a: pl.BlockSpec(a.shape, lambda b: (0,) * a.ndim)
    return pl.pallas_call(
        _mlstm_kernel,
        grid=(bsz,),
        in_specs=[pl.BlockSpec((seq, ZD_W), lambda b: (b, 0)),
                  pl.BlockSpec((seq, LANES), lambda b: (b, 0)), full(gb), full(cw), full(cb)],
        out_specs=pl.BlockSpec((seq, GW), lambda b: (b, 0)),
        out_shape=jax.ShapeDtypeStruct((bsz * seq, GW), BF16),
        scratch_shapes=[pltpu.VMEM((seq + CONV_PAD, 2 * GW), F32),
                        pltpu.VMEM((seq, GW), BF16), pltpu.VMEM((seq, GW), BF16),
                        pltpu.VMEM((seq // BLOCK, GW, BLOCK), F32),
                        pltpu.VMEM((8, seq), F32), pltpu.VMEM((8, seq), F32),
                        pltpu.VMEM((seq, LANES), F32), pltpu.VMEM((seq, LANES), F32),
                        pltpu.VMEM((seq, GW), F32),
                        pltpu.VMEM((GW, GW), F32), pltpu.VMEM((GW, GW), F32)],
        compiler_params=_params(),
        name="mlstm",
    )(zd, zm, gb, cw, cb)


FF_CHUNK = 256


def _dense_kernel(with_next, ya_ref, yb_ref, yd_ref, oc1_ref, oc4_ref, oc16_ref, lc1_ref, lc4_ref, lc16_ref,
                  x_ref, wo_ref, g1_ref, b1_ref, wg_ref, wu_ref, wd_ref, g2_ref, b2_ref, *rest):
    if with_next:
        (w_ref, wm_ref, cos_ref, sin_ref, o_ref, zd_ref, za_ref, zc1_ref, zc4_ref, zc16_ref, zb_ref, zm_ref,
         xb_ref, acc_ref, nat_ref, stage_ref) = rest
    else:
        o_ref, xb_ref, acc_ref, nat_ref = rest
    tm = x_ref.shape[0]

    def natural(view, d, slot):
        if d == 1:
            return view[...].astype(F32)
        for r in range(d):
            for half in range(GW // LANES):
                col = r * GW + half * LANES
                nat_ref[slot, half, pl.ds(r, tm // d, stride=d), :] = view[:, col:col + LANES].astype(F32)
        return jnp.concatenate([nat_ref[slot, half] for half in range(GW // LANES)], axis=1)

    views = (oc1_ref, oc4_ref, oc16_ref, lc1_ref, lc4_ref, lc16_ref)
    slots = iter(range(nat_ref.shape[0]))
    nat = [natural(v, d, next(slots) if d > 1 else None) for v, d in zip(views, DILATIONS * 2)]
    outs, lse = nat[:3], nat[3:]
    top = jnp.maximum(jnp.maximum(lse[0], lse[1]), lse[2])
    wts = [jnp.exp2(l - top) for l in lse]
    mix = wts[0] * outs[0] + wts[1] * outs[1] + wts[2] * outs[2]
    yc = (mix / (wts[0] + wts[1] + wts[2])).astype(BF16)
    acc = ALPHA * x_ref[...]
    for i, y in ((0, ya_ref[...]), (1, yb_ref[...]), (3, yd_ref[...]), (2, yc)):
        acc = acc + _dot(y, wo_ref[i * GW:(i + 1) * GW, :])
    x1 = _layer_norm(acc, g1_ref[...], b1_ref[...])
    xb_ref[...] = x1.astype(BF16)
    acc_ref[...] = ALPHA * x1
    for c in range(0, D_FF, FF_CHUNK):
        gate = _dot(xb_ref[...], wg_ref[:, c:c + FF_CHUNK])
        up = _dot(xb_ref[...], wu_ref[:, c:c + FF_CHUNK])
        act = (gate * jax.nn.sigmoid(gate) * up).astype(BF16)
        acc_ref[...] += _dot(act, wd_ref[c:c + FF_CHUNK, :])
    x2 = _layer_norm(acc_ref[...], g2_ref[...], b2_ref[...])
    o_ref[...] = x2
    if with_next:
        xb_ref[...] = x2.astype(BF16)
        _project_in(xb_ref, w_ref, wm_ref, cos_ref, sin_ref, zd_ref, za_ref, (zc1_ref, zc4_ref, zc16_ref),
                    zb_ref, zm_ref, stage_ref, 0, tm)


def _dense(ys, dil, x2, wo, g1, b1, wg, wu, wd, g2, b2, next_w, tables, tm):
    n = x2.shape[0]
    row = lambda w_: pl.BlockSpec((tm, w_), lambda i: (i, 0))
    view = lambda d: pl.BlockSpec((tm // d, d * GW), lambda i: (i, 0))
    once = lambda a: pl.BlockSpec(a.shape, lambda i: (0, 0), pipeline_mode=pl.Buffered(1))
    weights = [wo, g1, b1, wg, wu, wd, g2, b2] + list(next_w)
    tables = list(tables) if next_w else []
    out_specs = [row(D_MODEL)]
    out_shape = [jax.ShapeDtypeStruct((n, D_MODEL), F32)]
    scratch = [pltpu.VMEM((tm, D_MODEL), BF16), pltpu.VMEM((tm, D_MODEL), F32),
               pltpu.VMEM((2 * (len(DILATIONS) - 1), GW // LANES, tm, LANES), F32)]
    if next_w:
        z_specs, z_shapes, stage = _z_outputs(n, tm)
        out_specs += z_specs
        out_shape += z_shapes
        scratch.append(stage)
    return pl.pallas_call(
        functools.partial(_dense_kernel, bool(next_w)),
        grid=(n // tm,),
        in_specs=([row(GW)] * len(ys) + [view(d) for d in DILATIONS] * 2 + [row(D_MODEL)]
                  + [once(w) for w in weights] + [row(LANES)] * len(tables)),
        out_specs=out_specs,
        out_shape=out_shape,
        scratch_shapes=scratch,
        compiler_params=_params(),
        name="dense_block",
    )(*ys, *dil, x2, *weights, *tables)


def _pack_weights(w_in, b_w_uq, b_w_ukv, b_q_norm, a_bs, d_igate_b, d_fgate_b):
    nl = w_in.shape[0]
    o_b = ZA_W
    o_c = o_b + Q_RANK + KV_RANK + ROPE
    o_d = o_c + ZC_W
    o_g = o_d + ZD_W
    w_a = w_in[..., :o_b]
    w_cq = w_in[..., o_b:o_b + Q_RANK]
    w_ckv = w_in[..., o_b + Q_RANK:o_b + Q_RANK + KV_RANK]
    w_kr = w_in[..., o_b + Q_RANK + KV_RANK:o_c]
    w_c = w_in[..., o_c:o_d]
    w_d = w_in[..., o_d:o_g]
    w_gates = w_in[..., o_g:]
    zeros = lambda *s: jnp.zeros((nl,) + s, F32)
    w_b = jnp.concatenate([w_cq, zeros(D_MODEL, Q_PAD - Q_RANK), w_ckv], -1)
    w_main = jnp.concatenate([w_d, w_a, w_c, w_b], -1).astype(BF16)
    half = ROPE // 2
    w_kr_rot = jnp.concatenate([-w_kr[..., half:], w_kr[..., :half]], -1)
    w_misc = jnp.concatenate(
        [w_gates, zeros(D_MODEL, KR_LANE - 2 * HEADS), w_kr, w_kr_rot], -1).astype(BF16)

    wq = b_w_uq.reshape(nl, Q_RANK, HEADS, NOPE + ROPE)
    nope, x1, x2 = wq[..., :NOPE], wq[..., NOPE:NOPE + half], wq[..., NOPE + half:]
    tail = zeros(Q_RANK, HEADS, LANES - NOPE - ROPE)
    rowpad = ((0, 0), (0, Q_PAD - Q_RANK), (0, 0))
    wqm = jnp.pad(jnp.concatenate([nope, x1, x2, tail], -1).reshape(nl, Q_RANK, HEADS * LANES), rowpad)
    wqs = jnp.pad(jnp.concatenate([zeros(Q_RANK, HEADS, NOPE), -x2, x1, tail], -1)
                  .reshape(nl, Q_RANK, HEADS * LANES), rowpad)
    wkv = b_w_ukv.reshape(nl, KV_RANK, HEADS, NOPE + HEAD_DIM)
    wk = jnp.concatenate([wkv[..., :NOPE], zeros(KV_RANK, HEADS, LANES - NOPE)], -1)
    wk = wk.reshape(nl, KV_RANK, HEADS * LANES)
    wv = wkv[..., NOPE:].reshape(nl, KV_RANK, GW)
    qg = jnp.pad(b_q_norm, ((0, 0), (0, Q_PAD - Q_RANK)))[:, None, :]
    bias = jnp.repeat(jnp.swapaxes(a_bs, 1, 2), HEAD_DIM, axis=-1)
    gb = jnp.concatenate([d_igate_b, d_fgate_b, zeros(LANES - 2 * HEADS)], -1)[:, None, :]
    return dict(w_main=w_main, w_misc=w_misc, wqm=wqm.astype(BF16), wqs=wqs.astype(BF16),
                wk=wk.astype(BF16), wv=wv.astype(BF16), qg=qg, bias=bias, gb=gb)


@jax.jit
def _forward(x, positions, w_in, a_ln_g, a_ln_b, a_ws, a_bs, b_q_norm, b_kv_norm, b_w_uq, b_w_ukv,
             d_conv_w, d_conv_b, d_igate_b, d_fgate_b, w_out, ln1_g, ln1_b, w_gate, w_up, w_down,
             ln2_g, ln2_b):
    bsz, seq, _ = x.shape
    pk = _pack_weights(w_in, b_w_uq, b_w_ukv, b_q_norm, a_bs, d_igate_b, d_fgate_b)
    w_out_b, w_gate_b, w_up_b, w_down_b = (w.astype(BF16) for w in (w_out, w_gate, w_up, w_down))
    tm = 512
    x2 = x.reshape(bsz * seq, D_MODEL)
    row = lambda a, l: a[l][None, :]
    zd, za, *zc, zb, zm, cosd, sind, cosm, sinm = _inproj(x2, pk["w_main"][0], pk["w_misc"][0], positions, tm)
    rope_d = (cosd, sind)
    cosm, sinm = (t.reshape(bsz, seq, LANES) for t in (cosm, sinm))
    for l in range(DEPTH):
        ya = _gmlp(za, row(a_ln_g, l), row(a_ln_b, l), a_ws[l], pk["bias"][l], bsz, seq)
        yb = _mla(zb, zm, cosm, sinm, pk["qg"][l], row(b_kv_norm, l), pk["wqm"][l], pk["wqs"][l],
                  pk["wk"][l], pk["wv"][l], bsz, seq)
        dil = _dilated(zc, bsz, seq)
        yd = _mlstm(zd, zm, pk["gb"][l], d_conv_w[l], row(d_conv_b, l), bsz, seq)
        next_w = (pk["w_main"][l + 1], pk["w_misc"][l + 1]) if l + 1 < DEPTH else ()
        outs = _dense((ya, yb, yd), dil, x2, w_out_b[l], row(ln1_g, l), row(ln1_b, l), w_gate_b[l],
                      w_up_b[l], w_down_b[l], row(ln2_g, l), row(ln2_b, l), next_w, rope_d, tm)
        if next_w:
            x2, zd, za, *zc, zb, zm = outs
        else:
            x2 = outs[0]
    return x2.reshape(bsz, seq, D_MODEL)


def kernel(x, positions, w_in, a_ln_g, a_ln_b, a_ws, a_bs, b_q_norm, b_kv_norm, b_w_uq, b_w_ukv,
           d_conv_w, d_conv_b, d_igate_b, d_fgate_b, w_out, ln1_g, ln1_b, w_gate, w_up, w_down,
           ln2_g, ln2_b):
    return _forward(x, positions, w_in, a_ln_g, a_ln_b, a_ws, a_bs, b_q_norm, b_kv_norm, b_w_uq,
                    b_w_ukv, d_conv_w, d_conv_b, d_igate_b, d_fgate_b, w_out, ln1_g, ln1_b, w_gate,
                    w_up, w_down, ln2_g, ln2_b)
```

```python
import functools

import jax
import jax.numpy as jnp
from jax import lax
from jax.experimental import pallas as pl
from jax.experimental.pallas import tpu as pltpu

F32 = jnp.float32
BF16 = jnp.bfloat16

D_MODEL = 1024
DEPTH = 4
HEAD_DIM = 64
HEADS = 4
GW = HEADS * HEAD_DIM
BLOCK = 128
Q_RANK = 192
KV_RANK = 128
NOPE = 64
ROPE = 32
D_FF = 2816
ROPE_THETA = 10000.0
LN_EPS = 1e-5
RMS_EPS = 1e-6
ALPHA = (2 * DEPTH) ** 0.25
LANES = 128
NEG_INF = float("-inf")
LOG2_E = 1.4426950408889634

Q_PAD = 256
ZD_W, ZA_W, ZC_W, ZB_W = 4 * GW, 2 * GW, 3 * GW, Q_PAD + KV_RANK
ZMAIN_W = ZD_W + ZA_W + ZC_W + ZB_W
KR_LANE = 64

VMEM_LIMIT = 60000 * 1024


def _dot(a, b):
    return jnp.dot(a, b, preferred_element_type=F32)


def _dot_nt(a, b):
    return lax.dot_general(a, b, (((1,), (1,)), ((), ())), preferred_element_type=F32)


def _params(n_axes=1):
    return pltpu.CompilerParams(
        dimension_semantics=("arbitrary",) * n_axes, vmem_limit_bytes=VMEM_LIMIT)


def _layer_norm(r, g, b):
    mu = jnp.mean(r, axis=-1, keepdims=True)
    d = r - mu
    var = jnp.mean(d * d, axis=-1, keepdims=True)
    return d * lax.rsqrt(var + LN_EPS) * g + b


def _rope_table_values(pos, inv):
    lane = lax.broadcasted_iota(jnp.int32, (1, LANES), 1)
    ang = pos * inv
    cos, sin = jnp.cos(ang), jnp.sin(ang)
    low = lane < HEAD_DIM
    cosd = jnp.where(low, cos, pltpu.roll(cos, HEAD_DIM, 1))
    sign = jnp.where((lane % HEAD_DIM) < HEAD_DIM // 2, -1.0, 1.0)
    sind = jnp.where(low, sin, pltpu.roll(sin, HEAD_DIM, 1)) * sign
    in_rope = (lane >= KR_LANE) & (lane < KR_LANE + ROPE)
    cosm = jnp.where(in_rope, cos, jnp.where(lane < KR_LANE, 1.0, 0.0))
    sinm = jnp.where(in_rope, sin, 0.0)
    return cosd, sind, cosm, sinm


def _rope_frequencies():
    lane = jnp.arange(LANES)
    half_d = HEAD_DIM // 2
    inv_d = jnp.power(ROPE_THETA, -jnp.arange(half_d, dtype=F32) / half_d)
    half_m = ROPE // 2
    inv_m = jnp.power(ROPE_THETA, -jnp.arange(half_m, dtype=F32) / half_m)
    return jnp.where(lane < HEAD_DIM, inv_d[lane % half_d],
                     jnp.where(lane < KR_LANE + ROPE, inv_m[lane % half_m], 0.0))[None, :]


def _rope_heads(x, cos, sin):
    first = (lax.broadcasted_iota(jnp.int32, (1, LANES), 1) % HEAD_DIM) < HEAD_DIM // 2
    halves = []
    for half in range(GW // LANES):
        xh = x[:, half * LANES:(half + 1) * LANES]
        rot = jnp.where(first, pltpu.roll(xh, LANES - HEAD_DIM // 2, 1), pltpu.roll(xh, HEAD_DIM // 2, 1))
        halves.append(xh * cos + rot * sin)
    return jnp.concatenate(halves, axis=1)


def _project_in(xb_ref, w_ref, wm_ref, cos_ref, sin_ref, zd_ref, za_ref, zc_refs, zb_ref, zm_ref, stage_ref,
                r0, nrows):
    rs = slice(r0, r0 + nrows)
    groups = {id(zd_ref): 0, id(za_ref): ZD_W, id(zc_refs): ZD_W + ZA_W, id(zb_ref): ZD_W + ZA_W + ZC_W}
    for ref in (zc_refs, zd_ref, za_ref, zb_ref):
        off = groups[id(ref)]
        width = ZC_W if ref is zc_refs else ref.shape[1]
        for c in range(0, width, GW):
            cw = min(GW, width - c)
            z = _dot(xb_ref[rs, :], w_ref[:, off + c:off + c + cw])
            if ref is not zc_refs:
                ref[rs, c:c + cw] = z.astype(BF16)
                continue
            if c < 2 * GW:
                z = _rope_heads(z, cos_ref[rs, :], sin_ref[rs, :])
                if c == 0:
                    z = z * (HEAD_DIM ** -0.5 * LOG2_E)
            zc_refs[0][rs, c:c + cw] = z.astype(BF16)
            for half in range(GW // LANES):
                stage_ref[c // GW, half, rs, :] = z[:, half * LANES:(half + 1) * LANES]
            for d, view in zip(DILATIONS[1:], zc_refs[1:]):
                for r in range(d):
                    for half in range(GW // LANES):
                        col = r * ZC_W + c + half * LANES
                        rows = stage_ref[c // GW, half, pl.ds(r0 + r, nrows // d, stride=d), :]
                        view[r0 // d:(r0 + nrows) // d, col:col + LANES] = rows.astype(BF16)
    zm_ref[rs, :] = _dot(xb_ref[rs, :], wm_ref[...])


def _inproj_kernel(x_ref, w_ref, wm_ref, pos_ref, inv_ref, zd_ref, za_ref, zc1_ref, zc4_ref, zc16_ref, zb_ref,
                   zm_ref, cosd_ref, sind_ref, cosm_ref, sinm_ref, xb_ref, stage_ref):
    tables = _rope_table_values(pos_ref[...].astype(F32), inv_ref[...])
    for ref, tab in zip((cosd_ref, sind_ref, cosm_ref, sinm_ref), tables):
        ref[...] = tab
    xb_ref[...] = x_ref[...].astype(BF16)
    _project_in(xb_ref, w_ref, wm_ref, cosd_ref, sind_ref, zd_ref, za_ref, (zc1_ref, zc4_ref, zc16_ref),
                zb_ref, zm_ref, stage_ref, 0, x_ref.shape[0])


def _z_outputs(n, tm):
    row = lambda w: pl.BlockSpec((tm, w), lambda i: (i, 0))
    view = lambda d: pl.BlockSpec((tm // d, d * ZC_W), lambda i: (i, 0))
    specs = [row(ZD_W), row(ZA_W)] + [view(d) for d in DILATIONS] + [row(ZB_W), row(LANES)]
    shapes = ([jax.ShapeDtypeStruct((n, ZD_W), BF16), jax.ShapeDtypeStruct((n, ZA_W), BF16)]
              + [jax.ShapeDtypeStruct((n // d, d * ZC_W), BF16) for d in DILATIONS]
              + [jax.ShapeDtypeStruct((n, ZB_W), BF16), jax.ShapeDtypeStruct((n, LANES), F32)])
    stage = pltpu.VMEM((ZC_W // GW, GW // LANES, tm, LANES), F32)
    return specs, shapes, stage


def _inproj(x2, w_main, w_misc, positions, tm):
    n = x2.shape[0]
    row = lambda w: pl.BlockSpec((tm, w), lambda i: (i, 0))
    full = lambda a: pl.BlockSpec(a.shape, lambda i: (0, 0))
    z_specs, z_shapes, stage = _z_outputs(n, tm)
    inv = _rope_frequencies()
    return pl.pallas_call(
        _inproj_kernel,
        grid=(n // tm,),
        in_specs=[row(D_MODEL), full(w_main), full(w_misc), row(1), full(inv)],
        out_specs=z_specs + [row(LANES)] * 4,
        out_shape=z_shapes + [jax.ShapeDtypeStruct((n, LANES), F32)] * 4,
        scratch_shapes=[pltpu.VMEM((tm, D_MODEL), BF16), stage],
        compiler_params=_params(),
        name="inproj",
    )(x2, w_main, w_misc, positions.reshape(n, 1), inv)


def _gmlp_kernel(za_ref, lng_ref, lnb_ref, ws_ref, bias_ref, y_ref):
    seq = za_ref.shape[0]
    r_i = lax.broadcasted_iota(jnp.int32, (BLOCK, BLOCK), 0)
    c_i = lax.broadcasted_iota(jnp.int32, (BLOCK, BLOCK), 1)
    w_causal = [jnp.where(c_i <= r_i, ws_ref[h], 0.0).astype(BF16) for h in range(HEADS)]
    lane_head = lax.broadcasted_iota(jnp.int32, (BLOCK, GW), 1) // HEAD_DIM

    def chunk(c, carry):
        r0 = pl.multiple_of(c * BLOCK, BLOCK)
        z = za_ref[pl.ds(r0, BLOCK), :].astype(F32)
        g = 0.5 * z * (1.0 + lax.erf(z * (0.5 ** 0.5)))
        u, v = g[:, :GW], g[:, GW:]
        vb = _layer_norm(v, lng_ref[...], lnb_ref[...]).astype(BF16)
        mixed = jnp.zeros((BLOCK, GW), F32)
        for h in range(HEADS):
            mixed = jnp.where(lane_head == h, _dot(w_causal[h], vb), mixed)
        y_ref[pl.ds(r0, BLOCK), :] = (u * (mixed + bias_ref[...])).astype(BF16)
        return carry

    lax.fori_loop(0, seq // BLOCK, chunk, 0, unroll=8)


def _gmlp(za, ln_g, ln_b, ws, bias, bsz, seq):
    full = lambda a: pl.BlockSpec(a.shape, lambda b: (0,) * a.ndim)
    return pl.pallas_call(
        _gmlp_kernel,
        grid=(bsz,),
        in_specs=[pl.BlockSpec((seq, ZA_W), lambda b: (b, 0)), full(ln_g), full(ln_b), full(ws),
                  full(bias)],
        out_specs=pl.BlockSpec((seq, GW), lambda b: (b, 0)),
        out_shape=jax.ShapeDtypeStruct((bsz * seq, GW), BF16),
        compiler_params=_params(),
        name="gmlp",
    )(za, ln_g, ln_b, ws, bias)


MLA_TQ = 256


def _mla_kernel(zb_ref, zm_ref, cos_ref, sin_ref, qg_ref, kvg_ref, wqm_ref, wqs_ref, wk_ref, wv_ref,
                y_ref, q_s, k_s, vt_s, acc_s, sc_a, sc_b, p_a, p_b):
    seq = zb_ref.shape[0]
    tq = MLA_TQ
    scale = (NOPE + ROPE) ** -0.5 * LOG2_E
    lane = lax.broadcasted_iota(jnp.int32, (1, LANES), 1)
    in_rope = (lane >= KR_LANE) & (lane < KR_LANE + ROPE)
    rows = 512
    for r in range(0, seq, rows):
        cos = cos_ref[0, r:r + rows, :]
        sin = sin_ref[0, r:r + rows, :]
        cq = zb_ref[r:r + rows, 0:Q_PAD].astype(F32)
        ms = jnp.sum(cq * cq, axis=-1, keepdims=True) * (1.0 / Q_RANK)
        cqn = (cq * lax.rsqrt(ms + RMS_EPS) * qg_ref[...]).astype(BF16)
        qm = _dot(cqn, wqm_ref[...])
        qs = _dot(cqn, wqs_ref[...])
        ckv = zb_ref[r:r + rows, Q_PAD:ZB_W].astype(F32)
        ms = jnp.mean(ckv * ckv, axis=-1, keepdims=True)
        ckvn = (ckv * lax.rsqrt(ms + RMS_EPS) * kvg_ref[...]).astype(BF16)
        kn = _dot(ckvn, wk_ref[...])
        v = _dot(ckvn, wv_ref[...])
        for sub in range(rows // tq):
            vt_s[r // tq + sub] = v[sub * tq:(sub + 1) * tq, :].T.astype(BF16)
        zm = zm_ref[r:r + rows, :]
        kr = jnp.where(in_rope, zm * cos + pltpu.roll(zm, LANES - ROPE, 1) * sin, 0.0)
        for h in range(HEADS):
            sl = slice(h * LANES, (h + 1) * LANES)
            q_h = (qm[:, sl] * cos + qs[:, sl] * sin) * scale
            for sub in range(rows // tq):
                q_s[r // tq + sub, sl, :] = q_h[sub * tq:(sub + 1) * tq, :].T.astype(BF16)
            k_s[r:r + rows, sl] = (kn[:, sl] + kr).astype(BF16)

    key_i = lax.broadcasted_iota(jnp.int32, (tq, tq), 0)
    qry_i = lax.broadcasted_iota(jnp.int32, (tq, tq), 1)
    heads = range(HEADS)
    head_lanes = [slice(h * LANES, (h + 1) * LANES) for h in heads]

    def put_scores(qb, j, dst):
        k0 = pl.multiple_of(j * tq, tq)
        for h in heads:
            dst[h] = _dot(k_s[pl.ds(k0, tq), head_lanes[h]], q_s[qb, head_lanes[h], :])

    def qblock(i, shift):
        q0 = pl.multiple_of(i * tq, tq)
        acc_s[...] = jnp.zeros(acc_s.shape, F32)

        def softmax(src, p_dst, m_old, l_old, masked):
            m_new, l_new, alpha = [], [], []
            for h in heads:
                s = src[h]
                if masked:
                    s = jnp.where(key_i <= qry_i, s, NEG_INF)
                m = jnp.maximum(m_old[h], jnp.max(s, axis=0, keepdims=True))
                a = jnp.exp2(m_old[h] - m)
                p = jnp.exp2(s - m)
                m_new.append(m)
                alpha.append(a)
                l_new.append(a * l_old[h] + jnp.sum(p, axis=0, keepdims=True))
                p_dst[h] = p.astype(BF16)
            return tuple(m_new), tuple(l_new), tuple(alpha)

        def add_values(j, p_src, alpha):
            for h in heads:
                vt = vt_s[j, h * HEAD_DIM:(h + 1) * HEAD_DIM, :]
                acc_s[h] = alpha[h] * acc_s[h] + _dot(vt, p_src[h])

        even, odd = (sc_a, p_a), (sc_b, p_b)

        def on_parity(j, fn, state):
            return lax.cond((j + shift) % 2 == 0, lambda st: fn(even, odd, st), lambda st: fn(odd, even, st),
                            state)

        def step(j, state):
            def run(cur, other, st):
                m_old, l_old, alpha_prev = st
                put_scores(i, j + 1, other[0])
                m_new, l_new, alpha = softmax(cur[0], cur[1], m_old, l_old, False)
                add_values(jnp.maximum(j - 1, 0), other[1], alpha_prev)
                return m_new, l_new, alpha
            return on_parity(j, run, state)

        def finish(cur, other, st):
            m_old, l_old, alpha_prev = st
            put_scores(jnp.minimum(i + 1, n_qblocks - 1), 0, other[0])
            _, l_new, alpha = softmax(cur[0], cur[1], m_old, l_old, True)
            add_values(jnp.maximum(i - 1, 0), other[1], alpha_prev)
            add_values(i, cur[1], alpha)
            return l_new

        p_a[...] = jnp.zeros(p_a.shape, BF16)
        p_b[...] = jnp.zeros(p_b.shape, BF16)
        init = (tuple(jnp.full((1, tq), NEG_INF, F32) for _ in heads),
                tuple(jnp.zeros((1, tq), F32) for _ in heads),
                tuple(jnp.ones((1, tq), F32) for _ in heads))
        state = lax.fori_loop(0, i, step, init)
        l_fin = on_parity(i, finish, state)
        out_t = jnp.concatenate([acc_s[h] / l_fin[h] for h in heads], axis=0)
        y_ref[pl.ds(q0, tq), :] = out_t.T.astype(BF16)
        return (i + shift + 1) % 2

    n_qblocks = seq // tq
    put_scores(0, 0, sc_a)
    lax.fori_loop(0, n_qblocks, qblock, jnp.int32(0))


def _mla(zb, zm, cosm, sinm, qg, kvg, wqm, wqs, wk, wv, bsz, seq):
    full = lambda a: pl.BlockSpec(a.shape, lambda b: (0,) * a.ndim)
    tab = pl.BlockSpec((1, seq, LANES), lambda b: (b, 0, 0))
    return pl.pallas_call(
        _mla_kernel,
        grid=(bsz,),
        in_specs=[pl.BlockSpec((seq, ZB_W), lambda b: (b, 0)),
                  pl.BlockSpec((seq, LANES), lambda b: (b, 0)), tab, tab,
                  full(qg), full(kvg), full(wqm), full(wqs), full(wk), full(wv)],
        out_specs=pl.BlockSpec((seq, GW), lambda b: (b, 0)),
        out_shape=jax.ShapeDtypeStruct((bsz * seq, GW), BF16),
        scratch_shapes=[pltpu.VMEM((seq // MLA_TQ, HEADS * LANES, MLA_TQ), BF16),
                        pltpu.VMEM((seq, HEADS * LANES), BF16),
                        pltpu.VMEM((seq // MLA_TQ, GW, MLA_TQ), BF16),
                        pltpu.VMEM((HEADS, HEAD_DIM, MLA_TQ), F32),
                        pltpu.VMEM((HEADS, MLA_TQ, MLA_TQ), F32),
                        pltpu.VMEM((HEADS, MLA_TQ, MLA_TQ), F32),
                        pltpu.VMEM((HEADS, MLA_TQ, MLA_TQ), BF16),
                        pltpu.VMEM((HEADS, MLA_TQ, MLA_TQ), BF16)],
        compiler_params=_params(),
        name="mla",
    )(zb, zm, cosm, sinm, qg, kvg, wqm, wqs, wk, wv)


DILATIONS = (1, 4, 16)
HALVES = GW // LANES


def _head_columns(cols):
    low_half = lax.broadcasted_iota(jnp.int32, (1, LANES), 1) < HEAD_DIM
    return jnp.concatenate([jnp.where(low_half, cols[0], cols[1]),
                            jnp.where(low_half, cols[2], cols[3])], axis=1)


def _band_block(q, kh, vh, w0, nk, mask):
    pairs = range(HEADS // 2)
    scores = []
    for pr in pairs:
        k_win = jnp.concatenate([kh[2 * pr + e, pl.ds(w0, nk), :] for e in range(2)], axis=0)
        scores.append(_dot_nt(q[:, pr * LANES:(pr + 1) * LANES], k_win))
    m_cols, l_cols, probs = [], [], []
    for h in range(HEADS):
        s = scores[h // 2][:, (h % 2) * nk:(h % 2 + 1) * nk]
        s = jnp.where(mask, s, NEG_INF)
        m = jnp.max(s, axis=-1, keepdims=True)
        p = jnp.exp2(s - m)
        m_cols.append(m)
        l_cols.append(jnp.sum(p, axis=-1, keepdims=True))
        probs.append(p.astype(BF16))
    outs = []
    for pr in pairs:
        v_win = jnp.concatenate([vh[2 * pr + e, pl.ds(w0, nk), :] for e in range(2)], axis=0)
        outs.append(_dot(jnp.concatenate(probs[2 * pr:2 * pr + 2], axis=1), v_win))
    l_full = _head_columns(l_cols)
    o = jnp.concatenate(outs, axis=1) / l_full
    return o, _head_columns(m_cols) + jnp.log2(l_full)


def _dilated_kernel(z1_ref, z4_ref, z16_ref, o1_ref, o4_ref, o16_ref, l1_ref, l4_ref, l16_ref, qb, kh, vh):
    seq = z1_ref.shape[0]
    lane = lax.broadcasted_iota(jnp.int32, (1, LANES), 1)
    views = {1: (z1_ref, o1_ref, l1_ref), 4: (z4_ref, o4_ref, l4_ref), 16: (z16_ref, o16_ref, l16_ref)}

    r_i = lax.broadcasted_iota(jnp.int32, (BLOCK, 2 * BLOCK), 0)
    c_i = lax.broadcasted_iota(jnp.int32, (BLOCK, 2 * BLOCK), 1)
    band = (c_i >= r_i) & (c_i <= r_i + BLOCK)
    causal = (lax.broadcasted_iota(jnp.int32, (BLOCK, BLOCK), 1)
              <= lax.broadcasted_iota(jnp.int32, (BLOCK, BLOCK), 0))
    low_half = lane < HEAD_DIM
    for h in range(HEADS):
        kh[h, 0:BLOCK, :] = jnp.zeros((BLOCK, LANES), BF16)
        vh[h, 0:BLOCK, :] = jnp.zeros((BLOCK, LANES), BF16)

    for d in DILATIONS:
        sub = seq // d
        z_ref, o_ref, l_ref = views[d]
        for res in range(d):
            dst = slice(BLOCK + res * sub, BLOCK + (res + 1) * sub)
            qb[res * sub:(res + 1) * sub, :] = z_ref[:, res * ZC_W:res * ZC_W + GW]
            for half in range(HALVES):
                col = lambda base: slice(res * ZC_W + base + half * LANES,
                                         res * ZC_W + base + (half + 1) * LANES)
                k_half = z_ref[:, col(GW)]
                v_half = z_ref[:, col(2 * GW)]
                zero = jnp.zeros_like(k_half)
                kh[2 * half, dst, :] = jnp.where(low_half, k_half, zero)
                kh[2 * half + 1, dst, :] = jnp.where(low_half, zero, k_half)
                vh[2 * half, dst, :] = jnp.where(low_half, v_half, zero)
                vh[2 * half + 1, dst, :] = jnp.where(low_half, zero, v_half)
        blocks_per_class = sub // BLOCK

        for res in range(d):
            cols = slice(res * GW, (res + 1) * GW)

            def block(n, carry, res=res, cols=cols, o_ref=o_ref, l_ref=l_ref):
                r0 = pl.multiple_of(res * sub + n * BLOCK, BLOCK)
                q = qb[pl.ds(r0, BLOCK), :]
                if blocks_per_class == 1:
                    o, lse = _band_block(q, kh, vh, r0 + BLOCK, BLOCK, causal)
                else:
                    first_key = jnp.where(n > 0, 0, BLOCK)
                    o, lse = _band_block(q, kh, vh, r0, 2 * BLOCK, band & (c_i >= first_key))
                rows = pl.ds(pl.multiple_of(n * BLOCK, BLOCK), BLOCK)
                o_ref[rows, cols] = o.astype(BF16)
                l_ref[rows, cols] = lse
                return carry

            if blocks_per_class == 1:
                block(0, 0)
            else:
                lax.fori_loop(0, blocks_per_class, block, 0, unroll=True)


def _dilated(zc_views, bsz, seq):
    headbuf = pltpu.VMEM((HEADS, seq + BLOCK, LANES), BF16)
    n = bsz * seq
    view = lambda width, d: pl.BlockSpec((seq // d, d * width), lambda b: (b, 0))
    return pl.pallas_call(
        _dilated_kernel,
        grid=(bsz,),
        in_specs=[view(ZC_W, d) for d in DILATIONS],
        out_specs=[view(GW, d) for d in DILATIONS] * 2,
        out_shape=([jax.ShapeDtypeStruct((n // d, d * GW), BF16) for d in DILATIONS]
                   + [jax.ShapeDtypeStruct((n // d, d * GW), F32) for d in DILATIONS]),
        scratch_shapes=[pltpu.VMEM((seq, GW), BF16), headbuf, headbuf],
        compiler_params=_params(),
        name="dilated",
    )(*zc_views)


CONV_W = 4
CONV_PAD = 8


def _log_sigmoid(x):
    return jnp.minimum(x, 0.0) - jnp.log1p(jnp.exp(-jnp.abs(x)))


def _mlstm_kernel(zd_ref, zm_ref, gb_ref, cw_ref, cb_ref, y_ref, xpad, q_s, k_s, kt_s, a_r, b_r, cm_c,
                  b_c, b_f, c_s, n_s):
    seq = zd_ref.shape[0]
    rows = 512
    xpad[0:CONV_PAD, :] = jnp.zeros((CONV_PAD, 2 * GW), F32)
    for r in range(0, seq, rows):
        xpad[CONV_PAD + r:CONV_PAD + r + rows, :] = zd_ref[r:r + rows, 0:2 * GW].astype(F32)
    def conv_chunk(r):
        win = xpad[r:r + rows + CONV_PAD, :]
        acc = cb_ref[...] + cw_ref[CONV_W - 1:CONV_W, :] * win[CONV_PAD:, :]
        for k in range(1, CONV_W):
            acc = acc + cw_ref[CONV_W - 1 - k:CONV_W - k, :] * pltpu.roll(win, k, 0)[CONV_PAD:, :]
        qk = acc * jax.nn.sigmoid(acc)
        k = qk[:, GW:] * (HEAD_DIM ** -0.5)
        q_s[r:r + rows, :] = qk[:, :GW].astype(BF16)
        k_s[r:r + rows, :] = k.astype(BF16)
        for sub in range(rows // BLOCK):
            kt_s[r // BLOCK + sub] = k[sub * BLOCK:(sub + 1) * BLOCK, :].T

    heads = range(HEADS)
    low_half = lax.broadcasted_iota(jnp.int32, (1, LANES), 1) < HEAD_DIM

    def head_lanes(per_head):
        return jnp.concatenate([jnp.where(low_half, per_head[0], per_head[1]),
                                jnp.where(low_half, per_head[2], per_head[3])], axis=1)

    def decay_lanes(r):
        b_cols = b_c[r:r + rows, :]
        b_f[r:r + rows, :] = head_lanes([b_cols[:, h:h + 1] for h in heads])

    conv_chunks = iter(range(0, seq, rows))

    gt = (zm_ref[...] + gb_ref[...]).T[0:8, :]
    f_log = _log_sigmoid(pltpu.roll(gt, HEADS, 0))
    pos_in_chunk = lax.broadcasted_iota(jnp.int32, (8, seq), 1) % BLOCK
    steps = [1 << s for s in range(BLOCK.bit_length() - 1)]
    conv_chunk(next(conv_chunks))
    b = f_log
    for k in steps:
        b = b + jnp.where(pos_in_chunk >= k, pltpu.roll(b, k, 1), 0.0)
    conv_chunk(next(conv_chunks))
    a = gt - b
    cm = a
    for k in steps:
        cm = jnp.maximum(cm, jnp.where(pos_in_chunk >= k, pltpu.roll(cm, k, 1), NEG_INF))
    conv_chunk(next(conv_chunks))
    a_r[...] = a
    b_r[...] = b
    pad = jnp.zeros((LANES - 8, seq), F32)
    cm_c[...] = jnp.concatenate([cm, pad], axis=0).T
    b_c[...] = jnp.concatenate([b, pad], axis=0).T
    conv_chunk(next(conv_chunks))
    for r in range(0, seq, rows):
        decay_lanes(r)

    c_s[...] = jnp.zeros((GW, GW), F32)
    n_s[...] = jnp.zeros((GW, GW), F32)
    lane_head = lax.broadcasted_iota(jnp.int32, (1, GW), 1) // HEAD_DIM
    same_head = (lax.broadcasted_iota(jnp.int32, (GW, GW), 0) // HEAD_DIM
                 == lax.broadcasted_iota(jnp.int32, (GW, GW), 1) // HEAD_DIM)
    causal = (lax.broadcasted_iota(jnp.int32, (BLOCK, BLOCK), 1)
              <= lax.broadcasted_iota(jnp.int32, (BLOCK, BLOCK), 0))

    ones_blk = jnp.ones((BLOCK, LANES), BF16)
    head_sum = (lax.broadcasted_iota(jnp.int32, (HEADS * BLOCK, GW), 0) // BLOCK
                == lax.broadcasted_iota(jnp.int32, (HEADS * BLOCK, GW), 1) // HEAD_DIM).astype(BF16)

    def chunk(c, m_run):
        r0 = pl.multiple_of(c * BLOCK, BLOCK)
        qb = q_s[pl.ds(r0, BLOCK), :]
        kb = k_s[pl.ds(r0, BLOCK), :]
        v = zd_ref[pl.ds(r0, BLOCK), 2 * GW:3 * GW]
        og = zd_ref[pl.ds(r0, BLOCK), 3 * GW:4 * GW].astype(F32)
        a_rows = a_r[:, pl.ds(r0, BLOCK)]
        b_rows = b_r[:, pl.ds(r0, BLOCK)]
        cm_cols = cm_c[pl.ds(r0, BLOCK), :]

        k_heads = jnp.concatenate(
            [jnp.where(lane_head == h, kb, jnp.zeros_like(kb)) for h in heads], axis=0)
        v_heads = jnp.concatenate(
            [jnp.where(lane_head == h, v, jnp.zeros_like(v)) for h in heads], axis=0)
        scores = _dot_nt(qb, k_heads)
        carried = _dot(qb, jnp.concatenate([c_s[...], n_s[...]], axis=1).astype(BF16))

        w_intra, g_rep, wk_rows, decays, m_next = [], [], [], [], []
        for h in heads:
            a_row = a_rows[h:h + 1, :]
            g = jnp.maximum(jnp.broadcast_to(cm_cols[:, h:h + 1], (BLOCK, BLOCK)), m_run[h])
            g_rep.append(g)
            w_intra.append(jnp.where(causal, jnp.exp(a_row - g), 0.0))
            g_end = jnp.maximum(m_run[h], jnp.max(a_row, axis=-1, keepdims=True))
            m_next.append(b_rows[h:h + 1, BLOCK - 1:BLOCK] + g_end)
            decays.append(jnp.exp(m_run[h] - g_end))
            wk_rows.append(jnp.exp(a_row - g_end))
        g_full = head_lanes(g_rep)
        inter_scale = jnp.exp(head_lanes(m_run) - g_full)
        floor = jnp.exp(-(b_f[pl.ds(r0, BLOCK), :] + g_full))

        wk = jnp.concatenate([jnp.broadcast_to(w, (HEAD_DIM, BLOCK)) for w in wk_rows], axis=0)
        decay = jnp.concatenate([jnp.broadcast_to(dd, (HEAD_DIM, 1)) for dd in decays], axis=0)
        ktw = kt_s[c] * wk
        update = _dot(ktw.astype(BF16), jnp.concatenate([v, ones_blk], axis=1))

        sqk = (scores * jnp.concatenate(w_intra, axis=1)).astype(BF16)
        intra = _dot(sqk, jnp.concatenate([v_heads, head_sum], axis=1))
        num = intra[:, :GW] + inter_scale * carried[:, :GW]
        den = intra[:, GW:] + inter_scale * carried[:, GW:]
        h_out = num / jnp.maximum(jnp.abs(den), floor)
        y_ref[pl.ds(r0, BLOCK), :] = (jax.nn.sigmoid(og) * h_out).astype(BF16)

        k_sum = update[:, GW:]
        c_s[...] = decay * c_s[...] + jnp.where(same_head, update[:, :GW], 0.0)
        n_s[...] = decay * n_s[...] + jnp.where(same_head, jnp.concatenate([k_sum, k_sum], axis=1), 0.0)
        return tuple(m_next)

    lax.fori_loop(0, seq // BLOCK, chunk, tuple(jnp.zeros((1, 1), F32) for _ in range(HEADS)), unroll=8)


def _mlstm(zd, zm, gb, cw, cb, bsz, seq):
    full = lambda a: pl.BlockSpec(a.shape, lambda b: (0,) * a.ndim)
    return pl.pallas_call(
        _mlstm_kernel,
        grid=(bsz,),
        in_specs=[pl.BlockSpec((seq, ZD_W), lambda b: (b, 0)),
                  pl.BlockSpec((seq, LANES), lambda b: (b, 0)), full(gb), full(cw), full(cb)],
        out_specs=pl.BlockSpec((seq, GW), lambda b: (b, 0)),
        out_shape=jax.ShapeDtypeStruct((bsz * seq, GW), BF16),
        scratch_shapes=[pltpu.VMEM((seq + CONV_PAD, 2 * GW), F32),
                        pltpu.VMEM((seq, GW), BF16), pltpu.VMEM((seq, GW), BF16),
                        pltpu.VMEM((seq // BLOCK, GW, BLOCK), F32),
                        pltpu.VMEM((8, seq), F32), pltpu.VMEM((8, seq), F32),
                        pltpu.VMEM((seq, LANES), F32), pltpu.VMEM((seq, LANES), F32),
                        pltpu.VMEM((seq, GW), F32),
                        pltpu.VMEM((GW, GW), F32), pltpu.VMEM((GW, GW), F32)],
        compiler_params=_params(),
        name="mlstm",
    )(zd, zm, gb, cw, cb)


FF_CHUNK = 256


def _dense_kernel(with_next, ya_ref, yb_ref, yd_ref, oc1_ref, oc4_ref, oc16_ref, lc1_ref, lc4_ref, lc16_ref,
                  x_ref, wo_ref, g1_ref, b1_ref, wg_ref, wu_ref, wd_ref, g2_ref, b2_ref, *rest):
    if with_next:
        (w_ref, wm_ref, cos_ref, sin_ref, o_ref, zd_ref, za_ref, zc1_ref, zc4_ref, zc16_ref, zb_ref, zm_ref,
         xb_ref, acc_ref, nat_ref, stage_ref) = rest
    else:
        o_ref, xb_ref, acc_ref, nat_ref = rest
    tm = x_ref.shape[0]

    def natural(view, d, slot):
        if d == 1:
            return view[...].astype(F32)
        for r in range(d):
            for half in range(GW // LANES):
                col = r * GW + half * LANES
                nat_ref[slot, half, pl.ds(r, tm // d, stride=d), :] = view[:, col:col + LANES].astype(F32)
        return jnp.concatenate([nat_ref[slot, half] for half in range(GW // LANES)], axis=1)

    views = (oc1_ref, oc4_ref, oc16_ref, lc1_ref, lc4_ref, lc16_ref)
    slots = iter(range(nat_ref.shape[0]))
    nat = [natural(v, d, next(slots) if d > 1 else None) for v, d in zip(views, DILATIONS * 2)]
    outs, lse = nat[:3], nat[3:]
    top = jnp.maximum(jnp.maximum(lse[0], lse[1]), lse[2])
    wts = [jnp.exp2(l - top) for l in lse]
    mix = wts[0] * outs[0] + wts[1] * outs[1] + wts[2] * outs[2]
    yc = (mix / (wts[0] + wts[1] + wts[2])).astype(BF16)
    acc = ALPHA * x_ref[...]
    for i, y in ((0, ya_ref[...]), (1, yb_ref[...]), (3, yd_ref[...]), (2, yc)):
        acc = acc + _dot(y, wo_ref[i * GW:(i + 1) * GW, :])
    x1 = _layer_norm(acc, g1_ref[...], b1_ref[...])
    xb_ref[...] = x1.astype(BF16)
    acc_ref[...] = ALPHA * x1
    for c in range(0, D_FF, FF_CHUNK):
        gate = _dot(xb_ref[...], wg_ref[:, c:c + FF_CHUNK])
        up = _dot(xb_ref[...], wu_ref[:, c:c + FF_CHUNK])
        act = (gate * jax.nn.sigmoid(gate) * up).astype(BF16)
        acc_ref[...] += _dot(act, wd_ref[c:c + FF_CHUNK, :])
    x2 = _layer_norm(acc_ref[...], g2_ref[...], b2_ref[...])
    o_ref[...] = x2
    if with_next:
        xb_ref[...] = x2.astype(BF16)
        _project_in(xb_ref, w_ref, wm_ref, cos_ref, sin_ref, zd_ref, za_ref, (zc1_ref, zc4_ref, zc16_ref),
                    zb_ref, zm_ref, stage_ref, 0, tm)


def _dense(ys, dil, x2, wo, g1, b1, wg, wu, wd, g2, b2, next_w, tables, tm):
    n = x2.shape[0]
    row = lambda w_: pl.BlockSpec((tm, w_), lambda i: (i, 0))
    view = lambda d: pl.BlockSpec((tm // d, d * GW), lambda i: (i, 0))
    once = lambda a: pl.BlockSpec(a.shape, lambda i: (0, 0), pipeline_mode=pl.Buffered(1))
    weights = [wo, g1, b1, wg, wu, wd, g2, b2] + list(next_w)
    tables = list(tables) if next_w else []
    out_specs = [row(D_MODEL)]
    out_shape = [jax.ShapeDtypeStruct((n, D_MODEL), F32)]
    scratch = [pltpu.VMEM((tm, D_MODEL), BF16), pltpu.VMEM((tm, D_MODEL), F32),
               pltpu.VMEM((2 * (len(DILATIONS) - 1), GW // LANES, tm, LANES), F32)]
    if next_w:
        z_specs, z_shapes, stage = _z_outputs(n, tm)
        out_specs += z_specs
        out_shape += z_shapes
        scratch.append(stage)
    return pl.pallas_call(
        functools.partial(_dense_kernel, bool(next_w)),
        grid=(n // tm,),
        in_specs=([row(GW)] * len(ys) + [view(d) for d in DILATIONS] * 2 + [row(D_MODEL)]
                  + [once(w) for w in weights] + [row(LANES)] * len(tables)),
        out_specs=out_specs,
        out_shape=out_shape,
        scratch_shapes=scratch,
        compiler_params=_params(),
        name="dense_block",
    )(*ys, *dil, x2, *weights, *tables)


def _pack_weights(w_in, b_w_uq, b_w_ukv, b_q_norm, a_bs, d_igate_b, d_fgate_b):
    nl = w_in.shape[0]
    o_b = ZA_W
    o_c = o_b + Q_RANK + KV_RANK + ROPE
    o_d = o_c + ZC_W
    o_g = o_d + ZD_W
    w_a = w_in[..., :o_b]
    w_cq = w_in[..., o_b:o_b + Q_RANK]
    w_ckv = w_in[..., o_b + Q_RANK:o_b + Q_RANK + KV_RANK]
    w_kr = w_in[..., o_b + Q_RANK + KV_RANK:o_c]
    w_c = w_in[..., o_c:o_d]
    w_d = w_in[..., o_d:o_g]
    w_gates = w_in[..., o_g:]
    zeros = lambda *s: jnp.zeros((nl,) + s, F32)
    w_b = jnp.concatenate([w_cq, zeros(D_MODEL, Q_PAD - Q_RANK), w_ckv], -1)
    w_main = jnp.concatenate([w_d, w_a, w_c, w_b], -1).astype(BF16)
    half = ROPE // 2
    w_kr_rot = jnp.concatenate([-w_kr[..., half:], w_kr[..., :half]], -1)
    w_misc = jnp.concatenate(
        [w_gates, zeros(D_MODEL, KR_LANE - 2 * HEADS), w_kr, w_kr_rot], -1).astype(BF16)

    wq = b_w_uq.reshape(nl, Q_RANK, HEADS, NOPE + ROPE)
    nope, x1, x2 = wq[..., :NOPE], wq[..., NOPE:NOPE + half], wq[..., NOPE + half:]
    tail = zeros(Q_RANK, HEADS, LANES - NOPE - ROPE)
    rowpad = ((0, 0), (0, Q_PAD - Q_RANK), (0, 0))
    wqm = jnp.pad(jnp.concatenate([nope, x1, x2, tail], -1).reshape(nl, Q_RANK, HEADS * LANES), rowpad)
    wqs = jnp.pad(jnp.concatenate([zeros(Q_RANK, HEADS, NOPE), -x2, x1, tail], -1)
                  .reshape(nl, Q_RANK, HEADS * LANES), rowpad)
    wkv = b_w_ukv.reshape(nl, KV_RANK, HEADS, NOPE + HEAD_DIM)
    wk = jnp.concatenate([wkv[..., :NOPE], zeros(KV_RANK, HEADS, LANES - NOPE)], -1)
    wk = wk.reshape(nl, KV_RANK, HEADS * LANES)
    wv = wkv[..., NOPE:].reshape(nl, KV_RANK, GW)
    qg = jnp.pad(b_q_norm, ((0, 0), (0, Q_PAD - Q_RANK)))[:, None, :]
    bias = jnp.repeat(jnp.swapaxes(a_bs, 1, 2), HEAD_DIM, axis=-1)
    gb = jnp.concatenate([d_igate_b, d_fgate_b, zeros(LANES - 2 * HEADS)], -1)[:, None, :]
    return dict(w_main=w_main, w_misc=w_misc, wqm=wqm.astype(BF16), wqs=wqs.astype(BF16),
                wk=wk.astype(BF16), wv=wv.astype(BF16), qg=qg, bias=bias, gb=gb)


@jax.jit
def _forward(x, positions, w_in, a_ln_g, a_ln_b, a_ws, a_bs, b_q_norm, b_kv_norm, b_w_uq, b_w_ukv,
             d_conv_w, d_conv_b, d_igate_b, d_fgate_b, w_out, ln1_g, ln1_b, w_gate, w_up, w_down,
             ln2_g, ln2_b):
    bsz, seq, _ = x.shape
    pk = _pack_weights(w_in, b_w_uq, b_w_ukv, b_q_norm, a_bs, d_igate_b, d_fgate_b)
    w_out_b, w_gate_b, w_up_b, w_down_b = (w.astype(BF16) for w in (w_out, w_gate, w_up, w_down))
    tm = 512
    x2 = x.reshape(bsz * seq, D_MODEL)
    row = lambda a, l: a[l][None, :]
    zd, za, *zc, zb, zm, cosd, sind, cosm, sinm = _inproj(x2, pk["w_main"][0], pk["w_misc"][0], positions, tm)
    rope_d = (cosd, sind)
    cosm, sinm = (t.reshape(bsz, seq, LANES) for t in (cosm, sinm))
    for l in range(DEPTH):
        ya = _gmlp(za, row(a_ln_g, l), row(a_ln_b, l), a_ws[l], pk["bias"][l], bsz, seq)
        yb = _mla(zb, zm, cosm, sinm, pk["qg"][l], row(b_kv_norm, l), pk["wqm"][l], pk["wqs"][l],
                  pk["wk"][l], pk["wv"][l], bsz, seq)
        dil = _dilated(zc, bsz, seq)
        yd = _mlstm(zd, zm, pk["gb"][l], d_conv_w[l], row(d_conv_b, l), bsz, seq)
        next_w = (pk["w_main"][l + 1], pk["w_misc"][l + 1]) if l + 1 < DEPTH else ()
        outs = _dense((ya, yb, yd), dil, x2, w_out_b[l], row(ln1_g, l), row(ln1_b, l), w_gate_b[l],
                      w_up_b[l], w_down_b[l], row(ln2_g, l), row(ln2_b, l), next_w, rope_d, tm)
        if next_w:
            x2, zd, za, *zc, zb, zm = outs
        else:
            x2 = outs[0]
    return x2.reshape(bsz, seq, D_MODEL)


def kernel(x, positions, w_in, a_ln_g, a_ln_b, a_ws, a_bs, b_q_norm, b_kv_norm, b_w_uq, b_w_ukv,
           d_conv_w, d_conv_b, d_igate_b, d_fgate_b, w_out, ln1_g, ln1_b, w_gate, w_up, w_down,
           ln2_g, ln2_b):
    return _forward(x, positions, w_in, a_ln_g, a_ln_b, a_ws, a_bs, b_q_norm, b_kv_norm, b_w_uq,
                    b_w_ukv, d_conv_w, d_conv_b, d_igate_b, d_fgate_b, w_out, ln1_g, ln1_b, w_gate,
                    w_up, w_down, ln2_g, ln2_b)
```

```python
import functools

import jax
import jax.numpy as jnp
from jax import lax
from jax.experimental import pallas as pl
from jax.experimental.pallas import tpu as pltpu

F32 = jnp.float32
BF16 = jnp.bfloat16

D_MODEL = 1024
DEPTH = 4
HEAD_DIM = 64
HEADS = 4
GW = HEADS * HEAD_DIM
BLOCK = 128
Q_RANK = 192
KV_RANK = 128
NOPE = 64
ROPE = 32
D_FF = 2816
ROPE_THETA = 10000.0
LN_EPS = 1e-5
RMS_EPS = 1e-6
ALPHA = (2 * DEPTH) ** 0.25
LANES = 128
NEG_INF = float("-inf")
LOG2_E = 1.4426950408889634

Q_PAD = 256
ZD_W, ZA_W, ZC_W, ZB_W = 4 * GW, 2 * GW, 3 * GW, Q_PAD + KV_RANK
ZMAIN_W = ZD_W + ZA_W + ZC_W + ZB_W
KR_LANE = 64

VMEM_LIMIT = 60000 * 1024


def _dot(a, b):
    return jnp.dot(a, b, preferred_element_type=F32)


def _dot_nt(a, b):
    return lax.dot_general(a, b, (((1,), (1,)), ((), ())), preferred_element_type=F32)


def _params(n_axes=1):
    return pltpu.CompilerParams(
        dimension_semantics=("arbitrary",) * n_axes, vmem_limit_bytes=VMEM_LIMIT)


def _layer_norm(r, g, b):
    mu = jnp.mean(r, axis=-1, keepdims=True)
    d = r - mu
    var = jnp.mean(d * d, axis=-1, keepdims=True)
    return d * lax.rsqrt(var + LN_EPS) * g + b


def _rope_table_values(pos, inv):
    lane = lax.broadcasted_iota(jnp.int32, (1, LANES), 1)
    ang = pos * inv
    cos, sin = jnp.cos(ang), jnp.sin(ang)
    low = lane < HEAD_DIM
    cosd = jnp.where(low, cos, pltpu.roll(cos, HEAD_DIM, 1))
    sign = jnp.where((lane % HEAD_DIM) < HEAD_DIM // 2, -1.0, 1.0)
    sind = jnp.where(low, sin, pltpu.roll(sin, HEAD_DIM, 1)) * sign
    in_rope = (lane >= KR_LANE) & (lane < KR_LANE + ROPE)
    cosm = jnp.where(in_rope, cos, jnp.where(lane < KR_LANE, 1.0, 0.0))
    sinm = jnp.where(in_rope, sin, 0.0)
    return cosd, sind, cosm, sinm


def _rope_frequencies():
    lane = jnp.arange(LANES)
    half_d = HEAD_DIM // 2
    inv_d = jnp.power(ROPE_THETA, -jnp.arange(half_d, dtype=F32) / half_d)
    half_m = ROPE // 2
    inv_m = jnp.power(ROPE_THETA, -jnp.arange(half_m, dtype=F32) / half_m)
    return jnp.where(lane < HEAD_DIM, inv_d[lane % half_d],
                     jnp.where(lane < KR_LANE + ROPE, inv_m[lane % half_m], 0.0))[None, :]


def _rope_heads(x, cos, sin):
    first = (lax.broadcasted_iota(jnp.int32, (1, LANES), 1) % HEAD_DIM) < HEAD_DIM // 2
    halves = []
    for half in range(GW // LANES):
        xh = x[:, half * LANES:(half + 1) * LANES]
        rot = jnp.where(first, pltpu.roll(xh, LANES - HEAD_DIM // 2, 1), pltpu.roll(xh, HEAD_DIM // 2, 1))
        halves.append(xh * cos + rot * sin)
    return jnp.concatenate(halves, axis=1)


def _project_in(xb_ref, w_ref, wm_ref, cos_ref, sin_ref, zd_ref, za_ref, zc_refs, zb_ref, zm_ref, stage_ref,
                r0, nrows):
    rs = slice(r0, r0 + nrows)
    groups = {id(zd_ref): 0, id(za_ref): ZD_W, id(zc_refs): ZD_W + ZA_W, id(zb_ref): ZD_W + ZA_W + ZC_W}
    for ref in (zc_refs, zd_ref, za_ref, zb_ref):
        off = groups[id(ref)]
        width = ZC_W if ref is zc_refs else ref.shape[1]
        for c in range(0, width, GW):
            cw = min(GW, width - c)
            z = _dot(xb_ref[rs, :], w_ref[:, off + c:off + c + cw])
            if ref is not zc_refs:
                ref[rs, c:c + cw] = z.astype(BF16)
                continue
            if c < 2 * GW:
                z = _rope_heads(z, cos_ref[rs, :], sin_ref[rs, :])
                if c == 0:
                    z = z * (HEAD_DIM ** -0.5 * LOG2_E)
            zc_refs[0][rs, c:c + cw] = z.astype(BF16)
            for half in range(GW // LANES):
                stage_ref[c // GW, half, rs, :] = z[:, half * LANES:(half + 1) * LANES]
            for d, view in zip(DILATIONS[1:], zc_refs[1:]):
                for r in range(d):
                    for half in range(GW // LANES):
                        col = r * ZC_W + c + half * LANES
                        rows = stage_ref[c // GW, half, pl.ds(r0 + r, nrows // d, stride=d), :]
                        view[r0 // d:(r0 + nrows) // d, col:col + LANES] = rows.astype(BF16)
    zm_ref[rs, :] = _dot(xb_ref[rs, :], wm_ref[...])


def _inproj_kernel(x_ref, w_ref, wm_ref, pos_ref, inv_ref, zd_ref, za_ref, zc1_ref, zc4_ref, zc16_ref, zb_ref,
                   zm_ref, cosd_ref, sind_ref, cosm_ref, sinm_ref, xb_ref, stage_ref):
    tables = _rope_table_values(pos_ref[...].astype(F32), inv_ref[...])
    for ref, tab in zip((cosd_ref, sind_ref, cosm_ref, sinm_ref), tables):
        ref[...] = tab
    xb_ref[...] = x_ref[...].astype(BF16)
    _project_in(xb_ref, w_ref, wm_ref, cosd_ref, sind_ref, zd_ref, za_ref, (zc1_ref, zc4_ref, zc16_ref),
                zb_ref, zm_ref, stage_ref, 0, x_ref.shape[0])


def _z_outputs(n, tm):
    row = lambda w: pl.BlockSpec((tm, w), lambda i: (i, 0))
    view = lambda d: pl.BlockSpec((tm // d, d * ZC_W), lambda i: (i, 0))
    specs = [row(ZD_W), row(ZA_W)] + [view(d) for d in DILATIONS] + [row(ZB_W), row(LANES)]
    shapes = ([jax.ShapeDtypeStruct((n, ZD_W), BF16), jax.ShapeDtypeStruct((n, ZA_W), BF16)]
              + [jax.ShapeDtypeStruct((n // d, d * ZC_W), BF16) for d in DILATIONS]
              + [jax.ShapeDtypeStruct((n, ZB_W), BF16), jax.ShapeDtypeStruct((n, LANES), F32)])
    stage = pltpu.VMEM((ZC_W // GW, GW // LANES, tm, LANES), F32)
    return specs, shapes, stage


def _inproj(x2, w_main, w_misc, positions, tm):
    n = x2.shape[0]
    row = lambda w: pl.BlockSpec((tm, w), lambda i: (i, 0))
    full = lambda a: pl.BlockSpec(a.shape, lambda i: (0, 0))
    z_specs, z_shapes, stage = _z_outputs(n, tm)
    inv = _rope_frequencies()
    return pl.pallas_call(
        _inproj_kernel,
        grid=(n // tm,),
        in_specs=[row(D_MODEL), full(w_main), full(w_misc), row(1), full(inv)],
        out_specs=z_specs + [row(LANES)] * 4,
        out_shape=z_shapes + [jax.ShapeDtypeStruct((n, LANES), F32)] * 4,
        scratch_shapes=[pltpu.VMEM((tm, D_MODEL), BF16), stage],
        compiler_params=_params(),
        name="inproj",
    )(x2, w_main, w_misc, positions.reshape(n, 1), inv)


def _gmlp_kernel(za_ref, lng_ref, lnb_ref, ws_ref, bias_ref, y_ref):
    seq = za_ref.shape[0]
    r_i = lax.broadcasted_iota(jnp.int32, (BLOCK, BLOCK), 0)
    c_i = lax.broadcasted_iota(jnp.int32, (BLOCK, BLOCK), 1)
    w_causal = [jnp.where(c_i <= r_i, ws_ref[h], 0.0).astype(BF16) for h in range(HEADS)]
    lane_head = lax.broadcasted_iota(jnp.int32, (BLOCK, GW), 1) // HEAD_DIM

    def chunk(c, carry):
        r0 = pl.multiple_of(c * BLOCK, BLOCK)
        z = za_ref[pl.ds(r0, BLOCK), :].astype(F32)
        g = 0.5 * z * (1.0 + lax.erf(z * (0.5 ** 0.5)))
        u, v = g[:, :GW], g[:, GW:]
        vb = _layer_norm(v, lng_ref[...], lnb_ref[...]).astype(BF16)
        mixed = jnp.zeros((BLOCK, GW), F32)
        for h in range(HEADS):
            mixed = jnp.where(lane_head == h, _dot(w_causal[h], vb), mixed)
        y_ref[pl.ds(r0, BLOCK), :] = (u * (mixed + bias_ref[...])).astype(BF16)
        return carry

    lax.fori_loop(0, seq // BLOCK, chunk, 0, unroll=8)


def _gmlp(za, ln_g, ln_b, ws, bias, bsz, seq):
    full = lambda a: pl.BlockSpec(a.shape, lambda b: (0,) * a.ndim)
    return pl.pallas_call(
        _gmlp_kernel,
        grid=(bsz,),
        in_specs=[pl.BlockSpec((seq, ZA_W), lambda b: (b, 0)), full(ln_g), full(ln_b), full(ws),
                  full(bias)],
        out_specs=pl.BlockSpec((seq, GW), lambda b: (b, 0)),
        out_shape=jax.ShapeDtypeStruct((bsz * seq, GW), BF16),
        compiler_params=_params(),
        name="gmlp",
    )(za, ln_g, ln_b, ws, bias)


MLA_TQ = 256


def _mla_kernel(zb_ref, zm_ref, cos_ref, sin_ref, qg_ref, kvg_ref, wqm_ref, wqs_ref, wk_ref, wv_ref,
                y_ref, q_s, k_s, vt_s, acc_s, sc_a, sc_b, p_a, p_b):
    seq = zb_ref.shape[0]
    tq = MLA_TQ
    scale = (NOPE + ROPE) ** -0.5 * LOG2_E
    lane = lax.broadcasted_iota(jnp.int32, (1, LANES), 1)
    in_rope = (lane >= KR_LANE) & (lane < KR_LANE + ROPE)
    rows = 512
    for r in range(0, seq, rows):
        cos = cos_ref[0, r:r + rows, :]
        sin = sin_ref[0, r:r + rows, :]
        cq = zb_ref[r:r + rows, 0:Q_PAD].astype(F32)
        ms = jnp.sum(cq * cq, axis=-1, keepdims=True) * (1.0 / Q_RANK)
        cqn = (cq * lax.rsqrt(ms + RMS_EPS) * qg_ref[...]).astype(BF16)
        qm = _dot(cqn, wqm_ref[...])
        qs = _dot(cqn, wqs_ref[...])
        ckv = zb_ref[r:r + rows, Q_PAD:ZB_W].astype(F32)
        ms = jnp.mean(ckv * ckv, axis=-1, keepdims=True)
        ckvn = (ckv * lax.rsqrt(ms + RMS_EPS) * kvg_ref[...]).astype(BF16)
        kn = _dot(ckvn, wk_ref[...])
        v = _dot(ckvn, wv_ref[...])
        for sub in range(rows // tq):
            vt_s[r // tq + sub] = v[sub * tq:(sub + 1) * tq, :].T.astype(BF16)
        zm = zm_ref[r:r + rows, :]
        kr = jnp.where(in_rope, zm * cos + pltpu.roll(zm, LANES - ROPE, 1) * sin, 0.0)
        for h in range(HEADS):
            sl = slice(h * LANES, (h + 1) * LANES)
            q_h = (qm[:, sl] * cos + qs[:, sl] * sin) * scale
            for sub in range(rows // tq):
                q_s[r // tq + sub, sl, :] = q_h[sub * tq:(sub + 1) * tq, :].T.astype(BF16)
            k_s[r:r + rows, sl] = (kn[:, sl] + kr).astype(BF16)

    key_i = lax.broadcasted_iota(jnp.int32, (tq, tq), 0)
    qry_i = lax.broadcasted_iota(jnp.int32, (tq, tq), 1)
    heads = range(HEADS)
    head_lanes = [slice(h * LANES, (h + 1) * LANES) for h in heads]

    def put_scores(qb, j, dst):
        k0 = pl.multiple_of(j * tq, tq)
        for h in heads:
            dst[h] = _dot(k_s[pl.ds(k0, tq), head_lanes[h]], q_s[qb, head_lanes[h], :])

    def qblock(i, shift):
        q0 = pl.multiple_of(i * tq, tq)
        acc_s[...] = jnp.zeros(acc_s.shape, F32)

        def softmax(src, p_dst, m_old, l_old, masked):
            m_new, l_new, alpha = [], [], []
            for h in heads:
                s = src[h]
                if masked:
                    s = jnp.where(key_i <= qry_i, s, NEG_INF)
                m = jnp.maximum(m_old[h], jnp.max(s, axis=0, keepdims=True))
                a = jnp.exp2(m_old[h] - m)
                p = jnp.exp2(s - m)
                m_new.append(m)
                alpha.append(a)
                l_new.append(a * l_old[h] + jnp.sum(p, axis=0, keepdims=True))
                p_dst[h] = p.astype(BF16)
            return tuple(m_new), tuple(l_new), tuple(alpha)

        def add_values(j, p_src, alpha):
            for h in heads:
                vt = vt_s[j, h * HEAD_DIM:(h + 1) * HEAD_DIM, :]
                acc_s[h] = alpha[h] * acc_s[h] + _dot(vt, p_src[h])

        even, odd = (sc_a, p_a), (sc_b, p_b)

        def on_parity(j, fn, state):
            return lax.cond((j + shift) % 2 == 0, lambda st: fn(even, odd, st), lambda st: fn(odd, even, st),
                            state)

        def step(j, state):
            def run(cur, other, st):
                m_old, l_old, alpha_prev = st
                put_scores(i, j + 1, other[0])
                m_new, l_new, alpha = softmax(cur[0], cur[1], m_old, l_old, False)
                add_values(jnp.maximum(j - 1, 0), other[1], alpha_prev)
                return m_new, l_new, alpha
            return on_parity(j, run, state)

        def finish(cur, other, st):
            m_old, l_old, alpha_prev = st
            put_scores(jnp.minimum(i + 1, n_qblocks - 1), 0, other[0])
            _, l_new, alpha = softmax(cur[0], cur[1], m_old, l_old, True)
            add_values(jnp.maximum(i - 1, 0), other[1], alpha_prev)
            add_values(i, cur[1], alpha)
            return l_new

        p_a[...] = jnp.zeros(p_a.shape, BF16)
        p_b[...] = jnp.zeros(p_b.shape, BF16)
        init = (tuple(jnp.full((1, tq), NEG_INF, F32) for _ in heads),
                tuple(jnp.zeros((1, tq), F32) for _ in heads),
                tuple(jnp.ones((1, tq), F32) for _ in heads))
        state = lax.fori_loop(0, i, step, init)
        l_fin = on_parity(i, finish, state)
        out_t = jnp.concatenate([acc_s[h] / l_fin[h] for h in heads], axis=0)
        y_ref[pl.ds(q0, tq), :] = out_t.T.astype(BF16)
        return (i + shift + 1) % 2

    n_qblocks = seq // tq
    put_scores(0, 0, sc_a)
    lax.fori_loop(0, n_qblocks, qblock, jnp.int32(0))


def _mla(zb, zm, cosm, sinm, qg, kvg, wqm, wqs, wk, wv, bsz, seq):
    full = lambda a: pl.BlockSpec(a.shape, lambda b: (0,) * a.ndim)
    tab = pl.BlockSpec((1, seq, LANES), lambda b: (b, 0, 0))
    return pl.pallas_call(
        _mla_kernel,
        grid=(bsz,),
        in_specs=[pl.BlockSpec((seq, ZB_W), lambda b: (b, 0)),
                  pl.BlockSpec((seq, LANES), lambda b: (b, 0)), tab, tab,
                  full(qg), full(kvg), full(wqm), full(wqs), full(wk), full(wv)],
        out_specs=pl.BlockSpec((seq, GW), lambda b: (b, 0)),
        out_shape=jax.ShapeDtypeStruct((bsz * seq, GW), BF16),
        scratch_shapes=[pltpu.VMEM((seq // MLA_TQ, HEADS * LANES, MLA_TQ), BF16),
                        pltpu.VMEM((seq, HEADS * LANES), BF16),
                        pltpu.VMEM((seq // MLA_TQ, GW, MLA_TQ), BF16),
                        pltpu.VMEM((HEADS, HEAD_DIM, MLA_TQ), F32),
                        pltpu.VMEM((HEADS, MLA_TQ, MLA_TQ), F32),
                        pltpu.VMEM((HEADS, MLA_TQ, MLA_TQ), F32),
                        pltpu.VMEM((HEADS, MLA_TQ, MLA_TQ), BF16),
                        pltpu.VMEM((HEADS, MLA_TQ, MLA_TQ), BF16)],
        compiler_params=_params(),
        name="mla",
    )(zb, zm, cosm, sinm, qg, kvg, wqm, wqs, wk, wv)


DILATIONS = (1, 4, 16)
HALVES = GW // LANES


def _head_columns(cols):
    low_half = lax.broadcasted_iota(jnp.int32, (1, LANES), 1) < HEAD_DIM
    return jnp.concatenate([jnp.where(low_half, cols[0], cols[1]),
                            jnp.where(low_half, cols[2], cols[3])], axis=1)


def _band_block(q, kh, vh, w0, nk, mask):
    pairs = range(HEADS // 2)
    scores = []
    for pr in pairs:
        k_win = jnp.concatenate([kh[2 * pr + e, pl.ds(w0, nk), :] for e in range(2)], axis=0)
        scores.append(_dot_nt(q[:, pr * LANES:(pr + 1) * LANES], k_win))
    m_cols, l_cols, probs = [], [], []
    for h in range(HEADS):
        s = scores[h // 2][:, (h % 2) * nk:(h % 2 + 1) * nk]
        s = jnp.where(mask, s, NEG_INF)
        m = jnp.max(s, axis=-1, keepdims=True)
        p = jnp.exp2(s - m)
        m_cols.append(m)
        l_cols.append(jnp.sum(p, axis=-1, keepdims=True))
        probs.append(p.astype(BF16))
    outs = []
    for pr in pairs:
        v_win = jnp.concatenate([vh[2 * pr + e, pl.ds(w0, nk), :] for e in range(2)], axis=0)
        outs.append(_dot(jnp.concatenate(probs[2 * pr:2 * pr + 2], axis=1), v_win))
    l_full = _head_columns(l_cols)
    o = jnp.concatenate(outs, axis=1) / l_full
    return o, _head_columns(m_cols) + jnp.log2(l_full)


def _dilated_kernel(z1_ref, z4_ref, z16_ref, o1_ref, o4_ref, o16_ref, l1_ref, l4_ref, l16_ref, qb, kh, vh):
    seq = z1_ref.shape[0]
    lane = lax.broadcasted_iota(jnp.int32, (1, LANES), 1)
    views = {1: (z1_ref, o1_ref, l1_ref), 4: (z4_ref, o4_ref, l4_ref), 16: (z16_ref, o16_ref, l16_ref)}

    r_i = lax.broadcasted_iota(jnp.int32, (BLOCK, 2 * BLOCK), 0)
    c_i = lax.broadcasted_iota(jnp.int32, (BLOCK, 2 * BLOCK), 1)
    band = (c_i >= r_i) & (c_i <= r_i + BLOCK)
    causal = (lax.broadcasted_iota(jnp.int32, (BLOCK, BLOCK), 1)
              <= lax.broadcasted_iota(jnp.int32, (BLOCK, BLOCK), 0))
    low_half = lane < HEAD_DIM
    for h in range(HEADS):
        kh[h, 0:BLOCK, :] = jnp.zeros((BLOCK, LANES), BF16)
        vh[h, 0:BLOCK, :] = jnp.zeros((BLOCK, LANES), BF16)

    for d in DILATIONS:
        sub = seq // d
        z_ref, o_ref, l_ref = views[d]
        for res in range(d):
            dst = slice(BLOCK + res * sub, BLOCK + (res + 1) * sub)
            qb[res * sub:(res + 1) * sub, :] = z_ref[:, res * ZC_W:res * ZC_W + GW]
            for half in range(HALVES):
                col = lambda base: slice(res * ZC_W + base + half * LANES,
                                         res * ZC_W + base + (half + 1) * LANES)
                k_half = z_ref[:, col(GW)]
                v_half = z_ref[:, col(2 * GW)]
                zero = jnp.zeros_like(k_half)
                kh[2 * half, dst, :] = jnp.where(low_half, k_half, zero)
                kh[2 * half + 1, dst, :] = jnp.where(low_half, zero, k_half)
                vh[2 * half, dst, :] = jnp.where(low_half, v_half, zero)
                vh[2 * half + 1, dst, :] = jnp.where(low_half, zero, v_half)
        blocks_per_class = sub // BLOCK

        for res in range(d):
            cols = slice(res * GW, (res + 1) * GW)

            def block(n, carry, res=res, cols=cols, o_ref=o_ref, l_ref=l_ref):
                r0 = pl.multiple_of(res * sub + n * BLOCK, BLOCK)
                q = qb[pl.ds(r0, BLOCK), :]
                if blocks_per_class == 1:
                    o, lse = _band_block(q, kh, vh, r0 + BLOCK, BLOCK, causal)
                else:
                    first_key = jnp.where(n > 0, 0, BLOCK)
                    o, lse = _band_block(q, kh, vh, r0, 2 * BLOCK, band & (c_i >= first_key))
                rows = pl.ds(pl.multiple_of(n * BLOCK, BLOCK), BLOCK)
                o_ref[rows, cols] = o.astype(BF16)
                l_ref[rows, cols] = lse
                return carry

            if blocks_per_class == 1:
                block(0, 0)
            else:
                lax.fori_loop(0, blocks_per_class, block, 0, unroll=True)


def _dilated(zc_views, bsz, seq):
    headbuf = pltpu.VMEM((HEADS, seq + BLOCK, LANES), BF16)
    n = bsz * seq
    view = lambda width, d: pl.BlockSpec((seq // d, d * width), lambda b: (b, 0))
    return pl.pallas_call(
        _dilated_kernel,
        grid=(bsz,),
        in_specs=[view(ZC_W, d) for d in DILATIONS],
        out_specs=[view(GW, d) for d in DILATIONS] * 2,
        out_shape=([jax.ShapeDtypeStruct((n // d, d * GW), BF16) for d in DILATIONS]
                   + [jax.ShapeDtypeStruct((n // d, d * GW), F32) for d in DILATIONS]),
        scratch_shapes=[pltpu.VMEM((seq, GW), BF16), headbuf, headbuf],
        compiler_params=_params(),
        name="dilated",
    )(*zc_views)


CONV_W = 4
CONV_PAD = 8


def _log_sigmoid(x):
    return jnp.minimum(x, 0.0) - jnp.log1p(jnp.exp(-jnp.abs(x)))


def _mlstm_kernel(zd_ref, zm_ref, gb_ref, cw_ref, cb_ref, y_ref, xpad, q_s, k_s, kt_s, a_r, b_r, cm_c,
                  b_c, b_f, c_s, n_s):
    seq = zd_ref.shape[0]
    rows = 512
    xpad[0:CONV_PAD, :] = jnp.zeros((CONV_PAD, 2 * GW), F32)
    for r in range(0, seq, rows):
        xpad[CONV_PAD + r:CONV_PAD + r + rows, :] = zd_ref[r:r + rows, 0:2 * GW].astype(F32)
    def conv_chunk(r):
        win = xpad[r:r + rows + CONV_PAD, :]
        acc = cb_ref[...] + cw_ref[CONV_W - 1:CONV_W, :] * win[CONV_PAD:, :]
        for k in range(1, CONV_W):
            acc = acc + cw_ref[CONV_W - 1 - k:CONV_W - k, :] * pltpu.roll(win, k, 0)[CONV_PAD:, :]
        qk = acc * jax.nn.sigmoid(acc)
        k = qk[:, GW:] * (HEAD_DIM ** -0.5)
        q_s[r:r + rows, :] = qk[:, :GW].astype(BF16)
        k_s[r:r + rows, :] = k.astype(BF16)
        for sub in range(rows // BLOCK):
            kt_s[r // BLOCK + sub] = k[sub * BLOCK:(sub + 1) * BLOCK, :].T

    heads = range(HEADS)
    low_half = lax.broadcasted_iota(jnp.int32, (1, LANES), 1) < HEAD_DIM

    def head_lanes(per_head):
        return jnp.concatenate([jnp.where(low_half, per_head[0], per_head[1]),
                                jnp.where(low_half, per_head[2], per_head[3])], axis=1)

    def decay_lanes(r):
        b_cols = b_c[r:r + rows, :]
        b_f[r:r + rows, :] = head_lanes([b_cols[:, h:h + 1] for h in heads])

    conv_chunks = iter(range(0, seq, rows))

    gt = (zm_ref[...] + gb_ref[...]).T[0:8, :]
    f_log = _log_sigmoid(pltpu.roll(gt, HEADS, 0))
    pos_in_chunk = lax.broadcasted_iota(jnp.int32, (8, seq), 1) % BLOCK
    steps = [1 << s for s in range(BLOCK.bit_length() - 1)]
    conv_chunk(next(conv_chunks))
    b = f_log
    for k in steps:
        b = b + jnp.where(pos_in_chunk >= k, pltpu.roll(b, k, 1), 0.0)
    conv_chunk(next(conv_chunks))
    a = gt - b
    cm = a
    for k in steps:
        cm = jnp.maximum(cm, jnp.where(pos_in_chunk >= k, pltpu.roll(cm, k, 1), NEG_INF))
    conv_chunk(next(conv_chunks))
    a_r[...] = a
    b_r[...] = b
    pad = jnp.zeros((LANES - 8, seq), F32)
    cm_c[...] = jnp.concatenate([cm, pad], axis=0).T
    b_c[...] = jnp.concatenate([b, pad], axis=0).T
    conv_chunk(next(conv_chunks))
    for r in range(0, seq, rows):
        decay_lanes(r)

    c_s[...] = jnp.zeros((GW, GW), F32)
    n_s[...] = jnp.zeros((GW, GW), F32)
    lane_head = lax.broadcasted_iota(jnp.int32, (1, GW), 1) // HEAD_DIM
    same_head = (lax.broadcasted_iota(jnp.int32, (GW, GW), 0) // HEAD_DIM
                 == lax.broadcasted_iota(jnp.int32, (GW, GW), 1) // HEAD_DIM)
    causal = (lax.broadcasted_iota(jnp.int32, (BLOCK, BLOCK), 1)
              <= lax.broadcasted_iota(jnp.int32, (BLOCK, BLOCK), 0))

    ones_blk = jnp.ones((BLOCK, LANES), BF16)
    head_sum = (lax.broadcasted_iota(jnp.int32, (HEADS * BLOCK, GW), 0) // BLOCK
                == lax.broadcasted_iota(jnp.int32, (HEADS * BLOCK, GW), 1) // HEAD_DIM).astype(BF16)

    def chunk(c, m_run):
        r0 = pl.multiple_of(c * BLOCK, BLOCK)
        qb = q_s[pl.ds(r0, BLOCK), :]
        kb = k_s[pl.ds(r0, BLOCK), :]
        v = zd_ref[pl.ds(r0, BLOCK), 2 * GW:3 * GW]
        og = zd_ref[pl.ds(r0, BLOCK), 3 * GW:4 * GW].astype(F32)
        a_rows = a_r[:, pl.ds(r0, BLOCK)]
        b_rows = b_r[:, pl.ds(r0, BLOCK)]
        cm_cols = cm_c[pl.ds(r0, BLOCK), :]

        k_heads = jnp.concatenate(
            [jnp.where(lane_head == h, kb, jnp.zeros_like(kb)) for h in heads], axis=0)
        v_heads = jnp.concatenate(
            [jnp.where(lane_head == h, v, jnp.zeros_like(v)) for h in heads], axis=0)
        scores = _dot_nt(qb, k_heads)
        carried = _dot(qb, jnp.concatenate([c_s[...], n_s[...]], axis=1).astype(BF16))

        w_intra, g_rep, wk_rows, decays, m_next = [], [], [], [], []
        for h in heads:
            a_row = a_rows[h:h + 1, :]
            g = jnp.maximum(jnp.broadcast_to(cm_cols[:, h:h + 1], (BLOCK, BLOCK)), m_run[h])
            g_rep.append(g)
            w_intra.append(jnp.where(causal, jnp.exp(a_row - g), 0.0))
            g_end = jnp.maximum(m_run[h], jnp.max(a_row, axis=-1, keepdims=True))
            m_next.append(b_rows[h:h + 1, BLOCK - 1:BLOCK] + g_end)
            decays.append(jnp.exp(m_run[h] - g_end))
            wk_rows.append(jnp.exp(a_row - g_end))
        g_full = head_lanes(g_rep)
        inter_scale = jnp.exp(head_lanes(m_run) - g_full)
        floor = jnp.exp(-(b_f[pl.ds(r0, BLOCK), :] + g_full))

        wk = jnp.concatenate([jnp.broadcast_to(w, (HEAD_DIM, BLOCK)) for w in wk_rows], axis=0)
        decay = jnp.concatenate([jnp.broadcast_to(dd, (HEAD_DIM, 1)) for dd in decays], axis=0)
        ktw = kt_s[c] * wk
        update = _dot(ktw.astype(BF16), jnp.concatenate([v, ones_blk], axis=1))

        sqk = (scores * jnp.concatenate(w_intra, axis=1)).astype(BF16)
        intra = _dot(sqk, jnp.concatenate([v_heads, head_sum], axis=1))
        num = intra[:, :GW] + inter_scale * carried[:, :GW]
        den = intra[:, GW:] + inter_scale * carried[:, GW:]
        h_out = num / jnp.maximum(jnp.abs(den), floor)
        y_ref[pl.ds(r0, BLOCK), :] = (jax.nn.sigmoid(og) * h_out).astype(BF16)

        k_sum = update[:, GW:]
        c_s[...] = decay * c_s[...] + jnp.where(same_head, update[:, :GW], 0.0)
        n_s[...] = decay * n_s[...] + jnp.where(same_head, jnp.concatenate([k_sum, k_sum], axis=1), 0.0)
        return tuple(m_next)

    lax.fori_loop(0, seq // BLOCK, chunk, tuple(jnp.zeros((1, 1), F32) for _ in range(HEADS)), unroll=8)


def _mlstm(zd, zm, gb, cw, cb, bsz, seq):
    full = lambda a: pl.BlockSpec(a.shape, lambda b: (0,) * a.ndim)
    return pl.pallas_call(
        _mlstm_kernel,
        grid=(bsz,),
        in_specs=[pl.BlockSpec((seq, ZD_W), lambda b: (b, 0)),
                  pl.BlockSpec((seq, LANES), lambda b: (b, 0)), full(gb), full(cw), full(cb)],
        out_specs=pl.BlockSpec((seq, GW), lambda b: (b, 0)),
        out_shape=jax.ShapeDtypeStruct((bsz * seq, GW), BF16),
        scratch_shapes=[pltpu.VMEM((seq + CONV_PAD, 2 * GW), F32),
                        pltpu.VMEM((seq, GW), BF16), pltpu.VMEM((seq, GW), BF16),
                        pltpu.VMEM((seq // BLOCK, GW, BLOCK), F32),
                        pltpu.VMEM((8, seq), F32), pltpu.VMEM((8, seq), F32),
                        pltpu.VMEM((seq, LANES), F32), pltpu.VMEM((seq, LANES), F32),
                        pltpu.VMEM((seq, GW), F32),
                        pltpu.VMEM((GW, GW), F32), pltpu.VMEM((GW, GW), F32)],
        compiler_params=_params(),
        name="mlstm",
    )(zd, zm, gb, cw, cb)


FF_CHUNK = 256
LN_ROWS = 64


def _dense_kernel(with_next, ya_ref, yb_ref, yd_ref, oc1_ref, oc4_ref, oc16_ref, lc1_ref, lc4_ref, lc16_ref,
                  x_ref, wo_ref, g1_ref, b1_ref, wg_ref, wu_ref, wd_ref, g2_ref, b2_ref, *rest):
    if with_next:
        (w_ref, wm_ref, cos_ref, sin_ref, o_ref, zd_ref, za_ref, zc1_ref, zc4_ref, zc16_ref, zb_ref, zm_ref,
         xb_ref, acc_ref, nat_ref, stage_ref) = rest
    else:
        o_ref, xb_ref, acc_ref, nat_ref = rest
    tm = x_ref.shape[0]

    def natural(view, d, slot):
        if d == 1:
            return view[...].astype(F32)
        for r in range(d):
            for half in range(GW // LANES):
                col = r * GW + half * LANES
                nat_ref[slot, half, pl.ds(r, tm // d, stride=d), :] = view[:, col:col + LANES].astype(F32)
        return jnp.concatenate([nat_ref[slot, half] for half in range(GW // LANES)], axis=1)

    views = (oc1_ref, oc4_ref, oc16_ref, lc1_ref, lc4_ref, lc16_ref)
    slots = iter(range(nat_ref.shape[0]))
    nat = [natural(v, d, next(slots) if d > 1 else None) for v, d in zip(views, DILATIONS * 2)]
    outs, lse = nat[:3], nat[3:]
    top = jnp.maximum(jnp.maximum(lse[0], lse[1]), lse[2])
    wts = [jnp.exp2(l - top) for l in lse]
    mix = wts[0] * outs[0] + wts[1] * outs[1] + wts[2] * outs[2]
    yc = (mix / (wts[0] + wts[1] + wts[2])).astype(BF16)
    acc = ALPHA * x_ref[...]
    for i, y in ((0, ya_ref[...]), (1, yb_ref[...]), (3, yd_ref[...]), (2, yc)):
        acc = acc + _dot(y, wo_ref[i * GW:(i + 1) * GW, :])
    acc_ref[...] = acc
    for r in range(0, tm, LN_ROWS):
        rows = slice(r, r + LN_ROWS)
        x1 = _layer_norm(acc_ref[rows, :], g1_ref[...], b1_ref[...])
        xb_ref[rows, :] = x1.astype(BF16)
        acc_ref[rows, :] = ALPHA * x1
    for c in range(0, D_FF, FF_CHUNK):
        gate = _dot(xb_ref[...], wg_ref[:, c:c + FF_CHUNK])
        up = _dot(xb_ref[...], wu_ref[:, c:c + FF_CHUNK])
        act = (gate * jax.nn.sigmoid(gate) * up).astype(BF16)
        acc_ref[...] += _dot(act, wd_ref[c:c + FF_CHUNK, :])
    for r in range(0, tm, LN_ROWS):
        rows = slice(r, r + LN_ROWS)
        x2 = _layer_norm(acc_ref[rows, :], g2_ref[...], b2_ref[...])
        o_ref[rows, :] = x2
        if with_next:
            xb_ref[rows, :] = x2.astype(BF16)
    if with_next:
        _project_in(xb_ref, w_ref, wm_ref, cos_ref, sin_ref, zd_ref, za_ref, (zc1_ref, zc4_ref, zc16_ref),
                    zb_ref, zm_ref, stage_ref, 0, tm)


def _dense(ys, dil, x2, wo, g1, b1, wg, wu, wd, g2, b2, next_w, tables, tm):
    n = x2.shape[0]
    row = lambda w_: pl.BlockSpec((tm, w_), lambda i: (i, 0))
    view = lambda d: pl.BlockSpec((tm // d, d * GW), lambda i: (i, 0))
    once = lambda a: pl.BlockSpec(a.shape, lambda i: (0, 0), pipeline_mode=pl.Buffered(1))
    weights = [wo, g1, b1, wg, wu, wd, g2, b2] + list(next_w)
    tables = list(tables) if next_w else []
    out_specs = [row(D_MODEL)]
    out_shape = [jax.ShapeDtypeStruct((n, D_MODEL), F32)]
    scratch = [pltpu.VMEM((tm, D_MODEL), BF16), pltpu.VMEM((tm, D_MODEL), F32),
               pltpu.VMEM((2 * (len(DILATIONS) - 1), GW // LANES, tm, LANES), F32)]
    if next_w:
        z_specs, z_shapes, stage = _z_outputs(n, tm)
        out_specs += z_specs
        out_shape += z_shapes
        scratch.append(stage)
    return pl.pallas_call(
        functools.partial(_dense_kernel, bool(next_w)),
        grid=(n // tm,),
        in_specs=([row(GW)] * len(ys) + [view(d) for d in DILATIONS] * 2 + [row(D_MODEL)]
                  + [once(w) for w in weights] + [row(LANES)] * len(tables)),
        out_specs=out_specs,
        out_shape=out_shape,
        scratch_shapes=scratch,
        compiler_params=_params(),
        name="dense_block",
    )(*ys, *dil, x2, *weights, *tables)


def _pack_weights(w_in, b_w_uq, b_w_ukv, b_q_norm, a_bs, d_igate_b, d_fgate_b):
    nl = w_in.shape[0]
    o_b = ZA_W
    o_c = o_b + Q_RANK + KV_RANK + ROPE
    o_d = o_c + ZC_W
    o_g = o_d + ZD_W
    w_a = w_in[..., :o_b]
    w_cq = w_in[..., o_b:o_b + Q_RANK]
    w_ckv = w_in[..., o_b + Q_RANK:o_b + Q_RANK + KV_RANK]
    w_kr = w_in[..., o_b + Q_RANK + KV_RANK:o_c]
    w_c = w_in[..., o_c:o_d]
    w_d = w_in[..., o_d:o_g]
    w_gates = w_in[..., o_g:]
    zeros = lambda *s: jnp.zeros((nl,) + s, F32)
    w_b = jnp.concatenate([w_cq, zeros(D_MODEL, Q_PAD - Q_RANK), w_ckv], -1)
    w_main = jnp.concatenate([w_d, w_a, w_c, w_b], -1).astype(BF16)
    half = ROPE // 2
    w_kr_rot = jnp.concatenate([-w_kr[..., half:], w_kr[..., :half]], -1)
    w_misc = jnp.concatenate(
        [w_gates, zeros(D_MODEL, KR_LANE - 2 * HEADS), w_kr, w_kr_rot], -1).astype(BF16)

    wq = b_w_uq.reshape(nl, Q_RANK, HEADS, NOPE + ROPE)
    nope, x1, x2 = wq[..., :NOPE], wq[..., NOPE:NOPE + half], wq[..., NOPE + half:]
    tail = zeros(Q_RANK, HEADS, LANES - NOPE - ROPE)
    rowpad = ((0, 0), (0, Q_PAD - Q_RANK), (0, 0))
    wqm = jnp.pad(jnp.concatenate([nope, x1, x2, tail], -1).reshape(nl, Q_RANK, HEADS * LANES), rowpad)
    wqs = jnp.pad(jnp.concatenate([zeros(Q_RANK, HEADS, NOPE), -x2, x1, tail], -1)
                  .reshape(nl, Q_RANK, HEADS * LANES), rowpad)
    wkv = b_w_ukv.reshape(nl, KV_RANK, HEADS, NOPE + HEAD_DIM)
    wk = jnp.concatenate([wkv[..., :NOPE], zeros(KV_RANK, HEADS, LANES - NOPE)], -1)
    wk = wk.reshape(nl, KV_RANK, HEADS * LANES)
    wv = wkv[..., NOPE:].reshape(nl, KV_RANK, GW)
    qg = jnp.pad(b_q_norm, ((0, 0), (0, Q_PAD - Q_RANK)))[:, None, :]
    bias = jnp.repeat(jnp.swapaxes(a_bs, 1, 2), HEAD_DIM, axis=-1)
    gb = jnp.concatenate([d_igate_b, d_fgate_b, zeros(LANES - 2 * HEADS)], -1)[:, None, :]
    return dict(w_main=w_main, w_misc=w_misc, wqm=wqm.astype(BF16), wqs=wqs.astype(BF16),
                wk=wk.astype(BF16), wv=wv.astype(BF16), qg=qg, bias=bias, gb=gb)


@jax.jit
def _forward(x, positions, w_in, a_ln_g, a_ln_b, a_ws, a_bs, b_q_norm, b_kv_norm, b_w_uq, b_w_ukv,
             d_conv_w, d_conv_b, d_igate_b, d_fgate_b, w_out, ln1_g, ln1_b, w_gate, w_up, w_down,
             ln2_g, ln2_b):
    bsz, seq, _ = x.shape
    pk = _pack_weights(w_in, b_w_uq, b_w_ukv, b_q_norm, a_bs, d_igate_b, d_fgate_b)
    w_out_b, w_gate_b, w_up_b, w_down_b = (w.astype(BF16) for w in (w_out, w_gate, w_up, w_down))
    tm = 512
    x2 = x.reshape(bsz * seq, D_MODEL)
    row = lambda a, l: a[l][None, :]
    zd, za, *zc, zb, zm, cosd, sind, cosm, sinm = _inproj(x2, pk["w_main"][0], pk["w_misc"][0], positions, tm)
    rope_d = (cosd, sind)
    cosm, sinm = (t.reshape(bsz, seq, LANES) for t in (cosm, sinm))
    for l in range(DEPTH):
        ya = _gmlp(za, row(a_ln_g, l), row(a_ln_b, l), a_ws[l], pk["bias"][l], bsz, seq)
        yb = _mla(zb, zm, cosm, sinm, pk["qg"][l], row(b_kv_norm, l), pk["wqm"][l], pk["wqs"][l],
                  pk["wk"][l], pk["wv"][l], bsz, seq)
        dil = _dilated(zc, bsz, seq)
        yd = _mlstm(zd, zm, pk["gb"][l], d_conv_w[l], row(d_conv_b, l), bsz, seq)
        next_w = (pk["w_main"][l + 1], pk["w_misc"][l + 1]) if l + 1 < DEPTH else ()
        outs = _dense((ya, yb, yd), dil, x2, w_out_b[l], row(ln1_g, l), row(ln1_b, l), w_gate_b[l],
                      w_up_b[l], w_down_b[l], row(ln2_g, l), row(ln2_b, l), next_w, rope_d, tm)
        if next_w:
            x2, zd, za, *zc, zb, zm = outs
        else:
            x2 = outs[0]
    return x2.reshape(bsz, seq, D_MODEL)


def kernel(x, positions, w_in, a_ln_g, a_ln_b, a_ws, a_bs, b_q_norm, b_kv_norm, b_w_uq, b_w_ukv,
           d_conv_w, d_conv_b, d_igate_b, d_fgate_b, w_out, ln1_g, ln1_b, w_gate, w_up, w_down,
           ln2_g, ln2_b):
    return _forward(x, positions, w_in, a_ln_g, a_ln_b, a_ws, a_bs, b_q_norm, b_kv_norm, b_w_uq,
                    b_w_ukv, d_conv_w, d_conv_b, d_igate_b, d_fgate_b, w_out, ln1_g, ln1_b, w_gate,
                    w_up, w_down, ln2_g, ln2_b)
```
